```python
import jax
import jax.numpy as jnp
from jax import lax
import numpy as np

D_MODEL = 2048
BATCH = 16
SEQ = 2048
DEPTH = 2

CTX_LEN = 256
GRID_W = 64
NORM_EPS = 1e-6

GLA_HEADS = 4
GLA_DK = 64
GLA_DV = 128
GLA_LOW_RANK = 16
GLA_TAU = 16.0
GLA_CHUNK = 64

SWA_Q_HEADS = 8
SWA_KV_HEADS = 2
SWA_HEAD_DIM = 64
SWA_WINDOW = 128
SWA_BLOCK = 128
ROPE_BASE = 10000.0

NA_HEADS = 8
NA_HEAD_DIM = 64
NA_KR = 8
NA_KC = 16

SGU_GROUPS = 4
SGU_GROUP_CH = 128
SGU_WIDTH = SGU_GROUPS * SGU_GROUP_CH
SGU_CHUNK = 128

N_BRANCH = 4
BRANCH_WIDTH = 512

N_EXPERTS = 16
D_FF_EXPERT = 2048
EC_CAPACITY_FACTOR = 2

IN_SIZES = (
    GLA_HEADS * GLA_DK, GLA_HEADS * GLA_DK, GLA_HEADS * GLA_DV, GLA_HEADS * GLA_DV,
    GLA_LOW_RANK, GLA_LOW_RANK,
    SWA_Q_HEADS * SWA_HEAD_DIM, SWA_KV_HEADS * SWA_HEAD_DIM, SWA_KV_HEADS * SWA_HEAD_DIM,
    NA_HEADS * NA_HEAD_DIM, NA_HEADS * NA_HEAD_DIM, NA_HEADS * NA_HEAD_DIM,
    SGU_WIDTH, SGU_WIDTH,
)
IN_COLS = sum(IN_SIZES)
IN_SPLIT_POINTS = tuple(int(p) for p in np.cumsum(IN_SIZES)[:-1])

kernel_name = 'hybrid_dit_gla_swa_natten_sgu_ecmoe'


def rms_norm(x, g):
    xf = x.astype(jnp.float32)
    y = xf * lax.rsqrt(jnp.mean(xf * xf, axis=-1, keepdims=True) + NORM_EPS)
    return (y * g.astype(jnp.float32)).astype(x.dtype)


def layer_norm(x, g, b):
    xf = x.astype(jnp.float32)
    mu = jnp.mean(xf, axis=-1, keepdims=True)
    var = jnp.mean(jnp.square(xf - mu), axis=-1, keepdims=True)
    y = (xf - mu) * lax.rsqrt(var + NORM_EPS) * g.astype(jnp.float32) + b.astype(jnp.float32)
    return y.astype(x.dtype)


def modulate(x, g, shift, scale):
    return rms_norm(x, g) * (1 + scale) + shift


def axial_rope(x):
    T, d = x.shape[1], x.shape[-1]
    quarter = d // 4
    freqs = ROPE_BASE ** (-jnp.arange(quarter, dtype=jnp.float32) / quarter)
    t = jnp.arange(T)
    row = (t // GRID_W).astype(jnp.float32)
    col = (t % GRID_W).astype(jnp.float32)
    ang = jnp.concatenate([row[:, None] * freqs, col[:, None] * freqs], axis=-1)
    cos = jnp.cos(ang)[None, :, None, :]
    sin = jnp.sin(ang)[None, :, None, :]
    x1 = x[..., : d // 2].astype(jnp.float32)
    x2 = x[..., d // 2:].astype(jnp.float32)
    return jnp.concatenate([x1 * cos - x2 * sin, x1 * sin + x2 * cos], axis=-1).astype(x.dtype)


def softmax_with_sink(logits, sink):
    sink_col = jnp.broadcast_to(sink, logits.shape[:-1] + (1,))
    return jax.nn.softmax(jnp.concatenate([logits, sink_col], axis=-1), axis=-1)[..., :-1]


def gla_scan(q, k, v, log_a, s0):
    B, T, H, _ = q.shape
    n = T // GLA_CHUNK

    def chunkify(t):
        return t.reshape(B, n, GLA_CHUNK, H, t.shape[-1]).transpose(1, 0, 3, 2, 4)

    qc, kc, vc, ac = chunkify(q), chunkify(k), chunkify(v), chunkify(log_a)
    b = jnp.cumsum(ac, axis=3)
    b_last = b[:, :, :, -1:]
    q_in = qc * jnp.exp(b)
    k_in = kc * jnp.exp(-b)
    k_to_end = kc * jnp.exp(b_last - b)
    mask = jnp.tril(jnp.ones((GLA_CHUNK, GLA_CHUNK), dtype=bool))
    attn = jnp.where(mask, jnp.einsum('nbhld,nbhmd->nbhlm', q_in, k_in), 0.0)
    intra = jnp.einsum('nbhlm,nbhmv->nbhlv', attn, vc)

    def step(s, inp):
        q_i, kte_i, v_i, bl_i = inp
        inter = jnp.einsum('bhld,bhdv->bhlv', q_i, s)
        s = s * jnp.exp(bl_i)[:, :, 0, :, None] + jnp.einsum('bhld,bhlv->bhdv', kte_i, v_i)
        return s, inter

    s_fin, inter = lax.scan(step, s0, (q_in, k_to_end, vc, b_last))
    o = (intra + inter).transpose(1, 0, 3, 2, 4).reshape(B, T, H, v.shape[-1])
    return o, s_fin


def gla_final_state(k, v, log_a):
    b = jnp.cumsum(log_a, axis=1)
    w = jnp.exp(b[:, -1:] - b)
    return jnp.einsum('bthd,bthv->bhdv', k * w, v)


def gla_heads(q, k, v, a_f, a_b, w_a2, b_a2):
    B, T, _ = q.shape
    f32 = jnp.float32
    q = q.reshape(B, T, GLA_HEADS, GLA_DK).astype(f32) * GLA_DK ** -0.5
    k = k.reshape(B, T, GLA_HEADS, GLA_DK).astype(f32)
    v = v.reshape(B, T, GLA_HEADS, GLA_DV).astype(f32)
    la_f = jax.nn.log_sigmoid((a_f @ w_a2[0] + b_a2[0]).astype(f32)).reshape(B, T, GLA_HEADS, GLA_DK) / GLA_TAU
    la_b = jax.nn.log_sigmoid((a_b @ w_a2[1] + b_a2[1]).astype(f32)).reshape(B, T, GLA_HEADS, GLA_DK) / GLA_TAU
    return q, k, v, la_f, la_b


def gla_output(o, r, g):
    B, T = o.shape[:2]
    o = o * lax.rsqrt(jnp.mean(o * o, axis=-1, keepdims=True) + NORM_EPS) * g.astype(jnp.float32)
    return (o.reshape(B, T, GLA_HEADS * GLA_DV) * jax.nn.silu(r.astype(jnp.float32))).astype(r.dtype)


def gla_mixer(pl, pc, w_a2, b_a2, g_out, need_ctx):
    ql, kl, vl, la_fl, la_bl = gla_heads(pl[0], pl[1], pl[2], pl[4], pl[5], w_a2, b_a2)
    qc, kc, vc, la_fc, la_bc = gla_heads(pc[0], pc[1], pc[2], pc[4], pc[5], w_a2, b_a2)

    def flip(t):
        return t[:, ::-1]

    y_ctx = None
    if need_ctx:
        zero = jnp.zeros((qc.shape[0], GLA_HEADS, GLA_DK, GLA_DV), jnp.float32)
        oc_f, s_f = gla_scan(qc, kc, vc, la_fc, zero)
        oc_b, s_b = gla_scan(flip(qc), flip(kc), flip(vc), flip(la_bc), zero)
        y_ctx = gla_output(oc_f + flip(oc_b), pc[3], g_out)
    else:
        s_f = gla_final_state(kc, vc, la_fc)
        s_b = gla_final_state(flip(kc), flip(vc), flip(la_bc))
    ol_f, _ = gla_scan(ql, kl, vl, la_fl, s_f)
    ol_b, _ = gla_scan(flip(ql), flip(kl), flip(vl), flip(la_bl), s_b)
    y_lat = gla_output(ol_f + flip(ol_b), pl[3], g_out)
    return y_lat, y_ctx


def band3(t, nb, blk):
    tb = t.reshape((t.shape[0], nb, blk) + t.shape[2:])
    tp = jnp.pad(tb, [(0, 0), (1, 1)] + [(0, 0)] * (tb.ndim - 2))
    return jnp.concatenate([tp[:, :-2], tp[:, 1:-1], tp[:, 2:]], axis=2)


def swa_mixer(pl, pc, sink, need_ctx):
    B, T, _ = pl[0].shape
    L = pc[0].shape[1]
    G = SWA_Q_HEADS // SWA_KV_HEADS
    d = SWA_HEAD_DIM
    nb = T // SWA_BLOCK
    scale = d ** -0.5
    q = axial_rope(pl[0].reshape(B, T, SWA_Q_HEADS, d))
    k = axial_rope(pl[1].reshape(B, T, SWA_KV_HEADS, d))
    v = pl[2].reshape(B, T, SWA_KV_HEADS, d)
    kc = pc[1].reshape(B, L, SWA_KV_HEADS, d)
    vc = pc[2].reshape(B, L, SWA_KV_HEADS, d)
    sink_g = sink.reshape(SWA_KV_HEADS, G).astype(jnp.float32)

    qb = q.reshape(B, nb, SWA_BLOCK, SWA_KV_HEADS, G, d)
    kb = band3(k, nb, SWA_BLOCK)
    vb = band3(v, nb, SWA_BLOCK)
    n_loc = 3 * SWA_BLOCK
    qi = jnp.arange(SWA_BLOCK)
    kj = jnp.arange(n_loc)
    k_pos = jnp.arange(nb)[:, None] * SWA_BLOCK - SWA_BLOCK + kj[None, :]
    in_window = jnp.abs(kj[None, :] - SWA_BLOCK - qi[:, None]) <= SWA_WINDOW
    valid = in_window[None] & ((k_pos >= 0) & (k_pos < T))[:, None, :]
    s_loc = jnp.einsum('bnqhgd,bnkhd->bnhgqk', qb, kb).astype(jnp.float32) * scale
    s_loc = jnp.where(valid[None, :, None, None], s_loc, -jnp.inf)
    s_ctx = jnp.einsum('bnqhgd,blhd->bnhgql', qb, kc).astype(jnp.float32) * scale
    p = softmax_with_sink(jnp.concatenate([s_loc, s_ctx], axis=-1),
                          sink_g[None, None, :, :, None, None]).astype(v.dtype)
    o = (jnp.einsum('bnhgqk,bnkhd->bnqhgd', p[..., :n_loc], vb)
         + jnp.einsum('bnhgql,blhd->bnqhgd', p[..., n_loc:], vc))
    y_lat = o.reshape(B, T, SWA_Q_HEADS * d)

    y_ctx = None
    if need_ctx:
        qc = pc[0].reshape(B, L, SWA_KV_HEADS, G, d)
        s = jnp.einsum('bqhgd,bkhd->bhgqk', qc, kc).astype(jnp.float32) * scale
        pcx = softmax_with_sink(s, sink_g[None, :, :, None, None]).astype(vc.dtype)
        y_ctx = jnp.einsum('bhgqk,bkhd->bqhgd', pcx, vc).reshape(B, L, SWA_Q_HEADS * d)
    return y_lat, y_ctx


def na_mixer(pl, pc, rpb, rows, need_ctx):
    B, T, _ = pl[0].shape
    L = pc[0].shape[1]
    H, d = NA_HEADS, NA_HEAD_DIM
    kr = min(NA_KR, rows)
    scale = d ** -0.5
    q = pl[0].reshape(B, rows, GRID_W, H, d)
    k = pl[1].reshape(B, rows, GRID_W, H, d)
    v = pl[2].reshape(B, rows, GRID_W, H, d)
    kc = pc[1].reshape(B, L, H, d)
    vc = pc[2].reshape(B, L, H, d)

    r = jnp.arange(rows)
    row_idx = jnp.clip(r - kr // 2, 0, rows - kr)[:, None] + jnp.arange(kr)[None, :]
    col = jnp.arange(GRID_W)
    col_start = jnp.clip(col - NA_KC // 2, 0, GRID_W - NA_KC)
    col_ok = (col[None, :] >= col_start[:, None]) & (col[None, :] < col_start[:, None] + NA_KC)
    row_off = row_idx - r[:, None] + (NA_KR - 1)
    col_off = jnp.clip(col[None, :] - col[:, None], -(NA_KC - 1), NA_KC - 1) + (NA_KC - 1)
    bias = rpb[:, row_off[:, None, :, None], col_off[None, :, None, :]].transpose(1, 0, 2, 3, 4)

    kg = k[:, row_idx]
    vg = v[:, row_idx]
    n_loc = kr * GRID_W
    s_loc = jnp.einsum('brchd,brjkhd->brhcjk', q, kg).astype(jnp.float32) * scale + bias[None].astype(jnp.float32)
    s_loc = jnp.where(col_ok[:, None, :], s_loc, -jnp.inf).reshape(B, rows, H, GRID_W, n_loc)
    s_ctx = jnp.einsum('brchd,blhd->brhcl', q, kc).astype(jnp.float32) * scale
    p = jax.nn.softmax(jnp.concatenate([s_loc, s_ctx], axis=-1), axis=-1).astype(v.dtype)
    p_loc = p[..., :n_loc].reshape(B, rows, H, GRID_W, kr, GRID_W)
    o = (jnp.einsum('brhcjk,brjkhd->brchd', p_loc, vg)
         + jnp.einsum('brhcl,blhd->brchd', p[..., n_loc:], vc))
    y_lat = o.reshape(B, T, H * d)

    y_ctx = None
    if need_ctx:
        qc = pc[0].reshape(B, L, H, d)
        s = jnp.einsum('bqhd,bkhd->bhqk', qc, kc).astype(jnp.float32) * scale
        pcx = jax.nn.softmax(s, axis=-1).astype(vc.dtype)
        y_ctx = jnp.einsum('bhqk,bkhd->bqhd', pcx, vc).reshape(B, L, H * d)
    return y_lat, y_ctx


def sgu_mixer(u, v, ln_g, ln_b, w_s, b_s):
    B, T, _ = u.shape
    n = T // SGU_CHUNK
    u = jax.nn.gelu(u)
    vn = layer_norm(jax.nn.gelu(v), ln_g, ln_b).reshape(B, n, SGU_CHUNK, SGU_GROUPS, SGU_GROUP_CH)
    mixed = jnp.einsum('gpq,bnqgc->bnpgc', w_s, vn) + b_s.T[None, None, :, :, None]
    return u * mixed.reshape(B, T, SGU_WIDTH)


def merge_branches(h, branches, w_gate, b_gate, w_branch, w_out):
    acc = None
    for n in range(N_BRANCH):
        gate = jax.nn.sigmoid(h @ w_gate[n] + b_gate[n])
        term = gate * (branches[n] @ w_branch[n])
        acc = term if acc is None else acc + term
    return acc @ w_out


def expert_choice_moe(h, w_router, w_g, w_u, w_d):
    B, T, _ = h.shape
    cap = EC_CAPACITY_FACTOR * T // N_EXPERTS
    aff = jax.nn.softmax((h @ w_router).astype(jnp.float32), axis=-1)
    gate, idx = lax.top_k(jnp.swapaxes(aff, 1, 2), cap)
    bidx = jnp.arange(B)[:, None, None]
    xe = h[bidx, idx]
    hid = jax.nn.silu(jnp.einsum('becd,edf->becf', xe, w_g)) * jnp.einsum('becd,edf->becf', xe, w_u)
    ye = jnp.einsum('becf,efd->becd', hid, w_d) * gate[..., None].astype(h.dtype)
    return jnp.zeros_like(h).at[bidx, idx].add(ye)


def setup_inputs(seed: int = 0) -> dict:
    key = jax.random.key(seed)
    ks = jax.random.split(key, 30)
    f32 = jnp.float32
    L, D, F = DEPTH, D_MODEL, D_FF_EXPERT

    def nrm(k, shape, scale):
        return jax.random.normal(k, shape, f32) * scale

    return {
        'x': nrm(ks[0], (BATCH, SEQ, D), 1.0),
        'c': nrm(ks[1], (BATCH, D), 1.0),
        'ctx': nrm(ks[2], (BATCH, CTX_LEN, D), 1.0),
        'c_ctx': nrm(ks[3], (D,), 1.0),
        'w_ada': nrm(ks[4], (L, D, 6 * D), 0.5 * D ** -0.5),
        'b_ada': nrm(ks[5], (L, 6 * D), 0.02),
        'norm1_g': 1.0 + nrm(ks[6], (L, D), 0.02),
        'norm2_g': 1.0 + nrm(ks[7], (L, D), 0.02),
        'w_in': nrm(ks[8], (L, D, IN_COLS), D ** -0.5),
        'gla_w_a2': nrm(ks[9], (L, 2, GLA_LOW_RANK, GLA_HEADS * GLA_DK), GLA_LOW_RANK ** -0.5),
        'gla_b_a2': nrm(ks[10], (L, 2, GLA_HEADS * GLA_DK), 0.1),
        'gla_norm_g': 1.0 + nrm(ks[11], (L, GLA_DV), 0.02),
        'swa_sink': nrm(ks[12], (L, SWA_Q_HEADS), 0.5),
        'na_rpb': nrm(ks[13], (L, NA_HEADS, 2 * NA_KR - 1, 2 * NA_KC - 1), 0.2),
        'sgu_ln_g': 1.0 + nrm(ks[14], (L, SGU_WIDTH), 0.02),
        'sgu_ln_b': nrm(ks[15], (L, SGU_WIDTH), 0.02),
        'sgu_w_s': nrm(ks[16], (L, SGU_GROUPS, SGU_CHUNK, SGU_CHUNK), SGU_CHUNK ** -0.5),
        'sgu_b_s': 1.0 + nrm(ks[17], (L, SGU_GROUPS, SGU_CHUNK), 0.02),
        'w_gate': nrm(ks[18], (L, N_BRANCH, D, D), D ** -0.5),
        'b_gate': nrm(ks[19], (L, N_BRANCH, D), 0.02),
        'w_branch': nrm(ks[20], (L, N_BRANCH, BRANCH_WIDTH, D), BRANCH_WIDTH ** -0.5),
        'w_out': nrm(ks[21], (L, D, D), D ** -0.5),
        'w_router': nrm(ks[22], (L, D, N_EXPERTS), D ** -0.5),
        'w_exp_gate': nrm(ks[23], (L, N_EXPERTS, D, F), D ** -0.5),
        'w_exp_up': nrm(ks[24], (L, N_EXPERTS, D, F), D ** -0.5),
        'w_exp_down': nrm(ks[25], (L, N_EXPERTS, F, D), F ** -0.5),
        'final_norm_g': 1.0 + nrm(ks[26], (D,), 0.02),
    }


def reference(x, c, ctx, c_ctx, w_ada, b_ada, norm1_g, norm2_g, w_in, gla_w_a2, gla_b_a2,
              gla_norm_g, swa_sink, na_rpb, sgu_ln_g, sgu_ln_b, sgu_w_s, sgu_b_s, w_gate, b_gate,
              w_branch, w_out, w_router, w_exp_gate, w_exp_up, w_exp_down, final_norm_g):
    B, T, D = x.shape
    rows = T // GRID_W
    c_act = jax.nn.silu(c)
    cc_act = jax.nn.silu(c_ctx)
    xc = ctx
    for l in range(DEPTH):
        need_ctx = l < DEPTH - 1
        mod = (c_act @ w_ada[l] + b_ada[l]).reshape(B, 6, 1, D)
        modc = (cc_act @ w_ada[l] + b_ada[l]).reshape(6, D)
        h = modulate(x, norm1_g[l], mod[:, 0], mod[:, 1])
        hc = modulate(xc, norm1_g[l], modc[0], modc[1])
        pl = jnp.split(h @ w_in[l], IN_SPLIT_POINTS, axis=-1)
        pc = jnp.split(hc @ w_in[l], IN_SPLIT_POINTS, axis=-1)
        ya_l, ya_c = gla_mixer(pl[0:6], pc[0:6], gla_w_a2[l], gla_b_a2[l], gla_norm_g[l], need_ctx)
        yb_l, yb_c = swa_mixer(pl[6:9], pc[6:9], swa_sink[l], need_ctx)
        yn_l, yn_c = na_mixer(pl[9:12], pc[9:12], na_rpb[l], rows, need_ctx)
        yd_l = sgu_mixer(pl[12], pl[13], sgu_ln_g[l], sgu_ln_b[l], sgu_w_s[l], sgu_b_s[l])
        x = x + mod[:, 2] * merge_branches(h, (ya_l, yb_l, yn_l, yd_l), w_gate[l], b_gate[l], w_branch[l], w_out[l])
        h2 = modulate(x, norm2_g[l], mod[:, 3], mod[:, 4])
        x = x + mod[:, 5] * expert_choice_moe(h2, w_router[l], w_exp_gate[l], w_exp_up[l], w_exp_down[l])
        if need_ctx:
            yd_c = sgu_mixer(pc[12], pc[13], sgu_ln_g[l], sgu_ln_b[l], sgu_w_s[l], sgu_b_s[l])
            xc = xc + modc[2] * merge_branches(hc, (ya_c, yb_c, yn_c, yd_c), w_gate[l], b_gate[l], w_branch[l], w_out[l])
            hc2 = modulate(xc, norm2_g[l], modc[3], modc[4])
            xc = xc + modc[5] * expert_choice_moe(hc2, w_router[l], w_exp_gate[l], w_exp_up[l], w_exp_down[l])
    return rms_norm(x, final_norm_g)
```

```python
import functools

import numpy as np
import jax
import jax.numpy as jnp
from jax import lax
from jax.experimental import pallas as pl
from jax.experimental.pallas import tpu as pltpu

F32 = jnp.float32
BF16 = jnp.bfloat16
HIGHEST = lax.Precision.HIGHEST

D = 2048
T = 2048
L = 256
GRID_W = 64
ROWS = T // GRID_W
EPS = 1e-6
DEPTH = 2
NEG = -1e30

GLA_HEADS, GLA_DK, GLA_DV, GLA_RANK, GLA_TAU, GLA_CHUNK = 4, 64, 128, 16, 16.0, 64
GLA_BLOCK = 256
SWA_BLOCK, SWA_WINDOW = 128, 128
NA_KR, NA_KC, NA_HEADS = 8, 16, 8
SGU_CHUNK = 128
N_EXPERTS = 16
LANES = 128
HEAD = 64

NP = 5120
COL = dict(gla_v=0, gla_r=512, swa_q=1024, na_q=1536, na_k=2048, na_v=2560, sgu_u=3072, sgu_v=3584,
           gla_q=4096, gla_k=4352, swa_k=4608, swa_v=4736, gla_a=4864)
REF = dict(gla_q=0, gla_k=256, gla_v=512, gla_r=1024, gla_af=1536, gla_ab=1552, swa_q=1568, swa_k=2080,
           swa_v=2208, na_q=2336, na_k=2848, na_v=3360, sgu_u=3872, sgu_v=4384)

VMEM_LIMIT = 56 * 1024 * 1024


def _cp(*sem, vmem=VMEM_LIMIT):
    return pltpu.CompilerParams(dimension_semantics=sem, vmem_limit_bytes=vmem)


def _sigmoid(x):
    return 1.0 / (1.0 + jnp.exp(-x))


def _rms(x, g):
    return x * lax.rsqrt(jnp.mean(x * x, axis=-1, keepdims=True) + EPS) * g


def _dot(a, b):
    return jnp.dot(a, b, preferred_element_type=F32)


def _dot_nt(a, b):
    return lax.dot_general(a, b, (((1,), (1,)), ((), ())), preferred_element_type=F32)


def _ada_kernel(c_ref, w_ref, b_ref, o_ref):
    c = c_ref[...]
    act = (c * _sigmoid(c)).astype(BF16)
    o_ref[...] = _dot(act, w_ref[...].astype(BF16)) + b_ref[...]


def ada(cvec, w, b):
    rows, tn = cvec.shape[0], 1024
    return pl.pallas_call(
        _ada_kernel, grid=(6 * D // tn,),
        in_specs=[pl.BlockSpec((rows, D), lambda j: (0, 0)),
                  pl.BlockSpec((D, tn), lambda j: (0, j)),
                  pl.BlockSpec((1, tn), lambda j: (0, j))],
        out_specs=pl.BlockSpec((rows, tn), lambda j: (0, j)),
        out_shape=jax.ShapeDtypeStruct((rows, 6 * D), F32),
        compiler_params=_cp("parallel"), name="ada")(cvec, w, b)


def _inproj_kernel(x_ref, shift_ref, scale_ref, g_ref, w_ref, h_ref, p_ref, hs_ref):
    @pl.when(pl.program_id(1) == 0)
    def _():
        h = (_rms(x_ref[...], g_ref[...]) * (1.0 + scale_ref[0]) + shift_ref[0]).astype(BF16)
        hs_ref[...] = h
        h_ref[...] = h

    p_ref[...] = _dot(hs_ref[...], w_ref[...])


def inproj(x2, mod3, g, w, tm, mod_row):
    M, tn = x2.shape[0], 512
    tm = min(tm, M)
    return pl.pallas_call(
        _inproj_kernel, grid=(M // tm, NP // tn),
        in_specs=[pl.BlockSpec((tm, D), lambda i, j: (i, 0)),
                  pl.BlockSpec((1, 1, D), lambda i, j: (mod_row(i) * 6 + 0, 0, 0)),
                  pl.BlockSpec((1, 1, D), lambda i, j: (mod_row(i) * 6 + 1, 0, 0)),
                  pl.BlockSpec((1, D), lambda i, j: (0, 0)),
                  pl.BlockSpec((D, tn), lambda i, j: (0, j))],
        out_specs=[pl.BlockSpec((tm, D), lambda i, j: (i, 0)),
                   pl.BlockSpec((tm, tn), lambda i, j: (i, j))],
        out_shape=[jax.ShapeDtypeStruct((M, D), BF16), jax.ShapeDtypeStruct((M, NP), F32)],
        scratch_shapes=[pltpu.VMEM((tm, D), BF16)],
        compiler_params=_cp("parallel", "arbitrary"), name="inproj")(x2, mod3, mod3, g, w)


def _log_sigmoid(x):
    return jnp.minimum(x, 0.0) - jnp.log(1.0 + jnp.exp(-jnp.abs(x)))


def _gla_chunk(q, k, v, a, w2, b2, s_ref, tri, cmask2, last_row, half):
    la = _log_sigmoid(_dot(a.astype(BF16), w2) + b2) * (1.0 / GLA_TAU)
    bc = jnp.dot(tri, la, precision=HIGHEST, preferred_element_type=F32)
    bl = bc[last_row:last_row + 1, :]
    q_in = (q * GLA_DK ** -0.5) * jnp.exp(bc)
    k_in = k * jnp.exp(-bc)
    kte = k * jnp.exp(bl - bc)
    outs = []
    for pt in range(2):
        sl = slice(pt * LANES, (pt + 1) * LANES)
        qt = q_in[:, sl]
        qs = jnp.concatenate([jnp.where(half == 0, qt, 0.0), jnp.where(half == 1, qt, 0.0)],
                             axis=0).astype(BF16)
        att = jnp.where(cmask2, _dot_nt(qs, k_in[:, sl].astype(BF16)), 0.0)
        sp = s_ref[sl, :]
        inter = _dot(qs, sp.astype(BF16))
        kte_t = kte[:, sl].T
        dec = jnp.exp(jnp.sum(la[:, sl].T, axis=1, keepdims=True))
        new_s = []
        for hh in range(2):
            hd = 2 * pt + hh
            rs = slice(hh * HEAD, (hh + 1) * HEAD)
            vh = v[:, hd * GLA_DV:(hd + 1) * GLA_DV].astype(BF16)
            outs.append(_dot(att[rs].astype(BF16), vh) + inter[rs])
            new_s.append(sp[rs] * dec[rs] + _dot(kte_t[rs].astype(BF16), vh))
        s_ref[sl, :] = jnp.concatenate(new_s, axis=0)
    return jnp.concatenate(outs, axis=1)


def _gla_kernel(qc_ref, kc_ref, vc_ref, ac_ref, qf_ref, kf_ref, vf_ref, af_ref,
                qb_ref, kb_ref, vb_ref, ab_ref, w2_ref, b2_ref,
                ocf_ref, ocb_ref, of_ref, ob_ref, sf_ref, sb_ref):
    s = pl.program_id(1)
    n_chunk = GLA_BLOCK // GLA_CHUNK
    ri = lax.broadcasted_iota(jnp.int32, (GLA_CHUNK, GLA_CHUNK), 0)
    ci = lax.broadcasted_iota(jnp.int32, (GLA_CHUNK, GLA_CHUNK), 1)
    low, up = ri >= ci, ri <= ci
    tri_f, tri_b = low.astype(F32), up.astype(F32)
    cm_f = jnp.concatenate([low, low], axis=0)
    cm_b = jnp.concatenate([up, up], axis=0)
    half = lax.broadcasted_iota(jnp.int32, (GLA_CHUNK, LANES), 1) // HEAD
    w2f, w2b = w2_ref[:, :256], w2_ref[:, 256:]
    b2f, b2b = b2_ref[:, :256], b2_ref[:, 256:]

    @pl.when(s == 0)
    def _():
        sf_ref[...] = jnp.zeros_like(sf_ref)
        sb_ref[...] = jnp.zeros_like(sb_ref)

    def run(fwd, bwd, o_f, o_b):
        def body(c, carry):
            rf = pl.ds(pl.multiple_of(c * GLA_CHUNK, GLA_CHUNK), GLA_CHUNK)
            rb = pl.ds(pl.multiple_of((n_chunk - 1 - c) * GLA_CHUNK, GLA_CHUNK), GLA_CHUNK)
            q, k, v, a = fwd
            o_f[0, rf, :] = _gla_chunk(q[0, rf, :], k[0, rf, :], v[0, rf, :], a[0, rf, :],
                                       w2f, b2f, sf_ref, tri_f, cm_f, GLA_CHUNK - 1, half)
            q, k, v, a = bwd
            o_b[0, rb, :] = _gla_chunk(q[0, rb, :], k[0, rb, :], v[0, rb, :], a[0, rb, :],
                                       w2b, b2b, sb_ref, tri_b, cm_b, 0, half)
            return carry
        lax.fori_loop(0, n_chunk, body, 0)

    ctx = (qc_ref, kc_ref, vc_ref, ac_ref)

    @pl.when(s == 0)
    def _():
        run(ctx, ctx, ocf_ref, ocb_ref)

    @pl.when(s > 0)
    def _():
        run((qf_ref, kf_ref, vf_ref, af_ref), (qb_ref, kb_ref, vb_ref, ab_ref), of_ref, ob_ref)


def gla_scan(p3, pc3, w2, b2):
    B = p3.shape[0]
    nb = T // GLA_BLOCK
    cq, ck, cv, ca = COL["gla_q"] // 256, COL["gla_k"] // 256, COL["gla_v"] // 512, COL["gla_a"] // 128
    fwd = lambda s: jnp.maximum(s - 1, 0)
    bwd = lambda s: nb - jnp.maximum(s, 1)

    def spec(width, col, row):
        return pl.BlockSpec((1, GLA_BLOCK, width), lambda b, s: (b, row(s), col))

    zero = lambda s: 0
    in_specs = ([spec(256, cq, zero), spec(256, ck, zero), spec(512, cv, zero), spec(128, ca, zero)]
                + [spec(256, cq, fwd), spec(256, ck, fwd), spec(512, cv, fwd), spec(128, ca, fwd)]
                + [spec(256, cq, bwd), spec(256, ck, bwd), spec(512, cv, bwd), spec(128, ca, bwd)]
                + [pl.BlockSpec((128, 512), lambda b, s: (0, 0)), pl.BlockSpec((1, 512), lambda b, s: (0, 0))])
    out_specs = [spec(512, 0, zero), spec(512, 0, zero), spec(512, 0, fwd), spec(512, 0, bwd)]
    out_shape = [jax.ShapeDtypeStruct((B, L, 512), F32)] * 2 + [jax.ShapeDtypeStruct((B, T, 512), F32)] * 2
    return pl.pallas_call(
        _gla_kernel, grid=(B, nb + 1), in_specs=in_specs, out_specs=out_specs, out_shape=out_shape,
        scratch_shapes=[pltpu.VMEM((256, 128), F32), pltpu.VMEM((256, 128), F32)],
        compiler_params=_cp("parallel", "arbitrary"), name="gla_scan",
    )(pc3, pc3, pc3, pc3, p3, p3, p3, p3, p3, p3, p3, p3, w2, b2)


def _gla_out_kernel(of_ref, ob_ref, r_ref, g_ref, y_ref):
    o = of_ref[0] + ob_ref[0]
    r = r_ref[0]
    outs = []
    for hd in range(GLA_HEADS):
        sl = slice(hd * GLA_DV, (hd + 1) * GLA_DV)
        rh = r[:, sl]
        outs.append(_rms(o[:, sl], g_ref[...]) * (rh * _sigmoid(rh)))
    y_ref[0] = jnp.concatenate(outs, axis=1).astype(BF16)


def gla_out(o_f, o_b, p3, g):
    B, n, tr = o_f.shape[0], o_f.shape[1], 256
    blk = lambda col: pl.BlockSpec((1, tr, 512), lambda b, i: (b, i, col))
    return pl.pallas_call(
        _gla_out_kernel, grid=(B, n // tr),
        in_specs=[blk(0), blk(0), blk(COL["gla_r"] // 512), pl.BlockSpec((1, 128), lambda b, i: (0, 0))],
        out_specs=blk(0), out_shape=jax.ShapeDtypeStruct((B, n, 512), BF16),
        compiler_params=_cp("parallel", "parallel"), name="gla_out")(o_f, o_b, p3, g)


def _softmax_pv(s_list, v_list, sink=None):
    m = s_list[0].max(axis=1, keepdims=True)
    for s in s_list[1:]:
        m = jnp.maximum(m, s.max(axis=1, keepdims=True))
    if sink is not None:
        m = jnp.maximum(m, sink)
    es = [jnp.exp(s - m) for s in s_list]
    den = es[0].sum(axis=1, keepdims=True)
    for e in es[1:]:
        den = den + e.sum(axis=1, keepdims=True)
    if sink is not None:
        den = den + jnp.exp(sink - m)
    inv = 1.0 / den
    o = None
    for e, v in zip(es, v_list):
        t = _dot((e * inv).astype(BF16), v)
        o = t if o is None else o + t
    return o


def _rope(x, cos, sin, lane):
    partner = jnp.where((lane % HEAD) < HEAD // 2, pltpu.roll(x, LANES - HEAD // 2, 1),
                        pltpu.roll(x, HEAD // 2, 1))
    return x * cos + partner * sin


def _swa_kernel(sink_ref, q_ref, kp_ref, kc_ref, kn_ref, vp_ref, vc_ref, vn_ref, kx_ref, vx_ref,
                cos_ref, sin_ref, o_ref):
    n = pl.program_id(1)
    nb = pl.num_programs(1)
    blk = SWA_BLOCK
    lane = lax.broadcasted_iota(jnp.int32, (blk, LANES), 1)
    half = lane // HEAD

    def tab(ref, i):
        return ref[pl.ds(pl.multiple_of(i * blk, blk), blk), :]

    ip, inx = jnp.maximum(n - 1, 0), jnp.minimum(n + 1, nb - 1)
    kb = jnp.concatenate([_rope(kp_ref[0], tab(cos_ref, ip), tab(sin_ref, ip), lane),
                          _rope(kc_ref[0], tab(cos_ref, n), tab(sin_ref, n), lane),
                          _rope(kn_ref[0], tab(cos_ref, inx), tab(sin_ref, inx), lane)], axis=0).astype(BF16)
    vb = jnp.concatenate([vp_ref[0], vc_ref[0], vn_ref[0]], axis=0).astype(BF16)
    kx = kx_ref[0].astype(BF16)
    vx = vx_ref[0].astype(BF16)
    qi = lax.broadcasted_iota(jnp.int32, (blk, 3 * blk), 0)
    kj = lax.broadcasted_iota(jnp.int32, (blk, 3 * blk), 1)
    kpos = (n - 1) * blk + kj
    valid = (jnp.abs(kj - blk - qi) <= SWA_WINDOW) & (kpos >= 0) & (kpos < nb * blk)
    cos_q, sin_q = tab(cos_ref, n), tab(sin_ref, n)
    for p in range(4):
        sl = slice(p * LANES, (p + 1) * LANES)
        qt = _rope(q_ref[0, :, sl], cos_q, sin_q, lane) * HEAD ** -0.5
        o2 = []
        for hh in range(2):
            qm = jnp.where(half == hh, qt, 0.0).astype(BF16)
            s_loc = jnp.where(valid, _dot_nt(qm, kb), NEG)
            s_ctx = _dot_nt(qm, kx)
            o2.append(_softmax_pv([s_loc, s_ctx], [vb, vx], sink=sink_ref[p + 4 * hh]))
        o_ref[0, :, sl] = jnp.where(half == 0, o2[0], o2[1]).astype(BF16)


def swa(p3, pc3, sink, cos_t, sin_t):
    B = p3.shape[0]
    nb = T // SWA_BLOCK
    ck, cv = COL["swa_k"] // 128, COL["swa_v"] // 128
    prev = lambda n: jnp.maximum(n - 1, 0)
    cur = lambda n: n
    nxt = lambda n: jnp.minimum(n + 1, nb - 1)
    kv = lambda col, row: pl.BlockSpec((1, SWA_BLOCK, 128), lambda b, n: (b, row(n), col))
    full = lambda col: pl.BlockSpec((1, L, 128), lambda b, n: (b, 0, col))
    tab = pl.BlockSpec((T, 128), lambda b, n: (0, 0))
    return pl.pallas_call(
        _swa_kernel, grid=(B, nb),
        in_specs=[pl.BlockSpec(memory_space=pltpu.SMEM),
                  pl.BlockSpec((1, SWA_BLOCK, 512), lambda b, n: (b, n, COL["swa_q"] // 512)),
                  kv(ck, prev), kv(ck, cur), kv(ck, nxt), kv(cv, prev), kv(cv, cur), kv(cv, nxt),
                  full(ck), full(cv), tab, tab],
        out_specs=pl.BlockSpec((1, SWA_BLOCK, 512), lambda b, n: (b, n, 0)),
        out_shape=jax.ShapeDtypeStruct((B, T, 512), BF16),
        compiler_params=_cp("parallel", "parallel"), name="swa",
    )(sink, p3, p3, p3, p3, p3, p3, p3, pc3, pc3, cos_t, sin_t)


def _ctx_attn_kernel(sink_ref, q_ref, k_ref, v_ref, o_ref, *, use_sink, kv_tiles):
    n = q_ref.shape[1]
    half = lax.broadcasted_iota(jnp.int32, (n, LANES), 1) // HEAD
    for p in range(4):
        sl = slice(p * LANES, (p + 1) * LANES)
        ks = sl if kv_tiles == 4 else slice(0, LANES)
        kt = k_ref[0, :, ks].astype(BF16)
        vt = v_ref[0, :, ks].astype(BF16)
        qt = q_ref[0, :, sl] * HEAD ** -0.5
        o2 = []
        for hh in range(2):
            qm = jnp.where(half == hh, qt, 0.0).astype(BF16)
            sink = sink_ref[p + 4 * hh] if use_sink else None
            o2.append(_softmax_pv([_dot_nt(qm, kt)], [vt], sink=sink))
        o_ref[0, :, sl] = jnp.where(half == 0, o2[0], o2[1]).astype(BF16)


def ctx_attn(pc3, sink, qcol, kcol, vcol, kv_tiles, use_sink):
    B, w = pc3.shape[0], kv_tiles * 128
    return pl.pallas_call(
        functools.partial(_ctx_attn_kernel, use_sink=use_sink, kv_tiles=kv_tiles), grid=(B,),
        in_specs=[pl.BlockSpec(memory_space=pltpu.SMEM),
                  pl.BlockSpec((1, L, 512), lambda b: (b, 0, qcol // 512)),
                  pl.BlockSpec((1, L, w), lambda b: (b, 0, kcol // w)),
                  pl.BlockSpec((1, L, w), lambda b: (b, 0, vcol // w))],
        out_specs=pl.BlockSpec((1, L, 512), lambda b: (b, 0, 0)),
        out_shape=jax.ShapeDtypeStruct((B, L, 512), BF16),
        compiler_params=_cp("parallel"), name="ctx_attn")(sink, pc3, pc3, pc3)


def _na_kernel(q_ref, k_ref, v_ref, kx_ref, vx_ref, bias_ref, o_ref):
    r = pl.program_id(1)
    band = NA_KR * GRID_W
    start = pl.multiple_of(jnp.clip(r - NA_KR // 2, 0, ROWS - NA_KR) * GRID_W, GRID_W)
    half = lax.broadcasted_iota(jnp.int32, (GRID_W, LANES), 1) // HEAD
    for p in range(4):
        sl = slice(p * LANES, (p + 1) * LANES)
        kb = k_ref[0, pl.ds(start, band), sl].astype(BF16)
        vb = v_ref[0, pl.ds(start, band), sl].astype(BF16)
        kx = kx_ref[0, :, sl].astype(BF16)
        vx = vx_ref[0, :, sl].astype(BF16)
        qt = q_ref[0, :, sl] * HEAD ** -0.5
        o2 = []
        for hh in range(2):
            qm = jnp.where(half == hh, qt, 0.0).astype(BF16)
            s_loc = _dot_nt(qm, kb) + bias_ref[0, 2 * p + hh]
            o2.append(_softmax_pv([s_loc, _dot_nt(qm, kx)], [vb, vx]))
        o_ref[0, :, sl] = jnp.where(half == 0, o2[0], o2[1]).astype(BF16)


def na(p3, pc3, bias):
    B = p3.shape[0]
    cq, ck, cv = COL["na_q"] // 512, COL["na_k"] // 512, COL["na_v"] // 512
    shift = lambda r: r - jnp.clip(r - NA_KR // 2, 0, ROWS - NA_KR)
    return pl.pallas_call(
        _na_kernel, grid=(B, ROWS),
        in_specs=[pl.BlockSpec((1, GRID_W, 512), lambda b, r: (b, r, cq)),
                  pl.BlockSpec((1, T, 512), lambda b, r: (b, 0, ck)),
                  pl.BlockSpec((1, T, 512), lambda b, r: (b, 0, cv)),
                  pl.BlockSpec((1, L, 512), lambda b, r: (b, 0, ck)),
                  pl.BlockSpec((1, L, 512), lambda b, r: (b, 0, cv)),
                  pl.BlockSpec((1, NA_HEADS, GRID_W, NA_KR * GRID_W), lambda b, r: (shift(r), 0, 0, 0))],
        out_specs=pl.BlockSpec((1, GRID_W, 512), lambda b, r: (b, r, 0)),
        out_shape=jax.ShapeDtypeStruct((B, T, 512), BF16),
        compiler_params=_cp("parallel", "arbitrary"), name="na")(p3, p3, p3, pc3, pc3, bias)


def na_bias_table(rpb):
    col = jnp.arange(GRID_W)
    col_start = jnp.clip(col - NA_KC // 2, 0, GRID_W - NA_KC)
    col_ok = (col[None, :] >= col_start[:, None]) & (col[None, :] < col_start[:, None] + NA_KC)
    col_off = jnp.clip(col[None, :] - col[:, None], -(NA_KC - 1), NA_KC - 1) + (NA_KC - 1)
    shift = jnp.arange(NA_KR)[:, None]
    row_off = jnp.arange(NA_KR)[None, :] - shift + (NA_KR - 1)
    tab = rpb[:, row_off[:, None, :, None], col_off[None, :, None, :]]
    tab = jnp.where(col_ok[None, None, :, None, :], tab, NEG)
    return tab.transpose(1, 0, 2, 3, 4).reshape(NA_KR, NA_HEADS, GRID_W, NA_KR * GRID_W).astype(F32)


def _gelu(x):
    return 0.5 * x * (1.0 + jnp.tanh(np.sqrt(2.0 / np.pi).astype(np.float32) * (x + 0.044715 * (x * x * x))))


def _sgu_kernel(u_ref, v_ref, g_ref, b_ref, ws_ref, bs_ref, y_ref):
    n_chunk = u_ref.shape[1] // SGU_CHUNK
    v = _gelu(v_ref[0])
    mu = jnp.mean(v, axis=-1, keepdims=True)
    var = jnp.mean(jnp.square(v - mu), axis=-1, keepdims=True)
    vn = ((v - mu) * lax.rsqrt(var + EPS) * g_ref[...] + b_ref[...]).astype(BF16)
    cols = []
    for g in range(4):
        gs = slice(g * 128, (g + 1) * 128)
        rhs = jnp.concatenate([vn[c * SGU_CHUNK:(c + 1) * SGU_CHUNK, gs] for c in range(n_chunk)], axis=1)
        mixed = _dot(ws_ref[g], rhs)
        cols.append(jnp.concatenate([mixed[:, c * 128:(c + 1) * 128] + bs_ref[g] for c in range(n_chunk)], axis=0))
    y_ref[0] = (_gelu(u_ref[0]) * jnp.concatenate(cols, axis=1)).astype(BF16)


def sgu(p3, g, b, ws, bs):
    B, n = p3.shape[0], p3.shape[1]
    tr = min(n, 512)
    blk = lambda col: pl.BlockSpec((1, tr, 512), lambda bb, i: (bb, i, col))
    const = lambda shape: pl.BlockSpec(shape, lambda bb, i: (0,) * len(shape))
    return pl.pallas_call(
        _sgu_kernel, grid=(B, n // tr),
        in_specs=[blk(COL["sgu_u"] // 512), blk(COL["sgu_v"] // 512), const((1, 512)), const((1, 512)),
                  const((4, 128, 128)), const((4, 128, 128))],
        out_specs=blk(0), out_shape=jax.ShapeDtypeStruct((B, n, 512), BF16),
        compiler_params=_cp("parallel", "parallel"), name="sgu")(p3, p3, g, b, ws, bs)


def _merge_kernel(h_ref, a_ref, b_ref, n_ref, d_ref, wg_ref, bg_ref, wb_ref, o_ref):
    h = h_ref[...]
    acc = None
    for i, br in enumerate((a_ref, b_ref, n_ref, d_ref)):
        term = _sigmoid(_dot(h, wg_ref[i]) + bg_ref[i]) * _dot(br[...], wb_ref[i])
        acc = term if acc is None else acc + term
    o_ref[...] = acc.astype(BF16)


def merge(h, branches, wg, bg, wb, tm):
    M, tn = h.shape[0], 256
    tm = min(tm, M)
    br = pl.BlockSpec((tm, 512), lambda i, j: (i, 0))
    return pl.pallas_call(
        _merge_kernel, grid=(M // tm, D // tn),
        in_specs=[pl.BlockSpec((tm, D), lambda i, j: (i, 0)), br, br, br, br,
                  pl.BlockSpec((4, D, tn), lambda i, j: (0, 0, j)),
                  pl.BlockSpec((4, 1, tn), lambda i, j: (0, 0, j)),
                  pl.BlockSpec((4, 512, tn), lambda i, j: (0, 0, j))],
        out_specs=pl.BlockSpec((tm, tn), lambda i, j: (i, j)),
        out_shape=jax.ShapeDtypeStruct((M, D), BF16),
        compiler_params=_cp("parallel", "arbitrary"), name="merge")(h, *branches, wg, bg, wb)


def _outproj_kernel(acc_ref, w_ref, x_ref, gate_ref, g_ref, shift_ref, scale_ref, wr_ref,
                    x1_ref, h2_ref, lg_ref):
    x1 = x_ref[...] + gate_ref[0] * _dot(acc_ref[...], w_ref[...])
    x1_ref[...] = x1
    h2 = (_rms(x1, g_ref[...]) * (1.0 + scale_ref[0]) + shift_ref[0]).astype(BF16)
    h2_ref[...] = h2
    lg_ref[...] = _dot(h2, wr_ref[...])


def outproj(acc, w, x2, mod3, g, wr, tm, mod_row):
    M = acc.shape[0]
    tm = min(tm, M)
    row = lambda k: pl.BlockSpec((1, 1, D), lambda i: (mod_row(i) * 6 + k, 0, 0))
    tile = pl.BlockSpec((tm, D), lambda i: (i, 0))
    return pl.pallas_call(
        _outproj_kernel, grid=(M // tm,),
        in_specs=[tile, pl.BlockSpec((D, D), lambda i: (0, 0)), tile, row(2),
                  pl.BlockSpec((1, D), lambda i: (0, 0)), row(3), row(4),
                  pl.BlockSpec((D, 128), lambda i: (0, 0))],
        out_specs=[tile, tile, pl.BlockSpec((tm, 128), lambda i: (i, 0))],
        out_shape=[jax.ShapeDtypeStruct((M, D), F32), jax.ShapeDtypeStruct((M, D), BF16),
                   jax.ShapeDtypeStruct((M, 128), F32)],
        compiler_params=_cp("parallel"), name="outproj")(acc, w, x2, mod3, g, mod3, mod3, wr)


def _select_kernel(lg_ref, sel_ref, aff_ref, slot_ref, *, cap):
    n = lg_ref.shape[1]
    nt = n // LANES
    lane = lax.broadcasted_iota(jnp.int32, (n, LANES), 1)
    lg = jnp.where(lane < N_EXPERTS, lg_ref[0], NEG)
    e = jnp.exp(lg - lg.max(axis=1, keepdims=True))
    aff = e / e.sum(axis=1, keepdims=True)
    aff_ref[0] = aff
    bits = pltpu.bitcast(aff.T[:N_EXPERTS], jnp.int32)

    def count(mask):
        return jnp.sum(jnp.where(mask, 1.0, 0.0), axis=1, keepdims=True)

    def bit_step(i, thr):
        cand = thr | lax.shift_left(jnp.int32(1), 30 - i)
        return jnp.where(count(bits >= cand) >= cap, cand, thr)

    thr = lax.fori_loop(0, 31, bit_step, jnp.zeros((N_EXPERTS, 1), jnp.int32))
    gt, eq = bits > thr, bits == thr

    r = lax.broadcasted_iota(jnp.int32, (LANES, LANES), 0)
    c = lax.broadcasted_iota(jnp.int32, (LANES, LANES), 1)
    upper = (r <= c).astype(BF16)
    r2 = lax.broadcasted_iota(jnp.int32, (nt * N_EXPERTS, nt * N_EXPERTS), 0)
    c2 = lax.broadcasted_iota(jnp.int32, (nt * N_EXPERTS, nt * N_EXPERTS), 1)
    before = ((r2 % N_EXPERTS == c2 % N_EXPERTS) & (c2 // N_EXPERTS < r2 // N_EXPERTS)).astype(BF16)

    def excl_prefix(x):
        xs = jnp.concatenate([x[:, i * LANES:(i + 1) * LANES] for i in range(nt)], axis=0).astype(BF16)
        inc = _dot(xs, upper)
        tot = jnp.broadcast_to(inc[:, LANES - 1:LANES], inc.shape).astype(BF16)
        full = inc + _dot(before, tot)
        return jnp.concatenate([full[i * N_EXPERTS:(i + 1) * N_EXPERTS] for i in range(nt)], axis=1) - x

    eq01 = jnp.where(eq, 1.0, 0.0)
    mask = gt | (eq & (excl_prefix(eq01) < cap - count(gt)))
    slot = jnp.where(mask, excl_prefix(jnp.where(mask, 1.0, 0.0)), -1.0)
    sel_ref[0] = slot.astype(jnp.int32)
    pad = jnp.full((LANES - N_EXPERTS, n), -1.0, F32)
    slot_ref[0] = jnp.concatenate([slot, pad], axis=0).T


def select(lg3, cap):
    B, n = lg3.shape[0], lg3.shape[1]
    tok = pl.BlockSpec((1, n, 128), lambda b: (b, 0, 0))
    return pl.pallas_call(
        functools.partial(_select_kernel, cap=cap), grid=(B,),
        in_specs=[tok], out_specs=[pl.BlockSpec((1, N_EXPERTS, n), lambda b: (b, 0, 0)), tok, tok],
        out_shape=[jax.ShapeDtypeStruct((B, N_EXPERTS, n), jnp.int32),
                   jax.ShapeDtypeStruct((B, n, 128), F32), jax.ShapeDtypeStruct((B, n, 128), F32)],
        compiler_params=_cp("parallel"), name="select")(lg3)


def _dispatch_kernel(sel_ref, h_ref, o_ref, *, cap):
    n = h_ref.shape[1]
    j = lax.broadcasted_iota(jnp.int32, (cap, n), 0)
    onehot = jnp.where(j == sel_ref[0], 1.0, 0.0).astype(BF16)
    o_ref[0] = _dot(onehot, h_ref[0]).astype(BF16)


def dispatch(sel, h3, cap):
    B, n = h3.shape[0], h3.shape[1]
    sel3 = sel.reshape(B * N_EXPERTS, 1, n)
    return pl.pallas_call(
        functools.partial(_dispatch_kernel, cap=cap), grid=(B, N_EXPERTS),
        in_specs=[pl.BlockSpec((1, 1, n), lambda b, e: (b * N_EXPERTS + e, 0, 0)),
                  pl.BlockSpec((1, n, D), lambda b, e: (b, 0, 0))],
        out_specs=pl.BlockSpec((1, cap, D), lambda b, e: (e, b, 0)),
        out_shape=jax.ShapeDtypeStruct((N_EXPERTS, B * cap, D), BF16),
        compiler_params=_cp("parallel", "arbitrary"), name="dispatch")(sel3, h3)


def _expert_kernel(x_ref, wg_ref, wu_ref, wd_ref, o_ref, acc_ref):
    f = pl.program_id(2)

    @pl.when(f == 0)
    def _():
        acc_ref[...] = jnp.zeros_like(acc_ref)

    x = x_ref[0]
    g = _dot(x, wg_ref[0].astype(BF16))
    u = _dot(x, wu_ref[0].astype(BF16))
    hid = ((g * _sigmoid(g)) * u).astype(BF16)
    acc_ref[...] += _dot(hid, wd_ref[0].astype(BF16))

    @pl.when(f == pl.num_programs(2) - 1)
    def _():
        o_ref[0] = acc_ref[...].astype(BF16)


def experts(xe, wg, wu, wd, tm):
    E, M, tf = xe.shape[0], xe.shape[1], 256
    F = wg.shape[2]
    tm = min(tm, M)
    return pl.pallas_call(
        _expert_kernel, grid=(E, M // tm, F // tf),
        in_specs=[pl.BlockSpec((1, tm, D), lambda e, i, f: (e, i, 0)),
                  pl.BlockSpec((1, D, tf), lambda e, i, f: (e, 0, f)),
                  pl.BlockSpec((1, D, tf), lambda e, i, f: (e, 0, f)),
                  pl.BlockSpec((1, tf, D), lambda e, i, f: (e, f, 0))],
        out_specs=pl.BlockSpec((1, tm, D), lambda e, i, f: (e, i, 0)),
        out_shape=jax.ShapeDtypeStruct((E, M, D), BF16),
        scratch_shapes=[pltpu.VMEM((tm, D), F32)],
        compiler_params=_cp("parallel", "parallel", "arbitrary"), name="experts")(xe, wg, wu, wd)


def _combine_kernel(slot_ref, aff_ref, ye_ref, x_ref, gate_ref, gfin_ref, o_ref, acc_ref, *, cap, final):
    e = pl.program_id(2)
    tt = x_ref.shape[1]

    @pl.when(e == 0)
    def _():
        acc_ref[...] = jnp.zeros_like(acc_ref)

    lane = lax.broadcasted_iota(jnp.int32, (tt, LANES), 1)
    slot = jnp.sum(jnp.where(lane == e, slot_ref[0], 0.0), axis=1, keepdims=True)
    gate = jnp.sum(jnp.where(lane == e, aff_ref[0], 0.0), axis=1, keepdims=True)
    j = lax.broadcasted_iota(jnp.int32, (tt, cap), 1).astype(F32)
    onehot = jnp.where(j == slot, 1.0, 0.0).astype(BF16)
    acc_ref[...] += gate * _dot(onehot, ye_ref[0])

    @pl.when(e == pl.num_programs(2) - 1)
    def _():
        x2 = x_ref[0] + gate_ref[0] * acc_ref[...]
        if final:
            x2 = _rms(x2, gfin_ref[...])
        o_ref[0] = x2


def combine(slot_tok, aff_tok, ye, x3, mod3, gfin, cap, mod_row, final):
    B, n = x3.shape[0], x3.shape[1]
    tt = min(n, 512)
    tok = pl.BlockSpec((1, tt, 128), lambda b, t, e: (b, t, 0))
    xt = pl.BlockSpec((1, tt, D), lambda b, t, e: (b, t, 0))
    return pl.pallas_call(
        functools.partial(_combine_kernel, cap=cap, final=final), grid=(B, n // tt, N_EXPERTS),
        in_specs=[tok, tok, pl.BlockSpec((1, cap, D), lambda b, t, e: (e, b, 0)), xt,
                  pl.BlockSpec((1, 1, D), lambda b, t, e: (mod_row(b) * 6 + 5, 0, 0)),
                  pl.BlockSpec((1, D), lambda b, t, e: (0, 0))],
        out_specs=xt, out_shape=jax.ShapeDtypeStruct((B, n, D), F32),
        scratch_shapes=[pltpu.VMEM((tt, D), F32)],
        compiler_params=_cp("parallel", "parallel", "arbitrary"), name="combine",
    )(slot_tok, aff_tok, ye, x3, mod3, gfin)


def _swa_q_perm():
    return np.concatenate([np.arange(HEAD) + (p + 4 * hh) * HEAD for p in range(4) for hh in range(2)])


def _prep_w_in(w):
    seg = lambda name, width: w[:, REF[name]:REF[name] + width]
    swa_q = seg("swa_q", 512)[:, _swa_q_perm()]
    parts = [seg("gla_v", 512), seg("gla_r", 512), swa_q, seg("na_q", 512), seg("na_k", 512), seg("na_v", 512),
             seg("sgu_u", 512), seg("sgu_v", 512), seg("gla_q", 256), seg("gla_k", 256), seg("swa_k", 128),
             seg("swa_v", 128), seg("gla_af", 16), seg("gla_ab", 16)]
    out = jnp.concatenate(parts, axis=1)
    return jnp.pad(out, ((0, 0), (0, NP - out.shape[1]))).astype(BF16)


def _rope_tables():
    quarter = HEAD // 4
    freqs = 10000.0 ** (-jnp.arange(quarter, dtype=F32) / quarter)
    t = jnp.arange(T)
    row = (t // GRID_W).astype(F32)
    col = (t % GRID_W).astype(F32)
    ang = jnp.concatenate([row[:, None] * freqs, col[:, None] * freqs], axis=-1)
    cos, sin = jnp.cos(ang), jnp.sin(ang)
    cos_t = jnp.tile(cos, (1, 4))
    sin_t = jnp.tile(jnp.concatenate([-sin, sin], axis=-1), (1, 2))
    return cos_t, sin_t


def _moe(h2, lg, x3, mod3, gfin, wg, wu, wd, cap, tm, mod_row, final):
    B, n = x3.shape[0], x3.shape[1]
    sel, aff_tok, slot_tok = select(lg.reshape(B, n, 128), cap)
    xe = dispatch(sel, h2.reshape(B, n, D), cap)
    ye = experts(xe, wg, wu, wd, tm)
    return combine(slot_tok, aff_tok, ye, x3, mod3, gfin, cap, mod_row, final)


def kernel(x, c, ctx, c_ctx, w_ada, b_ada, norm1_g, norm2_g, w_in, gla_w_a2, gla_b_a2, gla_norm_g, swa_sink,
           na_rpb, sgu_ln_g, sgu_ln_b, sgu_w_s, sgu_b_s, w_gate, b_gate, w_branch, w_out, w_router,
           w_exp_gate, w_exp_up, w_exp_down, final_norm_g):
    B = x.shape[0]
    ctx_row = B
    n_mod = -(-(B + 1) // 8) * 8
    cvec = jnp.zeros((n_mod, D), F32).at[:B].set(c).at[ctx_row].set(c_ctx)
    cos_t, sin_t = _rope_tables()
    perm = _swa_q_perm()
    gfin = final_norm_g.reshape(1, D)
    tm = 1024
    lat_tile_row = lambda i: i // (T // tm)
    ctx_tile_row = lambda i: ctx_row
    x2 = x.reshape(B * T, D)
    xc2 = ctx.reshape(B * L, D)
    for l in range(DEPTH):
        need_ctx = l < DEPTH - 1
        mod3 = ada(cvec, w_ada[l], b_ada[l].reshape(1, 6 * D)).reshape(n_mod * 6, 1, D)
        w_in_p = _prep_w_in(w_in[l])
        g1 = norm1_g[l].reshape(1, D)
        h, p = inproj(x2, mod3, g1, w_in_p, tm, lat_tile_row)
        hc, pc = inproj(xc2, mod3, g1, w_in_p, tm, ctx_tile_row)
        p3, pc3 = p.reshape(B, T, NP), pc.reshape(B, L, NP)

        w2 = jnp.zeros((128, 512), F32).at[:GLA_RANK, :256].set(gla_w_a2[l, 0])
        w2 = w2.at[GLA_RANK:2 * GLA_RANK, 256:].set(gla_w_a2[l, 1]).astype(BF16)
        b2 = gla_b_a2[l].reshape(1, 512)
        ocf, ocb, of, ob = gla_scan(p3, pc3, w2, b2)
        gn = gla_norm_g[l].reshape(1, GLA_DV)
        ya = gla_out(of, ob, p3, gn)
        yb = swa(p3, pc3, swa_sink[l], cos_t, sin_t)
        yn = na(p3, pc3, na_bias_table(na_rpb[l]))
        ln_g, ln_b = sgu_ln_g[l].reshape(1, 512), sgu_ln_b[l].reshape(1, 512)
        ws = sgu_w_s[l].astype(BF16)
        bs = jnp.broadcast_to(sgu_b_s[l][:, :, None], (4, SGU_CHUNK, 128))
        yd = sgu(p3, ln_g, ln_b, ws, bs)

        wg = w_gate[l].astype(BF16)
        bg = b_gate[l].reshape(4, 1, D)
        wb = w_branch[l].at[1].set(w_branch[l, 1][perm]).astype(BF16)
        wo = w_out[l].astype(BF16)
        wr = jnp.pad(w_router[l], ((0, 0), (0, 128 - N_EXPERTS))).astype(BF16)
        g2 = norm2_g[l].reshape(1, D)
        flat = lambda y: y.reshape(-1, 512)
        acc = merge(h, [flat(ya), flat(yb), flat(yn), flat(yd)], wg, bg, wb, tm)
        x1, h2, lg = outproj(acc, wo, x2, mod3, g2, wr, 512, lambda i: i // (T // 512))
        x2 = _moe(h2, lg, x1.reshape(B, T, D), mod3, gfin, w_exp_gate[l], w_exp_up[l], w_exp_down[l],
                  2 * T // N_EXPERTS, 1024, lambda b: b, final=not need_ctx).reshape(B * T, D)
        if need_ctx:
            yac = gla_out(ocf, ocb, pc3, gn)
            ybc = ctx_attn(pc3, swa_sink[l], COL["swa_q"], COL["swa_k"], COL["swa_v"], 1, True)
            ync = ctx_attn(pc3, swa_sink[l], COL["na_q"], COL["na_k"], COL["na_v"], 4, False)
            ydc = sgu(pc3, ln_g, ln_b, ws, bs)
            accc = merge(hc, [flat(yac), flat(ybc), flat(ync), flat(ydc)], wg, bg, wb, tm)
            xc1, hc2, lgc = outproj(accc, wo, xc2, mod3, g2, wr, 512, ctx_tile_row)
            xc2 = _moe(hc2, lgc, xc1.reshape(B, L, D), mod3, gfin, w_exp_gate[l], w_exp_up[l], w_exp_down[l],
                       2 * L // N_EXPERTS, 512, lambda b: ctx_row, final=False).reshape(B * L, D)
    return x2.reshape(B, T, D)
```

```python
import functools

import numpy as np
import jax
import jax.numpy as jnp
from jax import lax
from jax.experimental import pallas as pl
from jax.experimental.pallas import tpu as pltpu

F32 = jnp.float32
BF16 = jnp.bfloat16
HIGHEST = lax.Precision.HIGHEST

D = 2048
T = 2048
L = 256
GRID_W = 64
ROWS = T // GRID_W
EPS = 1e-6
DEPTH = 2
NEG = -1e30

GLA_HEADS, GLA_DK, GLA_DV, GLA_RANK, GLA_TAU, GLA_CHUNK = 4, 64, 128, 16, 16.0, 64
GLA_BLOCK = 256
SWA_BLOCK, SWA_WINDOW = 128, 128
NA_KR, NA_KC, NA_HEADS = 8, 16, 8
SGU_CHUNK = 128
N_EXPERTS = 16
LANES = 128
HEAD = 64

NP = 5120
COL = dict(gla_v=0, gla_r=512, swa_q=1024, na_q=1536, na_k=2048, na_v=2560, sgu_u=3072, sgu_v=3584,
           gla_q=4096, gla_k=4352, swa_k=4608, swa_v=4736, gla_a=4864)
REF = dict(gla_q=0, gla_k=256, gla_v=512, gla_r=1024, gla_af=1536, gla_ab=1552, swa_q=1568, swa_k=2080,
           swa_v=2208, na_q=2336, na_k=2848, na_v=3360, sgu_u=3872, sgu_v=4384)

VMEM_LIMIT = 56 * 1024 * 1024


def _cp(*sem, vmem=VMEM_LIMIT):
    return pltpu.CompilerParams(dimension_semantics=sem, vmem_limit_bytes=vmem)


def _sigmoid(x):
    return 1.0 / (1.0 + jnp.exp(-x))


def _rms(x, g):
    return x * lax.rsqrt(jnp.mean(x * x, axis=-1, keepdims=True) + EPS) * g


def _dot(a, b):
    return jnp.dot(a, b, preferred_element_type=F32)


def _dot_nt(a, b):
    return lax.dot_general(a, b, (((1,), (1,)), ((), ())), preferred_element_type=F32)


def _ada_kernel(c_ref, w_ref, b_ref, o_ref):
    c = c_ref[...]
    act = (c * _sigmoid(c)).astype(BF16)
    o_ref[...] = _dot(act, w_ref[0].astype(BF16)) + b_ref[0]


def ada(cvec, w, b, l):
    rows, tn = cvec.shape[0], 1024
    return pl.pallas_call(
        _ada_kernel, grid=(6 * D // tn,),
        in_specs=[pl.BlockSpec((rows, D), lambda j: (0, 0)),
                  pl.BlockSpec((1, D, tn), lambda j: (l, 0, j)),
                  pl.BlockSpec((1, 1, tn), lambda j: (l, 0, j))],
        out_specs=pl.BlockSpec((rows, tn), lambda j: (0, j)),
        out_shape=jax.ShapeDtypeStruct((rows, 6 * D), F32),
        compiler_params=_cp("parallel"), name="ada")(cvec, w, b)


def _inproj_kernel(x_ref, shift_ref, scale_ref, g_ref, w_ref, h_ref, p_ref, hs_ref):
    @pl.when(pl.program_id(1) == 0)
    def _():
        h = (_rms(x_ref[...], g_ref[...]) * (1.0 + scale_ref[0]) + shift_ref[0]).astype(BF16)
        hs_ref[...] = h
        h_ref[...] = h

    p_ref[...] = _dot(hs_ref[...], w_ref[...])


def inproj(x2, mod3, g, w, tm, mod_row):
    M, tn = x2.shape[0], 512
    tm = min(tm, M)
    return pl.pallas_call(
        _inproj_kernel, grid=(M // tm, NP // tn),
        in_specs=[pl.BlockSpec((tm, D), lambda i, j: (i, 0)),
                  pl.BlockSpec((1, 1, D), lambda i, j: (mod_row(i) * 6 + 0, 0, 0)),
                  pl.BlockSpec((1, 1, D), lambda i, j: (mod_row(i) * 6 + 1, 0, 0)),
                  pl.BlockSpec((1, D), lambda i, j: (0, 0)),
                  pl.BlockSpec((D, tn), lambda i, j: (0, j))],
        out_specs=[pl.BlockSpec((tm, D), lambda i, j: (i, 0)),
                   pl.BlockSpec((tm, tn), lambda i, j: (i, j))],
        out_shape=[jax.ShapeDtypeStruct((M, D), BF16), jax.ShapeDtypeStruct((M, NP), F32)],
        scratch_shapes=[pltpu.VMEM((tm, D), BF16)],
        compiler_params=_cp("parallel", "arbitrary"), name="inproj")(x2, mod3, mod3, g, w)


def _log_sigmoid(x):
    return jnp.minimum(x, 0.0) - jnp.log(1.0 + jnp.exp(-jnp.abs(x)))


def _split3(x):
    hi = x.astype(BF16)
    r1 = x - hi.astype(F32)
    mid = r1.astype(BF16)
    lo = (r1 - mid.astype(F32)).astype(BF16)
    return jnp.concatenate([hi, mid, lo], axis=1)


def _gla_block(q, k, v, a, w2, b2, s_ref, sums, cmask2, order, half, row_lo):
    n, ch = GLA_BLOCK, GLA_CHUNK
    la = _log_sigmoid(_dot(a.astype(BF16), w2) + b2) * (1.0 / GLA_TAU)
    cum = _dot(sums, _split3(la))
    cum = cum[:, :256] + cum[:, 256:512] + cum[:, 512:]
    bc, bl = cum[:n], cum[n:]
    q_in = (q * GLA_DK ** -0.5) * jnp.exp(bc)
    k_in = k * jnp.exp(-bc)
    kte = k * jnp.exp(bl - bc)
    heads = [None] * GLA_HEADS
    for pt in range(2):
        sl = slice(pt * LANES, (pt + 1) * LANES)
        qt = q_in[:, sl]
        qs = jnp.concatenate([jnp.where(half == 0, qt, 0.0), jnp.where(half == 1, qt, 0.0)],
                             axis=0).astype(BF16)
        att = jnp.where(cmask2, _dot_nt(qs, k_in[:, sl].astype(BF16)), 0.0).astype(BF16)
        v2 = v[:, 2 * pt * GLA_DV:(2 * pt + 2) * GLA_DV].astype(BF16)
        intra = [_dot(att[hh * n:(hh + 1) * n], v2[:, hh * GLA_DV:(hh + 1) * GLA_DV]) for hh in range(2)]
        kte_t = kte[:, sl].T.astype(BF16)
        la_t = la[:, sl].T
        st = s_ref[sl, :]
        inter = [None] * (n // ch)
        for c in order:
            cs = slice(c * ch, (c + 1) * ch)
            qc = jnp.concatenate([qs[cs], qs[n + c * ch:n + (c + 1) * ch]], axis=0)
            inter[c] = _dot(qc, st.astype(BF16))
            upd = _dot(kte_t[:, cs], v2[cs])
            dec = jnp.exp(jnp.sum(la_t[:, cs], axis=1, keepdims=True))
            st = st * dec + jnp.where(row_lo, upd[:, :GLA_DV], upd[:, GLA_DV:])
        s_ref[sl, :] = st
        for hh in range(2):
            rs = slice(hh * HEAD, (hh + 1) * HEAD)
            heads[2 * pt + hh] = intra[hh] + jnp.concatenate([inter[c][rs] for c in range(n // ch)], axis=0)
    return jnp.concatenate(heads, axis=1)


def _gla_kernel(qc_ref, kc_ref, vc_ref, ac_ref, qf_ref, kf_ref, vf_ref, af_ref,
                qb_ref, kb_ref, vb_ref, ab_ref, w2_ref, b2_ref,
                ocf_ref, ocb_ref, of_ref, ob_ref, sf_ref, sb_ref):
    s = pl.program_id(1)
    n, ch = GLA_BLOCK, GLA_CHUNK
    ri = lax.broadcasted_iota(jnp.int32, (n, n), 0)
    ci = lax.broadcasted_iota(jnp.int32, (n, n), 1)
    same = (ri // ch) == (ci // ch)
    low, up = same & (ri >= ci), same & (ri <= ci)
    ones = same.astype(BF16)
    sums_f = jnp.concatenate([low.astype(BF16), ones], axis=0)
    sums_b = jnp.concatenate([up.astype(BF16), ones], axis=0)
    cm_f = jnp.concatenate([low, low], axis=0)
    cm_b = jnp.concatenate([up, up], axis=0)
    half = lax.broadcasted_iota(jnp.int32, (n, LANES), 1) // HEAD
    row_lo = lax.broadcasted_iota(jnp.int32, (LANES, GLA_DV), 0) < HEAD
    w2f, w2b = w2_ref[:, :256], w2_ref[:, 256:]
    b2f, b2b = b2_ref[:, :256], b2_ref[:, 256:]
    chunks = tuple(range(n // ch))

    @pl.when(s == 0)
    def _():
        sf_ref[...] = jnp.zeros_like(sf_ref)
        sb_ref[...] = jnp.zeros_like(sb_ref)

    def run(fwd, bwd, o_f, o_b):
        q, k, v, a = fwd
        o_f[0] = _gla_block(q[0], k[0], v[0], a[0], w2f, b2f, sf_ref, sums_f, cm_f, chunks, half, row_lo)
        q, k, v, a = bwd
        o_b[0] = _gla_block(q[0], k[0], v[0], a[0], w2b, b2b, sb_ref, sums_b, cm_b, chunks[::-1], half, row_lo)

    ctx = (qc_ref, kc_ref, vc_ref, ac_ref)

    @pl.when(s == 0)
    def _():
        run(ctx, ctx, ocf_ref, ocb_ref)

    @pl.when(s > 0)
    def _():
        run((qf_ref, kf_ref, vf_ref, af_ref), (qb_ref, kb_ref, vb_ref, ab_ref), of_ref, ob_ref)


def gla_scan(p3, pc3, w2, b2):
    B = p3.shape[0]
    nb = T // GLA_BLOCK
    cq, ck, cv, ca = COL["gla_q"] // 256, COL["gla_k"] // 256, COL["gla_v"] // 512, COL["gla_a"] // 128
    fwd = lambda s: jnp.maximum(s - 1, 0)
    bwd = lambda s: nb - jnp.maximum(s, 1)

    def spec(width, col, row):
        return pl.BlockSpec((1, GLA_BLOCK, width), lambda b, s: (b, row(s), col))

    zero = lambda s: 0
    in_specs = ([spec(256, cq, zero), spec(256, ck, zero), spec(512, cv, zero), spec(128, ca, zero)]
                + [spec(256, cq, fwd), spec(256, ck, fwd), spec(512, cv, fwd), spec(128, ca, fwd)]
                + [spec(256, cq, bwd), spec(256, ck, bwd), spec(512, cv, bwd), spec(128, ca, bwd)]
                + [pl.BlockSpec((128, 512), lambda b, s: (0, 0)), pl.BlockSpec((1, 512), lambda b, s: (0, 0))])
    out_specs = [spec(512, 0, zero), spec(512, 0, zero), spec(512, 0, fwd), spec(512, 0, bwd)]
    out_shape = [jax.ShapeDtypeStruct((B, L, 512), F32)] * 2 + [jax.ShapeDtypeStruct((B, T, 512), F32)] * 2
    return pl.pallas_call(
        _gla_kernel, grid=(B, nb + 1), in_specs=in_specs, out_specs=out_specs, out_shape=out_shape,
        scratch_shapes=[pltpu.VMEM((256, 128), F32), pltpu.VMEM((256, 128), F32)],
        compiler_params=_cp("parallel", "arbitrary"), name="gla_scan",
    )(pc3, pc3, pc3, pc3, p3, p3, p3, p3, p3, p3, p3, p3, w2, b2)


def _gla_out_kernel(of_ref, ob_ref, r_ref, g_ref, y_ref):
    o = of_ref[0] + ob_ref[0]
    r = r_ref[0]
    outs = []
    for hd in range(GLA_HEADS):
        sl = slice(hd * GLA_DV, (hd + 1) * GLA_DV)
        rh = r[:, sl]
        outs.append(_rms(o[:, sl], g_ref[...]) * (rh * _sigmoid(rh)))
    y_ref[0] = jnp.concatenate(outs, axis=1).astype(BF16)


def gla_out(o_f, o_b, p3, g):
    B, n, tr = o_f.shape[0], o_f.shape[1], 256
    blk = lambda col: pl.BlockSpec((1, tr, 512), lambda b, i: (b, i, col))
    return pl.pallas_call(
        _gla_out_kernel, grid=(B, n // tr),
        in_specs=[blk(0), blk(0), blk(COL["gla_r"] // 512), pl.BlockSpec((1, 128), lambda b, i: (0, 0))],
        out_specs=blk(0), out_shape=jax.ShapeDtypeStruct((B, n, 512), BF16),
        compiler_params=_cp("parallel", "parallel"), name="gla_out")(o_f, o_b, p3, g)


def _stack_halves(qt, half):
    return jnp.concatenate([jnp.where(half == 0, qt, 0.0), jnp.where(half == 1, qt, 0.0)], axis=0).astype(BF16)


def _merge_halves(o, half):
    n = o.shape[0] // 2
    return jnp.where(half == 0, o[:n], o[n:])


def _softmax_parts(s_list, sink=None):
    m = s_list[0].max(axis=1, keepdims=True)
    for s in s_list[1:]:
        m = jnp.maximum(m, s.max(axis=1, keepdims=True))
    if sink is not None:
        m = jnp.maximum(m, sink)
    es = [jnp.exp(s - m) for s in s_list]
    den = es[0].sum(axis=1, keepdims=True)
    for e in es[1:]:
        den = den + e.sum(axis=1, keepdims=True)
    if sink is not None:
        den = den + jnp.exp(sink - m)
    return [e.astype(BF16) for e in es], 1.0 / den


def _rope(x, cos, sin, lane):
    partner = jnp.where((lane % HEAD) < HEAD // 2, pltpu.roll(x, LANES - HEAD // 2, 1),
                        pltpu.roll(x, HEAD // 2, 1))
    return x * cos + partner * sin


def _swa_kernel(sink_ref, q_ref, kp_ref, kc_ref, kn_ref, vp_ref, vc_ref, vn_ref, kx_ref, vx_ref,
                cos_ref, sin_ref, o_ref):
    n = pl.program_id(1)
    nb = pl.num_programs(1)
    blk = SWA_BLOCK
    lane = lax.broadcasted_iota(jnp.int32, (blk, LANES), 1)
    half = lane // HEAD

    def tab(ref, i):
        return ref[pl.ds(pl.multiple_of(i * blk, blk), blk), :]

    ip, inx = jnp.maximum(n - 1, 0), jnp.minimum(n + 1, nb - 1)
    kb = jnp.concatenate([_rope(kp_ref[0], tab(cos_ref, ip), tab(sin_ref, ip), lane),
                          _rope(kc_ref[0], tab(cos_ref, n), tab(sin_ref, n), lane),
                          _rope(kn_ref[0], tab(cos_ref, inx), tab(sin_ref, inx), lane)], axis=0).astype(BF16)
    vb = jnp.concatenate([vp_ref[0], vc_ref[0], vn_ref[0]], axis=0).astype(BF16)
    cos_q, sin_q = tab(cos_ref, n), tab(sin_ref, n)
    qs = jnp.concatenate([_stack_halves(_rope(q_ref[0, :, p * LANES:(p + 1) * LANES], cos_q, sin_q, lane)
                                        * HEAD ** -0.5, half) for p in range(4)], axis=0)
    rows = 8 * blk
    qi = lax.broadcasted_iota(jnp.int32, (rows, 3 * blk), 0) % blk
    kj = lax.broadcasted_iota(jnp.int32, (rows, 3 * blk), 1)
    kpos = (n - 1) * blk + kj
    valid = (jnp.abs(kj - blk - qi) <= SWA_WINDOW) & (kpos >= 0) & (kpos < nb * blk)
    grp = lax.broadcasted_iota(jnp.int32, (rows, 1), 0) // blk
    sink = jnp.zeros((rows, 1), F32)
    for g in range(8):
        sink = jnp.where(grp == g, sink_ref[g // 2 + 4 * (g % 2)], sink)
    s_loc = jnp.where(valid, _dot_nt(qs, kb), NEG)
    s_ctx = _dot_nt(qs, kx_ref[0].astype(BF16))
    (e_loc, e_ctx), inv = _softmax_parts([s_loc, s_ctx], sink=sink)
    o = (_dot(e_loc, vb) + _dot(e_ctx, vx_ref[0].astype(BF16))) * inv
    for p in range(4):
        o_ref[0, :, p * LANES:(p + 1) * LANES] = _merge_halves(o[2 * p * blk:2 * (p + 1) * blk], half).astype(BF16)


def swa(p3, pc3, sink, cos_t, sin_t):
    B = p3.shape[0]
    nb = T // SWA_BLOCK
    ck, cv = COL["swa_k"] // 128, COL["swa_v"] // 128
    prev = lambda n: jnp.maximum(n - 1, 0)
    cur = lambda n: n
    nxt = lambda n: jnp.minimum(n + 1, nb - 1)
    kv = lambda col, row: pl.BlockSpec((1, SWA_BLOCK, 128), lambda b, n: (b, row(n), col))
    full = lambda col: pl.BlockSpec((1, L, 128), lambda b, n: (b, 0, col))
    tab = pl.BlockSpec((T, 128), lambda b, n: (0, 0))
    return pl.pallas_call(
        _swa_kernel, grid=(B, nb),
        in_specs=[pl.BlockSpec(memory_space=pltpu.SMEM),
                  pl.BlockSpec((1, SWA_BLOCK, 512), lambda b, n: (b, n, COL["swa_q"] // 512)),
                  kv(ck, prev), kv(ck, cur), kv(ck, nxt), kv(cv, prev), kv(cv, cur), kv(cv, nxt),
                  full(ck), full(cv), tab, tab],
        out_specs=pl.BlockSpec((1, SWA_BLOCK, 512), lambda b, n: (b, n, 0)),
        out_shape=jax.ShapeDtypeStruct((B, T, 512), BF16),
        compiler_params=_cp("parallel", "parallel"), name="swa",
    )(sink, p3, p3, p3, p3, p3, p3, p3, pc3, pc3, cos_t, sin_t)


def _ctx_attn_kernel(sink_ref, q_ref, k_ref, v_ref, o_ref, *, use_sink, kv_tiles):
    n = q_ref.shape[1]
    half = lax.broadcasted_iota(jnp.int32, (n, LANES), 1) // HEAD
    for p in range(4):
        sl = slice(p * LANES, (p + 1) * LANES)
        ks = sl if kv_tiles == 4 else slice(0, LANES)
        qs = _stack_halves(q_ref[0, :, sl] * HEAD ** -0.5, half)
        sink = None
        if use_sink:
            top = lax.broadcasted_iota(jnp.int32, (2 * n, 1), 0) < n
            sink = jnp.where(top, sink_ref[p], sink_ref[p + 4])
        (e,), inv = _softmax_parts([_dot_nt(qs, k_ref[0, :, ks].astype(BF16))], sink=sink)
        o = _dot(e, v_ref[0, :, ks].astype(BF16)) * inv
        o_ref[0, :, sl] = _merge_halves(o, half).astype(BF16)


def ctx_attn(pc3, sink, qcol, kcol, vcol, kv_tiles, use_sink):
    B, w = pc3.shape[0], kv_tiles * 128
    return pl.pallas_call(
        functools.partial(_ctx_attn_kernel, use_sink=use_sink, kv_tiles=kv_tiles), grid=(B,),
        in_specs=[pl.BlockSpec(memory_space=pltpu.SMEM),
                  pl.BlockSpec((1, L, 512), lambda b: (b, 0, qcol // 512)),
                  pl.BlockSpec((1, L, w), lambda b: (b, 0, kcol // w)),
                  pl.BlockSpec((1, L, w), lambda b: (b, 0, vcol // w))],
        out_specs=pl.BlockSpec((1, L, 512), lambda b: (b, 0, 0)),
        out_shape=jax.ShapeDtypeStruct((B, L, 512), BF16),
        compiler_params=_cp("parallel"), name="ctx_attn")(sink, pc3, pc3, pc3)


def _na_kernel(q_ref, k_ref, v_ref, kx_ref, vx_ref, *rest):
    bias_refs, o_ref = rest[:-1], rest[-1]
    nr = len(bias_refs)
    i = pl.program_id(1)
    band = NA_KR * GRID_W
    half = lax.broadcasted_iota(jnp.int32, (GRID_W, LANES), 1) // HEAD
    starts = [pl.multiple_of(jnp.clip(nr * i + k - NA_KR // 2, 0, ROWS - NA_KR) * GRID_W, GRID_W)
              for k in range(nr)]
    s_loc, s_ctx = [], []
    for p in range(4):
        sl = slice(p * LANES, (p + 1) * LANES)
        qs = [_stack_halves(q_ref[0, k * GRID_W:(k + 1) * GRID_W, sl] * HEAD ** -0.5, half) for k in range(nr)]
        for k in range(nr):
            kb = k_ref[0, pl.ds(starts[k], band), sl].astype(BF16)
            s_loc.append(_dot_nt(qs[k], kb) + bias_refs[k][0, 2 * p * GRID_W:(2 * p + 2) * GRID_W, :])
        s_ctx.append(_dot_nt(jnp.concatenate(qs, axis=0), kx_ref[0, :, sl].astype(BF16)))
    (e_loc, e_ctx), inv = _softmax_parts([jnp.concatenate(s_loc, axis=0), jnp.concatenate(s_ctx, axis=0)])
    blk = 2 * GRID_W
    for p in range(4):
        sl = slice(p * LANES, (p + 1) * LANES)
        r0 = p * nr * blk
        o_ctx = _dot(e_ctx[r0:r0 + nr * blk], vx_ref[0, :, sl].astype(BF16))
        for k in range(nr):
            rs = slice(r0 + k * blk, r0 + (k + 1) * blk)
            vb = v_ref[0, pl.ds(starts[k], band), sl].astype(BF16)
            o = (_dot(e_loc[rs], vb) + o_ctx[k * blk:(k + 1) * blk]) * inv[rs]
            o_ref[0, k * GRID_W:(k + 1) * GRID_W, sl] = _merge_halves(o, half).astype(BF16)


def na(p3, pc3, bias, rows_per_step=2):
    B, nr = p3.shape[0], rows_per_step
    cq, ck, cv = COL["na_q"] // 512, COL["na_k"] // 512, COL["na_v"] // 512
    shift = lambda r: r - jnp.clip(r - NA_KR // 2, 0, ROWS - NA_KR)
    bias_spec = lambda k: pl.BlockSpec((1, NA_HEADS * GRID_W, NA_KR * GRID_W),
                                       lambda b, i: (shift(nr * i + k), 0, 0))
    return pl.pallas_call(
        _na_kernel, grid=(B, ROWS // nr),
        in_specs=[pl.BlockSpec((1, nr * GRID_W, 512), lambda b, i: (b, i, cq)),
                  pl.BlockSpec((1, T, 512), lambda b, i: (b, 0, ck)),
                  pl.BlockSpec((1, T, 512), lambda b, i: (b, 0, cv)),
                  pl.BlockSpec((1, L, 512), lambda b, i: (b, 0, ck)),
                  pl.BlockSpec((1, L, 512), lambda b, i: (b, 0, cv))] + [bias_spec(k) for k in range(nr)],
        out_specs=pl.BlockSpec((1, nr * GRID_W, 512), lambda b, i: (b, i, 0)),
        out_shape=jax.ShapeDtypeStruct((B, T, 512), BF16),
        compiler_params=_cp("parallel", "arbitrary"), name="na")(p3, p3, p3, pc3, pc3, *([bias] * nr))


def na_bias_table(rpb):
    col = np.arange(GRID_W)
    col_start = np.clip(col - NA_KC // 2, 0, GRID_W - NA_KC)
    col_ok = (col[None, :] >= col_start[:, None]) & (col[None, :] < col_start[:, None] + NA_KC)
    col_off = np.clip(col[None, :] - col[:, None], -(NA_KC - 1), NA_KC - 1) + (NA_KC - 1)
    onehot = (col_off[:, :, None] == np.arange(2 * NA_KC - 1)).astype(np.float32)
    toep = jnp.einsum("hro,qko->hrqk", rpb, jnp.asarray(onehot), precision=HIGHEST)
    toep = jnp.where(jnp.asarray(col_ok)[None, None], toep, NEG)
    per_shift = [toep[:, NA_KR - 1 - s:2 * NA_KR - 1 - s] for s in range(NA_KR)]
    tab = jnp.stack(per_shift).transpose(0, 1, 3, 2, 4)
    return tab.reshape(NA_KR, NA_HEADS * GRID_W, NA_KR * GRID_W).astype(F32)


def _gelu(x):
    return 0.5 * x * (1.0 + jnp.tanh(np.sqrt(2.0 / np.pi).astype(np.float32) * (x + 0.044715 * (x * x * x))))


def _sgu_kernel(u_ref, v_ref, g_ref, b_ref, ws_ref, bs_ref, y_ref):
    n_chunk = u_ref.shape[1] // SGU_CHUNK
    v = _gelu(v_ref[0])
    mu = jnp.mean(v, axis=-1, keepdims=True)
    var = jnp.mean(jnp.square(v - mu), axis=-1, keepdims=True)
    vn = ((v - mu) * lax.rsqrt(var + EPS) * g_ref[...] + b_ref[...]).astype(BF16)
    cols = []
    for g in range(4):
        gs = slice(g * 128, (g + 1) * 128)
        rhs = jnp.concatenate([vn[c * SGU_CHUNK:(c + 1) * SGU_CHUNK, gs] for c in range(n_chunk)], axis=1)
        mixed = _dot(ws_ref[g], rhs)
        cols.append(jnp.concatenate([mixed[:, c * 128:(c + 1) * 128] + bs_ref[g] for c in range(n_chunk)], axis=0))
    y_ref[0] = (_gelu(u_ref[0]) * jnp.concatenate(cols, axis=1)).astype(BF16)


def sgu(p3, g, b, ws, bs):
    B, n = p3.shape[0], p3.shape[1]
    tr = min(n, 512)
    blk = lambda col: pl.BlockSpec((1, tr, 512), lambda bb, i: (bb, i, col))
    const = lambda shape: pl.BlockSpec(shape, lambda bb, i: (0,) * len(shape))
    return pl.pallas_call(
        _sgu_kernel, grid=(B, n // tr),
        in_specs=[blk(COL["sgu_u"] // 512), blk(COL["sgu_v"] // 512), const((1, 512)), const((1, 512)),
                  const((4, 128, 128)), const((4, 128, 128))],
        out_specs=blk(0), out_shape=jax.ShapeDtypeStruct((B, n, 512), BF16),
        compiler_params=_cp("parallel", "parallel"), name="sgu")(p3, p3, g, b, ws, bs)


def _merge_kernel(h_ref, a_ref, b_ref, n_ref, d_ref, wg_ref, bg_ref, wb_ref, o_ref):
    h = h_ref[...]
    acc = None
    for i, br in enumerate((a_ref, b_ref, n_ref, d_ref)):
        term = _sigmoid(_dot(h, wg_ref[i]) + bg_ref[i]) * _dot(br[...], wb_ref[i])
        acc = term if acc is None else acc + term
    o_ref[...] = acc.astype(BF16)


def merge(h, branches, wg, bg, wb, tm):
    M, tn = h.shape[0], 256
    tm = min(tm, M)
    br = pl.BlockSpec((tm, 512), lambda i, j: (i, 0))
    return pl.pallas_call(
        _merge_kernel, grid=(M // tm, D // tn),
        in_specs=[pl.BlockSpec((tm, D), lambda i, j: (i, 0)), br, br, br, br,
                  pl.BlockSpec((4, D, tn), lambda i, j: (0, 0, j)),
                  pl.BlockSpec((4, 1, tn), lambda i, j: (0, 0, j)),
                  pl.BlockSpec((4, 512, tn), lambda i, j: (0, 0, j))],
        out_specs=pl.BlockSpec((tm, tn), lambda i, j: (i, j)),
        out_shape=jax.ShapeDtypeStruct((M, D), BF16),
        compiler_params=_cp("parallel", "arbitrary"), name="merge")(h, *branches, wg, bg, wb)


def _outproj_kernel(acc_ref, w_ref, x_ref, gate_ref, g_ref, shift_ref, scale_ref, wr_ref,
                    x1_ref, h2_ref, lg_ref):
    x1 = x_ref[...] + gate_ref[0] * _dot(acc_ref[...], w_ref[...])
    x1_ref[...] = x1
    h2 = (_rms(x1, g_ref[...]) * (1.0 + scale_ref[0]) + shift_ref[0]).astype(BF16)
    h2_ref[...] = h2
    lg_ref[...] = _dot(h2, wr_ref[...])


def outproj(acc, w, x2, mod3, g, wr, tm, mod_row):
    M = acc.shape[0]
    tm = min(tm, M)
    row = lambda k: pl.BlockSpec((1, 1, D), lambda i: (mod_row(i) * 6 + k, 0, 0))
    tile = pl.BlockSpec((tm, D), lambda i: (i, 0))
    return pl.pallas_call(
        _outproj_kernel, grid=(M // tm,),
        in_specs=[tile, pl.BlockSpec((D, D), lambda i: (0, 0)), tile, row(2),
                  pl.BlockSpec((1, D), lambda i: (0, 0)), row(3), row(4),
                  pl.BlockSpec((D, 128), lambda i: (0, 0))],
        out_specs=[tile, tile, pl.BlockSpec((tm, 128), lambda i: (i, 0))],
        out_shape=[jax.ShapeDtypeStruct((M, D), F32), jax.ShapeDtypeStruct((M, D), BF16),
                   jax.ShapeDtypeStruct((M, 128), F32)],
        compiler_params=_cp("parallel"), name="outproj")(acc, w, x2, mod3, g, mod3, mod3, wr)


def _select_kernel(lg_ref, sel_ref, aff_ref, slot_ref, *, cap):
    n = lg_ref.shape[1]
    nt = n // LANES
    lane = lax.broadcasted_iota(jnp.int32, (n, LANES), 1)
    lg = jnp.where(lane < N_EXPERTS, lg_ref[0], NEG)
    e = jnp.exp(lg - lg.max(axis=1, keepdims=True))
    aff = e / e.sum(axis=1, keepdims=True)
    aff_ref[0] = aff
    bits = pltpu.bitcast(aff.T[:N_EXPERTS], jnp.int32)

    def count(mask):
        return jnp.sum(jnp.where(mask, 1.0, 0.0), axis=1, keepdims=True)

    def bit_step(i, thr):
        cand = thr | lax.shift_left(jnp.int32(1), 30 - i)
        return jnp.where(count(bits >= cand) >= cap, cand, thr)

    thr = lax.fori_loop(0, 31, bit_step, jnp.zeros((N_EXPERTS, 1), jnp.int32))
    gt, eq = bits > thr, bits == thr

    r = lax.broadcasted_iota(jnp.int32, (LANES, LANES), 0)
    c = lax.broadcasted_iota(jnp.int32, (LANES, LANES), 1)
    upper = (r <= c).astype(BF16)
    r2 = lax.broadcasted_iota(jnp.int32, (nt * N_EXPERTS, nt * N_EXPERTS), 0)
    c2 = lax.broadcasted_iota(jnp.int32, (nt * N_EXPERTS, nt * N_EXPERTS), 1)
    before = ((r2 % N_EXPERTS == c2 % N_EXPERTS) & (c2 // N_EXPERTS < r2 // N_EXPERTS)).astype(BF16)

    def excl_prefix(x):
        xs = jnp.concatenate([x[:, i * LANES:(i + 1) * LANES] for i in range(nt)], axis=0).astype(BF16)
        inc = _dot(xs, upper)
        tot = jnp.broadcast_to(inc[:, LANES - 1:LANES], inc.shape).astype(BF16)
        full = inc + _dot(before, tot)
        return jnp.concatenate([full[i * N_EXPERTS:(i + 1) * N_EXPERTS] for i in range(nt)], axis=1) - x

    eq01 = jnp.where(eq, 1.0, 0.0)
    mask = gt | (eq & (excl_prefix(eq01) < cap - count(gt)))
    slot = jnp.where(mask, excl_prefix(jnp.where(mask, 1.0, 0.0)), -1.0)
    sel_ref[0] = slot.astype(jnp.int32)
    pad = jnp.full((LANES - N_EXPERTS, n), -1.0, F32)
    slot_ref[0] = jnp.concatenate([slot, pad], axis=0).T


def select(lg3, cap):
    B, n = lg3.shape[0], lg3.shape[1]
    tok = pl.BlockSpec((1, n, 128), lambda b: (b, 0, 0))
    return pl.pallas_call(
        functools.partial(_select_kernel, cap=cap), grid=(B,),
        in_specs=[tok], out_specs=[pl.BlockSpec((1, N_EXPERTS, n), lambda b: (b, 0, 0)), tok, tok],
        out_shape=[jax.ShapeDtypeStruct((B, N_EXPERTS, n), jnp.int32),
                   jax.ShapeDtypeStruct((B, n, 128), F32), jax.ShapeDtypeStruct((B, n, 128), F32)],
        compiler_params=_cp("parallel"), name="select")(lg3)


def _dispatch_kernel(sel_ref, h_ref, o_ref, *, cap):
    n = h_ref.shape[1]
    j = lax.broadcasted_iota(jnp.int32, (cap, n), 0)
    onehot = jnp.where(j == sel_ref[0], 1.0, 0.0).astype(BF16)
    o_ref[0] = _dot(onehot, h_ref[0]).astype(BF16)


def dispatch(sel, h3, cap):
    B, n = h3.shape[0], h3.shape[1]
    sel3 = sel.reshape(B * N_EXPERTS, 1, n)
    return pl.pallas_call(
        functools.partial(_dispatch_kernel, cap=cap), grid=(B, N_EXPERTS),
        in_specs=[pl.BlockSpec((1, 1, n), lambda b, e: (b * N_EXPERTS + e, 0, 0)),
                  pl.BlockSpec((1, n, D), lambda b, e: (b, 0, 0))],
        out_specs=pl.BlockSpec((1, cap, D), lambda b, e: (e, b, 0)),
        out_shape=jax.ShapeDtypeStruct((N_EXPERTS, B * cap, D), BF16),
        compiler_params=_cp("parallel", "arbitrary"), name="dispatch")(sel3, h3)


def _expert_kernel(x_ref, wg_ref, wu_ref, wd_ref, o_ref, acc_ref):
    f = pl.program_id(2)

    @pl.when(f == 0)
    def _():
        acc_ref[...] = jnp.zeros_like(acc_ref)

    x = x_ref[0]
    g = _dot(x, wg_ref[0, 0].astype(BF16))
    u = _dot(x, wu_ref[0, 0].astype(BF16))
    hid = ((g * _sigmoid(g)) * u).astype(BF16)
    acc_ref[...] += _dot(hid, wd_ref[0, 0].astype(BF16))

    @pl.when(f == pl.num_programs(2) - 1)
    def _():
        o_ref[0] = acc_ref[...].astype(BF16)


def experts(xe, wg, wu, wd, l, tm):
    E, M, tf = xe.shape[0], xe.shape[1], 256
    F = wg.shape[3]
    tm = min(tm, M)
    return pl.pallas_call(
        _expert_kernel, grid=(E, M // tm, F // tf),
        in_specs=[pl.BlockSpec((1, tm, D), lambda e, i, f: (e, i, 0)),
                  pl.BlockSpec((1, 1, D, tf), lambda e, i, f: (l, e, 0, f)),
                  pl.BlockSpec((1, 1, D, tf), lambda e, i, f: (l, e, 0, f)),
                  pl.BlockSpec((1, 1, tf, D), lambda e, i, f: (l, e, f, 0))],
        out_specs=pl.BlockSpec((1, tm, D), lambda e, i, f: (e, i, 0)),
        out_shape=jax.ShapeDtypeStruct((E, M, D), BF16),
        scratch_shapes=[pltpu.VMEM((tm, D), F32)],
        compiler_params=_cp("parallel", "parallel", "arbitrary"), name="experts")(xe, wg, wu, wd)


def _combine_kernel(slot_ref, aff_ref, ye_ref, x_ref, gate_ref, gfin_ref, o_ref, acc_ref, *, cap, final):
    e = pl.program_id(2)
    tt = x_ref.shape[1]

    @pl.when(e == 0)
    def _():
        acc_ref[...] = jnp.zeros_like(acc_ref)

    lane = lax.broadcasted_iota(jnp.int32, (tt, LANES), 1)
    slot = jnp.sum(jnp.where(lane == e, slot_ref[0], 0.0), axis=1, keepdims=True)
    gate = jnp.sum(jnp.where(lane == e, aff_ref[0], 0.0), axis=1, keepdims=True)
    j = lax.broadcasted_iota(jnp.int32, (tt, cap), 1).astype(F32)
    onehot = jnp.where(j == slot, 1.0, 0.0).astype(BF16)
    acc_ref[...] += gate * _dot(onehot, ye_ref[0])

    @pl.when(e == pl.num_programs(2) - 1)
    def _():
        x2 = x_ref[0] + gate_ref[0] * acc_ref[...]
        if final:
            x2 = _rms(x2, gfin_ref[...])
        o_ref[0] = x2


def combine(slot_tok, aff_tok, ye, x3, mod3, gfin, cap, mod_row, final):
    B, n = x3.shape[0], x3.shape[1]
    tt = min(n, 512)
    tok = pl.BlockSpec((1, tt, 128), lambda b, t, e: (b, t, 0))
    xt = pl.BlockSpec((1, tt, D), lambda b, t, e: (b, t, 0))
    return pl.pallas_call(
        functools.partial(_combine_kernel, cap=cap, final=final), grid=(B, n // tt, N_EXPERTS),
        in_specs=[tok, tok, pl.BlockSpec((1, cap, D), lambda b, t, e: (e, b, 0)), xt,
                  pl.BlockSpec((1, 1, D), lambda b, t, e: (mod_row(b) * 6 + 5, 0, 0)),
                  pl.BlockSpec((1, D), lambda b, t, e: (0, 0))],
        out_specs=xt, out_shape=jax.ShapeDtypeStruct((B, n, D), F32),
        scratch_shapes=[pltpu.VMEM((tt, D), F32)],
        compiler_params=_cp("parallel", "parallel", "arbitrary"), name="combine",
    )(slot_tok, aff_tok, ye, x3, mod3, gfin)


def _swa_q_perm():
    return np.concatenate([np.arange(HEAD) + (p + 4 * hh) * HEAD for p in range(4) for hh in range(2)])


def _prep_w_in(w):
    seg = lambda name, width: w[:, REF[name]:REF[name] + width]
    swa_q = seg("swa_q", 512)[:, _swa_q_perm()]
    parts = [seg("gla_v", 512), seg("gla_r", 512), swa_q, seg("na_q", 512), seg("na_k", 512), seg("na_v", 512),
             seg("sgu_u", 512), seg("sgu_v", 512), seg("gla_q", 256), seg("gla_k", 256), seg("swa_k", 128),
             seg("swa_v", 128), seg("gla_af", 16), seg("gla_ab", 16)]
    out = jnp.concatenate(parts, axis=1)
    return jnp.pad(out, ((0, 0), (0, NP - out.shape[1]))).astype(BF16)


def _rope_tables():
    quarter = HEAD // 4
    freqs = 10000.0 ** (-jnp.arange(quarter, dtype=F32) / quarter)
    t = jnp.arange(T)
    row = (t // GRID_W).astype(F32)
    col = (t % GRID_W).astype(F32)
    ang = jnp.concatenate([row[:, None] * freqs, col[:, None] * freqs], axis=-1)
    cos, sin = jnp.cos(ang), jnp.sin(ang)
    cos_t = jnp.tile(cos, (1, 4))
    sin_t = jnp.tile(jnp.concatenate([-sin, sin], axis=-1), (1, 2))
    return cos_t, sin_t


def _moe(h2, lg, x3, mod3, gfin, wg, wu, wd, l, cap, tm, mod_row, final):
    B, n = x3.shape[0], x3.shape[1]
    sel, aff_tok, slot_tok = select(lg.reshape(B, n, 128), cap)
    xe = dispatch(sel, h2.reshape(B, n, D), cap)
    ye = experts(xe, wg, wu, wd, l, tm)
    return combine(slot_tok, aff_tok, ye, x3, mod3, gfin, cap, mod_row, final)


def kernel(x, c, ctx, c_ctx, w_ada, b_ada, norm1_g, norm2_g, w_in, gla_w_a2, gla_b_a2, gla_norm_g, swa_sink,
           na_rpb, sgu_ln_g, sgu_ln_b, sgu_w_s, sgu_b_s, w_gate, b_gate, w_branch, w_out, w_router,
           w_exp_gate, w_exp_up, w_exp_down, final_norm_g):
    B = x.shape[0]
    ctx_row = B
    n_mod = -(-(B + 1) // 8) * 8
    cvec = jnp.zeros((n_mod, D), F32).at[:B].set(c).at[ctx_row].set(c_ctx)
    cos_t, sin_t = _rope_tables()
    perm = _swa_q_perm()
    gfin = final_norm_g.reshape(1, D)
    tm = 1024
    lat_tile_row = lambda i: i // (T // tm)
    ctx_tile_row = lambda i: ctx_row
    x2 = x.reshape(B * T, D)
    xc2 = ctx.reshape(B * L, D)
    for l in range(DEPTH):
        need_ctx = l < DEPTH - 1
        mod3 = ada(cvec, w_ada, b_ada.reshape(DEPTH, 1, 6 * D), l).reshape(n_mod * 6, 1, D)
        w_in_p = _prep_w_in(w_in[l])
        g1 = norm1_g[l].reshape(1, D)
        h, p = inproj(x2, mod3, g1, w_in_p, tm, lat_tile_row)
        hc, pc = inproj(xc2, mod3, g1, w_in_p, tm, ctx_tile_row)
        p3, pc3 = p.reshape(B, T, NP), pc.reshape(B, L, NP)

        w2 = jnp.zeros((128, 512), F32).at[:GLA_RANK, :256].set(gla_w_a2[l, 0])
        w2 = w2.at[GLA_RANK:2 * GLA_RANK, 256:].set(gla_w_a2[l, 1]).astype(BF16)
        b2 = gla_b_a2[l].reshape(1, 512)
        ocf, ocb, of, ob = gla_scan(p3, pc3, w2, b2)
        gn = gla_norm_g[l].reshape(1, GLA_DV)
        ya = gla_out(of, ob, p3, gn)
        yb = swa(p3, pc3, swa_sink[l], cos_t, sin_t)
        yn = na(p3, pc3, na_bias_table(na_rpb[l]))
        ln_g, ln_b = sgu_ln_g[l].reshape(1, 512), sgu_ln_b[l].reshape(1, 512)
        ws = sgu_w_s[l].astype(BF16)
        bs = jnp.broadcast_to(sgu_b_s[l][:, :, None], (4, SGU_CHUNK, 128))
        yd = sgu(p3, ln_g, ln_b, ws, bs)

        wg = w_gate[l].astype(BF16)
        bg = b_gate[l].reshape(4, 1, D)
        wb = w_branch[l].at[1].set(w_branch[l, 1][perm]).astype(BF16)
        wo = w_out[l].astype(BF16)
        wr = jnp.pad(w_router[l], ((0, 0), (0, 128 - N_EXPERTS))).astype(BF16)
        g2 = norm2_g[l].reshape(1, D)
        flat = lambda y: y.reshape(-1, 512)
        acc = merge(h, [flat(ya), flat(yb), flat(yn), flat(yd)], wg, bg, wb, tm)
        x1, h2, lg = outproj(acc, wo, x2, mod3, g2, wr, 512, lambda i: i // (T // 512))
        x2 = _moe(h2, lg, x1.reshape(B, T, D), mod3, gfin, w_exp_gate, w_exp_up, w_exp_down, l,
                  2 * T // N_EXPERTS, 1024, lambda b: b, final=not need_ctx).reshape(B * T, D)
        if need_ctx:
            yac = gla_out(ocf, ocb, pc3, gn)
            ybc = ctx_attn(pc3, swa_sink[l], COL["swa_q"], COL["swa_k"], COL["swa_v"], 1, True)
            ync = ctx_attn(pc3, swa_sink[l], COL["na_q"], COL["na_k"], COL["na_v"], 4, False)
            ydc = sgu(pc3, ln_g, ln_b, ws, bs)
            accc = merge(hc, [flat(yac), flat(ybc), flat(ync), flat(ydc)], wg, bg, wb, tm)
            xc1, hc2, lgc = outproj(accc, wo, xc2, mod3, g2, wr, 512, ctx_tile_row)
            xc2 = _moe(hc2, lgc, xc1.reshape(B, L, D), mod3, gfin, w_exp_gate, w_exp_up, w_exp_down, l,
                       2 * L // N_EXPERTS, 512, lambda b: ctx_row, final=False).reshape(B * L, D)
    return x2.reshape(B, T, D)
```

```python
import functools

import numpy as np
import jax
import jax.numpy as jnp
from jax import lax
from jax.experimental import pallas as pl
from jax.experimental.pallas import tpu as pltpu

F32 = jnp.float32
BF16 = jnp.bfloat16
HIGHEST = lax.Precision.HIGHEST

D = 2048
T = 2048
L = 256
GRID_W = 64
ROWS = T // GRID_W
EPS = 1e-6
DEPTH = 2
NEG = -1e30

GLA_HEADS, GLA_DK, GLA_DV, GLA_RANK, GLA_TAU, GLA_CHUNK = 4, 64, 128, 16, 16.0, 64
GLA_BLOCK = 256
SWA_BLOCK, SWA_WINDOW = 128, 128
NA_KR, NA_KC, NA_HEADS = 8, 16, 8
SGU_CHUNK = 128
N_EXPERTS = 16
LANES = 128
HEAD = 64

NP = 5120
COL = dict(gla_v=0, gla_r=512, swa_q=1024, na_q=1536, na_k=2048, na_v=2560, sgu_u=3072, sgu_v=3584,
           gla_q=4096, gla_k=4352, swa_k=4608, swa_v=4736, gla_a=4864)
REF = dict(gla_q=0, gla_k=256, gla_v=512, gla_r=1024, gla_af=1536, gla_ab=1552, swa_q=1568, swa_k=2080,
           swa_v=2208, na_q=2336, na_k=2848, na_v=3360, sgu_u=3872, sgu_v=4384)

VMEM_LIMIT = 56 * 1024 * 1024


def _cp(*sem, vmem=VMEM_LIMIT):
    return pltpu.CompilerParams(dimension_semantics=sem, vmem_limit_bytes=vmem)


def _sigmoid(x):
    return 1.0 / (1.0 + jnp.exp(-x))


def _rms(x, g):
    return x * lax.rsqrt(jnp.mean(x * x, axis=-1, keepdims=True) + EPS) * g


def _dot(a, b):
    return jnp.dot(a, b, preferred_element_type=F32)


def _dot_nt(a, b):
    return lax.dot_general(a, b, (((1,), (1,)), ((), ())), preferred_element_type=F32)


def _ada_kernel(c_ref, w_ref, b_ref, o_ref):
    c = c_ref[...]
    act = (c * _sigmoid(c)).astype(BF16)
    o_ref[...] = _dot(act, w_ref[0].astype(BF16)) + b_ref[0]


def ada(cvec, w, b, l):
    rows, tn = cvec.shape[0], 1024
    return pl.pallas_call(
        _ada_kernel, grid=(6 * D // tn,),
        in_specs=[pl.BlockSpec((rows, D), lambda j: (0, 0)),
                  pl.BlockSpec((1, D, tn), lambda j: (l, 0, j)),
                  pl.BlockSpec((1, 1, tn), lambda j: (l, 0, j))],
        out_specs=pl.BlockSpec((rows, tn), lambda j: (0, j)),
        out_shape=jax.ShapeDtypeStruct((rows, 6 * D), F32),
        compiler_params=_cp("parallel"), name="ada")(cvec, w, b)


def _inproj_kernel(x_ref, shift_ref, scale_ref, g_ref, w_ref, h_ref, p_ref, hs_ref):
    @pl.when(pl.program_id(1) == 0)
    def _():
        h = (_rms(x_ref[...], g_ref[...]) * (1.0 + scale_ref[0]) + shift_ref[0]).astype(BF16)
        hs_ref[...] = h
        h_ref[...] = h

    p_ref[...] = _dot(hs_ref[...], w_ref[...])


def inproj(x2, mod3, g, w, tm, mod_row):
    M, tn = x2.shape[0], 512
    tm = min(tm, M)
    return pl.pallas_call(
        _inproj_kernel, grid=(M // tm, NP // tn),
        in_specs=[pl.BlockSpec((tm, D), lambda i, j: (i, 0)),
                  pl.BlockSpec((1, 1, D), lambda i, j: (mod_row(i) * 6 + 0, 0, 0)),
                  pl.BlockSpec((1, 1, D), lambda i, j: (mod_row(i) * 6 + 1, 0, 0)),
                  pl.BlockSpec((1, D), lambda i, j: (0, 0)),
                  pl.BlockSpec((D, tn), lambda i, j: (0, j))],
        out_specs=[pl.BlockSpec((tm, D), lambda i, j: (i, 0)),
                   pl.BlockSpec((tm, tn), lambda i, j: (i, j))],
        out_shape=[jax.ShapeDtypeStruct((M, D), BF16), jax.ShapeDtypeStruct((M, NP), F32)],
        scratch_shapes=[pltpu.VMEM((tm, D), BF16)],
        compiler_params=_cp("parallel", "arbitrary"), name="inproj")(x2, mod3, mod3, g, w)


def _log_sigmoid(x):
    return jnp.minimum(x, 0.0) - jnp.log(1.0 + jnp.exp(-jnp.abs(x)))


def _split3(x):
    hi = x.astype(BF16)
    r1 = x - hi.astype(F32)
    mid = r1.astype(BF16)
    lo = (r1 - mid.astype(F32)).astype(BF16)
    return jnp.concatenate([hi, mid, lo], axis=1)


def _gla_block(q, k, v, a, w2, b2, s_ref, sums, cmask2, order, half, row_lo):
    n, ch = GLA_BLOCK, GLA_CHUNK
    la = _log_sigmoid(_dot(a.astype(BF16), w2) + b2) * (1.0 / GLA_TAU)
    cum = _dot(sums, _split3(la))
    cum = cum[:, :256] + cum[:, 256:512] + cum[:, 512:]
    bc, bl = cum[:n], cum[n:]
    q_in = (q * GLA_DK ** -0.5) * jnp.exp(bc)
    k_in = k * jnp.exp(-bc)
    kte = k * jnp.exp(bl - bc)
    heads = [None] * GLA_HEADS
    for pt in range(2):
        sl = slice(pt * LANES, (pt + 1) * LANES)
        qt = q_in[:, sl]
        qs = jnp.concatenate([jnp.where(half == 0, qt, 0.0), jnp.where(half == 1, qt, 0.0)],
                             axis=0).astype(BF16)
        att = jnp.where(cmask2, _dot_nt(qs, k_in[:, sl].astype(BF16)), 0.0).astype(BF16)
        v2 = v[:, 2 * pt * GLA_DV:(2 * pt + 2) * GLA_DV].astype(BF16)
        intra = [_dot(att[hh * n:(hh + 1) * n], v2[:, hh * GLA_DV:(hh + 1) * GLA_DV]) for hh in range(2)]
        kte_t = kte[:, sl].T.astype(BF16)
        la_t = la[:, sl].T
        st = s_ref[sl, :]
        inter = [None] * (n // ch)
        for c in order:
            cs = slice(c * ch, (c + 1) * ch)
            qc = jnp.concatenate([qs[cs], qs[n + c * ch:n + (c + 1) * ch]], axis=0)
            inter[c] = _dot(qc, st.astype(BF16))
            upd = _dot(kte_t[:, cs], v2[cs])
            dec = jnp.exp(jnp.sum(la_t[:, cs], axis=1, keepdims=True))
            st = st * dec + jnp.where(row_lo, upd[:, :GLA_DV], upd[:, GLA_DV:])
        s_ref[sl, :] = st
        for hh in range(2):
            rs = slice(hh * HEAD, (hh + 1) * HEAD)
            heads[2 * pt + hh] = intra[hh] + jnp.concatenate([inter[c][rs] for c in range(n // ch)], axis=0)
    return jnp.concatenate(heads, axis=1)


def _gla_kernel(qc_ref, kc_ref, vc_ref, ac_ref, qf_ref, kf_ref, vf_ref, af_ref,
                qb_ref, kb_ref, vb_ref, ab_ref, w2_ref, b2_ref,
                ocf_ref, ocb_ref, of_ref, ob_ref, sf_ref, sb_ref):
    s = pl.program_id(1)
    n, ch = GLA_BLOCK, GLA_CHUNK
    ri = lax.broadcasted_iota(jnp.int32, (n, n), 0)
    ci = lax.broadcasted_iota(jnp.int32, (n, n), 1)
    same = (ri // ch) == (ci // ch)
    low, up = same & (ri >= ci), same & (ri <= ci)
    ones = same.astype(BF16)
    sums_f = jnp.concatenate([low.astype(BF16), ones], axis=0)
    sums_b = jnp.concatenate([up.astype(BF16), ones], axis=0)
    cm_f = jnp.concatenate([low, low], axis=0)
    cm_b = jnp.concatenate([up, up], axis=0)
    half = lax.broadcasted_iota(jnp.int32, (n, LANES), 1) // HEAD
    row_lo = lax.broadcasted_iota(jnp.int32, (LANES, GLA_DV), 0) < HEAD
    w2f, w2b = w2_ref[:, :256], w2_ref[:, 256:]
    b2f, b2b = b2_ref[:, :256], b2_ref[:, 256:]
    chunks = tuple(range(n // ch))

    @pl.when(s == 0)
    def _():
        sf_ref[...] = jnp.zeros_like(sf_ref)
        sb_ref[...] = jnp.zeros_like(sb_ref)

    def run(fwd, bwd, o_f, o_b):
        q, k, v, a = fwd
        o_f[0] = _gla_block(q[0], k[0], v[0], a[0], w2f, b2f, sf_ref, sums_f, cm_f, chunks, half, row_lo)
        q, k, v, a = bwd
        o_b[0] = _gla_block(q[0], k[0], v[0], a[0], w2b, b2b, sb_ref, sums_b, cm_b, chunks[::-1], half, row_lo)

    ctx = (qc_ref, kc_ref, vc_ref, ac_ref)

    @pl.when(s == 0)
    def _():
        run(ctx, ctx, ocf_ref, ocb_ref)

    @pl.when(s > 0)
    def _():
        run((qf_ref, kf_ref, vf_ref, af_ref), (qb_ref, kb_ref, vb_ref, ab_ref), of_ref, ob_ref)


def gla_scan(p3, pc3, w2, b2):
    B = p3.shape[0]
    nb = T // GLA_BLOCK
    cq, ck, cv, ca = COL["gla_q"] // 256, COL["gla_k"] // 256, COL["gla_v"] // 512, COL["gla_a"] // 128
    fwd = lambda s: jnp.maximum(s - 1, 0)
    bwd = lambda s: nb - jnp.maximum(s, 1)

    def spec(width, col, row):
        return pl.BlockSpec((1, GLA_BLOCK, width), lambda b, s: (b, row(s), col))

    zero = lambda s: 0
    in_specs = ([spec(256, cq, zero), spec(256, ck, zero), spec(512, cv, zero), spec(128, ca, zero)]
                + [spec(256, cq, fwd), spec(256, ck, fwd), spec(512, cv, fwd), spec(128, ca, fwd)]
                + [spec(256, cq, bwd), spec(256, ck, bwd), spec(512, cv, bwd), spec(128, ca, bwd)]
                + [pl.BlockSpec((128, 512), lambda b, s: (0, 0)), pl.BlockSpec((1, 512), lambda b, s: (0, 0))])
    out_specs = [spec(512, 0, zero), spec(512, 0, zero), spec(512, 0, fwd), spec(512, 0, bwd)]
    out_shape = [jax.ShapeDtypeStruct((B, L, 512), F32)] * 2 + [jax.ShapeDtypeStruct((B, T, 512), F32)] * 2
    return pl.pallas_call(
        _gla_kernel, grid=(B, nb + 1), in_specs=in_specs, out_specs=out_specs, out_shape=out_shape,
        scratch_shapes=[pltpu.VMEM((256, 128), F32), pltpu.VMEM((256, 128), F32)],
        compiler_params=_cp("parallel", "arbitrary"), name="gla_scan",
    )(pc3, pc3, pc3, pc3, p3, p3, p3, p3, p3, p3, p3, p3, w2, b2)


def _gla_out_kernel(of_ref, ob_ref, r_ref, g_ref, y_ref):
    o = of_ref[0] + ob_ref[0]
    r = r_ref[0]
    outs = []
    for hd in range(GLA_HEADS):
        sl = slice(hd * GLA_DV, (hd + 1) * GLA_DV)
        rh = r[:, sl]
        outs.append(_rms(o[:, sl], g_ref[...]) * (rh * _sigmoid(rh)))
    y_ref[0] = jnp.concatenate(outs, axis=1).astype(BF16)


def gla_out(o_f, o_b, p3, g):
    B, n, tr = o_f.shape[0], o_f.shape[1], 256
    blk = lambda col: pl.BlockSpec((1, tr, 512), lambda b, i: (b, i, col))
    return pl.pallas_call(
        _gla_out_kernel, grid=(B, n // tr),
        in_specs=[blk(0), blk(0), blk(COL["gla_r"] // 512), pl.BlockSpec((1, 128), lambda b, i: (0, 0))],
        out_specs=blk(0), out_shape=jax.ShapeDtypeStruct((B, n, 512), BF16),
        compiler_params=_cp("parallel", "parallel"), name="gla_out")(o_f, o_b, p3, g)


def _stack_halves(qt, half):
    return jnp.concatenate([jnp.where(half == 0, qt, 0.0), jnp.where(half == 1, qt, 0.0)], axis=0).astype(BF16)


def _merge_halves(o, half):
    n = o.shape[0] // 2
    return jnp.where(half == 0, o[:n], o[n:])


def _softmax_parts(s_list, sink=None):
    m = s_list[0].max(axis=1, keepdims=True)
    for s in s_list[1:]:
        m = jnp.maximum(m, s.max(axis=1, keepdims=True))
    if sink is not None:
        m = jnp.maximum(m, sink)
    es = [jnp.exp(s - m) for s in s_list]
    den = es[0].sum(axis=1, keepdims=True)
    for e in es[1:]:
        den = den + e.sum(axis=1, keepdims=True)
    if sink is not None:
        den = den + jnp.exp(sink - m)
    return [e.astype(BF16) for e in es], 1.0 / den


def _rope(x, cos, sin, lane):
    partner = jnp.where((lane % HEAD) < HEAD // 2, pltpu.roll(x, LANES - HEAD // 2, 1),
                        pltpu.roll(x, HEAD // 2, 1))
    return x * cos + partner * sin


def _swa_kernel(sink_ref, q_ref, kp_ref, kc_ref, kn_ref, vp_ref, vc_ref, vn_ref, kx_ref, vx_ref,
                cos_ref, sin_ref, o_ref):
    n = pl.program_id(1)
    nb = pl.num_programs(1)
    blk = SWA_BLOCK
    lane = lax.broadcasted_iota(jnp.int32, (blk, LANES), 1)
    half = lane // HEAD

    def tab(ref, i):
        return ref[pl.ds(pl.multiple_of(i * blk, blk), blk), :]

    ip, inx = jnp.maximum(n - 1, 0), jnp.minimum(n + 1, nb - 1)
    kb = jnp.concatenate([_rope(kp_ref[0], tab(cos_ref, ip), tab(sin_ref, ip), lane),
                          _rope(kc_ref[0], tab(cos_ref, n), tab(sin_ref, n), lane),
                          _rope(kn_ref[0], tab(cos_ref, inx), tab(sin_ref, inx), lane)], axis=0).astype(BF16)
    vb = jnp.concatenate([vp_ref[0], vc_ref[0], vn_ref[0]], axis=0).astype(BF16)
    cos_q, sin_q = tab(cos_ref, n), tab(sin_ref, n)
    qs = jnp.concatenate([_stack_halves(_rope(q_ref[0, :, p * LANES:(p + 1) * LANES], cos_q, sin_q, lane)
                                        * HEAD ** -0.5, half) for p in range(4)], axis=0)
    rows = 8 * blk
    qi = lax.broadcasted_iota(jnp.int32, (rows, 3 * blk), 0) % blk
    kj = lax.broadcasted_iota(jnp.int32, (rows, 3 * blk), 1)
    kpos = (n - 1) * blk + kj
    valid = (jnp.abs(kj - blk - qi) <= SWA_WINDOW) & (kpos >= 0) & (kpos < nb * blk)
    grp = lax.broadcasted_iota(jnp.int32, (rows, 1), 0) // blk
    sink = jnp.zeros((rows, 1), F32)
    for g in range(8):
        sink = jnp.where(grp == g, sink_ref[g // 2 + 4 * (g % 2)], sink)
    s_loc = jnp.where(valid, _dot_nt(qs, kb), NEG)
    s_ctx = _dot_nt(qs, kx_ref[0].astype(BF16))
    (e_loc, e_ctx), inv = _softmax_parts([s_loc, s_ctx], sink=sink)
    o = (_dot(e_loc, vb) + _dot(e_ctx, vx_ref[0].astype(BF16))) * inv
    for p in range(4):
        o_ref[0, :, p * LANES:(p + 1) * LANES] = _merge_halves(o[2 * p * blk:2 * (p + 1) * blk], half).astype(BF16)


def swa(p3, pc3, sink, cos_t, sin_t):
    B = p3.shape[0]
    nb = T // SWA_BLOCK
    ck, cv = COL["swa_k"] // 128, COL["swa_v"] // 128
    prev = lambda n: jnp.maximum(n - 1, 0)
    cur = lambda n: n
    nxt = lambda n: jnp.minimum(n + 1, nb - 1)
    kv = lambda col, row: pl.BlockSpec((1, SWA_BLOCK, 128), lambda b, n: (b, row(n), col))
    full = lambda col: pl.BlockSpec((1, L, 128), lambda b, n: (b, 0, col))
    tab = pl.BlockSpec((T, 128), lambda b, n: (0, 0))
    return pl.pallas_call(
        _swa_kernel, grid=(B, nb),
        in_specs=[pl.BlockSpec(memory_space=pltpu.SMEM),
                  pl.BlockSpec((1, SWA_BLOCK, 512), lambda b, n: (b, n, COL["swa_q"] // 512)),
                  kv(ck, prev), kv(ck, cur), kv(ck, nxt), kv(cv, prev), kv(cv, cur), kv(cv, nxt),
                  full(ck), full(cv), tab, tab],
        out_specs=pl.BlockSpec((1, SWA_BLOCK, 512), lambda b, n: (b, n, 0)),
        out_shape=jax.ShapeDtypeStruct((B, T, 512), BF16),
        compiler_params=_cp("parallel", "parallel"), name="swa",
    )(sink, p3, p3, p3, p3, p3, p3, p3, pc3, pc3, cos_t, sin_t)


def _ctx_attn_kernel(sink_ref, q_ref, k_ref, v_ref, o_ref, *, use_sink, kv_tiles):
    n = q_ref.shape[1]
    half = lax.broadcasted_iota(jnp.int32, (n, LANES), 1) // HEAD
    for p in range(4):
        sl = slice(p * LANES, (p + 1) * LANES)
        ks = sl if kv_tiles == 4 else slice(0, LANES)
        qs = _stack_halves(q_ref[0, :, sl] * HEAD ** -0.5, half)
        sink = None
        if use_sink:
            top = lax.broadcasted_iota(jnp.int32, (2 * n, 1), 0) < n
            sink = jnp.where(top, sink_ref[p], sink_ref[p + 4])
        (e,), inv = _softmax_parts([_dot_nt(qs, k_ref[0, :, ks].astype(BF16))], sink=sink)
        o = _dot(e, v_ref[0, :, ks].astype(BF16)) * inv
        o_ref[0, :, sl] = _merge_halves(o, half).astype(BF16)


def ctx_attn(pc3, sink, qcol, kcol, vcol, kv_tiles, use_sink):
    B, w = pc3.shape[0], kv_tiles * 128
    return pl.pallas_call(
        functools.partial(_ctx_attn_kernel, use_sink=use_sink, kv_tiles=kv_tiles), grid=(B,),
        in_specs=[pl.BlockSpec(memory_space=pltpu.SMEM),
                  pl.BlockSpec((1, L, 512), lambda b: (b, 0, qcol // 512)),
                  pl.BlockSpec((1, L, w), lambda b: (b, 0, kcol // w)),
                  pl.BlockSpec((1, L, w), lambda b: (b, 0, vcol // w))],
        out_specs=pl.BlockSpec((1, L, 512), lambda b: (b, 0, 0)),
        out_shape=jax.ShapeDtypeStruct((B, L, 512), BF16),
        compiler_params=_cp("parallel"), name="ctx_attn")(sink, pc3, pc3, pc3)


def _na_kernel(q_ref, k_ref, v_ref, kx_ref, vx_ref, *rest):
    bias_refs, o_ref = rest[:-1], rest[-1]
    nr = len(bias_refs)
    i = pl.program_id(1)
    band = NA_KR * GRID_W
    half = lax.broadcasted_iota(jnp.int32, (GRID_W, LANES), 1) // HEAD
    starts = [pl.multiple_of(jnp.clip(nr * i + k - NA_KR // 2, 0, ROWS - NA_KR) * GRID_W, GRID_W)
              for k in range(nr)]
    s_loc, s_ctx = [], []
    for p in range(4):
        sl = slice(p * LANES, (p + 1) * LANES)
        qs = [_stack_halves(q_ref[0, k * GRID_W:(k + 1) * GRID_W, sl] * HEAD ** -0.5, half) for k in range(nr)]
        for k in range(nr):
            kb = k_ref[0, pl.ds(starts[k], band), sl].astype(BF16)
            s_loc.append(_dot_nt(qs[k], kb) + bias_refs[k][0, 2 * p * GRID_W:(2 * p + 2) * GRID_W, :])
        s_ctx.append(_dot_nt(jnp.concatenate(qs, axis=0), kx_ref[0, :, sl].astype(BF16)))
    (e_loc, e_ctx), inv = _softmax_parts([jnp.concatenate(s_loc, axis=0), jnp.concatenate(s_ctx, axis=0)])
    blk = 2 * GRID_W
    for p in range(4):
        sl = slice(p * LANES, (p + 1) * LANES)
        r0 = p * nr * blk
        o_ctx = _dot(e_ctx[r0:r0 + nr * blk], vx_ref[0, :, sl].astype(BF16))
        for k in range(nr):
            rs = slice(r0 + k * blk, r0 + (k + 1) * blk)
            vb = v_ref[0, pl.ds(starts[k], band), sl].astype(BF16)
            o = (_dot(e_loc[rs], vb) + o_ctx[k * blk:(k + 1) * blk]) * inv[rs]
            o_ref[0, k * GRID_W:(k + 1) * GRID_W, sl] = _merge_halves(o, half).astype(BF16)


def na(p3, pc3, bias, rows_per_step=2):
    B, nr = p3.shape[0], rows_per_step
    cq, ck, cv = COL["na_q"] // 512, COL["na_k"] // 512, COL["na_v"] // 512
    shift = lambda r: r - jnp.clip(r - NA_KR // 2, 0, ROWS - NA_KR)
    bias_spec = lambda k: pl.BlockSpec((1, NA_HEADS * GRID_W, NA_KR * GRID_W),
                                       lambda b, i: (shift(nr * i + k), 0, 0))
    return pl.pallas_call(
        _na_kernel, grid=(B, ROWS // nr),
        in_specs=[pl.BlockSpec((1, nr * GRID_W, 512), lambda b, i: (b, i, cq)),
                  pl.BlockSpec((1, T, 512), lambda b, i: (b, 0, ck)),
                  pl.BlockSpec((1, T, 512), lambda b, i: (b, 0, cv)),
                  pl.BlockSpec((1, L, 512), lambda b, i: (b, 0, ck)),
                  pl.BlockSpec((1, L, 512), lambda b, i: (b, 0, cv))] + [bias_spec(k) for k in range(nr)],
        out_specs=pl.BlockSpec((1, nr * GRID_W, 512), lambda b, i: (b, i, 0)),
        out_shape=jax.ShapeDtypeStruct((B, T, 512), BF16),
        compiler_params=_cp("parallel", "arbitrary"), name="na")(p3, p3, p3, pc3, pc3, *([bias] * nr))


def na_bias_table(rpb):
    col = np.arange(GRID_W)
    col_start = np.clip(col - NA_KC // 2, 0, GRID_W - NA_KC)
    col_ok = (col[None, :] >= col_start[:, None]) & (col[None, :] < col_start[:, None] + NA_KC)
    col_off = np.clip(col[None, :] - col[:, None], -(NA_KC - 1), NA_KC - 1) + (NA_KC - 1)
    onehot = (col_off[:, :, None] == np.arange(2 * NA_KC - 1)).astype(np.float32)
    toep = jnp.einsum("hro,qko->hrqk", rpb, jnp.asarray(onehot), precision=HIGHEST)
    toep = jnp.where(jnp.asarray(col_ok)[None, None], toep, NEG)
    per_shift = [toep[:, NA_KR - 1 - s:2 * NA_KR - 1 - s] for s in range(NA_KR)]
    tab = jnp.stack(per_shift).transpose(0, 1, 3, 2, 4)
    return tab.reshape(NA_KR, NA_HEADS * GRID_W, NA_KR * GRID_W).astype(F32)


def _gelu(x):
    return 0.5 * x * (1.0 + jnp.tanh(np.sqrt(2.0 / np.pi).astype(np.float32) * (x + 0.044715 * (x * x * x))))


def _sgu_kernel(u_ref, v_ref, g_ref, b_ref, ws_ref, bs_ref, y_ref):
    n_chunk = u_ref.shape[1] // SGU_CHUNK
    v = _gelu(v_ref[0])
    mu = jnp.mean(v, axis=-1, keepdims=True)
    var = jnp.mean(jnp.square(v - mu), axis=-1, keepdims=True)
    vn = ((v - mu) * lax.rsqrt(var + EPS) * g_ref[...] + b_ref[...]).astype(BF16)
    cols = []
    for g in range(4):
        gs = slice(g * 128, (g + 1) * 128)
        rhs = jnp.concatenate([vn[c * SGU_CHUNK:(c + 1) * SGU_CHUNK, gs] for c in range(n_chunk)], axis=1)
        mixed = _dot(ws_ref[g], rhs)
        cols.append(jnp.concatenate([mixed[:, c * 128:(c + 1) * 128] + bs_ref[g] for c in range(n_chunk)], axis=0))
    y_ref[0] = (_gelu(u_ref[0]) * jnp.concatenate(cols, axis=1)).astype(BF16)


def sgu(p3, g, b, ws, bs):
    B, n = p3.shape[0], p3.shape[1]
    tr = min(n, 512)
    blk = lambda col: pl.BlockSpec((1, tr, 512), lambda bb, i: (bb, i, col))
    const = lambda shape: pl.BlockSpec(shape, lambda bb, i: (0,) * len(shape))
    return pl.pallas_call(
        _sgu_kernel, grid=(B, n // tr),
        in_specs=[blk(COL["sgu_u"] // 512), blk(COL["sgu_v"] // 512), const((1, 512)), const((1, 512)),
                  const((4, 128, 128)), const((4, 128, 128))],
        out_specs=blk(0), out_shape=jax.ShapeDtypeStruct((B, n, 512), BF16),
        compiler_params=_cp("parallel", "parallel"), name="sgu")(p3, p3, g, b, ws, bs)


def _merge_kernel(h_ref, a_ref, b_ref, n_ref, d_ref, wg_ref, bg_ref, wb_ref, o_ref):
    h = h_ref[...]
    acc = None
    for i, br in enumerate((a_ref, b_ref, n_ref, d_ref)):
        term = _sigmoid(_dot(h, wg_ref[i]) + bg_ref[i]) * _dot(br[...], wb_ref[i])
        acc = term if acc is None else acc + term
    o_ref[...] = acc.astype(BF16)


def merge(h, branches, wg, bg, wb, tm):
    M, tn = h.shape[0], 256
    tm = min(tm, M)
    br = pl.BlockSpec((tm, 512), lambda i, j: (i, 0))
    return pl.pallas_call(
        _merge_kernel, grid=(M // tm, D // tn),
        in_specs=[pl.BlockSpec((tm, D), lambda i, j: (i, 0)), br, br, br, br,
                  pl.BlockSpec((4, D, tn), lambda i, j: (0, 0, j)),
                  pl.BlockSpec((4, 1, tn), lambda i, j: (0, 0, j)),
                  pl.BlockSpec((4, 512, tn), lambda i, j: (0, 0, j))],
        out_specs=pl.BlockSpec((tm, tn), lambda i, j: (i, j)),
        out_shape=jax.ShapeDtypeStruct((M, D), BF16),
        compiler_params=_cp("parallel", "arbitrary"), name="merge")(h, *branches, wg, bg, wb)


def _outproj_kernel(acc_ref, w_ref, x_ref, gate_ref, g_ref, shift_ref, scale_ref, wr_ref,
                    x1_ref, h2_ref, lg_ref):
    x1 = x_ref[...] + gate_ref[0] * _dot(acc_ref[...], w_ref[...])
    x1_ref[...] = x1
    h2 = (_rms(x1, g_ref[...]) * (1.0 + scale_ref[0]) + shift_ref[0]).astype(BF16)
    h2_ref[...] = h2
    lg_ref[...] = _dot(h2, wr_ref[...])


def outproj(acc, w, x2, mod3, g, wr, tm, mod_row):
    M = acc.shape[0]
    tm = min(tm, M)
    row = lambda k: pl.BlockSpec((1, 1, D), lambda i: (mod_row(i) * 6 + k, 0, 0))
    tile = pl.BlockSpec((tm, D), lambda i: (i, 0))
    return pl.pallas_call(
        _outproj_kernel, grid=(M // tm,),
        in_specs=[tile, pl.BlockSpec((D, D), lambda i: (0, 0)), tile, row(2),
                  pl.BlockSpec((1, D), lambda i: (0, 0)), row(3), row(4),
                  pl.BlockSpec((D, 128), lambda i: (0, 0))],
        out_specs=[tile, tile, pl.BlockSpec((tm, 128), lambda i: (i, 0))],
        out_shape=[jax.ShapeDtypeStruct((M, D), F32), jax.ShapeDtypeStruct((M, D), BF16),
                   jax.ShapeDtypeStruct((M, 128), F32)],
        compiler_params=_cp("parallel"), name="outproj")(acc, w, x2, mod3, g, mod3, mod3, wr)


def _select_kernel(lg_ref, sel_ref, aff_ref, slot_ref, *, cap):
    n = lg_ref.shape[1]
    nt = n // LANES
    lane = lax.broadcasted_iota(jnp.int32, (n, LANES), 1)
    lg = jnp.where(lane < N_EXPERTS, lg_ref[0], NEG)
    e = jnp.exp(lg - lg.max(axis=1, keepdims=True))
    aff = (e / e.sum(axis=1, keepdims=True)).T[:N_EXPERTS]
    aff_ref[0] = aff
    bits = pltpu.bitcast(aff, jnp.int32)

    def count(mask):
        return jnp.sum(jnp.where(mask, 1.0, 0.0), axis=1, keepdims=True)

    def bit_step(i, thr):
        cand = thr | lax.shift_left(jnp.int32(1), 30 - i)
        return jnp.where(count(bits >= cand) >= cap, cand, thr)

    thr = lax.fori_loop(0, 31, bit_step, jnp.zeros((N_EXPERTS, 1), jnp.int32))
    gt, eq = bits > thr, bits == thr

    r = lax.broadcasted_iota(jnp.int32, (LANES, LANES), 0)
    c = lax.broadcasted_iota(jnp.int32, (LANES, LANES), 1)
    upper = (r <= c).astype(BF16)
    r2 = lax.broadcasted_iota(jnp.int32, (nt * N_EXPERTS, nt * N_EXPERTS), 0)
    c2 = lax.broadcasted_iota(jnp.int32, (nt * N_EXPERTS, nt * N_EXPERTS), 1)
    before = ((r2 % N_EXPERTS == c2 % N_EXPERTS) & (c2 // N_EXPERTS < r2 // N_EXPERTS)).astype(BF16)

    def excl_prefix(x):
        xs = jnp.concatenate([x[:, i * LANES:(i + 1) * LANES] for i in range(nt)], axis=0).astype(BF16)
        inc = _dot(xs, upper)
        tot = jnp.broadcast_to(inc[:, LANES - 1:LANES], inc.shape).astype(BF16)
        full = inc + _dot(before, tot)
        return jnp.concatenate([full[i * N_EXPERTS:(i + 1) * N_EXPERTS] for i in range(nt)], axis=1) - x

    eq01 = jnp.where(eq, 1.0, 0.0)
    mask = gt | (eq & (excl_prefix(eq01) < cap - count(gt)))
    slot = jnp.where(mask, excl_prefix(jnp.where(mask, 1.0, 0.0)), -1.0)
    sel_ref[0] = slot.astype(jnp.int32)
    pad = jnp.full((LANES - N_EXPERTS, n), -1.0, F32)
    slot_ref[0] = jnp.concatenate([slot, pad], axis=0).T


def select(lg3, cap):
    B, n = lg3.shape[0], lg3.shape[1]
    tok = pl.BlockSpec((1, n, 128), lambda b: (b, 0, 0))
    row = pl.BlockSpec((1, N_EXPERTS, n), lambda b: (b, 0, 0))
    return pl.pallas_call(
        functools.partial(_select_kernel, cap=cap), grid=(B,),
        in_specs=[tok], out_specs=[row, row, tok],
        out_shape=[jax.ShapeDtypeStruct((B, N_EXPERTS, n), jnp.int32),
                   jax.ShapeDtypeStruct((B, N_EXPERTS, n), F32), jax.ShapeDtypeStruct((B, n, 128), F32)],
        compiler_params=_cp("parallel"), name="select")(lg3)


def _dispatch_kernel(sel_ref, aff_ref, h_ref, o_ref, g_ref, *, cap):
    n = h_ref.shape[1]
    hit = lax.broadcasted_iota(jnp.int32, (cap, n), 0) == sel_ref[0]
    o_ref[0] = _dot(jnp.where(hit, 1.0, 0.0).astype(BF16), h_ref[0]).astype(BF16)
    g_ref[0] = jnp.sum(jnp.where(hit, aff_ref[0], 0.0), axis=1, keepdims=True)


def dispatch(sel, aff, h3, cap):
    B, n = h3.shape[0], h3.shape[1]
    row = pl.BlockSpec((1, 1, n), lambda b, e: (b * N_EXPERTS + e, 0, 0))
    return pl.pallas_call(
        functools.partial(_dispatch_kernel, cap=cap), grid=(B, N_EXPERTS),
        in_specs=[row, row, pl.BlockSpec((1, n, D), lambda b, e: (b, 0, 0))],
        out_specs=[pl.BlockSpec((1, cap, D), lambda b, e: (e, b, 0)),
                   pl.BlockSpec((1, cap, 1), lambda b, e: (e, b, 0))],
        out_shape=[jax.ShapeDtypeStruct((N_EXPERTS, B * cap, D), BF16),
                   jax.ShapeDtypeStruct((N_EXPERTS, B * cap, 1), F32)],
        compiler_params=_cp("parallel", "arbitrary"), name="dispatch",
    )(sel.reshape(B * N_EXPERTS, 1, n), aff.reshape(B * N_EXPERTS, 1, n), h3)


def _expert_kernel(x_ref, gate_ref, wg_ref, wu_ref, wd_ref, o_ref, acc_ref):
    f = pl.program_id(2)

    @pl.when(f == 0)
    def _():
        acc_ref[...] = jnp.zeros_like(acc_ref)

    x = x_ref[0]
    g = _dot(x, wg_ref[0, 0].astype(BF16))
    u = _dot(x, wu_ref[0, 0].astype(BF16))
    hid = ((g * _sigmoid(g)) * u).astype(BF16)
    acc_ref[...] += _dot(hid, wd_ref[0, 0].astype(BF16))

    @pl.when(f == pl.num_programs(2) - 1)
    def _():
        o_ref[0] = (acc_ref[...] * gate_ref[0]).astype(BF16)


def experts(xe, gate, wg, wu, wd, l, tm):
    E, M, tf = xe.shape[0], xe.shape[1], 256
    F = wg.shape[3]
    tm = min(tm, M)
    return pl.pallas_call(
        _expert_kernel, grid=(E, M // tm, F // tf),
        in_specs=[pl.BlockSpec((1, tm, D), lambda e, i, f: (e, i, 0)),
                  pl.BlockSpec((1, tm, 1), lambda e, i, f: (e, i, 0)),
                  pl.BlockSpec((1, 1, D, tf), lambda e, i, f: (l, e, 0, f)),
                  pl.BlockSpec((1, 1, D, tf), lambda e, i, f: (l, e, 0, f)),
                  pl.BlockSpec((1, 1, tf, D), lambda e, i, f: (l, e, f, 0))],
        out_specs=pl.BlockSpec((1, tm, D), lambda e, i, f: (e, i, 0)),
        out_shape=jax.ShapeDtypeStruct((E, M, D), BF16),
        scratch_shapes=[pltpu.VMEM((tm, D), F32)],
        compiler_params=_cp("parallel", "parallel", "arbitrary"), name="experts")(xe, gate, wg, wu, wd)


COMBINE_GROUP = 8


def _combine_kernel(slot_ref, ye_ref, x_ref, gate_ref, gfin_ref, o_ref, acc_ref, *, cap, final):
    g = pl.program_id(2)
    tt = x_ref.shape[1]
    lane = lax.broadcasted_iota(jnp.int32, (tt, LANES), 1)
    j = lax.broadcasted_iota(jnp.int32, (tt, cap), 1).astype(F32)
    slots = slot_ref[0]
    parts = []
    for k in range(COMBINE_GROUP):
        slot = jnp.sum(jnp.where(lane == g * COMBINE_GROUP + k, slots, 0.0), axis=1, keepdims=True)
        parts.append(jnp.where(j == slot, 1.0, 0.0).astype(BF16))
    onehot = jnp.concatenate(parts, axis=1)
    contrib = _dot(onehot, ye_ref[:, 0].reshape(COMBINE_GROUP * cap, D))

    @pl.when(g == 0)
    def _():
        acc_ref[...] = contrib

    @pl.when(g == pl.num_programs(2) - 1)
    def _():
        x2 = x_ref[0] + gate_ref[0] * (acc_ref[...] + contrib)
        if final:
            x2 = _rms(x2, gfin_ref[...])
        o_ref[0] = x2


def combine(slot_tok, ye, x3, mod3, gfin, cap, mod_row, final):
    B, n = x3.shape[0], x3.shape[1]
    tt = min(n, 512)
    assert N_EXPERTS == 2 * COMBINE_GROUP
    xt = pl.BlockSpec((1, tt, D), lambda b, t, g: (b, t, 0))
    return pl.pallas_call(
        functools.partial(_combine_kernel, cap=cap, final=final), grid=(B, n // tt, N_EXPERTS // COMBINE_GROUP),
        in_specs=[pl.BlockSpec((1, tt, 128), lambda b, t, g: (b, t, 0)),
                  pl.BlockSpec((COMBINE_GROUP, 1, cap, D), lambda b, t, g: (g, b, 0, 0)), xt,
                  pl.BlockSpec((1, 1, D), lambda b, t, g: (mod_row(b) * 6 + 5, 0, 0)),
                  pl.BlockSpec((1, D), lambda b, t, g: (0, 0))],
        out_specs=xt, out_shape=jax.ShapeDtypeStruct((B, n, D), F32),
        scratch_shapes=[pltpu.VMEM((tt, D), F32)],
        compiler_params=_cp("parallel", "parallel", "arbitrary"), name="combine",
    )(slot_tok, ye.reshape(N_EXPERTS, B, cap, D), x3, mod3, gfin)


def _swa_q_perm():
    return np.concatenate([np.arange(HEAD) + (p + 4 * hh) * HEAD for p in range(4) for hh in range(2)])


def _prep_w_in(w):
    seg = lambda name, width: w[:, REF[name]:REF[name] + width]
    swa_q = seg("swa_q", 512)[:, _swa_q_perm()]
    parts = [seg("gla_v", 512), seg("gla_r", 512), swa_q, seg("na_q", 512), seg("na_k", 512), seg("na_v", 512),
             seg("sgu_u", 512), seg("sgu_v", 512), seg("gla_q", 256), seg("gla_k", 256), seg("swa_k", 128),
             seg("swa_v", 128), seg("gla_af", 16), seg("gla_ab", 16)]
    out = jnp.concatenate(parts, axis=1)
    return jnp.pad(out, ((0, 0), (0, NP - out.shape[1]))).astype(BF16)


def _rope_tables():
    quarter = HEAD // 4
    freqs = 10000.0 ** (-jnp.arange(quarter, dtype=F32) / quarter)
    t = jnp.arange(T)
    row = (t // GRID_W).astype(F32)
    col = (t % GRID_W).astype(F32)
    ang = jnp.concatenate([row[:, None] * freqs, col[:, None] * freqs], axis=-1)
    cos, sin = jnp.cos(ang), jnp.sin(ang)
    cos_t = jnp.tile(cos, (1, 4))
    sin_t = jnp.tile(jnp.concatenate([-sin, sin], axis=-1), (1, 2))
    return cos_t, sin_t


def _moe(h2, lg, x3, mod3, gfin, wg, wu, wd, l, cap, tm, mod_row, final):
    B, n = x3.shape[0], x3.shape[1]
    sel, aff, slot_tok = select(lg.reshape(B, n, 128), cap)
    xe, gate = dispatch(sel, aff, h2.reshape(B, n, D), cap)
    ye = experts(xe, gate, wg, wu, wd, l, tm)
    return combine(slot_tok, ye, x3, mod3, gfin, cap, mod_row, final)


def kernel(x, c, ctx, c_ctx, w_ada, b_ada, norm1_g, norm2_g, w_in, gla_w_a2, gla_b_a2, gla_norm_g, swa_sink,
           na_rpb, sgu_ln_g, sgu_ln_b, sgu_w_s, sgu_b_s, w_gate, b_gate, w_branch, w_out, w_router,
           w_exp_gate, w_exp_up, w_exp_down, final_norm_g):
    B = x.shape[0]
    ctx_row = B
    n_mod = -(-(B + 1) // 8) * 8
    cvec = jnp.zeros((n_mod, D), F32).at[:B].set(c).at[ctx_row].set(c_ctx)
    cos_t, sin_t = _rope_tables()
    perm = _swa_q_perm()
    gfin = final_norm_g.reshape(1, D)
    tm = 1024
    lat_tile_row = lambda i: i // (T // tm)
    ctx_tile_row = lambda i: ctx_row
    x2 = x.reshape(B * T, D)
    xc2 = ctx.reshape(B * L, D)
    for l in range(DEPTH):
        need_ctx = l < DEPTH - 1
        mod3 = ada(cvec, w_ada, b_ada.reshape(DEPTH, 1, 6 * D), l).reshape(n_mod * 6, 1, D)
        w_in_p = _prep_w_in(w_in[l])
        g1 = norm1_g[l].reshape(1, D)
        h, p = inproj(x2, mod3, g1, w_in_p, tm, lat_tile_row)
        hc, pc = inproj(xc2, mod3, g1, w_in_p, tm, ctx_tile_row)
        p3, pc3 = p.reshape(B, T, NP), pc.reshape(B, L, NP)

        w2 = jnp.zeros((128, 512), F32).at[:GLA_RANK, :256].set(gla_w_a2[l, 0])
        w2 = w2.at[GLA_RANK:2 * GLA_RANK, 256:].set(gla_w_a2[l, 1]).astype(BF16)
        b2 = gla_b_a2[l].reshape(1, 512)
        ocf, ocb, of, ob = gla_scan(p3, pc3, w2, b2)
        gn = gla_norm_g[l].reshape(1, GLA_DV)
        ya = gla_out(of, ob, p3, gn)
        yb = swa(p3, pc3, swa_sink[l], cos_t, sin_t)
        yn = na(p3, pc3, na_bias_table(na_rpb[l]))
        ln_g, ln_b = sgu_ln_g[l].reshape(1, 512), sgu_ln_b[l].reshape(1, 512)
        ws = sgu_w_s[l].astype(BF16)
        bs = jnp.broadcast_to(sgu_b_s[l][:, :, None], (4, SGU_CHUNK, 128))
        yd = sgu(p3, ln_g, ln_b, ws, bs)

        wg = w_gate[l].astype(BF16)
        bg = b_gate[l].reshape(4, 1, D)
        wb = w_branch[l].at[1].set(w_branch[l, 1][perm]).astype(BF16)
        wo = w_out[l].astype(BF16)
        wr = jnp.pad(w_router[l], ((0, 0), (0, 128 - N_EXPERTS))).astype(BF16)
        g2 = norm2_g[l].reshape(1, D)
        flat = lambda y: y.reshape(-1, 512)
        acc = merge(h, [flat(ya), flat(yb), flat(yn), flat(yd)], wg, bg, wb, tm)
        x1, h2, lg = outproj(acc, wo, x2, mod3, g2, wr, 512, lambda i: i // (T // 512))
        x2 = _moe(h2, lg, x1.reshape(B, T, D), mod3, gfin, w_exp_gate, w_exp_up, w_exp_down, l,
                  2 * T // N_EXPERTS, 1024, lambda b: b, final=not need_ctx).reshape(B * T, D)
        if need_ctx:
            yac = gla_out(ocf, ocb, pc3, gn)
            ybc = ctx_attn(pc3, swa_sink[l], COL["swa_q"], COL["swa_k"], COL["swa_v"], 1, True)
            ync = ctx_attn(pc3, swa_sink[l], COL["na_q"], COL["na_k"], COL["na_v"], 4, False)
            ydc = sgu(pc3, ln_g, ln_b, ws, bs)
            accc = merge(hc, [flat(yac), flat(ybc), flat(ync), flat(ydc)], wg, bg, wb, tm)
            xc1, hc2, lgc = outproj(accc, wo, xc2, mod3, g2, wr, 512, ctx_tile_row)
            xc2 = _moe(hc2, lgc, xc1.reshape(B, L, D), mod3, gfin, w_exp_gate, w_exp_up, w_exp_down, l,
                       2 * L // N_EXPERTS, 512, lambda b: ctx_row, final=False).reshape(B * L, D)
    return x2.reshape(B, T, D)
```

```python
import functools

import numpy as np
import jax
import jax.numpy as jnp
from jax import lax
from jax.experimental import pallas as pl
from jax.experimental.pallas import tpu as pltpu

F32 = jnp.float32
BF16 = jnp.bfloat16
HIGHEST = lax.Precision.HIGHEST

D = 2048
T = 2048
L = 256
GRID_W = 64
ROWS = T // GRID_W
EPS = 1e-6
DEPTH = 2
NEG = -1e30

GLA_HEADS, GLA_DK, GLA_DV, GLA_RANK, GLA_TAU, GLA_CHUNK = 4, 64, 128, 16, 16.0, 64
GLA_BLOCK = 256
SWA_BLOCK, SWA_WINDOW = 128, 128
NA_KR, NA_KC, NA_HEADS = 8, 16, 8
SGU_CHUNK = 128
N_EXPERTS = 16
LANES = 128
HEAD = 64

NP = 5120
COL = dict(gla_v=0, gla_r=512, swa_q=1024, na_q=1536, na_k=2048, na_v=2560, sgu_u=3072, sgu_v=3584,
           gla_q=4096, gla_k=4352, swa_k=4608, swa_v=4736, gla_a=4864)
REF = dict(gla_q=0, gla_k=256, gla_v=512, gla_r=1024, gla_af=1536, gla_ab=1552, swa_q=1568, swa_k=2080,
           swa_v=2208, na_q=2336, na_k=2848, na_v=3360, sgu_u=3872, sgu_v=4384)

VMEM_LIMIT = 56 * 1024 * 1024


def _cp(*sem, vmem=VMEM_LIMIT):
    return pltpu.CompilerParams(dimension_semantics=sem, vmem_limit_bytes=vmem)


def _sigmoid(x):
    return 1.0 / (1.0 + jnp.exp(-x))


def _rms(x, g):
    return x * lax.rsqrt(jnp.mean(x * x, axis=-1, keepdims=True) + EPS) * g


def _dot(a, b):
    return jnp.dot(a, b, preferred_element_type=F32)


def _dot_nt(a, b):
    return lax.dot_general(a, b, (((1,), (1,)), ((), ())), preferred_element_type=F32)


def _ada_kernel(c_ref, w_ref, b_ref, o_ref):
    c = c_ref[...]
    act = (c * _sigmoid(c)).astype(BF16)
    o_ref[...] = _dot(act, w_ref[0].astype(BF16)) + b_ref[0]


def ada(cvec, w, b, l):
    rows, tn = cvec.shape[0], 1024
    return pl.pallas_call(
        _ada_kernel, grid=(6 * D // tn,),
        in_specs=[pl.BlockSpec((rows, D), lambda j: (0, 0)),
                  pl.BlockSpec((1, D, tn), lambda j: (l, 0, j)),
                  pl.BlockSpec((1, 1, tn), lambda j: (l, 0, j))],
        out_specs=pl.BlockSpec((rows, tn), lambda j: (0, j)),
        out_shape=jax.ShapeDtypeStruct((rows, 6 * D), F32),
        compiler_params=_cp("parallel"), name="ada")(cvec, w, b)


def _inproj_kernel(x_ref, shift_ref, scale_ref, g_ref, w_ref, h_ref, p_ref, hs_ref):
    @pl.when(pl.program_id(1) == 0)
    def _():
        h = (_rms(x_ref[...], g_ref[...]) * (1.0 + scale_ref[0]) + shift_ref[0]).astype(BF16)
        hs_ref[...] = h
        h_ref[...] = h

    p_ref[...] = _dot(hs_ref[...], w_ref[...])


def inproj(x2, mod3, g, w, tm, mod_row):
    M, tn = x2.shape[0], 512
    tm = min(tm, M)
    return pl.pallas_call(
        _inproj_kernel, grid=(M // tm, NP // tn),
        in_specs=[pl.BlockSpec((tm, D), lambda i, j: (i, 0)),
                  pl.BlockSpec((1, 1, D), lambda i, j: (mod_row(i) * 6 + 0, 0, 0)),
                  pl.BlockSpec((1, 1, D), lambda i, j: (mod_row(i) * 6 + 1, 0, 0)),
                  pl.BlockSpec((1, D), lambda i, j: (0, 0)),
                  pl.BlockSpec((D, tn), lambda i, j: (0, j))],
        out_specs=[pl.BlockSpec((tm, D), lambda i, j: (i, 0)),
                   pl.BlockSpec((tm, tn), lambda i, j: (i, j))],
        out_shape=[jax.ShapeDtypeStruct((M, D), BF16), jax.ShapeDtypeStruct((M, NP), F32)],
        scratch_shapes=[pltpu.VMEM((tm, D), BF16)],
        compiler_params=_cp("parallel", "arbitrary"), name="inproj")(x2, mod3, mod3, g, w)


def _log_sigmoid(x):
    return jnp.minimum(x, 0.0) - jnp.log(1.0 + jnp.exp(-jnp.abs(x)))


def _split3(x):
    hi = x.astype(BF16)
    r1 = x - hi.astype(F32)
    mid = r1.astype(BF16)
    lo = (r1 - mid.astype(F32)).astype(BF16)
    return jnp.concatenate([hi, mid, lo], axis=1)


def _gla_block(q, k, v, a, w2, b2, s_ref, sums, cmask2, order, half, row_lo):
    n, ch = GLA_BLOCK, GLA_CHUNK
    la = _log_sigmoid(_dot(a.astype(BF16), w2) + b2) * (1.0 / GLA_TAU)
    cum = _dot(sums, _split3(la))
    cum = cum[:, :256] + cum[:, 256:512] + cum[:, 512:]
    bc, bl = cum[:n], cum[n:]
    q_in = (q * GLA_DK ** -0.5) * jnp.exp(bc)
    k_in = k * jnp.exp(-bc)
    kte = k * jnp.exp(bl - bc)
    heads = [None] * GLA_HEADS
    for pt in range(2):
        sl = slice(pt * LANES, (pt + 1) * LANES)
        qt = q_in[:, sl]
        qs = jnp.concatenate([jnp.where(half == 0, qt, 0.0), jnp.where(half == 1, qt, 0.0)],
                             axis=0).astype(BF16)
        att = jnp.where(cmask2, _dot_nt(qs, k_in[:, sl].astype(BF16)), 0.0).astype(BF16)
        v2 = v[:, 2 * pt * GLA_DV:(2 * pt + 2) * GLA_DV].astype(BF16)
        intra = [_dot(att[hh * n:(hh + 1) * n], v2[:, hh * GLA_DV:(hh + 1) * GLA_DV]) for hh in range(2)]
        kte_t = kte[:, sl].T.astype(BF16)
        la_t = la[:, sl].T
        st = s_ref[sl, :]
        inter = [None] * (n // ch)
        for c in order:
            cs = slice(c * ch, (c + 1) * ch)
            qc = jnp.concatenate([qs[cs], qs[n + c * ch:n + (c + 1) * ch]], axis=0)
            inter[c] = _dot(qc, st.astype(BF16))
            upd = _dot(kte_t[:, cs], v2[cs])
            dec = jnp.exp(jnp.sum(la_t[:, cs], axis=1, keepdims=True))
            st = st * dec + jnp.where(row_lo, upd[:, :GLA_DV], upd[:, GLA_DV:])
        s_ref[sl, :] = st
        for hh in range(2):
            rs = slice(hh * HEAD, (hh + 1) * HEAD)
            heads[2 * pt + hh] = intra[hh] + jnp.concatenate([inter[c][rs] for c in range(n // ch)], axis=0)
    return jnp.concatenate(heads, axis=1)


def _gla_kernel(qc_ref, kc_ref, vc_ref, ac_ref, qf_ref, kf_ref, vf_ref, af_ref,
                qb_ref, kb_ref, vb_ref, ab_ref, w2_ref, b2_ref,
                ocf_ref, ocb_ref, of_ref, ob_ref, sf_ref, sb_ref):
    s = pl.program_id(1)
    n, ch = GLA_BLOCK, GLA_CHUNK
    ri = lax.broadcasted_iota(jnp.int32, (n, n), 0)
    ci = lax.broadcasted_iota(jnp.int32, (n, n), 1)
    same = (ri // ch) == (ci // ch)
    low, up = same & (ri >= ci), same & (ri <= ci)
    ones = same.astype(BF16)
    sums_f = jnp.concatenate([low.astype(BF16), ones], axis=0)
    sums_b = jnp.concatenate([up.astype(BF16), ones], axis=0)
    cm_f = jnp.concatenate([low, low], axis=0)
    cm_b = jnp.concatenate([up, up], axis=0)
    half = lax.broadcasted_iota(jnp.int32, (n, LANES), 1) // HEAD
    row_lo = lax.broadcasted_iota(jnp.int32, (LANES, GLA_DV), 0) < HEAD
    w2f, w2b = w2_ref[:, :256], w2_ref[:, 256:]
    b2f, b2b = b2_ref[:, :256], b2_ref[:, 256:]
    chunks = tuple(range(n // ch))

    @pl.when(s == 0)
    def _():
        sf_ref[...] = jnp.zeros_like(sf_ref)
        sb_ref[...] = jnp.zeros_like(sb_ref)

    def run(fwd, bwd, o_f, o_b):
        q, k, v, a = fwd
        o_f[0] = _gla_block(q[0], k[0], v[0], a[0], w2f, b2f, sf_ref, sums_f, cm_f, chunks, half, row_lo)
        q, k, v, a = bwd
        o_b[0] = _gla_block(q[0], k[0], v[0], a[0], w2b, b2b, sb_ref, sums_b, cm_b, chunks[::-1], half, row_lo)

    ctx = (qc_ref, kc_ref, vc_ref, ac_ref)

    @pl.when(s == 0)
    def _():
        run(ctx, ctx, ocf_ref, ocb_ref)

    @pl.when(s > 0)
    def _():
        run((qf_ref, kf_ref, vf_ref, af_ref), (qb_ref, kb_ref, vb_ref, ab_ref), of_ref, ob_ref)


def gla_scan(p3, pc3, w2, b2):
    B = p3.shape[0]
    nb = T // GLA_BLOCK
    cq, ck, cv, ca = COL["gla_q"] // 256, COL["gla_k"] // 256, COL["gla_v"] // 512, COL["gla_a"] // 128
    fwd = lambda s: jnp.maximum(s - 1, 0)
    bwd = lambda s: nb - jnp.maximum(s, 1)

    def spec(width, col, row):
        return pl.BlockSpec((1, GLA_BLOCK, width), lambda b, s: (b, row(s), col))

    zero = lambda s: 0
    in_specs = ([spec(256, cq, zero), spec(256, ck, zero), spec(512, cv, zero), spec(128, ca, zero)]
                + [spec(256, cq, fwd), spec(256, ck, fwd), spec(512, cv, fwd), spec(128, ca, fwd)]
                + [spec(256, cq, bwd), spec(256, ck, bwd), spec(512, cv, bwd), spec(128, ca, bwd)]
                + [pl.BlockSpec((128, 512), lambda b, s: (0, 0)), pl.BlockSpec((1, 512), lambda b, s: (0, 0))])
    out_specs = [spec(512, 0, zero), spec(512, 0, zero), spec(512, 0, fwd), spec(512, 0, bwd)]
    out_shape = [jax.ShapeDtypeStruct((B, L, 512), F32)] * 2 + [jax.ShapeDtypeStruct((B, T, 512), F32)] * 2
    return pl.pallas_call(
        _gla_kernel, grid=(B, nb + 1), in_specs=in_specs, out_specs=out_specs, out_shape=out_shape,
        scratch_shapes=[pltpu.VMEM((256, 128), F32), pltpu.VMEM((256, 128), F32)],
        compiler_params=_cp("parallel", "arbitrary"), name="gla_scan",
    )(pc3, pc3, pc3, pc3, p3, p3, p3, p3, p3, p3, p3, p3, w2, b2)


def _gla_out_kernel(of_ref, ob_ref, r_ref, g_ref, y_ref):
    o = of_ref[0] + ob_ref[0]
    r = r_ref[0]
    outs = []
    for hd in range(GLA_HEADS):
        sl = slice(hd * GLA_DV, (hd + 1) * GLA_DV)
        rh = r[:, sl]
        outs.append(_rms(o[:, sl], g_ref[...]) * (rh * _sigmoid(rh)))
    y_ref[0] = jnp.concatenate(outs, axis=1).astype(BF16)


def gla_out(o_f, o_b, p3, g):
    B, n, tr = o_f.shape[0], o_f.shape[1], 256
    blk = lambda col: pl.BlockSpec((1, tr, 512), lambda b, i: (b, i, col))
    return pl.pallas_call(
        _gla_out_kernel, grid=(B, n // tr),
        in_specs=[blk(0), blk(0), blk(COL["gla_r"] // 512), pl.BlockSpec((1, 128), lambda b, i: (0, 0))],
        out_specs=blk(0), out_shape=jax.ShapeDtypeStruct((B, n, 512), BF16),
        compiler_params=_cp("parallel", "parallel"), name="gla_out")(o_f, o_b, p3, g)


def _stack_halves(qt, half):
    return jnp.concatenate([jnp.where(half == 0, qt, 0.0), jnp.where(half == 1, qt, 0.0)], axis=0).astype(BF16)


def _merge_halves(o, half):
    n = o.shape[0] // 2
    return jnp.where(half == 0, o[:n], o[n:])


def _softmax_parts(s_list, sink=None):
    m = s_list[0].max(axis=1, keepdims=True)
    for s in s_list[1:]:
        m = jnp.maximum(m, s.max(axis=1, keepdims=True))
    if sink is not None:
        m = jnp.maximum(m, sink)
    es = [jnp.exp(s - m) for s in s_list]
    den = es[0].sum(axis=1, keepdims=True)
    for e in es[1:]:
        den = den + e.sum(axis=1, keepdims=True)
    if sink is not None:
        den = den + jnp.exp(sink - m)
    return [e.astype(BF16) for e in es], 1.0 / den


def _rope(x, cos, sin, lane):
    partner = jnp.where((lane % HEAD) < HEAD // 2, pltpu.roll(x, LANES - HEAD // 2, 1),
                        pltpu.roll(x, HEAD // 2, 1))
    return x * cos + partner * sin


def _swa_kernel(sink_ref, q_ref, kp_ref, kc_ref, kn_ref, vp_ref, vc_ref, vn_ref, kx_ref, vx_ref,
                cos_ref, sin_ref, o_ref):
    n = pl.program_id(1)
    nb = pl.num_programs(1)
    blk = SWA_BLOCK
    lane = lax.broadcasted_iota(jnp.int32, (blk, LANES), 1)
    half = lane // HEAD

    def tab(ref, i):
        return ref[pl.ds(pl.multiple_of(i * blk, blk), blk), :]

    ip, inx = jnp.maximum(n - 1, 0), jnp.minimum(n + 1, nb - 1)
    kb = jnp.concatenate([_rope(kp_ref[0], tab(cos_ref, ip), tab(sin_ref, ip), lane),
                          _rope(kc_ref[0], tab(cos_ref, n), tab(sin_ref, n), lane),
                          _rope(kn_ref[0], tab(cos_ref, inx), tab(sin_ref, inx), lane)], axis=0).astype(BF16)
    vb = jnp.concatenate([vp_ref[0], vc_ref[0], vn_ref[0]], axis=0).astype(BF16)
    cos_q, sin_q = tab(cos_ref, n), tab(sin_ref, n)
    qs = jnp.concatenate([_stack_halves(_rope(q_ref[0, :, p * LANES:(p + 1) * LANES], cos_q, sin_q, lane)
                                        * HEAD ** -0.5, half) for p in range(4)], axis=0)
    rows = 8 * blk
    qi = lax.broadcasted_iota(jnp.int32, (rows, 3 * blk), 0) % blk
    kj = lax.broadcasted_iota(jnp.int32, (rows, 3 * blk), 1)
    kpos = (n - 1) * blk + kj
    valid = (jnp.abs(kj - blk - qi) <= SWA_WINDOW) & (kpos >= 0) & (kpos < nb * blk)
    grp = lax.broadcasted_iota(jnp.int32, (rows, 1), 0) // blk
    sink = jnp.zeros((rows, 1), F32)
    for g in range(8):
        sink = jnp.where(grp == g, sink_ref[g // 2 + 4 * (g % 2)], sink)
    s_loc = jnp.where(valid, _dot_nt(qs, kb), NEG)
    s_ctx = _dot_nt(qs, kx_ref[0].astype(BF16))
    (e_loc, e_ctx), inv = _softmax_parts([s_loc, s_ctx], sink=sink)
    o = (_dot(e_loc, vb) + _dot(e_ctx, vx_ref[0].astype(BF16))) * inv
    for p in range(4):
        o_ref[0, :, p * LANES:(p + 1) * LANES] = _merge_halves(o[2 * p * blk:2 * (p + 1) * blk], half).astype(BF16)


def swa(p3, pc3, sink, cos_t, sin_t):
    B = p3.shape[0]
    nb = T // SWA_BLOCK
    ck, cv = COL["swa_k"] // 128, COL["swa_v"] // 128
    prev = lambda n: jnp.maximum(n - 1, 0)
    cur = lambda n: n
    nxt = lambda n: jnp.minimum(n + 1, nb - 1)
    kv = lambda col, row: pl.BlockSpec((1, SWA_BLOCK, 128), lambda b, n: (b, row(n), col))
    full = lambda col: pl.BlockSpec((1, L, 128), lambda b, n: (b, 0, col))
    tab = pl.BlockSpec((T, 128), lambda b, n: (0, 0))
    return pl.pallas_call(
        _swa_kernel, grid=(B, nb),
        in_specs=[pl.BlockSpec(memory_space=pltpu.SMEM),
                  pl.BlockSpec((1, SWA_BLOCK, 512), lambda b, n: (b, n, COL["swa_q"] // 512)),
                  kv(ck, prev), kv(ck, cur), kv(ck, nxt), kv(cv, prev), kv(cv, cur), kv(cv, nxt),
                  full(ck), full(cv), tab, tab],
        out_specs=pl.BlockSpec((1, SWA_BLOCK, 512), lambda b, n: (b, n, 0)),
        out_shape=jax.ShapeDtypeStruct((B, T, 512), BF16),
        compiler_params=_cp("parallel", "parallel"), name="swa",
    )(sink, p3, p3, p3, p3, p3, p3, p3, pc3, pc3, cos_t, sin_t)


def _ctx_attn_kernel(sink_ref, q_ref, k_ref, v_ref, o_ref, *, use_sink, kv_tiles):
    n = q_ref.shape[1]
    half = lax.broadcasted_iota(jnp.int32, (n, LANES), 1) // HEAD
    for p in range(4):
        sl = slice(p * LANES, (p + 1) * LANES)
        ks = sl if kv_tiles == 4 else slice(0, LANES)
        qs = _stack_halves(q_ref[0, :, sl] * HEAD ** -0.5, half)
        sink = None
        if use_sink:
            top = lax.broadcasted_iota(jnp.int32, (2 * n, 1), 0) < n
            sink = jnp.where(top, sink_ref[p], sink_ref[p + 4])
        (e,), inv = _softmax_parts([_dot_nt(qs, k_ref[0, :, ks].astype(BF16))], sink=sink)
        o = _dot(e, v_ref[0, :, ks].astype(BF16)) * inv
        o_ref[0, :, sl] = _merge_halves(o, half).astype(BF16)


def ctx_attn(pc3, sink, qcol, kcol, vcol, kv_tiles, use_sink):
    B, w = pc3.shape[0], kv_tiles * 128
    return pl.pallas_call(
        functools.partial(_ctx_attn_kernel, use_sink=use_sink, kv_tiles=kv_tiles), grid=(B,),
        in_specs=[pl.BlockSpec(memory_space=pltpu.SMEM),
                  pl.BlockSpec((1, L, 512), lambda b: (b, 0, qcol // 512)),
                  pl.BlockSpec((1, L, w), lambda b: (b, 0, kcol // w)),
                  pl.BlockSpec((1, L, w), lambda b: (b, 0, vcol // w))],
        out_specs=pl.BlockSpec((1, L, 512), lambda b: (b, 0, 0)),
        out_shape=jax.ShapeDtypeStruct((B, L, 512), BF16),
        compiler_params=_cp("parallel"), name="ctx_attn")(sink, pc3, pc3, pc3)


def _na_kernel(q_ref, k_ref, v_ref, kx_ref, vx_ref, *rest):
    bias_refs, o_ref = rest[:-1], rest[-1]
    nr = len(bias_refs)
    i = pl.program_id(1)
    band = NA_KR * GRID_W
    half = lax.broadcasted_iota(jnp.int32, (GRID_W, LANES), 1) // HEAD
    starts = [pl.multiple_of(jnp.clip(nr * i + k - NA_KR // 2, 0, ROWS - NA_KR) * GRID_W, GRID_W)
              for k in range(nr)]
    s_loc, s_ctx = [], []
    for p in range(4):
        sl = slice(p * LANES, (p + 1) * LANES)
        qs = [_stack_halves(q_ref[0, k * GRID_W:(k + 1) * GRID_W, sl] * HEAD ** -0.5, half) for k in range(nr)]
        for k in range(nr):
            kb = k_ref[0, pl.ds(starts[k], band), sl].astype(BF16)
            s_loc.append(_dot_nt(qs[k], kb) + bias_refs[k][0, 2 * p * GRID_W:(2 * p + 2) * GRID_W, :])
        s_ctx.append(_dot_nt(jnp.concatenate(qs, axis=0), kx_ref[0, :, sl].astype(BF16)))
    (e_loc, e_ctx), inv = _softmax_parts([jnp.concatenate(s_loc, axis=0), jnp.concatenate(s_ctx, axis=0)])
    blk = 2 * GRID_W
    for p in range(4):
        sl = slice(p * LANES, (p + 1) * LANES)
        r0 = p * nr * blk
        o_ctx = _dot(e_ctx[r0:r0 + nr * blk], vx_ref[0, :, sl].astype(BF16))
        for k in range(nr):
            rs = slice(r0 + k * blk, r0 + (k + 1) * blk)
            vb = v_ref[0, pl.ds(starts[k], band), sl].astype(BF16)
            o = (_dot(e_loc[rs], vb) + o_ctx[k * blk:(k + 1) * blk]) * inv[rs]
            o_ref[0, k * GRID_W:(k + 1) * GRID_W, sl] = _merge_halves(o, half).astype(BF16)


def na(p3, pc3, bias, rows_per_step=2):
    B, nr = p3.shape[0], rows_per_step
    cq, ck, cv = COL["na_q"] // 512, COL["na_k"] // 512, COL["na_v"] // 512
    shift = lambda r: r - jnp.clip(r - NA_KR // 2, 0, ROWS - NA_KR)
    bias_spec = lambda k: pl.BlockSpec((1, NA_HEADS * GRID_W, NA_KR * GRID_W),
                                       lambda b, i: (shift(nr * i + k), 0, 0))
    return pl.pallas_call(
        _na_kernel, grid=(B, ROWS // nr),
        in_specs=[pl.BlockSpec((1, nr * GRID_W, 512), lambda b, i: (b, i, cq)),
                  pl.BlockSpec((1, T, 512), lambda b, i: (b, 0, ck)),
                  pl.BlockSpec((1, T, 512), lambda b, i: (b, 0, cv)),
                  pl.BlockSpec((1, L, 512), lambda b, i: (b, 0, ck)),
                  pl.BlockSpec((1, L, 512), lambda b, i: (b, 0, cv))] + [bias_spec(k) for k in range(nr)],
        out_specs=pl.BlockSpec((1, nr * GRID_W, 512), lambda b, i: (b, i, 0)),
        out_shape=jax.ShapeDtypeStruct((B, T, 512), BF16),
        compiler_params=_cp("parallel", "arbitrary"), name="na")(p3, p3, p3, pc3, pc3, *([bias] * nr))


def na_bias_table(rpb):
    col = np.arange(GRID_W)
    col_start = np.clip(col - NA_KC // 2, 0, GRID_W - NA_KC)
    col_ok = (col[None, :] >= col_start[:, None]) & (col[None, :] < col_start[:, None] + NA_KC)
    col_off = np.clip(col[None, :] - col[:, None], -(NA_KC - 1), NA_KC - 1) + (NA_KC - 1)
    onehot = (col_off[:, :, None] == np.arange(2 * NA_KC - 1)).astype(np.float32)
    toep = jnp.einsum("hro,qko->hrqk", rpb, jnp.asarray(onehot), precision=HIGHEST)
    toep = jnp.where(jnp.asarray(col_ok)[None, None], toep, NEG)
    per_shift = [toep[:, NA_KR - 1 - s:2 * NA_KR - 1 - s] for s in range(NA_KR)]
    tab = jnp.stack(per_shift).transpose(0, 1, 3, 2, 4)
    return tab.reshape(NA_KR, NA_HEADS * GRID_W, NA_KR * GRID_W).astype(F32)


def _gelu(x):
    return 0.5 * x * (1.0 + jnp.tanh(np.sqrt(2.0 / np.pi).astype(np.float32) * (x + 0.044715 * (x * x * x))))


def _sgu_kernel(u_ref, v_ref, g_ref, b_ref, ws_ref, bs_ref, y_ref):
    n_chunk = u_ref.shape[1] // SGU_CHUNK
    v = _gelu(v_ref[0])
    mu = jnp.mean(v, axis=-1, keepdims=True)
    var = jnp.mean(jnp.square(v - mu), axis=-1, keepdims=True)
    vn = ((v - mu) * lax.rsqrt(var + EPS) * g_ref[...] + b_ref[...]).astype(BF16)
    cols = []
    for g in range(4):
        gs = slice(g * 128, (g + 1) * 128)
        rhs = jnp.concatenate([vn[c * SGU_CHUNK:(c + 1) * SGU_CHUNK, gs] for c in range(n_chunk)], axis=1)
        mixed = _dot(ws_ref[g], rhs)
        cols.append(jnp.concatenate([mixed[:, c * 128:(c + 1) * 128] + bs_ref[g] for c in range(n_chunk)], axis=0))
    y_ref[0] = (_gelu(u_ref[0]) * jnp.concatenate(cols, axis=1)).astype(BF16)


def sgu(p3, g, b, ws, bs):
    B, n = p3.shape[0], p3.shape[1]
    tr = min(n, 512)
    blk = lambda col: pl.BlockSpec((1, tr, 512), lambda bb, i: (bb, i, col))
    const = lambda shape: pl.BlockSpec(shape, lambda bb, i: (0,) * len(shape))
    return pl.pallas_call(
        _sgu_kernel, grid=(B, n // tr),
        in_specs=[blk(COL["sgu_u"] // 512), blk(COL["sgu_v"] // 512), const((1, 512)), const((1, 512)),
                  const((4, 128, 128)), const((4, 128, 128))],
        out_specs=blk(0), out_shape=jax.ShapeDtypeStruct((B, n, 512), BF16),
        compiler_params=_cp("parallel", "parallel"), name="sgu")(p3, p3, g, b, ws, bs)


def _merge_kernel(h_ref, a_ref, b_ref, n_ref, d_ref, wg_ref, bg_ref, wb_ref, o_ref):
    h = h_ref[...]
    acc = None
    for i, br in enumerate((a_ref, b_ref, n_ref, d_ref)):
        term = _sigmoid(_dot(h, wg_ref[i]) + bg_ref[i]) * _dot(br[...], wb_ref[i])
        acc = term if acc is None else acc + term
    o_ref[...] = acc.astype(BF16)


def merge(h, branches, wg, bg, wb, tm):
    M, tn = h.shape[0], 256
    tm = min(tm, M)
    br = pl.BlockSpec((tm, 512), lambda i, j: (i, 0))
    return pl.pallas_call(
        _merge_kernel, grid=(M // tm, D // tn),
        in_specs=[pl.BlockSpec((tm, D), lambda i, j: (i, 0)), br, br, br, br,
                  pl.BlockSpec((4, D, tn), lambda i, j: (0, 0, j)),
                  pl.BlockSpec((4, 1, tn), lambda i, j: (0, 0, j)),
                  pl.BlockSpec((4, 512, tn), lambda i, j: (0, 0, j))],
        out_specs=pl.BlockSpec((tm, tn), lambda i, j: (i, j)),
        out_shape=jax.ShapeDtypeStruct((M, D), BF16),
        compiler_params=_cp("parallel", "arbitrary"), name="merge")(h, *branches, wg, bg, wb)


def _outproj_kernel(acc_ref, w_ref, x_ref, gate_ref, g_ref, shift_ref, scale_ref, wr_ref,
                    x1_ref, h2_ref, lg_ref):
    x1 = x_ref[...] + gate_ref[0] * _dot(acc_ref[...], w_ref[...])
    x1_ref[...] = x1
    h2 = (_rms(x1, g_ref[...]) * (1.0 + scale_ref[0]) + shift_ref[0]).astype(BF16)
    lg_ref[...] = _dot(h2, wr_ref[...])
    bits = pltpu.bitcast(h2.astype(F32), jnp.int32)
    h2_ref[...] = lax.shift_right_logical(bits[:, :D // 2], 16) | (bits[:, D // 2:] & jnp.int32(-65536))


def outproj(acc, w, x2, mod3, g, wr, tm, mod_row):
    M = acc.shape[0]
    tm = min(tm, M)
    row = lambda k: pl.BlockSpec((1, 1, D), lambda i: (mod_row(i) * 6 + k, 0, 0))
    tile = pl.BlockSpec((tm, D), lambda i: (i, 0))
    return pl.pallas_call(
        _outproj_kernel, grid=(M // tm,),
        in_specs=[tile, pl.BlockSpec((D, D), lambda i: (0, 0)), tile, row(2),
                  pl.BlockSpec((1, D), lambda i: (0, 0)), row(3), row(4),
                  pl.BlockSpec((D, 128), lambda i: (0, 0))],
        out_specs=[tile, pl.BlockSpec((tm, D // 2), lambda i: (i, 0)), pl.BlockSpec((tm, 128), lambda i: (i, 0))],
        out_shape=[jax.ShapeDtypeStruct((M, D), F32), jax.ShapeDtypeStruct((M, D // 2), jnp.int32),
                   jax.ShapeDtypeStruct((M, 128), F32)],
        compiler_params=_cp("parallel"), name="outproj")(acc, w, x2, mod3, g, mod3, mod3, wr)


def _select_kernel(lg_ref, sel_ref, aff_ref, slot_ref, *, cap):
    n = lg_ref.shape[1]
    nt = n // LANES
    lane = lax.broadcasted_iota(jnp.int32, (n, LANES), 1)
    lg = jnp.where(lane < N_EXPERTS, lg_ref[0], NEG)
    e = jnp.exp(lg - lg.max(axis=1, keepdims=True))
    aff = (e / e.sum(axis=1, keepdims=True)).T[:N_EXPERTS]
    aff_ref[0] = aff
    bits = pltpu.bitcast(aff, jnp.int32)

    def count(mask):
        return jnp.sum(jnp.where(mask, 1.0, 0.0), axis=1, keepdims=True)

    def bit_step(i, thr):
        cand = thr | lax.shift_left(jnp.int32(1), 30 - i)
        return jnp.where(count(bits >= cand) >= cap, cand, thr)

    thr = lax.fori_loop(0, 31, bit_step, jnp.zeros((N_EXPERTS, 1), jnp.int32))
    gt, eq = bits > thr, bits == thr

    r = lax.broadcasted_iota(jnp.int32, (LANES, LANES), 0)
    c = lax.broadcasted_iota(jnp.int32, (LANES, LANES), 1)
    upper = (r <= c).astype(BF16)
    r2 = lax.broadcasted_iota(jnp.int32, (nt * N_EXPERTS, nt * N_EXPERTS), 0)
    c2 = lax.broadcasted_iota(jnp.int32, (nt * N_EXPERTS, nt * N_EXPERTS), 1)
    before = ((r2 % N_EXPERTS == c2 % N_EXPERTS) & (c2 // N_EXPERTS < r2 // N_EXPERTS)).astype(BF16)

    def excl_prefix(x):
        xs = jnp.concatenate([x[:, i * LANES:(i + 1) * LANES] for i in range(nt)], axis=0).astype(BF16)
        inc = _dot(xs, upper)
        tot = jnp.broadcast_to(inc[:, LANES - 1:LANES], inc.shape).astype(BF16)
        full = inc + _dot(before, tot)
        return jnp.concatenate([full[i * N_EXPERTS:(i + 1) * N_EXPERTS] for i in range(nt)], axis=1) - x

    eq01 = jnp.where(eq, 1.0, 0.0)
    mask = gt | (eq & (excl_prefix(eq01) < cap - count(gt)))
    slot = jnp.where(mask, excl_prefix(jnp.where(mask, 1.0, 0.0)), -1.0)
    sel_ref[0] = slot.astype(jnp.int32)
    pad = jnp.full((LANES - N_EXPERTS, n), -1.0, F32)
    slot_ref[0] = jnp.concatenate([slot, pad], axis=0).T


def select(lg3, cap):
    B, n = lg3.shape[0], lg3.shape[1]
    tok = pl.BlockSpec((1, n, 128), lambda b: (b, 0, 0))
    row = pl.BlockSpec((1, N_EXPERTS, n), lambda b: (b, 0, 0))
    return pl.pallas_call(
        functools.partial(_select_kernel, cap=cap), grid=(B,),
        in_specs=[tok], out_specs=[row, row, tok],
        out_shape=[jax.ShapeDtypeStruct((B, N_EXPERTS, n), jnp.int32),
                   jax.ShapeDtypeStruct((B, N_EXPERTS, n), F32), jax.ShapeDtypeStruct((B, n, 128), F32)],
        compiler_params=_cp("parallel"), name="select")(lg3)


def _slot_tokens_kernel(sel_ref, aff_ref, idx_ref, g_ref, *, cap):
    n = sel_ref.shape[2]
    hit = lax.broadcasted_iota(jnp.int32, (cap, n), 0) == sel_ref[0]
    tok = lax.broadcasted_iota(jnp.int32, (cap, n), 1).astype(F32)
    idx = jnp.sum(jnp.where(hit, tok, 0.0), axis=1, keepdims=True).astype(jnp.int32)
    idx_ref[0] = idx + pl.program_id(0) * n
    g_ref[0] = jnp.sum(jnp.where(hit, aff_ref[0], 0.0), axis=1, keepdims=True)


def slot_tokens(sel, aff, cap):
    B, n = sel.shape[0], sel.shape[2]
    row = pl.BlockSpec((1, 1, n), lambda b, e: (b * N_EXPERTS + e, 0, 0))
    col = pl.BlockSpec((1, cap, 1), lambda b, e: (e, b, 0))
    return pl.pallas_call(
        functools.partial(_slot_tokens_kernel, cap=cap), grid=(B, N_EXPERTS),
        in_specs=[row, row], out_specs=[col, col],
        out_shape=[jax.ShapeDtypeStruct((N_EXPERTS, B * cap, 1), jnp.int32),
                   jax.ShapeDtypeStruct((N_EXPERTS, B * cap, 1), F32)],
        compiler_params=_cp("parallel", "arbitrary"), name="slot_tokens",
    )(sel.reshape(B * N_EXPERTS, 1, n), aff.reshape(B * N_EXPERTS, 1, n))


def _expert_kernel(idx_ref, h_hbm, gate_ref, wg_ref, wu_ref, wd_ref, o_ref, xg_ref, xb_ref, acc_ref, sem,
                   *, tm):
    e, i, f = pl.program_id(0), pl.program_id(1), pl.program_id(2)
    n_i, n_f = pl.num_programs(1), pl.num_programs(2)
    q = e * n_i + i
    n_q = pl.num_programs(0) * n_i
    slot = q % 2
    share = tm // n_f

    def row_copy(tile, r, dst_slot):
        src = h_hbm.at[pl.ds(idx_ref[tile * tm + r], 1), :]
        return pltpu.make_async_copy(src, xg_ref.at[dst_slot, pl.ds(r, 1), :], sem.at[dst_slot])

    def wait_tile(dst_slot):
        pltpu.make_async_copy(h_hbm.at[pl.ds(0, tm), :], xg_ref.at[dst_slot], sem.at[dst_slot]).wait()

    @pl.when((q == 0) & (f == 0))
    def _():
        def first(r, carry):
            row_copy(0, r, 0).start()
            return carry
        lax.fori_loop(0, tm, first, 0)

    @pl.when(f == 0)
    def _():
        wait_tile(slot)
        p = xg_ref[slot]
        xb_ref[:, :D // 2] = pltpu.bitcast(lax.shift_left(p, 16), F32).astype(BF16)
        xb_ref[:, D // 2:] = pltpu.bitcast(p & jnp.int32(-65536), F32).astype(BF16)
        acc_ref[...] = jnp.zeros_like(acc_ref)

    x = xb_ref[...]
    g = _dot(x, wg_ref[0, 0].astype(BF16))
    u = _dot(x, wu_ref[0, 0].astype(BF16))
    hid = ((g * _sigmoid(g)) * u).astype(BF16)
    nxt = jnp.minimum(q + 1, n_q - 1)
    for r in range(share):
        row_copy(nxt, f * share + r, 1 - slot).start()
    acc_ref[...] += _dot(hid, wd_ref[0, 0].astype(BF16))

    @pl.when(f == n_f - 1)
    def _():
        o_ref[0] = (acc_ref[...] * gate_ref[0]).astype(BF16)

    @pl.when((q == n_q - 1) & (f == n_f - 1))
    def _():
        wait_tile(1 - slot)


def experts(idx, h2p, gate, wg, wu, wd, l, tm):
    E, M, tf = gate.shape[0], gate.shape[1], 256
    F = wg.shape[3]
    tm = min(tm, M)
    grid_spec = pltpu.PrefetchScalarGridSpec(
        num_scalar_prefetch=1, grid=(E, M // tm, F // tf),
        in_specs=[pl.BlockSpec(memory_space=pl.ANY),
                  pl.BlockSpec((1, tm, 1), lambda e, i, f, idx: (e, i, 0)),
                  pl.BlockSpec((1, 1, D, tf), lambda e, i, f, idx: (l, e, 0, f)),
                  pl.BlockSpec((1, 1, D, tf), lambda e, i, f, idx: (l, e, 0, f)),
                  pl.BlockSpec((1, 1, tf, D), lambda e, i, f, idx: (l, e, f, 0))],
        out_specs=pl.BlockSpec((1, tm, D), lambda e, i, f, idx: (e, i, 0)),
        scratch_shapes=[pltpu.VMEM((2, tm, D // 2), jnp.int32), pltpu.VMEM((tm, D), BF16),
                        pltpu.VMEM((tm, D), F32), pltpu.SemaphoreType.DMA((2,))])
    return pl.pallas_call(
        functools.partial(_expert_kernel, tm=tm), grid_spec=grid_spec,
        out_shape=jax.ShapeDtypeStruct((E, M, D), BF16),
        compiler_params=_cp("arbitrary", "arbitrary", "arbitrary"), name="experts")(idx, h2p, gate, wg, wu, wd)


COMBINE_GROUP = 8


def _combine_kernel(slot_ref, ye_ref, x_ref, gate_ref, gfin_ref, o_ref, acc_ref, *, cap, final):
    g = pl.program_id(2)
    tt = x_ref.shape[1]
    lane = lax.broadcasted_iota(jnp.int32, (tt, LANES), 1)
    j = lax.broadcasted_iota(jnp.int32, (tt, cap), 1).astype(F32)
    slots = slot_ref[0]
    parts = []
    for k in range(COMBINE_GROUP):
        slot = jnp.sum(jnp.where(lane == g * COMBINE_GROUP + k, slots, 0.0), axis=1, keepdims=True)
        parts.append(jnp.where(j == slot, 1.0, 0.0).astype(BF16))
    onehot = jnp.concatenate(parts, axis=1)
    contrib = _dot(onehot, ye_ref[:, 0].reshape(COMBINE_GROUP * cap, D))

    @pl.when(g == 0)
    def _():
        acc_ref[...] = contrib

    @pl.when(g == pl.num_programs(2) - 1)
    def _():
        x2 = x_ref[0] + gate_ref[0] * (acc_ref[...] + contrib)
        if final:
            x2 = _rms(x2, gfin_ref[...])
        o_ref[0] = x2


def combine(slot_tok, ye, x3, mod3, gfin, cap, mod_row, final):
    B, n = x3.shape[0], x3.shape[1]
    tt = min(n, 512)
    assert N_EXPERTS == 2 * COMBINE_GROUP
    xt = pl.BlockSpec((1, tt, D), lambda b, t, g: (b, t, 0))
    return pl.pallas_call(
        functools.partial(_combine_kernel, cap=cap, final=final), grid=(B, n // tt, N_EXPERTS // COMBINE_GROUP),
        in_specs=[pl.BlockSpec((1, tt, 128), lambda b, t, g: (b, t, 0)),
                  pl.BlockSpec((COMBINE_GROUP, 1, cap, D), lambda b, t, g: (g, b, 0, 0)), xt,
                  pl.BlockSpec((1, 1, D), lambda b, t, g: (mod_row(b) * 6 + 5, 0, 0)),
                  pl.BlockSpec((1, D), lambda b, t, g: (0, 0))],
        out_specs=xt, out_shape=jax.ShapeDtypeStruct((B, n, D), F32),
        scratch_shapes=[pltpu.VMEM((tt, D), F32)],
        compiler_params=_cp("parallel", "parallel", "arbitrary"), name="combine",
    )(slot_tok, ye.reshape(N_EXPERTS, B, cap, D), x3, mod3, gfin)


def _swa_q_perm():
    return np.concatenate([np.arange(HEAD) + (p + 4 * hh) * HEAD for p in range(4) for hh in range(2)])


def _prep_w_in(w):
    seg = lambda name, width: w[:, REF[name]:REF[name] + width]
    swa_q = seg("swa_q", 512)[:, _swa_q_perm()]
    parts = [seg("gla_v", 512), seg("gla_r", 512), swa_q, seg("na_q", 512), seg("na_k", 512), seg("na_v", 512),
             seg("sgu_u", 512), seg("sgu_v", 512), seg("gla_q", 256), seg("gla_k", 256), seg("swa_k", 128),
             seg("swa_v", 128), seg("gla_af", 16), seg("gla_ab", 16)]
    out = jnp.concatenate(parts, axis=1)
    return jnp.pad(out, ((0, 0), (0, NP - out.shape[1]))).astype(BF16)


def _rope_tables():
    quarter = HEAD // 4
    freqs = 10000.0 ** (-jnp.arange(quarter, dtype=F32) / quarter)
    t = jnp.arange(T)
    row = (t // GRID_W).astype(F32)
    col = (t % GRID_W).astype(F32)
    ang = jnp.concatenate([row[:, None] * freqs, col[:, None] * freqs], axis=-1)
    cos, sin = jnp.cos(ang), jnp.sin(ang)
    cos_t = jnp.tile(cos, (1, 4))
    sin_t = jnp.tile(jnp.concatenate([-sin, sin], axis=-1), (1, 2))
    return cos_t, sin_t


def _moe(h2, lg, x3, mod3, gfin, wg, wu, wd, l, cap, tm, mod_row, final):
    B, n = x3.shape[0], x3.shape[1]
    sel, aff, slot_tok = select(lg.reshape(B, n, 128), cap)
    idx, gate = slot_tokens(sel, aff, cap)
    ye = experts(idx.reshape(-1), h2, gate, wg, wu, wd, l, tm)
    return combine(slot_tok, ye, x3, mod3, gfin, cap, mod_row, final)


def kernel(x, c, ctx, c_ctx, w_ada, b_ada, norm1_g, norm2_g, w_in, gla_w_a2, gla_b_a2, gla_norm_g, swa_sink,
           na_rpb, sgu_ln_g, sgu_ln_b, sgu_w_s, sgu_b_s, w_gate, b_gate, w_branch, w_out, w_router,
           w_exp_gate, w_exp_up, w_exp_down, final_norm_g):
    B = x.shape[0]
    ctx_row = B
    n_mod = -(-(B + 1) // 8) * 8
    cvec = jnp.zeros((n_mod, D), F32).at[:B].set(c).at[ctx_row].set(c_ctx)
    cos_t, sin_t = _rope_tables()
    perm = _swa_q_perm()
    gfin = final_norm_g.reshape(1, D)
    tm = 1024
    lat_tile_row = lambda i: i // (T // tm)
    ctx_tile_row = lambda i: ctx_row
    x2 = x.reshape(B * T, D)
    xc2 = ctx.reshape(B * L, D)
    for l in range(DEPTH):
        need_ctx = l < DEPTH - 1
        mod3 = ada(cvec, w_ada, b_ada.reshape(DEPTH, 1, 6 * D), l).reshape(n_mod * 6, 1, D)
        w_in_p = _prep_w_in(w_in[l])
        g1 = norm1_g[l].reshape(1, D)
        h, p = inproj(x2, mod3, g1, w_in_p, tm, lat_tile_row)
        hc, pc = inproj(xc2, mod3, g1, w_in_p, tm, ctx_tile_row)
        p3, pc3 = p.reshape(B, T, NP), pc.reshape(B, L, NP)

        w2 = jnp.zeros((128, 512), F32).at[:GLA_RANK, :256].set(gla_w_a2[l, 0])
        w2 = w2.at[GLA_RANK:2 * GLA_RANK, 256:].set(gla_w_a2[l, 1]).astype(BF16)
        b2 = gla_b_a2[l].reshape(1, 512)
        ocf, ocb, of, ob = gla_scan(p3, pc3, w2, b2)
        gn = gla_norm_g[l].reshape(1, GLA_DV)
        ya = gla_out(of, ob, p3, gn)
        yb = swa(p3, pc3, swa_sink[l], cos_t, sin_t)
        yn = na(p3, pc3, na_bias_table(na_rpb[l]))
        ln_g, ln_b = sgu_ln_g[l].reshape(1, 512), sgu_ln_b[l].reshape(1, 512)
        ws = sgu_w_s[l].astype(BF16)
        bs = jnp.broadcast_to(sgu_b_s[l][:, :, None], (4, SGU_CHUNK, 128))
        yd = sgu(p3, ln_g, ln_b, ws, bs)

        wg = w_gate[l].astype(BF16)
        bg = b_gate[l].reshape(4, 1, D)
        wb = w_branch[l].at[1].set(w_branch[l, 1][perm]).astype(BF16)
        wo = w_out[l].astype(BF16)
        wr = jnp.pad(w_router[l], ((0, 0), (0, 128 - N_EXPERTS))).astype(BF16)
        g2 = norm2_g[l].reshape(1, D)
        flat = lambda y: y.reshape(-1, 512)
        acc = merge(h, [flat(ya), flat(yb), flat(yn), flat(yd)], wg, bg, wb, tm)
        x1, h2, lg = outproj(acc, wo, x2, mod3, g2, wr, 512, lambda i: i // (T // 512))
        x2 = _moe(h2, lg, x1.reshape(B, T, D), mod3, gfin, w_exp_gate, w_exp_up, w_exp_down, l,
                  2 * T // N_EXPERTS, 1024, lambda b: b, final=not need_ctx).reshape(B * T, D)
        if need_ctx:
            yac = gla_out(ocf, ocb, pc3, gn)
            ybc = ctx_attn(pc3, swa_sink[l], COL["swa_q"], COL["swa_k"], COL["swa_v"], 1, True)
            ync = ctx_attn(pc3, swa_sink[l], COL["na_q"], COL["na_k"], COL["na_v"], 4, False)
            ydc = sgu(pc3, ln_g, ln_b, ws, bs)
            accc = merge(hc, [flat(yac), flat(ybc), flat(ync), flat(ydc)], wg, bg, wb, tm)
            xc1, hc2, lgc = outproj(accc, wo, xc2, mod3, g2, wr, 512, ctx_tile_row)
            xc2 = _moe(hc2, lgc, xc1.reshape(B, L, D), mod3, gfin, w_exp_gate, w_exp_up, w_exp_down, l,
                       2 * L // N_EXPERTS, 512, lambda b: ctx_row, final=False).reshape(B * L, D)
    return x2.reshape(B, T, D)
```

```python
import functools

import numpy as np
import jax
import jax.numpy as jnp
from jax import lax
from jax.experimental import pallas as pl
from jax.experimental.pallas import tpu as pltpu

F32 = jnp.float32
BF16 = jnp.bfloat16
HIGHEST = lax.Precision.HIGHEST

D = 2048
T = 2048
L = 256
GRID_W = 64
ROWS = T // GRID_W
EPS = 1e-6
DEPTH = 2
NEG = -1e30

GLA_HEADS, GLA_DK, GLA_DV, GLA_RANK, GLA_TAU, GLA_CHUNK = 4, 64, 128, 16, 16.0, 64
GLA_BLOCK = 256
SWA_BLOCK, SWA_WINDOW = 128, 128
NA_KR, NA_KC, NA_HEADS = 8, 16, 8
SGU_CHUNK = 128
N_EXPERTS = 16
LANES = 128
HEAD = 64

NP = 5120
COL = dict(gla_v=0, gla_r=512, swa_q=1024, na_q=1536, na_k=2048, na_v=2560, sgu_u=3072, sgu_v=3584,
           gla_q=4096, gla_k=4352, swa_k=4608, swa_v=4736, gla_a=4864)
REF = dict(gla_q=0, gla_k=256, gla_v=512, gla_r=1024, gla_af=1536, gla_ab=1552, swa_q=1568, swa_k=2080,
           swa_v=2208, na_q=2336, na_k=2848, na_v=3360, sgu_u=3872, sgu_v=4384)

VMEM_LIMIT = 56 * 1024 * 1024


def _cp(*sem, vmem=VMEM_LIMIT):
    return pltpu.CompilerParams(dimension_semantics=sem, vmem_limit_bytes=vmem)


def _sigmoid(x):
    return 1.0 / (1.0 + jnp.exp(-x))


def _rms(x, g):
    return x * lax.rsqrt(jnp.mean(x * x, axis=-1, keepdims=True) + EPS) * g


def _dot(a, b):
    return jnp.dot(a, b, preferred_element_type=F32)


def _dot_nt(a, b):
    return lax.dot_general(a, b, (((1,), (1,)), ((), ())), preferred_element_type=F32)


def _ada_kernel(c_ref, w_ref, b_ref, o_ref):
    c = c_ref[...]
    act = (c * _sigmoid(c)).astype(BF16)
    o_ref[...] = _dot(act, w_ref[0].astype(BF16)) + b_ref[0]


def ada(cvec, w, b, l):
    rows, tn = cvec.shape[0], 1024
    return pl.pallas_call(
        _ada_kernel, grid=(6 * D // tn,),
        in_specs=[pl.BlockSpec((rows, D), lambda j: (0, 0)),
                  pl.BlockSpec((1, D, tn), lambda j: (l, 0, j)),
                  pl.BlockSpec((1, 1, tn), lambda j: (l, 0, j))],
        out_specs=pl.BlockSpec((rows, tn), lambda j: (0, j)),
        out_shape=jax.ShapeDtypeStruct((rows, 6 * D), F32),
        compiler_params=_cp("parallel"), name="ada")(cvec, w, b)


def _inproj_kernel(x_ref, shift_ref, scale_ref, g_ref, w_ref, h_ref, p_ref, hs_ref):
    @pl.when(pl.program_id(1) == 0)
    def _():
        h = (_rms(x_ref[...], g_ref[...]) * (1.0 + scale_ref[0]) + shift_ref[0]).astype(BF16)
        hs_ref[...] = h
        h_ref[...] = h

    p_ref[...] = _dot(hs_ref[...], w_ref[...]).astype(BF16)


def inproj(x2, mod3, g, w, tm, mod_row):
    M, tn = x2.shape[0], 1024
    tm = min(tm, M)
    return pl.pallas_call(
        _inproj_kernel, grid=(M // tm, NP // tn),
        in_specs=[pl.BlockSpec((tm, D), lambda i, j: (i, 0)),
                  pl.BlockSpec((1, 1, D), lambda i, j: (mod_row(i) * 6 + 0, 0, 0)),
                  pl.BlockSpec((1, 1, D), lambda i, j: (mod_row(i) * 6 + 1, 0, 0)),
                  pl.BlockSpec((1, D), lambda i, j: (0, 0)),
                  pl.BlockSpec((D, tn), lambda i, j: (0, j))],
        out_specs=[pl.BlockSpec((tm, D), lambda i, j: (i, 0)),
                   pl.BlockSpec((tm, tn), lambda i, j: (i, j))],
        out_shape=[jax.ShapeDtypeStruct((M, D), BF16), jax.ShapeDtypeStruct((M, NP), BF16)],
        scratch_shapes=[pltpu.VMEM((tm, D), BF16)],
        compiler_params=_cp("parallel", "arbitrary"), name="inproj")(x2, mod3, mod3, g, w)


def _log_sigmoid(x):
    return jnp.minimum(x, 0.0) - jnp.log(1.0 + jnp.exp(-jnp.abs(x)))


def _split3(x):
    hi = x.astype(BF16)
    r1 = x - hi.astype(F32)
    mid = r1.astype(BF16)
    lo = (r1 - mid.astype(F32)).astype(BF16)
    return jnp.concatenate([hi, mid, lo], axis=1)


def _gla_block(q, k, v, a, w2, b2, s_ref, sums, cmask2, order, half, row_lo):
    n, ch = GLA_BLOCK, GLA_CHUNK
    la = _log_sigmoid(_dot(a.astype(BF16), w2) + b2) * (1.0 / GLA_TAU)
    cum = _dot(sums, _split3(la))
    cum = cum[:, :256] + cum[:, 256:512] + cum[:, 512:]
    bc, bl = cum[:n], cum[n:]
    q, k = q.astype(F32), k.astype(F32)
    q_in = (q * GLA_DK ** -0.5) * jnp.exp(bc)
    k_in = k * jnp.exp(-bc)
    kte = k * jnp.exp(bl - bc)
    heads = [None] * GLA_HEADS
    for pt in range(2):
        sl = slice(pt * LANES, (pt + 1) * LANES)
        qt = q_in[:, sl]
        qs = jnp.concatenate([jnp.where(half == 0, qt, 0.0), jnp.where(half == 1, qt, 0.0)],
                             axis=0).astype(BF16)
        att = jnp.where(cmask2, _dot_nt(qs, k_in[:, sl].astype(BF16)), 0.0).astype(BF16)
        v2 = v[:, 2 * pt * GLA_DV:(2 * pt + 2) * GLA_DV].astype(BF16)
        intra = [_dot(att[hh * n:(hh + 1) * n], v2[:, hh * GLA_DV:(hh + 1) * GLA_DV]) for hh in range(2)]
        kte_t = kte[:, sl].T.astype(BF16)
        la_t = la[:, sl].T
        st = s_ref[sl, :]
        inter = [None] * (n // ch)
        for c in order:
            cs = slice(c * ch, (c + 1) * ch)
            qc = jnp.concatenate([qs[cs], qs[n + c * ch:n + (c + 1) * ch]], axis=0)
            inter[c] = _dot(qc, st.astype(BF16))
            upd = _dot(kte_t[:, cs], v2[cs])
            dec = jnp.exp(jnp.sum(la_t[:, cs], axis=1, keepdims=True))
            st = st * dec + jnp.where(row_lo, upd[:, :GLA_DV], upd[:, GLA_DV:])
        s_ref[sl, :] = st
        for hh in range(2):
            rs = slice(hh * HEAD, (hh + 1) * HEAD)
            heads[2 * pt + hh] = intra[hh] + jnp.concatenate([inter[c][rs] for c in range(n // ch)], axis=0)
    return jnp.concatenate(heads, axis=1)


def _gla_kernel(qc_ref, kc_ref, vc_ref, ac_ref, qf_ref, kf_ref, vf_ref, af_ref,
                qb_ref, kb_ref, vb_ref, ab_ref, w2_ref, b2_ref,
                ocf_ref, ocb_ref, of_ref, ob_ref, sf_ref, sb_ref):
    s = pl.program_id(1)
    n, ch = GLA_BLOCK, GLA_CHUNK
    ri = lax.broadcasted_iota(jnp.int32, (n, n), 0)
    ci = lax.broadcasted_iota(jnp.int32, (n, n), 1)
    same = (ri // ch) == (ci // ch)
    low, up = same & (ri >= ci), same & (ri <= ci)
    ones = same.astype(BF16)
    sums_f = jnp.concatenate([low.astype(BF16), ones], axis=0)
    sums_b = jnp.concatenate([up.astype(BF16), ones], axis=0)
    cm_f = jnp.concatenate([low, low], axis=0)
    cm_b = jnp.concatenate([up, up], axis=0)
    half = lax.broadcasted_iota(jnp.int32, (n, LANES), 1) // HEAD
    row_lo = lax.broadcasted_iota(jnp.int32, (LANES, GLA_DV), 0) < HEAD
    w2f, w2b = w2_ref[:, :256], w2_ref[:, 256:]
    b2f, b2b = b2_ref[:, :256], b2_ref[:, 256:]
    chunks = tuple(range(n // ch))

    @pl.when(s == 0)
    def _():
        sf_ref[...] = jnp.zeros_like(sf_ref)
        sb_ref[...] = jnp.zeros_like(sb_ref)

    def run(fwd, bwd, o_f, o_b):
        q, k, v, a = fwd
        o_f[0] = _gla_block(q[0], k[0], v[0], a[0], w2f, b2f, sf_ref, sums_f, cm_f, chunks, half, row_lo)
        q, k, v, a = bwd
        o_b[0] = _gla_block(q[0], k[0], v[0], a[0], w2b, b2b, sb_ref, sums_b, cm_b, chunks[::-1], half, row_lo)

    ctx = (qc_ref, kc_ref, vc_ref, ac_ref)

    @pl.when(s == 0)
    def _():
        run(ctx, ctx, ocf_ref, ocb_ref)

    @pl.when(s > 0)
    def _():
        run((qf_ref, kf_ref, vf_ref, af_ref), (qb_ref, kb_ref, vb_ref, ab_ref), of_ref, ob_ref)


def gla_scan(p3, pc3, w2, b2):
    B = p3.shape[0]
    nb = T // GLA_BLOCK
    cq, ck, cv, ca = COL["gla_q"] // 256, COL["gla_k"] // 256, COL["gla_v"] // 512, COL["gla_a"] // 128
    fwd = lambda s: jnp.maximum(s - 1, 0)
    bwd = lambda s: nb - jnp.maximum(s, 1)

    def spec(width, col, row):
        return pl.BlockSpec((1, GLA_BLOCK, width), lambda b, s: (b, row(s), col))

    zero = lambda s: 0
    in_specs = ([spec(256, cq, zero), spec(256, ck, zero), spec(512, cv, zero), spec(128, ca, zero)]
                + [spec(256, cq, fwd), spec(256, ck, fwd), spec(512, cv, fwd), spec(128, ca, fwd)]
                + [spec(256, cq, bwd), spec(256, ck, bwd), spec(512, cv, bwd), spec(128, ca, bwd)]
                + [pl.BlockSpec((128, 512), lambda b, s: (0, 0)), pl.BlockSpec((1, 512), lambda b, s: (0, 0))])
    out_specs = [spec(512, 0, zero), spec(512, 0, zero), spec(512, 0, fwd), spec(512, 0, bwd)]
    out_shape = [jax.ShapeDtypeStruct((B, L, 512), F32)] * 2 + [jax.ShapeDtypeStruct((B, T, 512), F32)] * 2
    return pl.pallas_call(
        _gla_kernel, grid=(B, nb + 1), in_specs=in_specs, out_specs=out_specs, out_shape=out_shape,
        scratch_shapes=[pltpu.VMEM((256, 128), F32), pltpu.VMEM((256, 128), F32)],
        compiler_params=_cp("parallel", "arbitrary"), name="gla_scan",
    )(pc3, pc3, pc3, pc3, p3, p3, p3, p3, p3, p3, p3, p3, w2, b2)


def _gla_out_kernel(of_ref, ob_ref, r_ref, g_ref, y_ref):
    o = of_ref[0] + ob_ref[0]
    r = r_ref[0].astype(F32)
    outs = []
    for hd in range(GLA_HEADS):
        sl = slice(hd * GLA_DV, (hd + 1) * GLA_DV)
        rh = r[:, sl]
        outs.append(_rms(o[:, sl], g_ref[...]) * (rh * _sigmoid(rh)))
    y_ref[0] = jnp.concatenate(outs, axis=1).astype(BF16)


def gla_out(o_f, o_b, p3, g):
    B, n, tr = o_f.shape[0], o_f.shape[1], 256
    blk = lambda col: pl.BlockSpec((1, tr, 512), lambda b, i: (b, i, col))
    return pl.pallas_call(
        _gla_out_kernel, grid=(B, n // tr),
        in_specs=[blk(0), blk(0), blk(COL["gla_r"] // 512), pl.BlockSpec((1, 128), lambda b, i: (0, 0))],
        out_specs=blk(0), out_shape=jax.ShapeDtypeStruct((B, n, 512), BF16),
        compiler_params=_cp("parallel", "parallel"), name="gla_out")(o_f, o_b, p3, g)


def _stack_halves(qt, half):
    zero = jnp.zeros_like(qt)
    return jnp.concatenate([jnp.where(half == 0, qt, zero), jnp.where(half == 1, qt, zero)], axis=0).astype(BF16)


def _merge_halves(o, half):
    n = o.shape[0] // 2
    return jnp.where(half == 0, o[:n], o[n:])


def _softmax_parts(s_list, sink=None):
    m = s_list[0].max(axis=1, keepdims=True)
    for s in s_list[1:]:
        m = jnp.maximum(m, s.max(axis=1, keepdims=True))
    if sink is not None:
        m = jnp.maximum(m, sink)
    es = [jnp.exp(s - m) for s in s_list]
    den = es[0].sum(axis=1, keepdims=True)
    for e in es[1:]:
        den = den + e.sum(axis=1, keepdims=True)
    if sink is not None:
        den = den + jnp.exp(sink - m)
    return [e.astype(BF16) for e in es], 1.0 / den


def _rope(x, cos, sin, lane):
    partner = jnp.where((lane % HEAD) < HEAD // 2, pltpu.roll(x, LANES - HEAD // 2, 1),
                        pltpu.roll(x, HEAD // 2, 1))
    return x * cos + partner * sin


def _swa_kernel(sink_ref, q_ref, kp_ref, kc_ref, kn_ref, vp_ref, vc_ref, vn_ref, kx_ref, vx_ref,
                cos_ref, sin_ref, o_ref):
    n = pl.program_id(1)
    nb = pl.num_programs(1)
    blk = SWA_BLOCK
    lane = lax.broadcasted_iota(jnp.int32, (blk, LANES), 1)
    half = lane // HEAD

    def tab(ref, i):
        return ref[pl.ds(pl.multiple_of(i * blk, blk), blk), :]

    ip, inx = jnp.maximum(n - 1, 0), jnp.minimum(n + 1, nb - 1)
    kb = jnp.concatenate([_rope(kp_ref[0].astype(F32), tab(cos_ref, ip), tab(sin_ref, ip), lane),
                          _rope(kc_ref[0].astype(F32), tab(cos_ref, n), tab(sin_ref, n), lane),
                          _rope(kn_ref[0].astype(F32), tab(cos_ref, inx), tab(sin_ref, inx), lane)],
                         axis=0).astype(BF16)
    vb = jnp.concatenate([vp_ref[0], vc_ref[0], vn_ref[0]], axis=0)
    cos_q, sin_q = tab(cos_ref, n), tab(sin_ref, n)
    qs = jnp.concatenate([_stack_halves(_rope(q_ref[0, :, p * LANES:(p + 1) * LANES].astype(F32), cos_q, sin_q, lane)
                                        * HEAD ** -0.5, half) for p in range(4)], axis=0)
    rows = 8 * blk
    qi = lax.broadcasted_iota(jnp.int32, (rows, 3 * blk), 0) % blk
    kj = lax.broadcasted_iota(jnp.int32, (rows, 3 * blk), 1)
    kpos = (n - 1) * blk + kj
    valid = (jnp.abs(kj - blk - qi) <= SWA_WINDOW) & (kpos >= 0) & (kpos < nb * blk)
    grp = lax.broadcasted_iota(jnp.int32, (rows, 1), 0) // blk
    sink = jnp.zeros((rows, 1), F32)
    for g in range(8):
        sink = jnp.where(grp == g, sink_ref[g // 2 + 4 * (g % 2)], sink)
    s_loc = jnp.where(valid, _dot_nt(qs, kb), NEG)
    s_ctx = _dot_nt(qs, kx_ref[0])
    (e_loc, e_ctx), inv = _softmax_parts([s_loc, s_ctx], sink=sink)
    o = (_dot(e_loc, vb) + _dot(e_ctx, vx_ref[0])) * inv
    for p in range(4):
        o_ref[0, :, p * LANES:(p + 1) * LANES] = _merge_halves(o[2 * p * blk:2 * (p + 1) * blk], half).astype(BF16)


def swa(p3, pc3, sink, cos_t, sin_t):
    B = p3.shape[0]
    nb = T // SWA_BLOCK
    ck, cv = COL["swa_k"] // 128, COL["swa_v"] // 128
    prev = lambda n: jnp.maximum(n - 1, 0)
    cur = lambda n: n
    nxt = lambda n: jnp.minimum(n + 1, nb - 1)
    kv = lambda col, row: pl.BlockSpec((1, SWA_BLOCK, 128), lambda b, n: (b, row(n), col))
    full = lambda col: pl.BlockSpec((1, L, 128), lambda b, n: (b, 0, col))
    tab = pl.BlockSpec((T, 128), lambda b, n: (0, 0))
    return pl.pallas_call(
        _swa_kernel, grid=(B, nb),
        in_specs=[pl.BlockSpec(memory_space=pltpu.SMEM),
                  pl.BlockSpec((1, SWA_BLOCK, 512), lambda b, n: (b, n, COL["swa_q"] // 512)),
                  kv(ck, prev), kv(ck, cur), kv(ck, nxt), kv(cv, prev), kv(cv, cur), kv(cv, nxt),
                  full(ck), full(cv), tab, tab],
        out_specs=pl.BlockSpec((1, SWA_BLOCK, 512), lambda b, n: (b, n, 0)),
        out_shape=jax.ShapeDtypeStruct((B, T, 512), BF16),
        compiler_params=_cp("parallel", "parallel"), name="swa",
    )(sink, p3, p3, p3, p3, p3, p3, p3, pc3, pc3, cos_t, sin_t)


def _ctx_attn_kernel(sink_ref, q_ref, k_ref, v_ref, o_ref, *, use_sink, kv_tiles):
    n = q_ref.shape[1]
    half = lax.broadcasted_iota(jnp.int32, (n, LANES), 1) // HEAD
    for p in range(4):
        sl = slice(p * LANES, (p + 1) * LANES)
        ks = sl if kv_tiles == 4 else slice(0, LANES)
        qs = _stack_halves(q_ref[0, :, sl] * HEAD ** -0.5, half)
        sink = None
        if use_sink:
            top = lax.broadcasted_iota(jnp.int32, (2 * n, 1), 0) < n
            sink = jnp.where(top, sink_ref[p], sink_ref[p + 4])
        (e,), inv = _softmax_parts([_dot_nt(qs, k_ref[0, :, ks])], sink=sink)
        o = _dot(e, v_ref[0, :, ks]) * inv
        o_ref[0, :, sl] = _merge_halves(o, half).astype(BF16)


def ctx_attn(pc3, sink, qcol, kcol, vcol, kv_tiles, use_sink):
    B, w = pc3.shape[0], kv_tiles * 128
    return pl.pallas_call(
        functools.partial(_ctx_attn_kernel, use_sink=use_sink, kv_tiles=kv_tiles), grid=(B,),
        in_specs=[pl.BlockSpec(memory_space=pltpu.SMEM),
                  pl.BlockSpec((1, L, 512), lambda b: (b, 0, qcol // 512)),
                  pl.BlockSpec((1, L, w), lambda b: (b, 0, kcol // w)),
                  pl.BlockSpec((1, L, w), lambda b: (b, 0, vcol // w))],
        out_specs=pl.BlockSpec((1, L, 512), lambda b: (b, 0, 0)),
        out_shape=jax.ShapeDtypeStruct((B, L, 512), BF16),
        compiler_params=_cp("parallel"), name="ctx_attn")(sink, pc3, pc3, pc3)


def _na_kernel(q_ref, k_ref, v_ref, kx_ref, vx_ref, *rest):
    bias_refs, o_ref = rest[:-1], rest[-1]
    nr = len(bias_refs)
    i = pl.program_id(1)
    band = NA_KR * GRID_W
    half = lax.broadcasted_iota(jnp.int32, (GRID_W, LANES), 1) // HEAD
    starts = [pl.multiple_of(jnp.clip(nr * i + k - NA_KR // 2, 0, ROWS - NA_KR) * GRID_W, GRID_W)
              for k in range(nr)]
    s_loc, s_ctx = [], []
    for p in range(4):
        sl = slice(p * LANES, (p + 1) * LANES)
        qs = [_stack_halves(q_ref[0, k * GRID_W:(k + 1) * GRID_W, sl] * HEAD ** -0.5, half) for k in range(nr)]
        for k in range(nr):
            kb = k_ref[0, pl.ds(starts[k], band), sl]
            s_loc.append(_dot_nt(qs[k], kb) + bias_refs[k][0, 2 * p * GRID_W:(2 * p + 2) * GRID_W, :])
        s_ctx.append(_dot_nt(jnp.concatenate(qs, axis=0), kx_ref[0, :, sl]))
    (e_loc, e_ctx), inv = _softmax_parts([jnp.concatenate(s_loc, axis=0), jnp.concatenate(s_ctx, axis=0)])
    blk = 2 * GRID_W
    for p in range(4):
        sl = slice(p * LANES, (p + 1) * LANES)
        r0 = p * nr * blk
        o_ctx = _dot(e_ctx[r0:r0 + nr * blk], vx_ref[0, :, sl])
        for k in range(nr):
            rs = slice(r0 + k * blk, r0 + (k + 1) * blk)
            vb = v_ref[0, pl.ds(starts[k], band), sl]
            o = (_dot(e_loc[rs], vb) + o_ctx[k * blk:(k + 1) * blk]) * inv[rs]
            o_ref[0, k * GRID_W:(k + 1) * GRID_W, sl] = _merge_halves(o, half).astype(BF16)


def na(p3, pc3, bias, rows_per_step=2):
    B, nr = p3.shape[0], rows_per_step
    cq, ck, cv = COL["na_q"] // 512, COL["na_k"] // 512, COL["na_v"] // 512
    shift = lambda r: r - jnp.clip(r - NA_KR // 2, 0, ROWS - NA_KR)
    bias_spec = lambda k: pl.BlockSpec((1, NA_HEADS * GRID_W, NA_KR * GRID_W),
                                       lambda b, i: (shift(nr * i + k), 0, 0))
    return pl.pallas_call(
        _na_kernel, grid=(B, ROWS // nr),
        in_specs=[pl.BlockSpec((1, nr * GRID_W, 512), lambda b, i: (b, i, cq)),
                  pl.BlockSpec((1, T, 512), lambda b, i: (b, 0, ck)),
                  pl.BlockSpec((1, T, 512), lambda b, i: (b, 0, cv)),
                  pl.BlockSpec((1, L, 512), lambda b, i: (b, 0, ck)),
                  pl.BlockSpec((1, L, 512), lambda b, i: (b, 0, cv))] + [bias_spec(k) for k in range(nr)],
        out_specs=pl.BlockSpec((1, nr * GRID_W, 512), lambda b, i: (b, i, 0)),
        out_shape=jax.ShapeDtypeStruct((B, T, 512), BF16),
        compiler_params=_cp("parallel", "arbitrary"), name="na")(p3, p3, p3, pc3, pc3, *([bias] * nr))


def na_bias_table(rpb):
    col = np.arange(GRID_W)
    col_start = np.clip(col - NA_KC // 2, 0, GRID_W - NA_KC)
    col_ok = (col[None, :] >= col_start[:, None]) & (col[None, :] < col_start[:, None] + NA_KC)
    col_off = np.clip(col[None, :] - col[:, None], -(NA_KC - 1), NA_KC - 1) + (NA_KC - 1)
    onehot = (col_off[:, :, None] == np.arange(2 * NA_KC - 1)).astype(np.float32)
    toep = jnp.einsum("hro,qko->hrqk", rpb, jnp.asarray(onehot), precision=HIGHEST)
    toep = jnp.where(jnp.asarray(col_ok)[None, None], toep, NEG)
    per_shift = [toep[:, NA_KR - 1 - s:2 * NA_KR - 1 - s] for s in range(NA_KR)]
    tab = jnp.stack(per_shift).transpose(0, 1, 3, 2, 4)
    return tab.reshape(NA_KR, NA_HEADS * GRID_W, NA_KR * GRID_W).astype(F32)


def _gelu(x):
    return 0.5 * x * (1.0 + jnp.tanh(np.sqrt(2.0 / np.pi).astype(np.float32) * (x + 0.044715 * (x * x * x))))


def _sgu_kernel(u_ref, v_ref, g_ref, b_ref, ws_ref, bs_ref, y_ref):
    n_chunk = u_ref.shape[1] // SGU_CHUNK
    v = _gelu(v_ref[0].astype(F32))
    mu = jnp.mean(v, axis=-1, keepdims=True)
    var = jnp.mean(jnp.square(v - mu), axis=-1, keepdims=True)
    vn = ((v - mu) * lax.rsqrt(var + EPS) * g_ref[...] + b_ref[...]).astype(BF16)
    cols = []
    for g in range(4):
        gs = slice(g * 128, (g + 1) * 128)
        rhs = jnp.concatenate([vn[c * SGU_CHUNK:(c + 1) * SGU_CHUNK, gs] for c in range(n_chunk)], axis=1)
        mixed = _dot(ws_ref[g], rhs)
        cols.append(jnp.concatenate([mixed[:, c * 128:(c + 1) * 128] + bs_ref[g] for c in range(n_chunk)], axis=0))
    y_ref[0] = (_gelu(u_ref[0].astype(F32)) * jnp.concatenate(cols, axis=1)).astype(BF16)


def sgu(p3, g, b, ws, bs):
    B, n = p3.shape[0], p3.shape[1]
    tr = min(n, 512)
    blk = lambda col: pl.BlockSpec((1, tr, 512), lambda bb, i: (bb, i, col))
    const = lambda shape: pl.BlockSpec(shape, lambda bb, i: (0,) * len(shape))
    return pl.pallas_call(
        _sgu_kernel, grid=(B, n // tr),
        in_specs=[blk(COL["sgu_u"] // 512), blk(COL["sgu_v"] // 512), const((1, 512)), const((1, 512)),
                  const((4, 128, 128)), const((4, 128, 128))],
        out_specs=blk(0), out_shape=jax.ShapeDtypeStruct((B, n, 512), BF16),
        compiler_params=_cp("parallel", "parallel"), name="sgu")(p3, p3, g, b, ws, bs)


def _merge_kernel(h_ref, a_ref, b_ref, n_ref, d_ref, wg_ref, bg_ref, wb_ref, o_ref):
    h = h_ref[...]
    acc = None
    for i, br in enumerate((a_ref, b_ref, n_ref, d_ref)):
        term = _sigmoid(_dot(h, wg_ref[i]) + bg_ref[i]) * _dot(br[...], wb_ref[i])
        acc = term if acc is None else acc + term
    o_ref[...] = acc.astype(BF16)


def merge(h, branches, wg, bg, wb, tm):
    M, tn = h.shape[0], 256
    tm = min(tm, M)
    br = pl.BlockSpec((tm, 512), lambda i, j: (i, 0))
    return pl.pallas_call(
        _merge_kernel, grid=(M // tm, D // tn),
        in_specs=[pl.BlockSpec((tm, D), lambda i, j: (i, 0)), br, br, br, br,
                  pl.BlockSpec((4, D, tn), lambda i, j: (0, 0, j)),
                  pl.BlockSpec((4, 1, tn), lambda i, j: (0, 0, j)),
                  pl.BlockSpec((4, 512, tn), lambda i, j: (0, 0, j))],
        out_specs=pl.BlockSpec((tm, tn), lambda i, j: (i, j)),
        out_shape=jax.ShapeDtypeStruct((M, D), BF16),
        compiler_params=_cp("parallel", "arbitrary"), name="merge")(h, *branches, wg, bg, wb)


def _outproj_kernel(acc_ref, w_ref, x_ref, gate_ref, g_ref, shift_ref, scale_ref, wr_ref,
                    x1_ref, h2_ref, lg_ref):
    x1 = x_ref[...] + gate_ref[0] * _dot(acc_ref[...], w_ref[...])
    x1_ref[...] = x1
    h2 = (_rms(x1, g_ref[...]) * (1.0 + scale_ref[0]) + shift_ref[0]).astype(BF16)
    lg_ref[...] = _dot(h2, wr_ref[...])
    bits = pltpu.bitcast(h2.astype(F32), jnp.int32)
    h2_ref[...] = lax.shift_right_logical(bits[:, :D // 2], 16) | (bits[:, D // 2:] & jnp.int32(-65536))


def outproj(acc, w, x2, mod3, g, wr, tm, mod_row):
    M = acc.shape[0]
    tm = min(tm, M)
    row = lambda k: pl.BlockSpec((1, 1, D), lambda i: (mod_row(i) * 6 + k, 0, 0))
    tile = pl.BlockSpec((tm, D), lambda i: (i, 0))
    return pl.pallas_call(
        _outproj_kernel, grid=(M // tm,),
        in_specs=[tile, pl.BlockSpec((D, D), lambda i: (0, 0)), tile, row(2),
                  pl.BlockSpec((1, D), lambda i: (0, 0)), row(3), row(4),
                  pl.BlockSpec((D, 128), lambda i: (0, 0))],
        out_specs=[tile, pl.BlockSpec((tm, D // 2), lambda i: (i, 0)), pl.BlockSpec((tm, 128), lambda i: (i, 0))],
        out_shape=[jax.ShapeDtypeStruct((M, D), F32), jax.ShapeDtypeStruct((M, D // 2), jnp.int32),
                   jax.ShapeDtypeStruct((M, 128), F32)],
        compiler_params=_cp("parallel"), name="outproj")(acc, w, x2, mod3, g, mod3, mod3, wr)


def _select_kernel(lg_ref, sel_ref, aff_ref, slot_ref, *, cap):
    n = lg_ref.shape[1]
    nt = n // LANES
    lane = lax.broadcasted_iota(jnp.int32, (n, LANES), 1)
    lg = jnp.where(lane < N_EXPERTS, lg_ref[0], NEG)
    e = jnp.exp(lg - lg.max(axis=1, keepdims=True))
    aff = (e / e.sum(axis=1, keepdims=True)).T[:N_EXPERTS]
    aff_ref[0] = aff
    bits = pltpu.bitcast(aff, jnp.int32)

    def count(mask):
        return jnp.sum(jnp.where(mask, 1.0, 0.0), axis=1, keepdims=True)

    def bit_step(i, thr):
        cand = thr | lax.shift_left(jnp.int32(1), 30 - i)
        return jnp.where(count(bits >= cand) >= cap, cand, thr)

    thr = lax.fori_loop(0, 31, bit_step, jnp.zeros((N_EXPERTS, 1), jnp.int32))
    gt, eq = bits > thr, bits == thr

    r = lax.broadcasted_iota(jnp.int32, (LANES, LANES), 0)
    c = lax.broadcasted_iota(jnp.int32, (LANES, LANES), 1)
    upper = (r <= c).astype(BF16)
    r2 = lax.broadcasted_iota(jnp.int32, (nt * N_EXPERTS, nt * N_EXPERTS), 0)
    c2 = lax.broadcasted_iota(jnp.int32, (nt * N_EXPERTS, nt * N_EXPERTS), 1)
    before = ((r2 % N_EXPERTS == c2 % N_EXPERTS) & (c2 // N_EXPERTS < r2 // N_EXPERTS)).astype(BF16)

    def excl_prefix(x):
        xs = jnp.concatenate([x[:, i * LANES:(i + 1) * LANES] for i in range(nt)], axis=0).astype(BF16)
        inc = _dot(xs, upper)
        tot = jnp.broadcast_to(inc[:, LANES - 1:LANES], inc.shape).astype(BF16)
        full = inc + _dot(before, tot)
        return jnp.concatenate([full[i * N_EXPERTS:(i + 1) * N_EXPERTS] for i in range(nt)], axis=1) - x

    eq01 = jnp.where(eq, 1.0, 0.0)
    mask = gt | (eq & (excl_prefix(eq01) < cap - count(gt)))
    slot = jnp.where(mask, excl_prefix(jnp.where(mask, 1.0, 0.0)), -1.0)
    sel_ref[0] = slot.astype(jnp.int32)
    pad = jnp.full((LANES - N_EXPERTS, n), -1.0, F32)
    slot_ref[0] = jnp.concatenate([slot, pad], axis=0).T


def select(lg3, cap):
    B, n = lg3.shape[0], lg3.shape[1]
    tok = pl.BlockSpec((1, n, 128), lambda b: (b, 0, 0))
    row = pl.BlockSpec((1, N_EXPERTS, n), lambda b: (b, 0, 0))
    return pl.pallas_call(
        functools.partial(_select_kernel, cap=cap), grid=(B,),
        in_specs=[tok], out_specs=[row, row, tok],
        out_shape=[jax.ShapeDtypeStruct((B, N_EXPERTS, n), jnp.int32),
                   jax.ShapeDtypeStruct((B, N_EXPERTS, n), F32), jax.ShapeDtypeStruct((B, n, 128), F32)],
        compiler_params=_cp("parallel"), name="select")(lg3)


def _slot_tokens_kernel(sel_ref, aff_ref, idx_ref, g_ref, *, cap):
    n = sel_ref.shape[2]
    row = lax.broadcasted_iota(jnp.int32, (cap, n), 0)
    tok = lax.broadcasted_iota(jnp.int32, (cap, n), 1).astype(F32)
    for e in range(N_EXPERTS):
        hit = row == sel_ref[0, e:e + 1, :]
        idx = jnp.sum(jnp.where(hit, tok, 0.0), axis=1, keepdims=True).astype(jnp.int32)
        idx_ref[e, 0] = idx + pl.program_id(0) * n
        g_ref[e, 0] = jnp.sum(jnp.where(hit, aff_ref[0, e:e + 1, :], 0.0), axis=1, keepdims=True)


def slot_tokens(sel, aff, cap):
    B, n = sel.shape[0], sel.shape[2]
    row = pl.BlockSpec((1, N_EXPERTS, n), lambda b: (b, 0, 0))
    col = pl.BlockSpec((N_EXPERTS, 1, cap, 1), lambda b: (0, b, 0, 0))
    idx, gate = pl.pallas_call(
        functools.partial(_slot_tokens_kernel, cap=cap), grid=(B,),
        in_specs=[row, row], out_specs=[col, col],
        out_shape=[jax.ShapeDtypeStruct((N_EXPERTS, B, cap, 1), jnp.int32),
                   jax.ShapeDtypeStruct((N_EXPERTS, B, cap, 1), F32)],
        compiler_params=_cp("parallel"), name="slot_tokens")(sel, aff)
    return idx.reshape(N_EXPERTS, B * cap, 1), gate.reshape(N_EXPERTS, B * cap, 1)


def _expert_kernel(idx_ref, h_hbm, gate_ref, wg_ref, wu_ref, wd_ref, o_ref, xg_ref, xb_ref, acc_ref, sem,
                   *, tm):
    e, i, f = pl.program_id(0), pl.program_id(1), pl.program_id(2)
    n_i, n_f = pl.num_programs(1), pl.num_programs(2)
    q = e * n_i + i
    n_q = pl.num_programs(0) * n_i
    slot = q % 2
    share = tm // n_f

    def row_copy(tile, r, dst_slot):
        src = h_hbm.at[pl.ds(idx_ref[tile * tm + r], 1), :]
        return pltpu.make_async_copy(src, xg_ref.at[dst_slot, pl.ds(r, 1), :], sem.at[dst_slot])

    def wait_tile(dst_slot):
        pltpu.make_async_copy(h_hbm.at[pl.ds(0, tm), :], xg_ref.at[dst_slot], sem.at[dst_slot]).wait()

    @pl.when((q == 0) & (f == 0))
    def _():
        def first(r, carry):
            row_copy(0, r, 0).start()
            return carry
        lax.fori_loop(0, tm, first, 0)

    @pl.when(f == 0)
    def _():
        wait_tile(slot)
        p = xg_ref[slot]
        xb_ref[:, :D // 2] = pltpu.bitcast(lax.shift_left(p, 16), F32).astype(BF16)
        xb_ref[:, D // 2:] = pltpu.bitcast(p & jnp.int32(-65536), F32).astype(BF16)
        acc_ref[...] = jnp.zeros_like(acc_ref)

    x = xb_ref[...]
    g = _dot(x, wg_ref[0, 0].astype(BF16))
    u = _dot(x, wu_ref[0, 0].astype(BF16))
    hid = ((g * _sigmoid(g)) * u).astype(BF16)
    nxt = jnp.minimum(q + 1, n_q - 1)
    for r in range(share):
        row_copy(nxt, f * share + r, 1 - slot).start()
    acc_ref[...] += _dot(hid, wd_ref[0, 0].astype(BF16))

    @pl.when(f == n_f - 1)
    def _():
        o_ref[0] = (acc_ref[...] * gate_ref[0]).astype(BF16)

    @pl.when((q == n_q - 1) & (f == n_f - 1))
    def _():
        wait_tile(1 - slot)


def experts(idx, h2p, gate, wg, wu, wd, l, tm):
    E, M, tf = gate.shape[0], gate.shape[1], 256
    F = wg.shape[3]
    tm = min(tm, M)
    grid_spec = pltpu.PrefetchScalarGridSpec(
        num_scalar_prefetch=1, grid=(E, M // tm, F // tf),
        in_specs=[pl.BlockSpec(memory_space=pl.ANY),
                  pl.BlockSpec((1, tm, 1), lambda e, i, f, idx: (e, i, 0)),
                  pl.BlockSpec((1, 1, D, tf), lambda e, i, f, idx: (l, e, 0, f)),
                  pl.BlockSpec((1, 1, D, tf), lambda e, i, f, idx: (l, e, 0, f)),
                  pl.BlockSpec((1, 1, tf, D), lambda e, i, f, idx: (l, e, f, 0))],
        out_specs=pl.BlockSpec((1, tm, D), lambda e, i, f, idx: (e, i, 0)),
        scratch_shapes=[pltpu.VMEM((2, tm, D // 2), jnp.int32), pltpu.VMEM((tm, D), BF16),
                        pltpu.VMEM((tm, D), F32), pltpu.SemaphoreType.DMA((2,))])
    return pl.pallas_call(
        functools.partial(_expert_kernel, tm=tm), grid_spec=grid_spec,
        out_shape=jax.ShapeDtypeStruct((E, M, D), BF16),
        compiler_params=_cp("arbitrary", "arbitrary", "arbitrary"), name="experts")(idx, h2p, gate, wg, wu, wd)


COMBINE_GROUP = 8


def _combine_kernel(slot_ref, ye_ref, x_ref, gate_ref, gfin_ref, o_ref, acc_ref, *, cap, final):
    g = pl.program_id(2)
    tt = x_ref.shape[1]
    lane = lax.broadcasted_iota(jnp.int32, (tt, LANES), 1)
    j = lax.broadcasted_iota(jnp.int32, (tt, cap), 1).astype(F32)
    slots = slot_ref[0]
    parts = []
    for k in range(COMBINE_GROUP):
        slot = jnp.sum(jnp.where(lane == g * COMBINE_GROUP + k, slots, 0.0), axis=1, keepdims=True)
        parts.append(jnp.where(j == slot, 1.0, 0.0).astype(BF16))
    onehot = jnp.concatenate(parts, axis=1)
    contrib = _dot(onehot, ye_ref[:, 0].reshape(COMBINE_GROUP * cap, D))

    @pl.when(g == 0)
    def _():
        acc_ref[...] = contrib

    @pl.when(g == pl.num_programs(2) - 1)
    def _():
        x2 = x_ref[0] + gate_ref[0] * (acc_ref[...] + contrib)
        if final:
            x2 = _rms(x2, gfin_ref[...])
        o_ref[0] = x2


def combine(slot_tok, ye, x3, mod3, gfin, cap, mod_row, final):
    B, n = x3.shape[0], x3.shape[1]
    tt = min(n, 512)
    assert N_EXPERTS == 2 * COMBINE_GROUP
    xt = pl.BlockSpec((1, tt, D), lambda b, t, g: (b, t, 0))
    return pl.pallas_call(
        functools.partial(_combine_kernel, cap=cap, final=final), grid=(B, n // tt, N_EXPERTS // COMBINE_GROUP),
        in_specs=[pl.BlockSpec((1, tt, 128), lambda b, t, g: (b, t, 0)),
                  pl.BlockSpec((COMBINE_GROUP, 1, cap, D), lambda b, t, g: (g, b, 0, 0)), xt,
                  pl.BlockSpec((1, 1, D), lambda b, t, g: (mod_row(b) * 6 + 5, 0, 0)),
                  pl.BlockSpec((1, D), lambda b, t, g: (0, 0))],
        out_specs=xt, out_shape=jax.ShapeDtypeStruct((B, n, D), F32),
        scratch_shapes=[pltpu.VMEM((tt, D), F32)],
        compiler_params=_cp("parallel", "parallel", "arbitrary"), name="combine",
    )(slot_tok, ye.reshape(N_EXPERTS, B, cap, D), x3, mod3, gfin)


def _swa_q_perm():
    return np.concatenate([np.arange(HEAD) + (p + 4 * hh) * HEAD for p in range(4) for hh in range(2)])


def _prep_w_in(w):
    seg = lambda name, width: w[:, REF[name]:REF[name] + width]
    swa_q = seg("swa_q", 512)[:, _swa_q_perm()]
    parts = [seg("gla_v", 512), seg("gla_r", 512), swa_q, seg("na_q", 512), seg("na_k", 512), seg("na_v", 512),
             seg("sgu_u", 512), seg("sgu_v", 512), seg("gla_q", 256), seg("gla_k", 256), seg("swa_k", 128),
             seg("swa_v", 128), seg("gla_af", 16), seg("gla_ab", 16)]
    out = jnp.concatenate(parts, axis=1)
    return jnp.pad(out, ((0, 0), (0, NP - out.shape[1]))).astype(BF16)


def _rope_tables():
    quarter = HEAD // 4
    freqs = 10000.0 ** (-jnp.arange(quarter, dtype=F32) / quarter)
    t = jnp.arange(T)
    row = (t // GRID_W).astype(F32)
    col = (t % GRID_W).astype(F32)
    ang = jnp.concatenate([row[:, None] * freqs, col[:, None] * freqs], axis=-1)
    cos, sin = jnp.cos(ang), jnp.sin(ang)
    cos_t = jnp.tile(cos, (1, 4))
    sin_t = jnp.tile(jnp.concatenate([-sin, sin], axis=-1), (1, 2))
    return cos_t, sin_t


def _moe(h2, lg, x3, mod3, gfin, wg, wu, wd, l, cap, tm, mod_row, final):
    B, n = x3.shape[0], x3.shape[1]
    sel, aff, slot_tok = select(lg.reshape(B, n, 128), cap)
    idx, gate = slot_tokens(sel, aff, cap)
    ye = experts(idx.reshape(-1), h2, gate, wg, wu, wd, l, tm)
    return combine(slot_tok, ye, x3, mod3, gfin, cap, mod_row, final)


def kernel(x, c, ctx, c_ctx, w_ada, b_ada, norm1_g, norm2_g, w_in, gla_w_a2, gla_b_a2, gla_norm_g, swa_sink,
           na_rpb, sgu_ln_g, sgu_ln_b, sgu_w_s, sgu_b_s, w_gate, b_gate, w_branch, w_out, w_router,
           w_exp_gate, w_exp_up, w_exp_down, final_norm_g):
    B = x.shape[0]
    ctx_row = B
    n_mod = -(-(B + 1) // 8) * 8
    cvec = jnp.zeros((n_mod, D), F32).at[:B].set(c).at[ctx_row].set(c_ctx)
    cos_t, sin_t = _rope_tables()
    perm = _swa_q_perm()
    gfin = final_norm_g.reshape(1, D)
    tm = 1024
    lat_tile_row = lambda i: i // (T // tm)
    ctx_tile_row = lambda i: ctx_row
    x2 = x.reshape(B * T, D)
    xc2 = ctx.reshape(B * L, D)
    for l in range(DEPTH):
        need_ctx = l < DEPTH - 1
        mod3 = ada(cvec, w_ada, b_ada.reshape(DEPTH, 1, 6 * D), l).reshape(n_mod * 6, 1, D)
        w_in_p = _prep_w_in(w_in[l])
        g1 = norm1_g[l].reshape(1, D)
        h, p = inproj(x2, mod3, g1, w_in_p, tm, lat_tile_row)
        hc, pc = inproj(xc2, mod3, g1, w_in_p, tm, ctx_tile_row)
        p3, pc3 = p.reshape(B, T, NP), pc.reshape(B, L, NP)

        w2 = jnp.zeros((128, 512), F32).at[:GLA_RANK, :256].set(gla_w_a2[l, 0])
        w2 = w2.at[GLA_RANK:2 * GLA_RANK, 256:].set(gla_w_a2[l, 1]).astype(BF16)
        b2 = gla_b_a2[l].reshape(1, 512)
        ocf, ocb, of, ob = gla_scan(p3, pc3, w2, b2)
        gn = gla_norm_g[l].reshape(1, GLA_DV)
        ya = gla_out(of, ob, p3, gn)
        yb = swa(p3, pc3, swa_sink[l], cos_t, sin_t)
        yn = na(p3, pc3, na_bias_table(na_rpb[l]))
        ln_g, ln_b = sgu_ln_g[l].reshape(1, 512), sgu_ln_b[l].reshape(1, 512)
        ws = sgu_w_s[l].astype(BF16)
        bs = jnp.broadcast_to(sgu_b_s[l][:, :, None], (4, SGU_CHUNK, 128))
        yd = sgu(p3, ln_g, ln_b, ws, bs)

        wg = w_gate[l].astype(BF16)
        bg = b_gate[l].reshape(4, 1, D)
        wb = w_branch[l].at[1].set(w_branch[l, 1][perm]).astype(BF16)
        wo = w_out[l].astype(BF16)
        wr = jnp.pad(w_router[l], ((0, 0), (0, 128 - N_EXPERTS))).astype(BF16)
        g2 = norm2_g[l].reshape(1, D)
        flat = lambda y: y.reshape(-1, 512)
        acc = merge(h, [flat(ya), flat(yb), flat(yn), flat(yd)], wg, bg, wb, tm)
        x1, h2, lg = outproj(acc, wo, x2, mod3, g2, wr, 512, lambda i: i // (T // 512))
        x2 = _moe(h2, lg, x1.reshape(B, T, D), mod3, gfin, w_exp_gate, w_exp_up, w_exp_down, l,
                  2 * T // N_EXPERTS, 1024, lambda b: b, final=not need_ctx).reshape(B * T, D)
        if need_ctx:
            yac = gla_out(ocf, ocb, pc3, gn)
            ybc = ctx_attn(pc3, swa_sink[l], COL["swa_q"], COL["swa_k"], COL["swa_v"], 1, True)
            ync = ctx_attn(pc3, swa_sink[l], COL["na_q"], COL["na_k"], COL["na_v"], 4, False)
            ydc = sgu(pc3, ln_g, ln_b, ws, bs)
            accc = merge(hc, [flat(yac), flat(ybc), flat(ync), flat(ydc)], wg, bg, wb, tm)
            xc1, hc2, lgc = outproj(accc, wo, xc2, mod3, g2, wr, 512, ctx_tile_row)
            xc2 = _moe(hc2, lgc, xc1.reshape(B, L, D), mod3, gfin, w_exp_gate, w_exp_up, w_exp_down, l,
                       2 * L // N_EXPERTS, 512, lambda b: ctx_row, final=False).reshape(B * L, D)
    return x2.reshape(B, T, D)
```

```python
import functools

import numpy as np
import jax
import jax.numpy as jnp
from jax import lax
from jax.experimental import pallas as pl
from jax.experimental.pallas import tpu as pltpu

F32 = jnp.float32
BF16 = jnp.bfloat16
HIGHEST = lax.Precision.HIGHEST

D = 2048
T = 2048
L = 256
GRID_W = 64
ROWS = T // GRID_W
EPS = 1e-6
DEPTH = 2
NEG = -1e30

GLA_HEADS, GLA_DK, GLA_DV, GLA_RANK, GLA_TAU, GLA_CHUNK = 4, 64, 128, 16, 16.0, 64
GLA_BLOCK = 256
SWA_BLOCK, SWA_WINDOW = 128, 128
NA_KR, NA_KC, NA_HEADS = 8, 16, 8
SGU_CHUNK = 128
N_EXPERTS = 16
LANES = 128
HEAD = 64

NP = 5120
COL = dict(gla_v=0, gla_r=512, swa_q=1024, na_q=1536, na_k=2048, na_v=2560, sgu_u=3072, sgu_v=3584,
           gla_q=4096, gla_k=4352, swa_k=4608, swa_v=4736, gla_a=4864)
REF = dict(gla_q=0, gla_k=256, gla_v=512, gla_r=1024, gla_af=1536, gla_ab=1552, swa_q=1568, swa_k=2080,
           swa_v=2208, na_q=2336, na_k=2848, na_v=3360, sgu_u=3872, sgu_v=4384)

VMEM_LIMIT = 56 * 1024 * 1024


def _cp(*sem, vmem=VMEM_LIMIT):
    return pltpu.CompilerParams(dimension_semantics=sem, vmem_limit_bytes=vmem)


def _sigmoid(x):
    return 1.0 / (1.0 + jnp.exp(-x))


def _rms(x, g):
    return x * lax.rsqrt(jnp.mean(x * x, axis=-1, keepdims=True) + EPS) * g


def _dot(a, b):
    return jnp.dot(a, b, preferred_element_type=F32)


def _dot_nt(a, b):
    return lax.dot_general(a, b, (((1,), (1,)), ((), ())), preferred_element_type=F32)


def _ada_kernel(c_ref, w_ref, b_ref, o_ref):
    c = c_ref[...]
    act = (c * _sigmoid(c)).astype(BF16)
    o_ref[...] = _dot(act, w_ref[0].astype(BF16)) + b_ref[0]


def ada(cvec, w, b, l):
    rows, tn = cvec.shape[0], 1024
    return pl.pallas_call(
        _ada_kernel, grid=(6 * D // tn,),
        in_specs=[pl.BlockSpec((rows, D), lambda j: (0, 0)),
                  pl.BlockSpec((1, D, tn), lambda j: (l, 0, j)),
                  pl.BlockSpec((1, 1, tn), lambda j: (l, 0, j))],
        out_specs=pl.BlockSpec((rows, tn), lambda j: (0, j)),
        out_shape=jax.ShapeDtypeStruct((rows, 6 * D), F32),
        compiler_params=_cp("parallel"), name="ada")(cvec, w, b)


def _inproj_kernel(x_ref, shift_ref, scale_ref, g_ref, w_ref, h_ref, p_ref, hs_ref):
    @pl.when(pl.program_id(1) == 0)
    def _():
        h = (_rms(x_ref[...], g_ref[...]) * (1.0 + scale_ref[0]) + shift_ref[0]).astype(BF16)
        hs_ref[...] = h
        h_ref[...] = h

    p_ref[...] = _dot(hs_ref[...], w_ref[...]).astype(BF16)


def inproj(x2, mod3, g, w, tm, mod_row):
    M, tn = x2.shape[0], 1024
    tm = min(tm, M)
    return pl.pallas_call(
        _inproj_kernel, grid=(M // tm, NP // tn),
        in_specs=[pl.BlockSpec((tm, D), lambda i, j: (i, 0)),
                  pl.BlockSpec((1, 1, D), lambda i, j: (mod_row(i) * 6 + 0, 0, 0)),
                  pl.BlockSpec((1, 1, D), lambda i, j: (mod_row(i) * 6 + 1, 0, 0)),
                  pl.BlockSpec((1, D), lambda i, j: (0, 0)),
                  pl.BlockSpec((D, tn), lambda i, j: (0, j))],
        out_specs=[pl.BlockSpec((tm, D), lambda i, j: (i, 0)),
                   pl.BlockSpec((tm, tn), lambda i, j: (i, j))],
        out_shape=[jax.ShapeDtypeStruct((M, D), BF16), jax.ShapeDtypeStruct((M, NP), BF16)],
        scratch_shapes=[pltpu.VMEM((tm, D), BF16)],
        compiler_params=_cp("parallel", "arbitrary"), name="inproj")(x2, mod3, mod3, g, w)


def _log_sigmoid(x):
    return jnp.minimum(x, 0.0) - jnp.log(1.0 + jnp.exp(-jnp.abs(x)))


def _split3(x):
    hi = x.astype(BF16)
    r1 = x - hi.astype(F32)
    mid = r1.astype(BF16)
    lo = (r1 - mid.astype(F32)).astype(BF16)
    return jnp.concatenate([hi, mid, lo], axis=1)


def _gla_block(q, k, v, a, w2, b2, s_ref, sums, cmask2, order, half, row_lo):
    n, ch = GLA_BLOCK, GLA_CHUNK
    la = _log_sigmoid(_dot(a.astype(BF16), w2) + b2) * (1.0 / GLA_TAU)
    cum = _dot(sums, _split3(la))
    cum = cum[:, :256] + cum[:, 256:512] + cum[:, 512:]
    bc, bl = cum[:n], cum[n:]
    q, k = q.astype(F32), k.astype(F32)
    q_in = (q * GLA_DK ** -0.5) * jnp.exp(bc)
    k_in = k * jnp.exp(-bc)
    kte = k * jnp.exp(bl - bc)
    heads = [None] * GLA_HEADS
    for pt in range(2):
        sl = slice(pt * LANES, (pt + 1) * LANES)
        qt = q_in[:, sl]
        qs = jnp.concatenate([jnp.where(half == 0, qt, 0.0), jnp.where(half == 1, qt, 0.0)],
                             axis=0).astype(BF16)
        att = jnp.where(cmask2, _dot_nt(qs, k_in[:, sl].astype(BF16)), 0.0).astype(BF16)
        v2 = v[:, 2 * pt * GLA_DV:(2 * pt + 2) * GLA_DV].astype(BF16)
        intra = [_dot(att[hh * n:(hh + 1) * n], v2[:, hh * GLA_DV:(hh + 1) * GLA_DV]) for hh in range(2)]
        kte_t = kte[:, sl].T.astype(BF16)
        la_t = la[:, sl].T
        st = s_ref[sl, :]
        inter = [None] * (n // ch)
        for c in order:
            cs = slice(c * ch, (c + 1) * ch)
            qc = jnp.concatenate([qs[cs], qs[n + c * ch:n + (c + 1) * ch]], axis=0)
            inter[c] = _dot(qc, st.astype(BF16))
            upd = _dot(kte_t[:, cs], v2[cs])
            dec = jnp.exp(jnp.sum(la_t[:, cs], axis=1, keepdims=True))
            st = st * dec + jnp.where(row_lo, upd[:, :GLA_DV], upd[:, GLA_DV:])
        s_ref[sl, :] = st
        for hh in range(2):
            rs = slice(hh * HEAD, (hh + 1) * HEAD)
            heads[2 * pt + hh] = intra[hh] + jnp.concatenate([inter[c][rs] for c in range(n // ch)], axis=0)
    return jnp.concatenate(heads, axis=1)


def _gla_kernel(qc_ref, kc_ref, vc_ref, ac_ref, qf_ref, kf_ref, vf_ref, af_ref,
                qb_ref, kb_ref, vb_ref, ab_ref, w2_ref, b2_ref,
                ocf_ref, ocb_ref, of_ref, ob_ref, sf_ref, sb_ref):
    s = pl.program_id(1)
    n, ch = GLA_BLOCK, GLA_CHUNK
    ri = lax.broadcasted_iota(jnp.int32, (n, n), 0)
    ci = lax.broadcasted_iota(jnp.int32, (n, n), 1)
    same = (ri // ch) == (ci // ch)
    low, up = same & (ri >= ci), same & (ri <= ci)
    ones = same.astype(BF16)
    sums_f = jnp.concatenate([low.astype(BF16), ones], axis=0)
    sums_b = jnp.concatenate([up.astype(BF16), ones], axis=0)
    cm_f = jnp.concatenate([low, low], axis=0)
    cm_b = jnp.concatenate([up, up], axis=0)
    half = lax.broadcasted_iota(jnp.int32, (n, LANES), 1) // HEAD
    row_lo = lax.broadcasted_iota(jnp.int32, (LANES, GLA_DV), 0) < HEAD
    w2f, w2b = w2_ref[:, :256], w2_ref[:, 256:]
    b2f, b2b = b2_ref[:, :256], b2_ref[:, 256:]
    chunks = tuple(range(n // ch))

    @pl.when(s == 0)
    def _():
        sf_ref[...] = jnp.zeros_like(sf_ref)
        sb_ref[...] = jnp.zeros_like(sb_ref)

    def run(fwd, bwd, o_f, o_b):
        q, k, v, a = fwd
        o_f[0] = _gla_block(q[0], k[0], v[0], a[0], w2f, b2f, sf_ref, sums_f, cm_f, chunks, half, row_lo)
        q, k, v, a = bwd
        o_b[0] = _gla_block(q[0], k[0], v[0], a[0], w2b, b2b, sb_ref, sums_b, cm_b, chunks[::-1], half, row_lo)

    ctx = (qc_ref, kc_ref, vc_ref, ac_ref)

    @pl.when(s == 0)
    def _():
        run(ctx, ctx, ocf_ref, ocb_ref)

    @pl.when(s > 0)
    def _():
        run((qf_ref, kf_ref, vf_ref, af_ref), (qb_ref, kb_ref, vb_ref, ab_ref), of_ref, ob_ref)


def gla_scan(p3, pc3, w2, b2):
    B = p3.shape[0]
    nb = T // GLA_BLOCK
    cq, ck, cv, ca = COL["gla_q"] // 256, COL["gla_k"] // 256, COL["gla_v"] // 512, COL["gla_a"] // 128
    fwd = lambda s: jnp.maximum(s - 1, 0)
    bwd = lambda s: nb - jnp.maximum(s, 1)

    def spec(width, col, row):
        return pl.BlockSpec((1, GLA_BLOCK, width), lambda b, s: (b, row(s), col))

    zero = lambda s: 0
    in_specs = ([spec(256, cq, zero), spec(256, ck, zero), spec(512, cv, zero), spec(128, ca, zero)]
                + [spec(256, cq, fwd), spec(256, ck, fwd), spec(512, cv, fwd), spec(128, ca, fwd)]
                + [spec(256, cq, bwd), spec(256, ck, bwd), spec(512, cv, bwd), spec(128, ca, bwd)]
                + [pl.BlockSpec((128, 512), lambda b, s: (0, 0)), pl.BlockSpec((1, 512), lambda b, s: (0, 0))])
    out_specs = [spec(512, 0, zero), spec(512, 0, zero), spec(512, 0, fwd), spec(512, 0, bwd)]
    out_shape = [jax.ShapeDtypeStruct((B, L, 512), F32)] * 2 + [jax.ShapeDtypeStruct((B, T, 512), F32)] * 2
    return pl.pallas_call(
        _gla_kernel, grid=(B, nb + 1), in_specs=in_specs, out_specs=out_specs, out_shape=out_shape,
        scratch_shapes=[pltpu.VMEM((256, 128), F32), pltpu.VMEM((256, 128), F32)],
        compiler_params=_cp("parallel", "arbitrary"), name="gla_scan",
    )(pc3, pc3, pc3, pc3, p3, p3, p3, p3, p3, p3, p3, p3, w2, b2)


def _gla_gate(o, r, g):
    outs = []
    for hd in range(GLA_HEADS):
        sl = slice(hd * GLA_DV, (hd + 1) * GLA_DV)
        rh = r[:, sl]
        outs.append(_rms(o[:, sl], g) * (rh * _sigmoid(rh)))
    return jnp.concatenate(outs, axis=1)


def _stack_halves(qt, half):
    zero = jnp.zeros_like(qt)
    return jnp.concatenate([jnp.where(half == 0, qt, zero), jnp.where(half == 1, qt, zero)], axis=0).astype(BF16)


def _merge_halves(o, half):
    n = o.shape[0] // 2
    return jnp.where(half == 0, o[:n], o[n:])


def _softmax_parts(s_list, sink=None):
    m = s_list[0].max(axis=1, keepdims=True)
    for s in s_list[1:]:
        m = jnp.maximum(m, s.max(axis=1, keepdims=True))
    if sink is not None:
        m = jnp.maximum(m, sink)
    es = [jnp.exp(s - m) for s in s_list]
    den = es[0].sum(axis=1, keepdims=True)
    for e in es[1:]:
        den = den + e.sum(axis=1, keepdims=True)
    if sink is not None:
        den = den + jnp.exp(sink - m)
    return [e.astype(BF16) for e in es], 1.0 / den


def _rope(x, cos, sin, lane):
    partner = jnp.where((lane % HEAD) < HEAD // 2, pltpu.roll(x, LANES - HEAD // 2, 1),
                        pltpu.roll(x, HEAD // 2, 1))
    return x * cos + partner * sin


def _swa_kernel(sink_ref, mask_ref, q_ref, kp_ref, kc_ref, kn_ref, vp_ref, vc_ref, vn_ref, kx_ref, vx_ref,
                cos_ref, sin_ref, o_ref):
    n = pl.program_id(1)
    nb = pl.num_programs(1)
    blk = SWA_BLOCK
    lane = lax.broadcasted_iota(jnp.int32, (blk, LANES), 1)
    half = lane // HEAD

    def tab(ref, i):
        return ref[pl.ds(pl.multiple_of(i * blk, blk), blk), :]

    ip, inx = jnp.maximum(n - 1, 0), jnp.minimum(n + 1, nb - 1)
    kb = jnp.concatenate([_rope(kp_ref[0].astype(F32), tab(cos_ref, ip), tab(sin_ref, ip), lane),
                          _rope(kc_ref[0].astype(F32), tab(cos_ref, n), tab(sin_ref, n), lane),
                          _rope(kn_ref[0].astype(F32), tab(cos_ref, inx), tab(sin_ref, inx), lane)],
                         axis=0).astype(BF16)
    vb = jnp.concatenate([vp_ref[0], vc_ref[0], vn_ref[0]], axis=0)
    cos_q, sin_q = tab(cos_ref, n), tab(sin_ref, n)
    qs = jnp.concatenate([_stack_halves(_rope(q_ref[0, :, p * LANES:(p + 1) * LANES].astype(F32), cos_q, sin_q, lane)
                                        * HEAD ** -0.5, half) for p in range(4)], axis=0)
    s_loc = _dot_nt(qs, kb) + mask_ref[...]
    s_loc = jnp.concatenate([s_loc[:, :blk] + jnp.where(n == 0, NEG, 0.0), s_loc[:, blk:2 * blk],
                             s_loc[:, 2 * blk:] + jnp.where(n == nb - 1, NEG, 0.0)], axis=1)
    s_ctx = _dot_nt(qs, kx_ref[0])
    grp = lax.broadcasted_iota(jnp.int32, (8 * blk, 1), 0) // blk
    sink = jnp.zeros((8 * blk, 1), F32)
    for g in range(8):
        sink = jnp.where(grp == g, sink_ref[g // 2 + 4 * (g % 2)], sink)
    (e_loc, e_ctx), inv = _softmax_parts([s_loc, s_ctx], sink=sink)
    o = (_dot(e_loc, vb) + _dot(e_ctx, vx_ref[0])) * inv
    for p in range(4):
        o_ref[0, :, p * LANES:(p + 1) * LANES] = _merge_halves(o[2 * p * blk:2 * (p + 1) * blk], half).astype(BF16)


def swa(p3, pc3, sink, cos_t, sin_t):
    B = p3.shape[0]
    nb = T // SWA_BLOCK
    qi = np.arange(SWA_BLOCK)[:, None]
    kj = np.arange(3 * SWA_BLOCK)[None, :]
    mask = jnp.asarray(np.tile(np.where(np.abs(kj - SWA_BLOCK - qi) <= SWA_WINDOW, 0.0, NEG), (8, 1)).astype(np.float32))
    ck, cv = COL["swa_k"] // 128, COL["swa_v"] // 128
    prev = lambda n: jnp.maximum(n - 1, 0)
    cur = lambda n: n
    nxt = lambda n: jnp.minimum(n + 1, nb - 1)
    kv = lambda col, row: pl.BlockSpec((1, SWA_BLOCK, 128), lambda b, n: (b, row(n), col))
    full = lambda col: pl.BlockSpec((1, L, 128), lambda b, n: (b, 0, col))
    tab = pl.BlockSpec((T, 128), lambda b, n: (0, 0))
    return pl.pallas_call(
        _swa_kernel, grid=(B, nb),
        in_specs=[pl.BlockSpec(memory_space=pltpu.SMEM),
                  pl.BlockSpec((8 * SWA_BLOCK, 3 * SWA_BLOCK), lambda b, n: (0, 0)),
                  pl.BlockSpec((1, SWA_BLOCK, 512), lambda b, n: (b, n, COL["swa_q"] // 512)),
                  kv(ck, prev), kv(ck, cur), kv(ck, nxt), kv(cv, prev), kv(cv, cur), kv(cv, nxt),
                  full(ck), full(cv), tab, tab],
        out_specs=pl.BlockSpec((1, SWA_BLOCK, 512), lambda b, n: (b, n, 0)),
        out_shape=jax.ShapeDtypeStruct((B, T, 512), BF16),
        compiler_params=_cp("parallel", "parallel"), name="swa",
    )(sink, mask, p3, p3, p3, p3, p3, p3, p3, pc3, pc3, cos_t, sin_t)


def _ctx_attn_kernel(sink_ref, q_ref, k_ref, v_ref, o_ref, *, use_sink, kv_tiles):
    n = q_ref.shape[1]
    half = lax.broadcasted_iota(jnp.int32, (n, LANES), 1) // HEAD
    for p in range(4):
        sl = slice(p * LANES, (p + 1) * LANES)
        ks = sl if kv_tiles == 4 else slice(0, LANES)
        qs = _stack_halves(q_ref[0, :, sl] * HEAD ** -0.5, half)
        sink = None
        if use_sink:
            top = lax.broadcasted_iota(jnp.int32, (2 * n, 1), 0) < n
            sink = jnp.where(top, sink_ref[p], sink_ref[p + 4])
        (e,), inv = _softmax_parts([_dot_nt(qs, k_ref[0, :, ks])], sink=sink)
        o = _dot(e, v_ref[0, :, ks]) * inv
        o_ref[0, :, sl] = _merge_halves(o, half).astype(BF16)


def ctx_attn(pc3, sink, qcol, kcol, vcol, kv_tiles, use_sink):
    B, w = pc3.shape[0], kv_tiles * 128
    return pl.pallas_call(
        functools.partial(_ctx_attn_kernel, use_sink=use_sink, kv_tiles=kv_tiles), grid=(B,),
        in_specs=[pl.BlockSpec(memory_space=pltpu.SMEM),
                  pl.BlockSpec((1, L, 512), lambda b: (b, 0, qcol // 512)),
                  pl.BlockSpec((1, L, w), lambda b: (b, 0, kcol // w)),
                  pl.BlockSpec((1, L, w), lambda b: (b, 0, vcol // w))],
        out_specs=pl.BlockSpec((1, L, 512), lambda b: (b, 0, 0)),
        out_shape=jax.ShapeDtypeStruct((B, L, 512), BF16),
        compiler_params=_cp("parallel"), name="ctx_attn")(sink, pc3, pc3, pc3)


def _na_kernel(q_ref, k_ref, v_ref, kx_ref, vx_ref, *rest):
    bias_refs, o_ref = rest[:-1], rest[-1]
    nr = len(bias_refs)
    i = pl.program_id(1)
    band = NA_KR * GRID_W
    half = lax.broadcasted_iota(jnp.int32, (GRID_W, LANES), 1) // HEAD
    starts = [pl.multiple_of(jnp.clip(nr * i + k - NA_KR // 2, 0, ROWS - NA_KR) * GRID_W, GRID_W)
              for k in range(nr)]
    s_loc, s_ctx = [], []
    for p in range(4):
        sl = slice(p * LANES, (p + 1) * LANES)
        qs = [_stack_halves(q_ref[0, k * GRID_W:(k + 1) * GRID_W, sl] * HEAD ** -0.5, half) for k in range(nr)]
        for k in range(nr):
            kb = k_ref[0, pl.ds(starts[k], band), sl]
            s_loc.append(_dot_nt(qs[k], kb) + bias_refs[k][0, 2 * p * GRID_W:(2 * p + 2) * GRID_W, :])
        s_ctx.append(_dot_nt(jnp.concatenate(qs, axis=0), kx_ref[0, :, sl]))
    (e_loc, e_ctx), inv = _softmax_parts([jnp.concatenate(s_loc, axis=0), jnp.concatenate(s_ctx, axis=0)])
    blk = 2 * GRID_W
    for p in range(4):
        sl = slice(p * LANES, (p + 1) * LANES)
        r0 = p * nr * blk
        o_ctx = _dot(e_ctx[r0:r0 + nr * blk], vx_ref[0, :, sl])
        for k in range(nr):
            rs = slice(r0 + k * blk, r0 + (k + 1) * blk)
            vb = v_ref[0, pl.ds(starts[k], band), sl]
            o = (_dot(e_loc[rs], vb) + o_ctx[k * blk:(k + 1) * blk]) * inv[rs]
            o_ref[0, k * GRID_W:(k + 1) * GRID_W, sl] = _merge_halves(o, half).astype(BF16)


def na(p3, pc3, bias, rows_per_step=4):
    B, nr = p3.shape[0], rows_per_step
    cq, ck, cv = COL["na_q"] // 512, COL["na_k"] // 512, COL["na_v"] // 512
    shift = lambda r: r - jnp.clip(r - NA_KR // 2, 0, ROWS - NA_KR)
    bias_spec = lambda k: pl.BlockSpec((1, NA_HEADS * GRID_W, NA_KR * GRID_W),
                                       lambda b, i: (shift(nr * i + k), 0, 0))
    return pl.pallas_call(
        _na_kernel, grid=(B, ROWS // nr),
        in_specs=[pl.BlockSpec((1, nr * GRID_W, 512), lambda b, i: (b, i, cq)),
                  pl.BlockSpec((1, T, 512), lambda b, i: (b, 0, ck)),
                  pl.BlockSpec((1, T, 512), lambda b, i: (b, 0, cv)),
                  pl.BlockSpec((1, L, 512), lambda b, i: (b, 0, ck)),
                  pl.BlockSpec((1, L, 512), lambda b, i: (b, 0, cv))] + [bias_spec(k) for k in range(nr)],
        out_specs=pl.BlockSpec((1, nr * GRID_W, 512), lambda b, i: (b, i, 0)),
        out_shape=jax.ShapeDtypeStruct((B, T, 512), BF16),
        compiler_params=_cp("parallel", "arbitrary"), name="na")(p3, p3, p3, pc3, pc3, *([bias] * nr))


def na_bias_table(rpb):
    col = np.arange(GRID_W)
    col_start = np.clip(col - NA_KC // 2, 0, GRID_W - NA_KC)
    col_ok = (col[None, :] >= col_start[:, None]) & (col[None, :] < col_start[:, None] + NA_KC)
    col_off = np.clip(col[None, :] - col[:, None], -(NA_KC - 1), NA_KC - 1) + (NA_KC - 1)
    onehot = (col_off[:, :, None] == np.arange(2 * NA_KC - 1)).astype(np.float32)
    toep = jnp.einsum("hro,qko->hrqk", rpb, jnp.asarray(onehot), precision=HIGHEST)
    toep = jnp.where(jnp.asarray(col_ok)[None, None], toep, NEG)
    per_shift = [toep[:, NA_KR - 1 - s:2 * NA_KR - 1 - s] for s in range(NA_KR)]
    tab = jnp.stack(per_shift).transpose(0, 1, 3, 2, 4)
    return tab.reshape(NA_KR, NA_HEADS * GRID_W, NA_KR * GRID_W).astype(F32)


def _gelu(x):
    return 0.5 * x * (1.0 + jnp.tanh(np.sqrt(2.0 / np.pi).astype(np.float32) * (x + 0.044715 * (x * x * x))))


def _sgu_kernel(u_ref, v_ref, g_ref, b_ref, ws_ref, bs_ref, y_ref):
    n_chunk = u_ref.shape[1] // SGU_CHUNK
    v = _gelu(v_ref[0].astype(F32))
    mu = jnp.mean(v, axis=-1, keepdims=True)
    var = jnp.mean(jnp.square(v - mu), axis=-1, keepdims=True)
    vn = ((v - mu) * lax.rsqrt(var + EPS) * g_ref[...] + b_ref[...]).astype(BF16)
    cols = []
    for g in range(4):
        gs = slice(g * 128, (g + 1) * 128)
        rhs = jnp.concatenate([vn[c * SGU_CHUNK:(c + 1) * SGU_CHUNK, gs] for c in range(n_chunk)], axis=1)
        mixed = _dot(ws_ref[g], rhs)
        cols.append(jnp.concatenate([mixed[:, c * 128:(c + 1) * 128] + bs_ref[g] for c in range(n_chunk)], axis=0))
    y_ref[0] = (_gelu(u_ref[0].astype(F32)) * jnp.concatenate(cols, axis=1)).astype(BF16)


def sgu(p3, g, b, ws, bs):
    B, n = p3.shape[0], p3.shape[1]
    tr = min(n, 512)
    blk = lambda col: pl.BlockSpec((1, tr, 512), lambda bb, i: (bb, i, col))
    const = lambda shape: pl.BlockSpec(shape, lambda bb, i: (0,) * len(shape))
    return pl.pallas_call(
        _sgu_kernel, grid=(B, n // tr),
        in_specs=[blk(COL["sgu_u"] // 512), blk(COL["sgu_v"] // 512), const((1, 512)), const((1, 512)),
                  const((4, 128, 128)), const((4, 128, 128))],
        out_specs=blk(0), out_shape=jax.ShapeDtypeStruct((B, n, 512), BF16),
        compiler_params=_cp("parallel", "parallel"), name="sgu")(p3, p3, g, b, ws, bs)


def _merge_kernel(h_ref, of_ref, ob_ref, r_ref, gn_ref, b_ref, n_ref, d_ref, wg_ref, bg_ref, wb_ref, o_ref,
                  a_ref):
    @pl.when(pl.program_id(1) == 0)
    def _():
        a_ref[...] = _gla_gate(of_ref[...] + ob_ref[...], r_ref[...].astype(F32), gn_ref[...]).astype(BF16)

    h = h_ref[...]
    acc = None
    for i, br in enumerate((a_ref, b_ref, n_ref, d_ref)):
        term = _sigmoid(_dot(h, wg_ref[i]) + bg_ref[i]) * _dot(br[...], wb_ref[i])
        acc = term if acc is None else acc + term
    o_ref[...] = acc.astype(BF16)


def merge(h, o_f, o_b, p, gn, branches, wg, bg, wb, tm):
    M, tn = h.shape[0], 256
    tm = min(tm, M)
    br = pl.BlockSpec((tm, 512), lambda i, j: (i, 0))
    return pl.pallas_call(
        _merge_kernel, grid=(M // tm, D // tn),
        in_specs=[pl.BlockSpec((tm, D), lambda i, j: (i, 0)), br, br,
                  pl.BlockSpec((tm, 512), lambda i, j: (i, COL["gla_r"] // 512)),
                  pl.BlockSpec((1, GLA_DV), lambda i, j: (0, 0)), br, br, br,
                  pl.BlockSpec((4, D, tn), lambda i, j: (0, 0, j)),
                  pl.BlockSpec((4, 1, tn), lambda i, j: (0, 0, j)),
                  pl.BlockSpec((4, 512, tn), lambda i, j: (0, 0, j))],
        out_specs=pl.BlockSpec((tm, tn), lambda i, j: (i, j)),
        out_shape=jax.ShapeDtypeStruct((M, D), BF16),
        scratch_shapes=[pltpu.VMEM((tm, 512), BF16)],
        compiler_params=_cp("parallel", "arbitrary"), name="merge")(h, o_f, o_b, p, gn, *branches, wg, bg, wb)


def _outproj_kernel(acc_ref, w_ref, x_ref, gate_ref, g_ref, shift_ref, scale_ref, wr_ref,
                    x1_ref, h2_ref, lg_ref):
    n_sub = 2
    sub = acc_ref.shape[0] // n_sub
    for k in range(n_sub):
        rs = slice(k * sub, (k + 1) * sub)
        x1 = x_ref[rs, :] + gate_ref[0] * _dot(acc_ref[rs, :], w_ref[...])
        x1_ref[rs, :] = x1
        h2 = (_rms(x1, g_ref[...]) * (1.0 + scale_ref[0]) + shift_ref[0]).astype(BF16)
        lg_ref[rs, :] = _dot(h2, wr_ref[...])
        bits = pltpu.bitcast(h2.astype(F32), jnp.int32)
        h2_ref[rs, :] = lax.shift_right_logical(bits[:, :D // 2], 16) | (bits[:, D // 2:] & jnp.int32(-65536))


def outproj(acc, w, x2, mod3, g, wr, tm, mod_row):
    M = acc.shape[0]
    tm = min(tm, M)
    row = lambda k: pl.BlockSpec((1, 1, D), lambda i: (mod_row(i) * 6 + k, 0, 0))
    tile = pl.BlockSpec((tm, D), lambda i: (i, 0))
    return pl.pallas_call(
        _outproj_kernel, grid=(M // tm,),
        in_specs=[tile, pl.BlockSpec((D, D), lambda i: (0, 0)), tile, row(2),
                  pl.BlockSpec((1, D), lambda i: (0, 0)), row(3), row(4),
                  pl.BlockSpec((D, 128), lambda i: (0, 0))],
        out_specs=[tile, pl.BlockSpec((tm, D // 2), lambda i: (i, 0)), pl.BlockSpec((tm, 128), lambda i: (i, 0))],
        out_shape=[jax.ShapeDtypeStruct((M, D), F32), jax.ShapeDtypeStruct((M, D // 2), jnp.int32),
                   jax.ShapeDtypeStruct((M, 128), F32)],
        compiler_params=_cp("parallel"), name="outproj")(acc, w, x2, mod3, g, mod3, mod3, wr)


def _select_kernel(lg_ref, sel_ref, aff_ref, slot_ref, *, cap):
    n = lg_ref.shape[1]
    nt = n // LANES
    lane = lax.broadcasted_iota(jnp.int32, (n, LANES), 1)
    lg = jnp.where(lane < N_EXPERTS, lg_ref[0], NEG)
    e = jnp.exp(lg - lg.max(axis=1, keepdims=True))
    aff = (e / e.sum(axis=1, keepdims=True)).T[:N_EXPERTS]
    aff_ref[0] = aff
    bits = pltpu.bitcast(aff, jnp.int32)

    def count(mask):
        return jnp.sum(jnp.where(mask, 1.0, 0.0), axis=1, keepdims=True)

    def bit_step(i, thr):
        cand = thr | lax.shift_left(jnp.int32(1), 30 - i)
        return jnp.where(count(bits >= cand) >= cap, cand, thr)

    thr = lax.fori_loop(0, 31, bit_step, jnp.zeros((N_EXPERTS, 1), jnp.int32))
    gt, eq = bits > thr, bits == thr

    r = lax.broadcasted_iota(jnp.int32, (LANES, LANES), 0)
    c = lax.broadcasted_iota(jnp.int32, (LANES, LANES), 1)
    upper = (r <= c).astype(BF16)
    r2 = lax.broadcasted_iota(jnp.int32, (nt * N_EXPERTS, nt * N_EXPERTS), 0)
    c2 = lax.broadcasted_iota(jnp.int32, (nt * N_EXPERTS, nt * N_EXPERTS), 1)
    before = ((r2 % N_EXPERTS == c2 % N_EXPERTS) & (c2 // N_EXPERTS < r2 // N_EXPERTS)).astype(BF16)

    def excl_prefix(x):
        xs = jnp.concatenate([x[:, i * LANES:(i + 1) * LANES] for i in range(nt)], axis=0).astype(BF16)
        inc = _dot(xs, upper)
        tot = jnp.broadcast_to(inc[:, LANES - 1:LANES], inc.shape).astype(BF16)
        full = inc + _dot(before, tot)
        return jnp.concatenate([full[i * N_EXPERTS:(i + 1) * N_EXPERTS] for i in range(nt)], axis=1) - x

    eq01 = jnp.where(eq, 1.0, 0.0)
    mask = gt | (eq & (excl_prefix(eq01) < cap - count(gt)))
    slot = jnp.where(mask, excl_prefix(jnp.where(mask, 1.0, 0.0)), -1.0)
    sel_ref[0] = slot.astype(jnp.int32)
    pad = jnp.full((LANES - N_EXPERTS, n), -1.0, F32)
    slot_ref[0] = jnp.concatenate([slot, pad], axis=0).T


def select(lg3, cap):
    B, n = lg3.shape[0], lg3.shape[1]
    tok = pl.BlockSpec((1, n, 128), lambda b: (b, 0, 0))
    row = pl.BlockSpec((1, N_EXPERTS, n), lambda b: (b, 0, 0))
    return pl.pallas_call(
        functools.partial(_select_kernel, cap=cap), grid=(B,),
        in_specs=[tok], out_specs=[row, row, tok],
        out_shape=[jax.ShapeDtypeStruct((B, N_EXPERTS, n), jnp.int32),
                   jax.ShapeDtypeStruct((B, N_EXPERTS, n), F32), jax.ShapeDtypeStruct((B, n, 128), F32)],
        compiler_params=_cp("parallel"), name="select")(lg3)


def _slot_tokens_kernel(sel_ref, aff_ref, idx_ref, g_ref, *, cap):
    n = sel_ref.shape[2]
    row = lax.broadcasted_iota(jnp.int32, (cap, n), 0)
    tok = lax.broadcasted_iota(jnp.int32, (cap, n), 1).astype(F32)
    for e in range(N_EXPERTS):
        hit = row == sel_ref[0, e:e + 1, :]
        idx = jnp.sum(jnp.where(hit, tok, 0.0), axis=1, keepdims=True).astype(jnp.int32)
        idx_ref[e, 0] = idx + pl.program_id(0) * n
        g_ref[e, 0] = jnp.sum(jnp.where(hit, aff_ref[0, e:e + 1, :], 0.0), axis=1, keepdims=True)


def slot_tokens(sel, aff, cap):
    B, n = sel.shape[0], sel.shape[2]
    row = pl.BlockSpec((1, N_EXPERTS, n), lambda b: (b, 0, 0))
    col = pl.BlockSpec((N_EXPERTS, 1, cap, 1), lambda b: (0, b, 0, 0))
    idx, gate = pl.pallas_call(
        functools.partial(_slot_tokens_kernel, cap=cap), grid=(B,),
        in_specs=[row, row], out_specs=[col, col],
        out_shape=[jax.ShapeDtypeStruct((N_EXPERTS, B, cap, 1), jnp.int32),
                   jax.ShapeDtypeStruct((N_EXPERTS, B, cap, 1), F32)],
        compiler_params=_cp("parallel"), name="slot_tokens")(sel, aff)
    return idx.reshape(N_EXPERTS, B * cap, 1), gate.reshape(N_EXPERTS, B * cap, 1)


def _expert_kernel(idx_ref, h_hbm, gate_ref, wg_ref, wu_ref, wd_ref, o_ref, xg_ref, xb_ref, acc_ref, sem,
                   *, tm):
    e, i, f = pl.program_id(0), pl.program_id(1), pl.program_id(2)
    n_i, n_f = pl.num_programs(1), pl.num_programs(2)
    q = e * n_i + i
    n_q = pl.num_programs(0) * n_i
    slot = q % 2
    share = tm // n_f

    def row_copy(tile, r, dst_slot):
        src = h_hbm.at[pl.ds(idx_ref[tile * tm + r], 1), :]
        return pltpu.make_async_copy(src, xg_ref.at[dst_slot, pl.ds(r, 1), :], sem.at[dst_slot])

    def wait_tile(dst_slot):
        pltpu.make_async_copy(h_hbm.at[pl.ds(0, tm), :], xg_ref.at[dst_slot], sem.at[dst_slot]).wait()

    @pl.when((q == 0) & (f == 0))
    def _():
        def first(r, carry):
            row_copy(0, r, 0).start()
            return carry
        lax.fori_loop(0, tm, first, 0)

    @pl.when(f == 0)
    def _():
        wait_tile(slot)
        p = xg_ref[slot]
        xb_ref[:, :D // 2] = pltpu.bitcast(lax.shift_left(p, 16), F32).astype(BF16)
        xb_ref[:, D // 2:] = pltpu.bitcast(p & jnp.int32(-65536), F32).astype(BF16)
        acc_ref[...] = jnp.zeros_like(acc_ref)

    x = xb_ref[...]
    g = _dot(x, wg_ref[0, 0].astype(BF16))
    u = _dot(x, wu_ref[0, 0].astype(BF16))
    hid = ((g * _sigmoid(g)) * u).astype(BF16)
    nxt = jnp.minimum(q + 1, n_q - 1)
    for r in range(share):
        row_copy(nxt, f * share + r, 1 - slot).start()
    acc_ref[...] += _dot(hid, wd_ref[0, 0].astype(BF16))

    @pl.when(f == n_f - 1)
    def _():
        o_ref[0] = (acc_ref[...] * gate_ref[0]).astype(BF16)

    @pl.when((q == n_q - 1) & (f == n_f - 1))
    def _():
        wait_tile(1 - slot)


def experts(idx, h2p, gate, wg, wu, wd, l, tm):
    E, M, tf = gate.shape[0], gate.shape[1], 256
    F = wg.shape[3]
    tm = min(tm, M)
    grid_spec = pltpu.PrefetchScalarGridSpec(
        num_scalar_prefetch=1, grid=(E, M // tm, F // tf),
        in_specs=[pl.BlockSpec(memory_space=pl.ANY),
                  pl.BlockSpec((1, tm, 1), lambda e, i, f, idx: (e, i, 0)),
                  pl.BlockSpec((1, 1, D, tf), lambda e, i, f, idx: (l, e, 0, f)),
                  pl.BlockSpec((1, 1, D, tf), lambda e, i, f, idx: (l, e, 0, f)),
                  pl.BlockSpec((1, 1, tf, D), lambda e, i, f, idx: (l, e, f, 0))],
        out_specs=pl.BlockSpec((1, tm, D), lambda e, i, f, idx: (e, i, 0)),
        scratch_shapes=[pltpu.VMEM((2, tm, D // 2), jnp.int32), pltpu.VMEM((tm, D), BF16),
                        pltpu.VMEM((tm, D), F32), pltpu.SemaphoreType.DMA((2,))])
    return pl.pallas_call(
        functools.partial(_expert_kernel, tm=tm), grid_spec=grid_spec,
        out_shape=jax.ShapeDtypeStruct((E, M, D), BF16),
        compiler_params=_cp("arbitrary", "arbitrary", "arbitrary"), name="experts")(idx, h2p, gate, wg, wu, wd)


COMBINE_GROUP = 8


def _combine_kernel(slot_ref, ye_ref, x_ref, gate_ref, gfin_ref, o_ref, acc_ref, *, cap, final):
    g = pl.program_id(2)
    tt = x_ref.shape[1]
    lane = lax.broadcasted_iota(jnp.int32, (tt, LANES), 1)
    j = lax.broadcasted_iota(jnp.int32, (tt, cap), 1).astype(F32)
    slots = slot_ref[0]
    parts = []
    for k in range(COMBINE_GROUP):
        slot = jnp.sum(jnp.where(lane == g * COMBINE_GROUP + k, slots, 0.0), axis=1, keepdims=True)
        parts.append(jnp.where(j == slot, 1.0, 0.0).astype(BF16))
    onehot = jnp.concatenate(parts, axis=1)
    contrib = _dot(onehot, ye_ref[:, 0].reshape(COMBINE_GROUP * cap, D))

    @pl.when(g == 0)
    def _():
        acc_ref[...] = contrib

    @pl.when(g == pl.num_programs(2) - 1)
    def _():
        x2 = x_ref[0] + gate_ref[0] * (acc_ref[...] + contrib)
        if final:
            x2 = _rms(x2, gfin_ref[...])
        o_ref[0] = x2


def combine(slot_tok, ye, x3, mod3, gfin, cap, mod_row, final):
    B, n = x3.shape[0], x3.shape[1]
    tt = min(n, 512)
    assert N_EXPERTS == 2 * COMBINE_GROUP
    xt = pl.BlockSpec((1, tt, D), lambda b, t, g: (b, t, 0))
    return pl.pallas_call(
        functools.partial(_combine_kernel, cap=cap, final=final), grid=(B, n // tt, N_EXPERTS // COMBINE_GROUP),
        in_specs=[pl.BlockSpec((1, tt, 128), lambda b, t, g: (b, t, 0)),
                  pl.BlockSpec((COMBINE_GROUP, 1, cap, D), lambda b, t, g: (g, b, 0, 0)), xt,
                  pl.BlockSpec((1, 1, D), lambda b, t, g: (mod_row(b) * 6 + 5, 0, 0)),
                  pl.BlockSpec((1, D), lambda b, t, g: (0, 0))],
        out_specs=xt, out_shape=jax.ShapeDtypeStruct((B, n, D), F32),
        scratch_shapes=[pltpu.VMEM((tt, D), F32)],
        compiler_params=_cp("parallel", "parallel", "arbitrary"), name="combine",
    )(slot_tok, ye.reshape(N_EXPERTS, B, cap, D), x3, mod3, gfin)


def _swa_q_perm():
    return np.concatenate([np.arange(HEAD) + (p + 4 * hh) * HEAD for p in range(4) for hh in range(2)])


def _prep_w_in(w):
    seg = lambda name, width: w[:, REF[name]:REF[name] + width]
    swa_q = seg("swa_q", 512)[:, _swa_q_perm()]
    parts = [seg("gla_v", 512), seg("gla_r", 512), swa_q, seg("na_q", 512), seg("na_k", 512), seg("na_v", 512),
             seg("sgu_u", 512), seg("sgu_v", 512), seg("gla_q", 256), seg("gla_k", 256), seg("swa_k", 128),
             seg("swa_v", 128), seg("gla_af", 16), seg("gla_ab", 16)]
    out = jnp.concatenate(parts, axis=1)
    return jnp.pad(out, ((0, 0), (0, NP - out.shape[1]))).astype(BF16)


def _rope_tables():
    quarter = HEAD // 4
    freqs = 10000.0 ** (-jnp.arange(quarter, dtype=F32) / quarter)
    t = jnp.arange(T)
    row = (t // GRID_W).astype(F32)
    col = (t % GRID_W).astype(F32)
    ang = jnp.concatenate([row[:, None] * freqs, col[:, None] * freqs], axis=-1)
    cos, sin = jnp.cos(ang), jnp.sin(ang)
    cos_t = jnp.tile(cos, (1, 4))
    sin_t = jnp.tile(jnp.concatenate([-sin, sin], axis=-1), (1, 2))
    return cos_t, sin_t


def _moe(h2, lg, x3, mod3, gfin, wg, wu, wd, l, cap, tm, mod_row, final):
    B, n = x3.shape[0], x3.shape[1]
    sel, aff, slot_tok = select(lg.reshape(B, n, 128), cap)
    idx, gate = slot_tokens(sel, aff, cap)
    ye = experts(idx.reshape(-1), h2, gate, wg, wu, wd, l, tm)
    return combine(slot_tok, ye, x3, mod3, gfin, cap, mod_row, final)


def kernel(x, c, ctx, c_ctx, w_ada, b_ada, norm1_g, norm2_g, w_in, gla_w_a2, gla_b_a2, gla_norm_g, swa_sink,
           na_rpb, sgu_ln_g, sgu_ln_b, sgu_w_s, sgu_b_s, w_gate, b_gate, w_branch, w_out, w_router,
           w_exp_gate, w_exp_up, w_exp_down, final_norm_g):
    B = x.shape[0]
    ctx_row = B
    n_mod = -(-(B + 1) // 8) * 8
    cvec = jnp.zeros((n_mod, D), F32).at[:B].set(c).at[ctx_row].set(c_ctx)
    cos_t, sin_t = _rope_tables()
    perm = _swa_q_perm()
    gfin = final_norm_g.reshape(1, D)
    tm = 1024
    lat_tile_row = lambda i: i // (T // tm)
    ctx_tile_row = lambda i: ctx_row
    x2 = x.reshape(B * T, D)
    xc2 = ctx.reshape(B * L, D)
    for l in range(DEPTH):
        need_ctx = l < DEPTH - 1
        mod3 = ada(cvec, w_ada, b_ada.reshape(DEPTH, 1, 6 * D), l).reshape(n_mod * 6, 1, D)
        w_in_p = _prep_w_in(w_in[l])
        g1 = norm1_g[l].reshape(1, D)
        h, p = inproj(x2, mod3, g1, w_in_p, tm, lat_tile_row)
        hc, pc = inproj(xc2, mod3, g1, w_in_p, tm, ctx_tile_row)
        p3, pc3 = p.reshape(B, T, NP), pc.reshape(B, L, NP)

        w2 = jnp.zeros((128, 512), F32).at[:GLA_RANK, :256].set(gla_w_a2[l, 0])
        w2 = w2.at[GLA_RANK:2 * GLA_RANK, 256:].set(gla_w_a2[l, 1]).astype(BF16)
        b2 = gla_b_a2[l].reshape(1, 512)
        ocf, ocb, of, ob = gla_scan(p3, pc3, w2, b2)
        gn = gla_norm_g[l].reshape(1, GLA_DV)
        yb = swa(p3, pc3, swa_sink[l], cos_t, sin_t)
        yn = na(p3, pc3, na_bias_table(na_rpb[l]))
        ln_g, ln_b = sgu_ln_g[l].reshape(1, 512), sgu_ln_b[l].reshape(1, 512)
        ws = sgu_w_s[l].astype(BF16)
        bs = jnp.broadcast_to(sgu_b_s[l][:, :, None], (4, SGU_CHUNK, 128))
        yd = sgu(p3, ln_g, ln_b, ws, bs)

        wg = w_gate[l].astype(BF16)
        bg = b_gate[l].reshape(4, 1, D)
        wb = w_branch[l].at[1].set(w_branch[l, 1][perm]).astype(BF16)
        wo = w_out[l].astype(BF16)
        wr = jnp.pad(w_router[l], ((0, 0), (0, 128 - N_EXPERTS))).astype(BF16)
        g2 = norm2_g[l].reshape(1, D)
        flat = lambda y: y.reshape(-1, 512)
        acc = merge(h, flat(of), flat(ob), p, gn, [flat(yb), flat(yn), flat(yd)], wg, bg, wb, tm)
        x1, h2, lg = outproj(acc, wo, x2, mod3, g2, wr, 512, lambda i: i // (T // 512))
        x2 = _moe(h2, lg, x1.reshape(B, T, D), mod3, gfin, w_exp_gate, w_exp_up, w_exp_down, l,
                  2 * T // N_EXPERTS, 1024, lambda b: b, final=not need_ctx).reshape(B * T, D)
        if need_ctx:
            ybc = ctx_attn(pc3, swa_sink[l], COL["swa_q"], COL["swa_k"], COL["swa_v"], 1, True)
            ync = ctx_attn(pc3, swa_sink[l], COL["na_q"], COL["na_k"], COL["na_v"], 4, False)
            ydc = sgu(pc3, ln_g, ln_b, ws, bs)
            accc = merge(hc, flat(ocf), flat(ocb), pc, gn, [flat(ybc), flat(ync), flat(ydc)], wg, bg, wb, tm)
            xc1, hc2, lgc = outproj(accc, wo, xc2, mod3, g2, wr, 512, ctx_tile_row)
            xc2 = _moe(hc2, lgc, xc1.reshape(B, L, D), mod3, gfin, w_exp_gate, w_exp_up, w_exp_down, l,
                       2 * L // N_EXPERTS, 512, lambda b: ctx_row, final=False).reshape(B * L, D)
    return x2.reshape(B, T, D)
```

```python
import functools

import numpy as np
import jax
import jax.numpy as jnp
from jax import lax
from jax.experimental import pallas as pl
from jax.experimental.pallas import tpu as pltpu

F32 = jnp.float32
BF16 = jnp.bfloat16
HIGHEST = lax.Precision.HIGHEST

D = 2048
T = 2048
L = 256
GRID_W = 64
ROWS = T // GRID_W
EPS = 1e-6
DEPTH = 2
NEG = -1e30

GLA_HEADS, GLA_DK, GLA_DV, GLA_RANK, GLA_TAU, GLA_CHUNK = 4, 64, 128, 16, 16.0, 64
GLA_BLOCK = 256
SWA_BLOCK, SWA_WINDOW = 128, 128
NA_KR, NA_KC, NA_HEADS = 8, 16, 8
SGU_CHUNK = 128
N_EXPERTS = 16
LANES = 128
HEAD = 64

NP = 5120
COL = dict(gla_v=0, gla_r=512, swa_q=1024, na_q=1536, na_k=2048, na_v=2560, sgu_u=3072, sgu_v=3584,
           gla_q=4096, gla_k=4352, swa_k=4608, swa_v=4736, gla_a=4864)
REF = dict(gla_q=0, gla_k=256, gla_v=512, gla_r=1024, gla_af=1536, gla_ab=1552, swa_q=1568, swa_k=2080,
           swa_v=2208, na_q=2336, na_k=2848, na_v=3360, sgu_u=3872, sgu_v=4384)

VMEM_LIMIT = 56 * 1024 * 1024
SUBLANES = 8

ROW_TILE = 1024
OUTPROJ_ROW_TILE = 512
ADA_COL_TILE = 1024
INPROJ_COL_TILE = 1024
MERGE_COL_TILE = 256
EXPERT_HIDDEN_TILE = 256
TOKEN_TILE = 512
NA_ROWS_PER_STEP = 4


def _cp(*sem, vmem=VMEM_LIMIT):
    return pltpu.CompilerParams(dimension_semantics=sem, vmem_limit_bytes=vmem)


def _sigmoid(x):
    return 1.0 / (1.0 + jnp.exp(-x))


def _rms(x, g):
    return x * lax.rsqrt(jnp.mean(x * x, axis=-1, keepdims=True) + EPS) * g


def _dot(a, b):
    return jnp.dot(a, b, preferred_element_type=F32)


def _dot_nt(a, b):
    return lax.dot_general(a, b, (((1,), (1,)), ((), ())), preferred_element_type=F32)


def _ada_kernel(c_ref, w_ref, b_ref, o_ref):
    c = c_ref[...]
    act = (c * _sigmoid(c)).astype(BF16)
    o_ref[...] = _dot(act, w_ref[0].astype(BF16)) + b_ref[0]


def ada(cvec, w, b, l):
    rows, tn = cvec.shape[0], ADA_COL_TILE
    return pl.pallas_call(
        _ada_kernel, grid=(6 * D // tn,),
        in_specs=[pl.BlockSpec((rows, D), lambda j: (0, 0)),
                  pl.BlockSpec((1, D, tn), lambda j: (l, 0, j)),
                  pl.BlockSpec((1, 1, tn), lambda j: (l, 0, j))],
        out_specs=pl.BlockSpec((rows, tn), lambda j: (0, j)),
        out_shape=jax.ShapeDtypeStruct((rows, 6 * D), F32),
        compiler_params=_cp("parallel"), name="ada")(cvec, w, b)


def _inproj_kernel(x_ref, shift_ref, scale_ref, g_ref, w_ref, h_ref, p_ref, hs_ref):
    @pl.when(pl.program_id(1) == 0)
    def _():
        h = (_rms(x_ref[...], g_ref[...]) * (1.0 + scale_ref[0]) + shift_ref[0]).astype(BF16)
        hs_ref[...] = h
        h_ref[...] = h

    p_ref[...] = _dot(hs_ref[...], w_ref[...]).astype(BF16)


def inproj(x2, mod3, g, w, tm, mod_row):
    M, tn = x2.shape[0], INPROJ_COL_TILE
    tm = min(tm, M)
    return pl.pallas_call(
        _inproj_kernel, grid=(M // tm, NP // tn),
        in_specs=[pl.BlockSpec((tm, D), lambda i, j: (i, 0)),
                  pl.BlockSpec((1, 1, D), lambda i, j: (mod_row(i) * 6 + 0, 0, 0)),
                  pl.BlockSpec((1, 1, D), lambda i, j: (mod_row(i) * 6 + 1, 0, 0)),
                  pl.BlockSpec((1, D), lambda i, j: (0, 0)),
                  pl.BlockSpec((D, tn), lambda i, j: (0, j))],
        out_specs=[pl.BlockSpec((tm, D), lambda i, j: (i, 0)),
                   pl.BlockSpec((tm, tn), lambda i, j: (i, j))],
        out_shape=[jax.ShapeDtypeStruct((M, D), BF16), jax.ShapeDtypeStruct((M, NP), BF16)],
        scratch_shapes=[pltpu.VMEM((tm, D), BF16)],
        compiler_params=_cp("parallel", "arbitrary"), name="inproj")(x2, mod3, mod3, g, w)


def _log_sigmoid(x):
    return jnp.minimum(x, 0.0) - jnp.log(1.0 + jnp.exp(-jnp.abs(x)))


def _split3(x):
    hi = x.astype(BF16)
    r1 = x - hi.astype(F32)
    mid = r1.astype(BF16)
    lo = (r1 - mid.astype(F32)).astype(BF16)
    return jnp.concatenate([hi, mid, lo], axis=1)


def _gla_block(q, k, v, a, w2, b2, s_ref, sums, cmask2, order, half, row_lo):
    n, ch = GLA_BLOCK, GLA_CHUNK
    la = _log_sigmoid(_dot(a.astype(BF16), w2) + b2) * (1.0 / GLA_TAU)
    cum = _dot(sums, _split3(la))
    cum = cum[:, :256] + cum[:, 256:512] + cum[:, 512:]
    bc, bl = cum[:n], cum[n:]
    q, k = q.astype(F32), k.astype(F32)
    q_in = (q * GLA_DK ** -0.5) * jnp.exp(bc)
    k_in = k * jnp.exp(-bc)
    kte = k * jnp.exp(bl - bc)
    heads = [None] * GLA_HEADS
    for pt in range(2):
        sl = slice(pt * LANES, (pt + 1) * LANES)
        qt = q_in[:, sl]
        qs = jnp.concatenate([jnp.where(half == 0, qt, 0.0), jnp.where(half == 1, qt, 0.0)],
                             axis=0).astype(BF16)
        att = jnp.where(cmask2, _dot_nt(qs, k_in[:, sl].astype(BF16)), 0.0).astype(BF16)
        v2 = v[:, 2 * pt * GLA_DV:(2 * pt + 2) * GLA_DV].astype(BF16)
        intra = [_dot(att[hh * n:(hh + 1) * n], v2[:, hh * GLA_DV:(hh + 1) * GLA_DV]) for hh in range(2)]
        kte_t = kte[:, sl].T.astype(BF16)
        la_t = la[:, sl].T
        st = s_ref[sl, :]
        inter = [None] * (n // ch)
        for c in order:
            cs = slice(c * ch, (c + 1) * ch)
            qc = jnp.concatenate([qs[cs], qs[n + c * ch:n + (c + 1) * ch]], axis=0)
            inter[c] = _dot(qc, st.astype(BF16))
            upd = _dot(kte_t[:, cs], v2[cs])
            dec = jnp.exp(jnp.sum(la_t[:, cs], axis=1, keepdims=True))
            st = st * dec + jnp.where(row_lo, upd[:, :GLA_DV], upd[:, GLA_DV:])
        s_ref[sl, :] = st
        for hh in range(2):
            rs = slice(hh * HEAD, (hh + 1) * HEAD)
            heads[2 * pt + hh] = intra[hh] + jnp.concatenate([inter[c][rs] for c in range(n // ch)], axis=0)
    return jnp.concatenate(heads, axis=1)


def _gla_kernel(qc_ref, kc_ref, vc_ref, ac_ref, qf_ref, kf_ref, vf_ref, af_ref,
                qb_ref, kb_ref, vb_ref, ab_ref, w2_ref, b2_ref,
                ocf_ref, ocb_ref, of_ref, ob_ref, sf_ref, sb_ref):
    s = pl.program_id(1)
    n, ch = GLA_BLOCK, GLA_CHUNK
    ri = lax.broadcasted_iota(jnp.int32, (n, n), 0)
    ci = lax.broadcasted_iota(jnp.int32, (n, n), 1)
    same = (ri // ch) == (ci // ch)
    low, up = same & (ri >= ci), same & (ri <= ci)
    ones = same.astype(BF16)
    sums_f = jnp.concatenate([low.astype(BF16), ones], axis=0)
    sums_b = jnp.concatenate([up.astype(BF16), ones], axis=0)
    cm_f = jnp.concatenate([low, low], axis=0)
    cm_b = jnp.concatenate([up, up], axis=0)
    half = lax.broadcasted_iota(jnp.int32, (n, LANES), 1) // HEAD
    row_lo = lax.broadcasted_iota(jnp.int32, (LANES, GLA_DV), 0) < HEAD
    w2f, w2b = w2_ref[:, :256], w2_ref[:, 256:]
    b2f, b2b = b2_ref[:, :256], b2_ref[:, 256:]
    chunks = tuple(range(n // ch))

    @pl.when(s == 0)
    def _():
        sf_ref[...] = jnp.zeros_like(sf_ref)
        sb_ref[...] = jnp.zeros_like(sb_ref)

    def run(fwd, bwd, o_f, o_b):
        q, k, v, a = fwd
        o_f[0] = _gla_block(q[0], k[0], v[0], a[0], w2f, b2f, sf_ref, sums_f, cm_f, chunks, half, row_lo)
        q, k, v, a = bwd
        o_b[0] = _gla_block(q[0], k[0], v[0], a[0], w2b, b2b, sb_ref, sums_b, cm_b, chunks[::-1], half, row_lo)

    ctx = (qc_ref, kc_ref, vc_ref, ac_ref)

    @pl.when(s == 0)
    def _():
        run(ctx, ctx, ocf_ref, ocb_ref)

    @pl.when(s > 0)
    def _():
        run((qf_ref, kf_ref, vf_ref, af_ref), (qb_ref, kb_ref, vb_ref, ab_ref), of_ref, ob_ref)


def gla_scan(p3, pc3, w2, b2):
    B = p3.shape[0]
    nb = T // GLA_BLOCK
    cq, ck, cv, ca = COL["gla_q"] // 256, COL["gla_k"] // 256, COL["gla_v"] // 512, COL["gla_a"] // 128
    fwd = lambda s: jnp.maximum(s - 1, 0)
    bwd = lambda s: nb - jnp.maximum(s, 1)

    def spec(width, col, row):
        return pl.BlockSpec((1, GLA_BLOCK, width), lambda b, s: (b, row(s), col))

    zero = lambda s: 0
    in_specs = ([spec(256, cq, zero), spec(256, ck, zero), spec(512, cv, zero), spec(128, ca, zero)]
                + [spec(256, cq, fwd), spec(256, ck, fwd), spec(512, cv, fwd), spec(128, ca, fwd)]
                + [spec(256, cq, bwd), spec(256, ck, bwd), spec(512, cv, bwd), spec(128, ca, bwd)]
                + [pl.BlockSpec((128, 512), lambda b, s: (0, 0)), pl.BlockSpec((1, 512), lambda b, s: (0, 0))])
    out_specs = [spec(512, 0, zero), spec(512, 0, zero), spec(512, 0, fwd), spec(512, 0, bwd)]
    out_shape = [jax.ShapeDtypeStruct((B, L, 512), F32)] * 2 + [jax.ShapeDtypeStruct((B, T, 512), F32)] * 2
    return pl.pallas_call(
        _gla_kernel, grid=(B, nb + 1), in_specs=in_specs, out_specs=out_specs, out_shape=out_shape,
        scratch_shapes=[pltpu.VMEM((256, 128), F32), pltpu.VMEM((256, 128), F32)],
        compiler_params=_cp("parallel", "arbitrary"), name="gla_scan",
    )(pc3, pc3, pc3, pc3, p3, p3, p3, p3, p3, p3, p3, p3, w2, b2)


def _gla_gate(o, r, g):
    outs = []
    for hd in range(GLA_HEADS):
        sl = slice(hd * GLA_DV, (hd + 1) * GLA_DV)
        rh = r[:, sl]
        outs.append(_rms(o[:, sl], g) * (rh * _sigmoid(rh)))
    return jnp.concatenate(outs, axis=1)


def _stack_halves(qt, half):
    zero = jnp.zeros_like(qt)
    return jnp.concatenate([jnp.where(half == 0, qt, zero), jnp.where(half == 1, qt, zero)], axis=0).astype(BF16)


def _merge_halves(o, half):
    n = o.shape[0] // 2
    return jnp.where(half == 0, o[:n], o[n:])


def _softmax_parts(s_list, sink=None):
    m = s_list[0].max(axis=1, keepdims=True)
    for s in s_list[1:]:
        m = jnp.maximum(m, s.max(axis=1, keepdims=True))
    if sink is not None:
        m = jnp.maximum(m, sink)
    es = [jnp.exp(s - m) for s in s_list]
    den = es[0].sum(axis=1, keepdims=True)
    for e in es[1:]:
        den = den + e.sum(axis=1, keepdims=True)
    if sink is not None:
        den = den + jnp.exp(sink - m)
    return [e.astype(BF16) for e in es], 1.0 / den


def _rope(x, cos, sin, lane):
    partner = jnp.where((lane % HEAD) < HEAD // 2, pltpu.roll(x, LANES - HEAD // 2, 1),
                        pltpu.roll(x, HEAD // 2, 1))
    return x * cos + partner * sin


def _swa_kernel(sink_ref, mask_ref, q_ref, kp_ref, kc_ref, kn_ref, vp_ref, vc_ref, vn_ref, kx_ref, vx_ref,
                cos_ref, sin_ref, o_ref):
    n = pl.program_id(1)
    nb = pl.num_programs(1)
    blk = SWA_BLOCK
    lane = lax.broadcasted_iota(jnp.int32, (blk, LANES), 1)
    half = lane // HEAD

    def tab(ref, i):
        return ref[pl.ds(pl.multiple_of(i * blk, blk), blk), :]

    ip, inx = jnp.maximum(n - 1, 0), jnp.minimum(n + 1, nb - 1)
    kb = jnp.concatenate([_rope(kp_ref[0].astype(F32), tab(cos_ref, ip), tab(sin_ref, ip), lane),
                          _rope(kc_ref[0].astype(F32), tab(cos_ref, n), tab(sin_ref, n), lane),
                          _rope(kn_ref[0].astype(F32), tab(cos_ref, inx), tab(sin_ref, inx), lane)],
                         axis=0).astype(BF16)
    vb = jnp.concatenate([vp_ref[0], vc_ref[0], vn_ref[0]], axis=0)
    cos_q, sin_q = tab(cos_ref, n), tab(sin_ref, n)
    qs = jnp.concatenate([_stack_halves(_rope(q_ref[0, :, p * LANES:(p + 1) * LANES].astype(F32), cos_q, sin_q, lane)
                                        * HEAD ** -0.5, half) for p in range(4)], axis=0)
    s_loc = _dot_nt(qs, kb) + mask_ref[...]
    s_loc = jnp.concatenate([s_loc[:, :blk] + jnp.where(n == 0, NEG, 0.0), s_loc[:, blk:2 * blk],
                             s_loc[:, 2 * blk:] + jnp.where(n == nb - 1, NEG, 0.0)], axis=1)
    s_ctx = _dot_nt(qs, kx_ref[0])
    grp = lax.broadcasted_iota(jnp.int32, (8 * blk, 1), 0) // blk
    sink = jnp.zeros((8 * blk, 1), F32)
    for g in range(8):
        sink = jnp.where(grp == g, sink_ref[g // 2 + 4 * (g % 2)], sink)
    (e_loc, e_ctx), inv = _softmax_parts([s_loc, s_ctx], sink=sink)
    o = (_dot(e_loc, vb) + _dot(e_ctx, vx_ref[0])) * inv
    for p in range(4):
        o_ref[0, :, p * LANES:(p + 1) * LANES] = _merge_halves(o[2 * p * blk:2 * (p + 1) * blk], half).astype(BF16)


def swa(p3, pc3, sink, cos_t, sin_t):
    B = p3.shape[0]
    nb = T // SWA_BLOCK
    qi = np.arange(SWA_BLOCK)[:, None]
    kj = np.arange(3 * SWA_BLOCK)[None, :]
    mask = jnp.asarray(np.tile(np.where(np.abs(kj - SWA_BLOCK - qi) <= SWA_WINDOW, 0.0, NEG), (8, 1)).astype(np.float32))
    ck, cv = COL["swa_k"] // 128, COL["swa_v"] // 128
    prev = lambda n: jnp.maximum(n - 1, 0)
    cur = lambda n: n
    nxt = lambda n: jnp.minimum(n + 1, nb - 1)
    kv = lambda col, row: pl.BlockSpec((1, SWA_BLOCK, 128), lambda b, n: (b, row(n), col))
    full = lambda col: pl.BlockSpec((1, L, 128), lambda b, n: (b, 0, col))
    tab = pl.BlockSpec((T, 128), lambda b, n: (0, 0))
    return pl.pallas_call(
        _swa_kernel, grid=(B, nb),
        in_specs=[pl.BlockSpec(memory_space=pltpu.SMEM),
                  pl.BlockSpec((8 * SWA_BLOCK, 3 * SWA_BLOCK), lambda b, n: (0, 0)),
                  pl.BlockSpec((1, SWA_BLOCK, 512), lambda b, n: (b, n, COL["swa_q"] // 512)),
                  kv(ck, prev), kv(ck, cur), kv(ck, nxt), kv(cv, prev), kv(cv, cur), kv(cv, nxt),
                  full(ck), full(cv), tab, tab],
        out_specs=pl.BlockSpec((1, SWA_BLOCK, 512), lambda b, n: (b, n, 0)),
        out_shape=jax.ShapeDtypeStruct((B, T, 512), BF16),
        compiler_params=_cp("parallel", "parallel"), name="swa",
    )(sink, mask, p3, p3, p3, p3, p3, p3, p3, pc3, pc3, cos_t, sin_t)


def _ctx_attn_kernel(sink_ref, q_ref, k_ref, v_ref, o_ref, *, use_sink, kv_tiles):
    n = q_ref.shape[1]
    half = lax.broadcasted_iota(jnp.int32, (n, LANES), 1) // HEAD
    for p in range(4):
        sl = slice(p * LANES, (p + 1) * LANES)
        ks = sl if kv_tiles == 4 else slice(0, LANES)
        qs = _stack_halves(q_ref[0, :, sl] * HEAD ** -0.5, half)
        sink = None
        if use_sink:
            top = lax.broadcasted_iota(jnp.int32, (2 * n, 1), 0) < n
            sink = jnp.where(top, sink_ref[p], sink_ref[p + 4])
        (e,), inv = _softmax_parts([_dot_nt(qs, k_ref[0, :, ks])], sink=sink)
        o = _dot(e, v_ref[0, :, ks]) * inv
        o_ref[0, :, sl] = _merge_halves(o, half).astype(BF16)


def ctx_attn(pc3, sink, qcol, kcol, vcol, kv_tiles, use_sink):
    B, w = pc3.shape[0], kv_tiles * 128
    return pl.pallas_call(
        functools.partial(_ctx_attn_kernel, use_sink=use_sink, kv_tiles=kv_tiles), grid=(B,),
        in_specs=[pl.BlockSpec(memory_space=pltpu.SMEM),
                  pl.BlockSpec((1, L, 512), lambda b: (b, 0, qcol // 512)),
                  pl.BlockSpec((1, L, w), lambda b: (b, 0, kcol // w)),
                  pl.BlockSpec((1, L, w), lambda b: (b, 0, vcol // w))],
        out_specs=pl.BlockSpec((1, L, 512), lambda b: (b, 0, 0)),
        out_shape=jax.ShapeDtypeStruct((B, L, 512), BF16),
        compiler_params=_cp("parallel"), name="ctx_attn")(sink, pc3, pc3, pc3)


def _na_kernel(q_ref, k_ref, v_ref, kx_ref, vx_ref, *rest):
    bias_refs, o_ref = rest[:-1], rest[-1]
    nr = len(bias_refs)
    i = pl.program_id(1)
    band = NA_KR * GRID_W
    half = lax.broadcasted_iota(jnp.int32, (GRID_W, LANES), 1) // HEAD
    starts = [pl.multiple_of(jnp.clip(nr * i + k - NA_KR // 2, 0, ROWS - NA_KR) * GRID_W, GRID_W)
              for k in range(nr)]
    s_loc, s_ctx = [], []
    for p in range(4):
        sl = slice(p * LANES, (p + 1) * LANES)
        qs = [_stack_halves(q_ref[0, k * GRID_W:(k + 1) * GRID_W, sl] * HEAD ** -0.5, half) for k in range(nr)]
        for k in range(nr):
            kb = k_ref[0, pl.ds(starts[k], band), sl]
            s_loc.append(_dot_nt(qs[k], kb) + bias_refs[k][0, 2 * p * GRID_W:(2 * p + 2) * GRID_W, :])
        s_ctx.append(_dot_nt(jnp.concatenate(qs, axis=0), kx_ref[0, :, sl]))
    (e_loc, e_ctx), inv = _softmax_parts([jnp.concatenate(s_loc, axis=0), jnp.concatenate(s_ctx, axis=0)])
    blk = 2 * GRID_W
    for p in range(4):
        sl = slice(p * LANES, (p + 1) * LANES)
        r0 = p * nr * blk
        o_ctx = _dot(e_ctx[r0:r0 + nr * blk], vx_ref[0, :, sl])
        for k in range(nr):
            rs = slice(r0 + k * blk, r0 + (k + 1) * blk)
            vb = v_ref[0, pl.ds(starts[k], band), sl]
            o = (_dot(e_loc[rs], vb) + o_ctx[k * blk:(k + 1) * blk]) * inv[rs]
            o_ref[0, k * GRID_W:(k + 1) * GRID_W, sl] = _merge_halves(o, half).astype(BF16)


def na(p3, pc3, bias, rows_per_step=NA_ROWS_PER_STEP):
    B, nr = p3.shape[0], rows_per_step
    cq, ck, cv = COL["na_q"] // 512, COL["na_k"] // 512, COL["na_v"] // 512
    shift = lambda r: r - jnp.clip(r - NA_KR // 2, 0, ROWS - NA_KR)
    bias_spec = lambda k: pl.BlockSpec((1, NA_HEADS * GRID_W, NA_KR * GRID_W),
                                       lambda b, i: (shift(nr * i + k), 0, 0))
    return pl.pallas_call(
        _na_kernel, grid=(B, ROWS // nr),
        in_specs=[pl.BlockSpec((1, nr * GRID_W, 512), lambda b, i: (b, i, cq)),
                  pl.BlockSpec((1, T, 512), lambda b, i: (b, 0, ck)),
                  pl.BlockSpec((1, T, 512), lambda b, i: (b, 0, cv)),
                  pl.BlockSpec((1, L, 512), lambda b, i: (b, 0, ck)),
                  pl.BlockSpec((1, L, 512), lambda b, i: (b, 0, cv))] + [bias_spec(k) for k in range(nr)],
        out_specs=pl.BlockSpec((1, nr * GRID_W, 512), lambda b, i: (b, i, 0)),
        out_shape=jax.ShapeDtypeStruct((B, T, 512), BF16),
        compiler_params=_cp("parallel", "arbitrary"), name="na")(p3, p3, p3, pc3, pc3, *([bias] * nr))


def na_bias_table(rpb):
    col = np.arange(GRID_W)
    col_start = np.clip(col - NA_KC // 2, 0, GRID_W - NA_KC)
    col_ok = (col[None, :] >= col_start[:, None]) & (col[None, :] < col_start[:, None] + NA_KC)
    col_off = np.clip(col[None, :] - col[:, None], -(NA_KC - 1), NA_KC - 1) + (NA_KC - 1)
    onehot = (col_off[:, :, None] == np.arange(2 * NA_KC - 1)).astype(np.float32)
    toep = jnp.einsum("hro,qko->hrqk", rpb, jnp.asarray(onehot), precision=HIGHEST)
    toep = jnp.where(jnp.asarray(col_ok)[None, None], toep, NEG)
    per_shift = [toep[:, NA_KR - 1 - s:2 * NA_KR - 1 - s] for s in range(NA_KR)]
    tab = jnp.stack(per_shift).transpose(0, 1, 3, 2, 4)
    return tab.reshape(NA_KR, NA_HEADS * GRID_W, NA_KR * GRID_W).astype(F32)


def _gelu(x):
    return 0.5 * x * (1.0 + jnp.tanh(np.sqrt(2.0 / np.pi).astype(np.float32) * (x + 0.044715 * (x * x * x))))


def _sgu_kernel(u_ref, v_ref, g_ref, b_ref, ws_ref, bs_ref, y_ref):
    n_chunk = u_ref.shape[1] // SGU_CHUNK
    v = _gelu(v_ref[0].astype(F32))
    mu = jnp.mean(v, axis=-1, keepdims=True)
    var = jnp.mean(jnp.square(v - mu), axis=-1, keepdims=True)
    vn = ((v - mu) * lax.rsqrt(var + EPS) * g_ref[...] + b_ref[...]).astype(BF16)
    cols = []
    for g in range(4):
        gs = slice(g * 128, (g + 1) * 128)
        rhs = jnp.concatenate([vn[c * SGU_CHUNK:(c + 1) * SGU_CHUNK, gs] for c in range(n_chunk)], axis=1)
        mixed = _dot(ws_ref[g], rhs)
        cols.append(jnp.concatenate([mixed[:, c * 128:(c + 1) * 128] + bs_ref[g] for c in range(n_chunk)], axis=0))
    y_ref[0] = (_gelu(u_ref[0].astype(F32)) * jnp.concatenate(cols, axis=1)).astype(BF16)


def sgu(p3, g, b, ws, bs):
    B, n = p3.shape[0], p3.shape[1]
    tr = min(n, TOKEN_TILE)
    blk = lambda col: pl.BlockSpec((1, tr, 512), lambda bb, i: (bb, i, col))
    const = lambda shape: pl.BlockSpec(shape, lambda bb, i: (0,) * len(shape))
    return pl.pallas_call(
        _sgu_kernel, grid=(B, n // tr),
        in_specs=[blk(COL["sgu_u"] // 512), blk(COL["sgu_v"] // 512), const((1, 512)), const((1, 512)),
                  const((4, 128, 128)), const((4, 128, 128))],
        out_specs=blk(0), out_shape=jax.ShapeDtypeStruct((B, n, 512), BF16),
        compiler_params=_cp("parallel", "parallel"), name="sgu")(p3, p3, g, b, ws, bs)


def _merge_kernel(h_ref, of_ref, ob_ref, r_ref, gn_ref, b_ref, n_ref, d_ref, wg_ref, bg_ref, wb_ref, o_ref,
                  a_ref):
    @pl.when(pl.program_id(1) == 0)
    def _():
        a_ref[...] = _gla_gate(of_ref[...] + ob_ref[...], r_ref[...].astype(F32), gn_ref[...]).astype(BF16)

    h = h_ref[...]
    acc = None
    for i, br in enumerate((a_ref, b_ref, n_ref, d_ref)):
        term = _sigmoid(_dot(h, wg_ref[0, i].astype(BF16)) + bg_ref[i]) * _dot(br[...], wb_ref[i])
        acc = term if acc is None else acc + term
    o_ref[...] = acc.astype(BF16)


def merge(h, o_f, o_b, p, gn, branches, wg, l, bg, wb, tm):
    M, tn = h.shape[0], MERGE_COL_TILE
    tm = min(tm, M)
    br = pl.BlockSpec((tm, 512), lambda i, j: (i, 0))
    return pl.pallas_call(
        _merge_kernel, grid=(M // tm, D // tn),
        in_specs=[pl.BlockSpec((tm, D), lambda i, j: (i, 0)), br, br,
                  pl.BlockSpec((tm, 512), lambda i, j: (i, COL["gla_r"] // 512)),
                  pl.BlockSpec((1, GLA_DV), lambda i, j: (0, 0)), br, br, br,
                  pl.BlockSpec((1, 4, D, tn), lambda i, j: (l, 0, 0, j)),
                  pl.BlockSpec((4, 1, tn), lambda i, j: (0, 0, j)),
                  pl.BlockSpec((4, 512, tn), lambda i, j: (0, 0, j))],
        out_specs=pl.BlockSpec((tm, tn), lambda i, j: (i, j)),
        out_shape=jax.ShapeDtypeStruct((M, D), BF16),
        scratch_shapes=[pltpu.VMEM((tm, 512), BF16)],
        compiler_params=_cp("parallel", "arbitrary"), name="merge")(h, o_f, o_b, p, gn, *branches, wg, bg, wb)


def _outproj_kernel(acc_ref, w_ref, x_ref, gate_ref, g_ref, shift_ref, scale_ref, wr_ref,
                    x1_ref, h2_ref, lg_ref):
    n_sub = 2
    sub = acc_ref.shape[0] // n_sub
    for k in range(n_sub):
        rs = slice(k * sub, (k + 1) * sub)
        x1 = x_ref[rs, :] + gate_ref[0] * _dot(acc_ref[rs, :], w_ref[...])
        x1_ref[rs, :] = x1
        h2 = (_rms(x1, g_ref[...]) * (1.0 + scale_ref[0]) + shift_ref[0]).astype(BF16)
        lg_ref[rs, :] = _dot(h2, wr_ref[...])
        bits = pltpu.bitcast(h2.astype(F32), jnp.int32)
        h2_ref[rs, :] = lax.shift_right_logical(bits[:, :D // 2], 16) | (bits[:, D // 2:] & jnp.int32(-65536))


def outproj(acc, w, x2, mod3, g, wr, tm, mod_row):
    M = acc.shape[0]
    tm = min(tm, M)
    row = lambda k: pl.BlockSpec((1, 1, D), lambda i: (mod_row(i) * 6 + k, 0, 0))
    tile = pl.BlockSpec((tm, D), lambda i: (i, 0))
    return pl.pallas_call(
        _outproj_kernel, grid=(M // tm,),
        in_specs=[tile, pl.BlockSpec((D, D), lambda i: (0, 0)), tile, row(2),
                  pl.BlockSpec((1, D), lambda i: (0, 0)), row(3), row(4),
                  pl.BlockSpec((D, 128), lambda i: (0, 0))],
        out_specs=[tile, pl.BlockSpec((tm, D // 2), lambda i: (i, 0)), pl.BlockSpec((tm, 128), lambda i: (i, 0))],
        out_shape=[jax.ShapeDtypeStruct((M, D), F32), jax.ShapeDtypeStruct((M, D // 2), jnp.int32),
                   jax.ShapeDtypeStruct((M, 128), F32)],
        compiler_params=_cp("parallel"), name="outproj")(acc, w, x2, mod3, g, mod3, mod3, wr)


def _select_kernel(lg_ref, sel_ref, aff_ref, slot_ref, *, cap):
    n = lg_ref.shape[1]
    nt = n // LANES
    lane = lax.broadcasted_iota(jnp.int32, (n, LANES), 1)
    lg = jnp.where(lane < N_EXPERTS, lg_ref[0], NEG)
    e = jnp.exp(lg - lg.max(axis=1, keepdims=True))
    aff = (e / e.sum(axis=1, keepdims=True)).T[:N_EXPERTS]
    aff_ref[0] = aff
    bits = pltpu.bitcast(aff, jnp.int32)

    def count(mask):
        return jnp.sum(jnp.where(mask, 1.0, 0.0), axis=1, keepdims=True)

    def bit_step(i, thr):
        cand = thr | lax.shift_left(jnp.int32(1), 30 - i)
        return jnp.where(count(bits >= cand) >= cap, cand, thr)

    thr = lax.fori_loop(0, 31, bit_step, jnp.zeros((N_EXPERTS, 1), jnp.int32))
    gt, eq = bits > thr, bits == thr

    r = lax.broadcasted_iota(jnp.int32, (LANES, LANES), 0)
    c = lax.broadcasted_iota(jnp.int32, (LANES, LANES), 1)
    upper = (r <= c).astype(BF16)
    r2 = lax.broadcasted_iota(jnp.int32, (nt * N_EXPERTS, nt * N_EXPERTS), 0)
    c2 = lax.broadcasted_iota(jnp.int32, (nt * N_EXPERTS, nt * N_EXPERTS), 1)
    before = ((r2 % N_EXPERTS == c2 % N_EXPERTS) & (c2 // N_EXPERTS < r2 // N_EXPERTS)).astype(BF16)

    def excl_prefix(x):
        xs = jnp.concatenate([x[:, i * LANES:(i + 1) * LANES] for i in range(nt)], axis=0).astype(BF16)
        inc = _dot(xs, upper)
        tot = jnp.broadcast_to(inc[:, LANES - 1:LANES], inc.shape).astype(BF16)
        full = inc + _dot(before, tot)
        return jnp.concatenate([full[i * N_EXPERTS:(i + 1) * N_EXPERTS] for i in range(nt)], axis=1) - x

    eq01 = jnp.where(eq, 1.0, 0.0)
    mask = gt | (eq & (excl_prefix(eq01) < cap - count(gt)))
    slot = jnp.where(mask, excl_prefix(jnp.where(mask, 1.0, 0.0)), -1.0)
    sel_ref[0] = slot.astype(jnp.int32)
    pad = jnp.full((LANES - N_EXPERTS, n), -1.0, F32)
    slot_ref[0] = jnp.concatenate([slot, pad], axis=0).T


def select(lg3, cap):
    B, n = lg3.shape[0], lg3.shape[1]
    tok = pl.BlockSpec((1, n, 128), lambda b: (b, 0, 0))
    row = pl.BlockSpec((1, N_EXPERTS, n), lambda b: (b, 0, 0))
    return pl.pallas_call(
        functools.partial(_select_kernel, cap=cap), grid=(B,),
        in_specs=[tok], out_specs=[row, row, tok],
        out_shape=[jax.ShapeDtypeStruct((B, N_EXPERTS, n), jnp.int32),
                   jax.ShapeDtypeStruct((B, N_EXPERTS, n), F32), jax.ShapeDtypeStruct((B, n, 128), F32)],
        compiler_params=_cp("parallel"), name="select")(lg3)


def _slot_tokens_kernel(sel_ref, aff_ref, idx_ref, g_ref, *, cap):
    n = sel_ref.shape[2]
    row = lax.broadcasted_iota(jnp.int32, (cap, n), 0)
    tok = lax.broadcasted_iota(jnp.int32, (cap, n), 1).astype(F32)
    for e in range(N_EXPERTS):
        hit = row == sel_ref[0, e:e + 1, :]
        idx = jnp.sum(jnp.where(hit, tok, 0.0), axis=1, keepdims=True).astype(jnp.int32)
        idx_ref[e, 0] = idx + pl.program_id(0) * n
        g_ref[e, 0] = jnp.sum(jnp.where(hit, aff_ref[0, e:e + 1, :], 0.0), axis=1, keepdims=True)


def slot_tokens(sel, aff, cap):
    B, n = sel.shape[0], sel.shape[2]
    row = pl.BlockSpec((1, N_EXPERTS, n), lambda b: (b, 0, 0))
    col = pl.BlockSpec((N_EXPERTS, 1, cap, 1), lambda b: (0, b, 0, 0))
    idx, gate = pl.pallas_call(
        functools.partial(_slot_tokens_kernel, cap=cap), grid=(B,),
        in_specs=[row, row], out_specs=[col, col],
        out_shape=[jax.ShapeDtypeStruct((N_EXPERTS, B, cap, 1), jnp.int32),
                   jax.ShapeDtypeStruct((N_EXPERTS, B, cap, 1), F32)],
        compiler_params=_cp("parallel"), name="slot_tokens")(sel, aff)
    return idx.reshape(N_EXPERTS, B * cap, 1), gate.reshape(N_EXPERTS, B * cap, 1)


def _expert_kernel(idx_ref, h_hbm, gate_ref, wg_ref, wu_ref, wd_ref, o_ref, xg_ref, xb_ref, acc_ref, sem,
                   *, tm):
    e, i, f = pl.program_id(0), pl.program_id(1), pl.program_id(2)
    n_i, n_f = pl.num_programs(1), pl.num_programs(2)
    q = e * n_i + i
    n_q = pl.num_programs(0) * n_i
    slot = q % 2
    share = tm // n_f

    def row_copy(tile, r, dst_slot):
        src = h_hbm.at[pl.ds(idx_ref[tile * tm + r], 1), :]
        return pltpu.make_async_copy(src, xg_ref.at[dst_slot, pl.ds(r, 1), :], sem.at[dst_slot])

    def wait_tile(dst_slot):
        pltpu.make_async_copy(h_hbm.at[pl.ds(0, tm), :], xg_ref.at[dst_slot], sem.at[dst_slot]).wait()

    @pl.when((q == 0) & (f == 0))
    def _():
        def first(r, carry):
            row_copy(0, r, 0).start()
            return carry
        lax.fori_loop(0, tm, first, 0)

    @pl.when(f == 0)
    def _():
        wait_tile(slot)
        p = xg_ref[slot]
        xb_ref[:, :D // 2] = pltpu.bitcast(lax.shift_left(p, 16), F32).astype(BF16)
        xb_ref[:, D // 2:] = pltpu.bitcast(p & jnp.int32(-65536), F32).astype(BF16)
        acc_ref[...] = jnp.zeros_like(acc_ref)

    x = xb_ref[...]
    g = _dot(x, wg_ref[0, 0].astype(BF16))
    u = _dot(x, wu_ref[0, 0].astype(BF16))
    hid = ((g * _sigmoid(g)) * u).astype(BF16)
    nxt = jnp.minimum(q + 1, n_q - 1)
    for r in range(share):
        row_copy(nxt, f * share + r, 1 - slot).start()
    acc_ref[...] += _dot(hid, wd_ref[0, 0].astype(BF16))

    @pl.when(f == n_f - 1)
    def _():
        o_ref[0] = (acc_ref[...] * gate_ref[0]).astype(BF16)

    @pl.when((q == n_q - 1) & (f == n_f - 1))
    def _():
        wait_tile(1 - slot)


def experts(idx, h2p, gate, wg, wu, wd, l, tm):
    E, M, tf = gate.shape[0], gate.shape[1], EXPERT_HIDDEN_TILE
    F = wg.shape[3]
    tm = min(tm, M)
    grid_spec = pltpu.PrefetchScalarGridSpec(
        num_scalar_prefetch=1, grid=(E, M // tm, F // tf),
        in_specs=[pl.BlockSpec(memory_space=pl.ANY),
                  pl.BlockSpec((1, tm, 1), lambda e, i, f, idx: (e, i, 0)),
                  pl.BlockSpec((1, 1, D, tf), lambda e, i, f, idx: (l, e, 0, f)),
                  pl.BlockSpec((1, 1, D, tf), lambda e, i, f, idx: (l, e, 0, f)),
                  pl.BlockSpec((1, 1, tf, D), lambda e, i, f, idx: (l, e, f, 0))],
        out_specs=pl.BlockSpec((1, tm, D), lambda e, i, f, idx: (e, i, 0)),
        scratch_shapes=[pltpu.VMEM((2, tm, D // 2), jnp.int32), pltpu.VMEM((tm, D), BF16),
                        pltpu.VMEM((tm, D), F32), pltpu.SemaphoreType.DMA((2,))])
    return pl.pallas_call(
        functools.partial(_expert_kernel, tm=tm), grid_spec=grid_spec,
        out_shape=jax.ShapeDtypeStruct((E, M, D), BF16),
        compiler_params=_cp("arbitrary", "arbitrary", "arbitrary"), name="experts")(idx, h2p, gate, wg, wu, wd)


COMBINE_GROUP = 8


def _combine_kernel(slot_ref, ye_ref, x_ref, gate_ref, gfin_ref, o_ref, acc_ref, *, cap, final):
    g = pl.program_id(2)
    tt = x_ref.shape[1]
    lane = lax.broadcasted_iota(jnp.int32, (tt, LANES), 1)
    j = lax.broadcasted_iota(jnp.int32, (tt, cap), 1).astype(F32)
    slots = slot_ref[0]
    parts = []
    for k in range(COMBINE_GROUP):
        slot = jnp.sum(jnp.where(lane == g * COMBINE_GROUP + k, slots, 0.0), axis=1, keepdims=True)
        parts.append(jnp.where(j == slot, 1.0, 0.0).astype(BF16))
    onehot = jnp.concatenate(parts, axis=1)
    contrib = _dot(onehot, ye_ref[:, 0].reshape(COMBINE_GROUP * cap, D))

    @pl.when(g == 0)
    def _():
        acc_ref[...] = contrib

    @pl.when(g == pl.num_programs(2) - 1)
    def _():
        x2 = x_ref[0] + gate_ref[0] * (acc_ref[...] + contrib)
        if final:
            x2 = _rms(x2, gfin_ref[...])
        o_ref[0] = x2


def combine(slot_tok, ye, x3, mod3, gfin, cap, mod_row, final):
    B, n = x3.shape[0], x3.shape[1]
    tt = min(n, TOKEN_TILE)
    assert N_EXPERTS == 2 * COMBINE_GROUP
    xt = pl.BlockSpec((1, tt, D), lambda b, t, g: (b, t, 0))
    return pl.pallas_call(
        functools.partial(_combine_kernel, cap=cap, final=final), grid=(B, n // tt, N_EXPERTS // COMBINE_GROUP),
        in_specs=[pl.BlockSpec((1, tt, 128), lambda b, t, g: (b, t, 0)),
                  pl.BlockSpec((COMBINE_GROUP, 1, cap, D), lambda b, t, g: (g, b, 0, 0)), xt,
                  pl.BlockSpec((1, 1, D), lambda b, t, g: (mod_row(b) * 6 + 5, 0, 0)),
                  pl.BlockSpec((1, D), lambda b, t, g: (0, 0))],
        out_specs=xt, out_shape=jax.ShapeDtypeStruct((B, n, D), F32),
        scratch_shapes=[pltpu.VMEM((tt, D), F32)],
        compiler_params=_cp("parallel", "parallel", "arbitrary"), name="combine",
    )(slot_tok, ye.reshape(N_EXPERTS, B, cap, D), x3, mod3, gfin)


def _swa_q_perm():
    return np.concatenate([np.arange(HEAD) + (p + 4 * hh) * HEAD for p in range(4) for hh in range(2)])


def _prep_w_in(w):
    seg = lambda name, width: w[:, REF[name]:REF[name] + width]
    swa_q = seg("swa_q", 512)[:, _swa_q_perm()]
    parts = [seg("gla_v", 512), seg("gla_r", 512), swa_q, seg("na_q", 512), seg("na_k", 512), seg("na_v", 512),
             seg("sgu_u", 512), seg("sgu_v", 512), seg("gla_q", 256), seg("gla_k", 256), seg("swa_k", 128),
             seg("swa_v", 128), seg("gla_af", 16), seg("gla_ab", 16)]
    out = jnp.concatenate(parts, axis=1)
    return jnp.pad(out, ((0, 0), (0, NP - out.shape[1]))).astype(BF16)


def _rope_tables():
    quarter = HEAD // 4
    freqs = 10000.0 ** (-jnp.arange(quarter, dtype=F32) / quarter)
    t = jnp.arange(T)
    row = (t // GRID_W).astype(F32)
    col = (t % GRID_W).astype(F32)
    ang = jnp.concatenate([row[:, None] * freqs, col[:, None] * freqs], axis=-1)
    cos, sin = jnp.cos(ang), jnp.sin(ang)
    cos_t = jnp.tile(cos, (1, 4))
    sin_t = jnp.tile(jnp.concatenate([-sin, sin], axis=-1), (1, 2))
    return cos_t, sin_t


def _moe(h2, lg, x3, mod3, gfin, wg, wu, wd, l, cap, tm, mod_row, final):
    B, n = x3.shape[0], x3.shape[1]
    sel, aff, slot_tok = select(lg.reshape(B, n, 128), cap)
    idx, gate = slot_tokens(sel, aff, cap)
    ye = experts(idx.reshape(-1), h2, gate, wg, wu, wd, l, tm)
    return combine(slot_tok, ye, x3, mod3, gfin, cap, mod_row, final)


def kernel(x, c, ctx, c_ctx, w_ada, b_ada, norm1_g, norm2_g, w_in, gla_w_a2, gla_b_a2, gla_norm_g, swa_sink,
           na_rpb, sgu_ln_g, sgu_ln_b, sgu_w_s, sgu_b_s, w_gate, b_gate, w_branch, w_out, w_router,
           w_exp_gate, w_exp_up, w_exp_down, final_norm_g):
    B = x.shape[0]
    ctx_row = B
    n_mod = -(-(B + 1) // SUBLANES) * SUBLANES
    cvec = jnp.zeros((n_mod, D), F32).at[:B].set(c).at[ctx_row].set(c_ctx)
    cos_t, sin_t = _rope_tables()
    perm = _swa_q_perm()
    gfin = final_norm_g.reshape(1, D)
    tm = ROW_TILE
    lat_tile_row = lambda i: i // (T // tm)
    ctx_tile_row = lambda i: ctx_row
    x2 = x.reshape(B * T, D)
    xc2 = ctx.reshape(B * L, D)
    for l in range(DEPTH):
        need_ctx = l < DEPTH - 1
        mod3 = ada(cvec, w_ada, b_ada.reshape(DEPTH, 1, 6 * D), l).reshape(n_mod * 6, 1, D)
        w_in_p = _prep_w_in(w_in[l])
        g1 = norm1_g[l].reshape(1, D)
        h, p = inproj(x2, mod3, g1, w_in_p, tm, lat_tile_row)
        hc, pc = inproj(xc2, mod3, g1, w_in_p, tm, ctx_tile_row)
        p3, pc3 = p.reshape(B, T, NP), pc.reshape(B, L, NP)

        w2 = jnp.zeros((128, 512), F32).at[:GLA_RANK, :256].set(gla_w_a2[l, 0])
        w2 = w2.at[GLA_RANK:2 * GLA_RANK, 256:].set(gla_w_a2[l, 1]).astype(BF16)
        b2 = gla_b_a2[l].reshape(1, 512)
        ocf, ocb, of, ob = gla_scan(p3, pc3, w2, b2)
        gn = gla_norm_g[l].reshape(1, GLA_DV)
        yb = swa(p3, pc3, swa_sink[l], cos_t, sin_t)
        yn = na(p3, pc3, na_bias_table(na_rpb[l]))
        ln_g, ln_b = sgu_ln_g[l].reshape(1, 512), sgu_ln_b[l].reshape(1, 512)
        ws = sgu_w_s[l].astype(BF16)
        bs = jnp.broadcast_to(sgu_b_s[l][:, :, None], (4, SGU_CHUNK, 128))
        yd = sgu(p3, ln_g, ln_b, ws, bs)

        bg = b_gate[l].reshape(4, 1, D)
        wb = w_branch[l].at[1].set(w_branch[l, 1][perm]).astype(BF16)
        wo = w_out[l].astype(BF16)
        wr = jnp.pad(w_router[l], ((0, 0), (0, 128 - N_EXPERTS))).astype(BF16)
        g2 = norm2_g[l].reshape(1, D)
        flat = lambda y: y.reshape(-1, 512)
        acc = merge(h, flat(of), flat(ob), p, gn, [flat(yb), flat(yn), flat(yd)], w_gate, l, bg, wb, tm)
        x1, h2, lg = outproj(acc, wo, x2, mod3, g2, wr, OUTPROJ_ROW_TILE, lambda i: i // (T // OUTPROJ_ROW_TILE))
        x2 = _moe(h2, lg, x1.reshape(B, T, D), mod3, gfin, w_exp_gate, w_exp_up, w_exp_down, l,
                  2 * T // N_EXPERTS, tm, lambda b: b, final=not need_ctx).reshape(B * T, D)
        if need_ctx:
            ybc = ctx_attn(pc3, swa_sink[l], COL["swa_q"], COL["swa_k"], COL["swa_v"], 1, True)
            ync = ctx_attn(pc3, swa_sink[l], COL["na_q"], COL["na_k"], COL["na_v"], 4, False)
            ydc = sgu(pc3, ln_g, ln_b, ws, bs)
            accc = merge(hc, flat(ocf), flat(ocb), pc, gn, [flat(ybc), flat(ync), flat(ydc)], w_gate, l, bg, wb, tm)
            xc1, hc2, lgc = outproj(accc, wo, xc2, mod3, g2, wr, OUTPROJ_ROW_TILE, ctx_tile_row)
            xc2 = _moe(hc2, lgc, xc1.reshape(B, L, D), mod3, gfin, w_exp_gate, w_exp_up, w_exp_down, l,
                       2 * L // N_EXPERTS, tm, lambda b: ctx_row, final=False).reshape(B * L, D)
    return x2.reshape(B, T, D)
```

```python
import functools

import numpy as np
import jax
import jax.numpy as jnp
from jax import lax
from jax.experimental import pallas as pl
from jax.experimental.pallas import tpu as pltpu

F32 = jnp.float32
BF16 = jnp.bfloat16
HIGHEST = lax.Precision.HIGHEST

D = 2048
T = 2048
L = 256
GRID_W = 64
ROWS = T // GRID_W
EPS = 1e-6
DEPTH = 2
NEG = -1e30

GLA_HEADS, GLA_DK, GLA_DV, GLA_RANK, GLA_TAU, GLA_CHUNK = 4, 64, 128, 16, 16.0, 64
GLA_BLOCK = 256
SWA_BLOCK, SWA_WINDOW = 128, 128
NA_KR, NA_KC, NA_HEADS = 8, 16, 8
SGU_CHUNK = 128
N_EXPERTS = 16
LANES = 128
HEAD = 64

NP = 5120
COL = dict(gla_v=0, gla_r=512, swa_q=1024, na_q=1536, na_k=2048, na_v=2560, sgu_u=3072, sgu_v=3584,
           gla_q=4096, gla_k=4352, swa_k=4608, swa_v=4736, gla_a=4864)
REF = dict(gla_q=0, gla_k=256, gla_v=512, gla_r=1024, gla_af=1536, gla_ab=1552, swa_q=1568, swa_k=2080,
           swa_v=2208, na_q=2336, na_k=2848, na_v=3360, sgu_u=3872, sgu_v=4384)

VMEM_LIMIT = 56 * 1024 * 1024
SUBLANES = 8

ROW_TILE = 1024
OUTPROJ_ROW_TILE = 512
ADA_COL_TILE = 1024
INPROJ_COL_TILE = 1024
MERGE_COL_TILE = 256
EXPERT_HIDDEN_TILE = 256
TOKEN_TILE = 512
NA_ROWS_PER_STEP = 4


def _cp(*sem, vmem=VMEM_LIMIT):
    return pltpu.CompilerParams(dimension_semantics=sem, vmem_limit_bytes=vmem)


def _sigmoid(x):
    return 1.0 / (1.0 + jnp.exp(-x))


def _rms(x, g):
    return x * lax.rsqrt(jnp.mean(x * x, axis=-1, keepdims=True) + EPS) * g


def _dot(a, b):
    return jnp.dot(a, b, preferred_element_type=F32)


def _dot_nt(a, b):
    return lax.dot_general(a, b, (((1,), (1,)), ((), ())), preferred_element_type=F32)


def _ada_kernel(c_ref, w_ref, b_ref, o_ref):
    c = c_ref[...]
    act = (c * _sigmoid(c)).astype(BF16)
    o_ref[...] = _dot(act, w_ref[0].astype(BF16)) + b_ref[0]


def ada(cvec, w, b, l):
    rows, tn = cvec.shape[0], ADA_COL_TILE
    return pl.pallas_call(
        _ada_kernel, grid=(6 * D // tn,),
        in_specs=[pl.BlockSpec((rows, D), lambda j: (0, 0)),
                  pl.BlockSpec((1, D, tn), lambda j: (l, 0, j)),
                  pl.BlockSpec((1, 1, tn), lambda j: (l, 0, j))],
        out_specs=pl.BlockSpec((rows, tn), lambda j: (0, j)),
        out_shape=jax.ShapeDtypeStruct((rows, 6 * D), F32),
        compiler_params=_cp("parallel"), name="ada")(cvec, w, b)


def _inproj_kernel(x_ref, shift_ref, scale_ref, g_ref, w_ref, h_ref, p_ref, hs_ref):
    @pl.when(pl.program_id(1) == 0)
    def _():
        h = (_rms(x_ref[...], g_ref[...]) * (1.0 + scale_ref[0]) + shift_ref[0]).astype(BF16)
        hs_ref[...] = h
        h_ref[...] = h

    p_ref[...] = _dot(hs_ref[...], w_ref[...]).astype(BF16)


def inproj(x2, mod3, g, w, tm, mod_row):
    M, tn = x2.shape[0], INPROJ_COL_TILE
    tm = min(tm, M)
    return pl.pallas_call(
        _inproj_kernel, grid=(M // tm, NP // tn),
        in_specs=[pl.BlockSpec((tm, D), lambda i, j: (i, 0)),
                  pl.BlockSpec((1, 1, D), lambda i, j: (mod_row(i) * 6 + 0, 0, 0)),
                  pl.BlockSpec((1, 1, D), lambda i, j: (mod_row(i) * 6 + 1, 0, 0)),
                  pl.BlockSpec((1, D), lambda i, j: (0, 0)),
                  pl.BlockSpec((D, tn), lambda i, j: (0, j))],
        out_specs=[pl.BlockSpec((tm, D), lambda i, j: (i, 0)),
                   pl.BlockSpec((tm, tn), lambda i, j: (i, j))],
        out_shape=[jax.ShapeDtypeStruct((M, D), BF16), jax.ShapeDtypeStruct((M, NP), BF16)],
        scratch_shapes=[pltpu.VMEM((tm, D), BF16)],
        compiler_params=_cp("parallel", "arbitrary"), name="inproj")(x2, mod3, mod3, g, w)


def _log_sigmoid(x):
    return jnp.minimum(x, 0.0) - jnp.log(1.0 + jnp.exp(-jnp.abs(x)))


def _split3(x):
    hi = x.astype(BF16)
    r1 = x - hi.astype(F32)
    mid = r1.astype(BF16)
    lo = (r1 - mid.astype(F32)).astype(BF16)
    return jnp.concatenate([hi, mid, lo], axis=1)


def _gla_block(fwd, bwd, w2_ref, b2_ref, sf_ref, sb_ref, sums_f, sums_b, cmask, half, row_lo):
    n, ch = GLA_BLOCK, GLA_CHUNK
    n_ch = n // ch
    (qf, kf, vf, af), (qb, kb, vb, ab) = fwd, bwd
    pre = jnp.concatenate([_dot(af.astype(BF16), w2_ref[:, :256]) + b2_ref[:, :256],
                           _dot(ab.astype(BF16), w2_ref[:, 256:]) + b2_ref[:, 256:]], axis=0)
    la = _log_sigmoid(pre) * (1.0 / GLA_TAU)
    la3 = _split3(la)
    cum = jnp.concatenate([_dot(sums_f, la3[:n]), _dot(sums_b, la3[n:])], axis=0)
    bc = cum[:, :256] + cum[:, 256:512] + cum[:, 512:]
    ends = [c * ch + ch - 1 for c in range(n_ch)] + [n + c * ch for c in range(n_ch)]
    bl = jnp.concatenate([jnp.broadcast_to(bc[r:r + 1], (ch, bc.shape[1])) for r in ends], axis=0)
    q = jnp.concatenate([qf, qb], axis=0).astype(F32)
    k = jnp.concatenate([kf, kb], axis=0).astype(F32)
    q_in = (q * GLA_DK ** -0.5) * jnp.exp(bc)
    k_in = (k * jnp.exp(-bc)).astype(BF16)
    kte = k * jnp.exp(bl - bc)
    half2 = jnp.concatenate([half, half], axis=0)
    tiles = range(2)
    sls = [slice(pt * LANES, (pt + 1) * LANES) for pt in tiles]
    qs = [jnp.concatenate([jnp.where(half2 == 0, q_in[:, sl], 0.0), jnp.where(half2 == 1, q_in[:, sl], 0.0)],
                          axis=0).astype(BF16) for sl in sls]
    att = [jnp.where(cmask, jnp.concatenate(
        [_dot_nt(qs[pt][0:n], k_in[:n, sls[pt]]), _dot_nt(qs[pt][n:2 * n], k_in[n:, sls[pt]]),
         _dot_nt(qs[pt][2 * n:3 * n], k_in[:n, sls[pt]]), _dot_nt(qs[pt][3 * n:], k_in[n:, sls[pt]])], axis=0),
        0.0).astype(BF16) for pt in tiles]
    v2 = [[vd[:, 2 * pt * GLA_DV:(2 * pt + 2) * GLA_DV].astype(BF16) for vd in (vf, vb)] for pt in tiles]
    intra = [[_dot(att[pt][(2 * hh + d) * n:(2 * hh + d + 1) * n], v2[pt][d][:, hh * GLA_DV:(hh + 1) * GLA_DV])
              for hh in range(2) for d in range(2)] for pt in tiles]
    kte_t = [kte[:, sl].T.astype(BF16) for sl in sls]
    la_t = [la[:, sl].T for sl in sls]
    st = [[ref[sl, :] for ref in (sf_ref, sb_ref)] for sl in sls]
    inter = [[[None] * n_ch for _ in range(2)] for _ in tiles]
    for step in range(n_ch):
        for pt in tiles:
            for d in range(2):
                c = step if d == 0 else n_ch - 1 - step
                r0, c0 = d * n + c * ch, d * n + c * ch
                qc = jnp.concatenate([qs[pt][r0:r0 + ch], qs[pt][2 * n + r0:2 * n + r0 + ch]], axis=0)
                inter[pt][d][c] = _dot(qc, st[pt][d].astype(BF16))
                upd = _dot(kte_t[pt][:, c0:c0 + ch], v2[pt][d][c * ch:(c + 1) * ch])
                dec = jnp.exp(jnp.sum(la_t[pt][:, c0:c0 + ch], axis=1, keepdims=True))
                st[pt][d] = st[pt][d] * dec + jnp.where(row_lo, upd[:, :GLA_DV], upd[:, GLA_DV:])
    outs = [[None] * GLA_HEADS for _ in range(2)]
    for pt in tiles:
        sf_ref[sls[pt], :] = st[pt][0]
        sb_ref[sls[pt], :] = st[pt][1]
        for hh in range(2):
            rs = slice(hh * HEAD, (hh + 1) * HEAD)
            for d in range(2):
                outs[d][2 * pt + hh] = intra[pt][2 * hh + d] + jnp.concatenate(
                    [inter[pt][d][c][rs] for c in range(n_ch)], axis=0)
    return jnp.concatenate(outs[0], axis=1), jnp.concatenate(outs[1], axis=1)


def _gla_kernel(qc_ref, kc_ref, vc_ref, ac_ref, qf_ref, kf_ref, vf_ref, af_ref,
                qb_ref, kb_ref, vb_ref, ab_ref, w2_ref, b2_ref,
                ocf_ref, ocb_ref, of_ref, ob_ref, sf_ref, sb_ref):
    s = pl.program_id(1)
    n, ch = GLA_BLOCK, GLA_CHUNK
    ri = lax.broadcasted_iota(jnp.int32, (n, n), 0)
    ci = lax.broadcasted_iota(jnp.int32, (n, n), 1)
    same = (ri // ch) == (ci // ch)
    low, up = same & (ri >= ci), same & (ri <= ci)
    sums_f, sums_b = low.astype(BF16), up.astype(BF16)
    cmask = jnp.concatenate([low, up, low, up], axis=0)
    half = lax.broadcasted_iota(jnp.int32, (n, LANES), 1) // HEAD
    row_lo = lax.broadcasted_iota(jnp.int32, (LANES, GLA_DV), 0) < HEAD

    @pl.when(s == 0)
    def _():
        sf_ref[...] = jnp.zeros_like(sf_ref)
        sb_ref[...] = jnp.zeros_like(sb_ref)

    def run(fwd, bwd, o_f, o_b):
        o_f[0], o_b[0] = _gla_block(tuple(r[0] for r in fwd), tuple(r[0] for r in bwd), w2_ref, b2_ref,
                                    sf_ref, sb_ref, sums_f, sums_b, cmask, half, row_lo)

    ctx = (qc_ref, kc_ref, vc_ref, ac_ref)

    @pl.when(s == 0)
    def _():
        run(ctx, ctx, ocf_ref, ocb_ref)

    @pl.when(s > 0)
    def _():
        run((qf_ref, kf_ref, vf_ref, af_ref), (qb_ref, kb_ref, vb_ref, ab_ref), of_ref, ob_ref)


def gla_scan(p3, pc3, w2, b2):
    B = p3.shape[0]
    nb = T // GLA_BLOCK
    cq, ck, cv, ca = COL["gla_q"] // 256, COL["gla_k"] // 256, COL["gla_v"] // 512, COL["gla_a"] // 128
    fwd = lambda s: jnp.maximum(s - 1, 0)
    bwd = lambda s: nb - jnp.maximum(s, 1)

    def spec(width, col, row):
        return pl.BlockSpec((1, GLA_BLOCK, width), lambda b, s: (b, row(s), col))

    zero = lambda s: 0
    in_specs = ([spec(256, cq, zero), spec(256, ck, zero), spec(512, cv, zero), spec(128, ca, zero)]
                + [spec(256, cq, fwd), spec(256, ck, fwd), spec(512, cv, fwd), spec(128, ca, fwd)]
                + [spec(256, cq, bwd), spec(256, ck, bwd), spec(512, cv, bwd), spec(128, ca, bwd)]
                + [pl.BlockSpec((128, 512), lambda b, s: (0, 0)), pl.BlockSpec((1, 512), lambda b, s: (0, 0))])
    out_specs = [spec(512, 0, zero), spec(512, 0, zero), spec(512, 0, fwd), spec(512, 0, bwd)]
    out_shape = [jax.ShapeDtypeStruct((B, L, 512), F32)] * 2 + [jax.ShapeDtypeStruct((B, T, 512), F32)] * 2
    return pl.pallas_call(
        _gla_kernel, grid=(B, nb + 1), in_specs=in_specs, out_specs=out_specs, out_shape=out_shape,
        scratch_shapes=[pltpu.VMEM((256, 128), F32), pltpu.VMEM((256, 128), F32)],
        compiler_params=_cp("parallel", "arbitrary"), name="gla_scan",
    )(pc3, pc3, pc3, pc3, p3, p3, p3, p3, p3, p3, p3, p3, w2, b2)


def _gla_gate(o, r, g):
    outs = []
    for hd in range(GLA_HEADS):
        sl = slice(hd * GLA_DV, (hd + 1) * GLA_DV)
        rh = r[:, sl]
        outs.append(_rms(o[:, sl], g) * (rh * _sigmoid(rh)))
    return jnp.concatenate(outs, axis=1)


def _stack_halves(qt, half):
    zero = jnp.zeros_like(qt)
    return jnp.concatenate([jnp.where(half == 0, qt, zero), jnp.where(half == 1, qt, zero)], axis=0).astype(BF16)


def _merge_halves(o, half):
    n = o.shape[0] // 2
    return jnp.where(half == 0, o[:n], o[n:])


def _softmax_parts(s_list, sink=None):
    m = s_list[0].max(axis=1, keepdims=True)
    for s in s_list[1:]:
        m = jnp.maximum(m, s.max(axis=1, keepdims=True))
    if sink is not None:
        m = jnp.maximum(m, sink)
    es = [jnp.exp(s - m) for s in s_list]
    den = es[0].sum(axis=1, keepdims=True)
    for e in es[1:]:
        den = den + e.sum(axis=1, keepdims=True)
    if sink is not None:
        den = den + jnp.exp(sink - m)
    return [e.astype(BF16) for e in es], 1.0 / den


def _rope(x, cos, sin, lane):
    partner = jnp.where((lane % HEAD) < HEAD // 2, pltpu.roll(x, LANES - HEAD // 2, 1),
                        pltpu.roll(x, HEAD // 2, 1))
    return x * cos + partner * sin


def _swa_kernel(sink_ref, mask_ref, q_ref, kp_ref, kc_ref, kn_ref, vp_ref, vc_ref, vn_ref, kx_ref, vx_ref,
                cos_ref, sin_ref, o_ref):
    n = pl.program_id(1)
    nb = pl.num_programs(1)
    blk = SWA_BLOCK
    lane = lax.broadcasted_iota(jnp.int32, (blk, LANES), 1)
    half = lane // HEAD

    def tab(ref, i):
        return ref[pl.ds(pl.multiple_of(i * blk, blk), blk), :]

    ip, inx = jnp.maximum(n - 1, 0), jnp.minimum(n + 1, nb - 1)
    kb = jnp.concatenate([_rope(kp_ref[0].astype(F32), tab(cos_ref, ip), tab(sin_ref, ip), lane),
                          _rope(kc_ref[0].astype(F32), tab(cos_ref, n), tab(sin_ref, n), lane),
                          _rope(kn_ref[0].astype(F32), tab(cos_ref, inx), tab(sin_ref, inx), lane)],
                         axis=0).astype(BF16)
    vb = jnp.concatenate([vp_ref[0], vc_ref[0], vn_ref[0]], axis=0)
    cos_q, sin_q = tab(cos_ref, n), tab(sin_ref, n)
    qs = jnp.concatenate([_stack_halves(_rope(q_ref[0, :, p * LANES:(p + 1) * LANES].astype(F32), cos_q, sin_q, lane)
                                        * HEAD ** -0.5, half) for p in range(4)], axis=0)
    s_loc = _dot_nt(qs, kb) + mask_ref[...]
    s_loc = jnp.concatenate([s_loc[:, :blk] + jnp.where(n == 0, NEG, 0.0), s_loc[:, blk:2 * blk],
                             s_loc[:, 2 * blk:] + jnp.where(n == nb - 1, NEG, 0.0)], axis=1)
    s_ctx = _dot_nt(qs, kx_ref[0])
    grp = lax.broadcasted_iota(jnp.int32, (8 * blk, 1), 0) // blk
    sink = jnp.zeros((8 * blk, 1), F32)
    for g in range(8):
        sink = jnp.where(grp == g, sink_ref[g // 2 + 4 * (g % 2)], sink)
    (e_loc, e_ctx), inv = _softmax_parts([s_loc, s_ctx], sink=sink)
    o = (_dot(e_loc, vb) + _dot(e_ctx, vx_ref[0])) * inv
    for p in range(4):
        o_ref[0, :, p * LANES:(p + 1) * LANES] = _merge_halves(o[2 * p * blk:2 * (p + 1) * blk], half).astype(BF16)


def swa(p3, pc3, sink, cos_t, sin_t):
    B = p3.shape[0]
    nb = T // SWA_BLOCK
    qi = np.arange(SWA_BLOCK)[:, None]
    kj = np.arange(3 * SWA_BLOCK)[None, :]
    mask = jnp.asarray(np.tile(np.where(np.abs(kj - SWA_BLOCK - qi) <= SWA_WINDOW, 0.0, NEG), (8, 1)).astype(np.float32))
    ck, cv = COL["swa_k"] // 128, COL["swa_v"] // 128
    prev = lambda n: jnp.maximum(n - 1, 0)
    cur = lambda n: n
    nxt = lambda n: jnp.minimum(n + 1, nb - 1)
    kv = lambda col, row: pl.BlockSpec((1, SWA_BLOCK, 128), lambda b, n: (b, row(n), col))
    full = lambda col: pl.BlockSpec((1, L, 128), lambda b, n: (b, 0, col))
    tab = pl.BlockSpec((T, 128), lambda b, n: (0, 0))
    return pl.pallas_call(
        _swa_kernel, grid=(B, nb),
        in_specs=[pl.BlockSpec(memory_space=pltpu.SMEM),
                  pl.BlockSpec((8 * SWA_BLOCK, 3 * SWA_BLOCK), lambda b, n: (0, 0)),
                  pl.BlockSpec((1, SWA_BLOCK, 512), lambda b, n: (b, n, COL["swa_q"] // 512)),
                  kv(ck, prev), kv(ck, cur), kv(ck, nxt), kv(cv, prev), kv(cv, cur), kv(cv, nxt),
                  full(ck), full(cv), tab, tab],
        out_specs=pl.BlockSpec((1, SWA_BLOCK, 512), lambda b, n: (b, n, 0)),
        out_shape=jax.ShapeDtypeStruct((B, T, 512), BF16),
        compiler_params=_cp("parallel", "parallel"), name="swa",
    )(sink, mask, p3, p3, p3, p3, p3, p3, p3, pc3, pc3, cos_t, sin_t)


def _ctx_attn_kernel(sink_ref, q_ref, k_ref, v_ref, o_ref, *, use_sink, kv_tiles):
    n = q_ref.shape[1]
    half = lax.broadcasted_iota(jnp.int32, (n, LANES), 1) // HEAD
    for p in range(4):
        sl = slice(p * LANES, (p + 1) * LANES)
        ks = sl if kv_tiles == 4 else slice(0, LANES)
        qs = _stack_halves(q_ref[0, :, sl] * HEAD ** -0.5, half)
        sink = None
        if use_sink:
            top = lax.broadcasted_iota(jnp.int32, (2 * n, 1), 0) < n
            sink = jnp.where(top, sink_ref[p], sink_ref[p + 4])
        (e,), inv = _softmax_parts([_dot_nt(qs, k_ref[0, :, ks])], sink=sink)
        o = _dot(e, v_ref[0, :, ks]) * inv
        o_ref[0, :, sl] = _merge_halves(o, half).astype(BF16)


def ctx_attn(pc3, sink, qcol, kcol, vcol, kv_tiles, use_sink):
    B, w = pc3.shape[0], kv_tiles * 128
    return pl.pallas_call(
        functools.partial(_ctx_attn_kernel, use_sink=use_sink, kv_tiles=kv_tiles), grid=(B,),
        in_specs=[pl.BlockSpec(memory_space=pltpu.SMEM),
                  pl.BlockSpec((1, L, 512), lambda b: (b, 0, qcol // 512)),
                  pl.BlockSpec((1, L, w), lambda b: (b, 0, kcol // w)),
                  pl.BlockSpec((1, L, w), lambda b: (b, 0, vcol // w))],
        out_specs=pl.BlockSpec((1, L, 512), lambda b: (b, 0, 0)),
        out_shape=jax.ShapeDtypeStruct((B, L, 512), BF16),
        compiler_params=_cp("parallel"), name="ctx_attn")(sink, pc3, pc3, pc3)


def _na_kernel(q_ref, k_ref, v_ref, kx_ref, vx_ref, *rest):
    bias_refs, o_ref = rest[:-1], rest[-1]
    nr = len(bias_refs)
    i = pl.program_id(1)
    band = NA_KR * GRID_W
    half = lax.broadcasted_iota(jnp.int32, (GRID_W, LANES), 1) // HEAD
    starts = [pl.multiple_of(jnp.clip(nr * i + k - NA_KR // 2, 0, ROWS - NA_KR) * GRID_W, GRID_W)
              for k in range(nr)]
    s_loc, s_ctx = [], []
    for p in range(4):
        sl = slice(p * LANES, (p + 1) * LANES)
        qs = [_stack_halves(q_ref[0, k * GRID_W:(k + 1) * GRID_W, sl] * HEAD ** -0.5, half) for k in range(nr)]
        for k in range(nr):
            kb = k_ref[0, pl.ds(starts[k], band), sl]
            s_loc.append(_dot_nt(qs[k], kb) + bias_refs[k][0, 2 * p * GRID_W:(2 * p + 2) * GRID_W, :])
        s_ctx.append(_dot_nt(jnp.concatenate(qs, axis=0), kx_ref[0, :, sl]))
    (e_loc, e_ctx), inv = _softmax_parts([jnp.concatenate(s_loc, axis=0), jnp.concatenate(s_ctx, axis=0)])
    blk = 2 * GRID_W
    for p in range(4):
        sl = slice(p * LANES, (p + 1) * LANES)
        r0 = p * nr * blk
        o_ctx = _dot(e_ctx[r0:r0 + nr * blk], vx_ref[0, :, sl])
        for k in range(nr):
            rs = slice(r0 + k * blk, r0 + (k + 1) * blk)
            vb = v_ref[0, pl.ds(starts[k], band), sl]
            o = (_dot(e_loc[rs], vb) + o_ctx[k * blk:(k + 1) * blk]) * inv[rs]
            o_ref[0, k * GRID_W:(k + 1) * GRID_W, sl] = _merge_halves(o, half).astype(BF16)


def na(p3, pc3, bias, rows_per_step=NA_ROWS_PER_STEP):
    B, nr = p3.shape[0], rows_per_step
    cq, ck, cv = COL["na_q"] // 512, COL["na_k"] // 512, COL["na_v"] // 512
    shift = lambda r: r - jnp.clip(r - NA_KR // 2, 0, ROWS - NA_KR)
    bias_spec = lambda k: pl.BlockSpec((1, NA_HEADS * GRID_W, NA_KR * GRID_W),
                                       lambda b, i: (shift(nr * i + k), 0, 0))
    return pl.pallas_call(
        _na_kernel, grid=(B, ROWS // nr),
        in_specs=[pl.BlockSpec((1, nr * GRID_W, 512), lambda b, i: (b, i, cq)),
                  pl.BlockSpec((1, T, 512), lambda b, i: (b, 0, ck)),
                  pl.BlockSpec((1, T, 512), lambda b, i: (b, 0, cv)),
                  pl.BlockSpec((1, L, 512), lambda b, i: (b, 0, ck)),
                  pl.BlockSpec((1, L, 512), lambda b, i: (b, 0, cv))] + [bias_spec(k) for k in range(nr)],
        out_specs=pl.BlockSpec((1, nr * GRID_W, 512), lambda b, i: (b, i, 0)),
        out_shape=jax.ShapeDtypeStruct((B, T, 512), BF16),
        compiler_params=_cp("parallel", "arbitrary"), name="na")(p3, p3, p3, pc3, pc3, *([bias] * nr))


def na_bias_table(rpb):
    col = np.arange(GRID_W)
    col_start = np.clip(col - NA_KC // 2, 0, GRID_W - NA_KC)
    col_ok = (col[None, :] >= col_start[:, None]) & (col[None, :] < col_start[:, None] + NA_KC)
    col_off = np.clip(col[None, :] - col[:, None], -(NA_KC - 1), NA_KC - 1) + (NA_KC - 1)
    onehot = (col_off[:, :, None] == np.arange(2 * NA_KC - 1)).astype(np.float32)
    toep = jnp.einsum("hro,qko->hrqk", rpb, jnp.asarray(onehot), precision=HIGHEST)
    toep = jnp.where(jnp.asarray(col_ok)[None, None], toep, NEG)
    per_shift = [toep[:, NA_KR - 1 - s:2 * NA_KR - 1 - s] for s in range(NA_KR)]
    tab = jnp.stack(per_shift).transpose(0, 1, 3, 2, 4)
    return tab.reshape(NA_KR, NA_HEADS * GRID_W, NA_KR * GRID_W).astype(F32)


def _gelu(x):
    return 0.5 * x * (1.0 + jnp.tanh(np.sqrt(2.0 / np.pi).astype(np.float32) * (x + 0.044715 * (x * x * x))))


def _sgu_kernel(u_ref, v_ref, g_ref, b_ref, ws_ref, bs_ref, y_ref):
    n_chunk = u_ref.shape[1] // SGU_CHUNK
    v = _gelu(v_ref[0].astype(F32))
    mu = jnp.mean(v, axis=-1, keepdims=True)
    var = jnp.mean(jnp.square(v - mu), axis=-1, keepdims=True)
    vn = ((v - mu) * lax.rsqrt(var + EPS) * g_ref[...] + b_ref[...]).astype(BF16)
    cols = []
    for g in range(4):
        gs = slice(g * 128, (g + 1) * 128)
        rhs = jnp.concatenate([vn[c * SGU_CHUNK:(c + 1) * SGU_CHUNK, gs] for c in range(n_chunk)], axis=1)
        mixed = _dot(ws_ref[g], rhs)
        cols.append(jnp.concatenate([mixed[:, c * 128:(c + 1) * 128] + bs_ref[g] for c in range(n_chunk)], axis=0))
    y_ref[0] = (_gelu(u_ref[0].astype(F32)) * jnp.concatenate(cols, axis=1)).astype(BF16)


def sgu(p3, g, b, ws, bs):
    B, n = p3.shape[0], p3.shape[1]
    tr = min(n, TOKEN_TILE)
    blk = lambda col: pl.BlockSpec((1, tr, 512), lambda bb, i: (bb, i, col))
    const = lambda shape: pl.BlockSpec(shape, lambda bb, i: (0,) * len(shape))
    return pl.pallas_call(
        _sgu_kernel, grid=(B, n // tr),
        in_specs=[blk(COL["sgu_u"] // 512), blk(COL["sgu_v"] // 512), const((1, 512)), const((1, 512)),
                  const((4, 128, 128)), const((4, 128, 128))],
        out_specs=blk(0), out_shape=jax.ShapeDtypeStruct((B, n, 512), BF16),
        compiler_params=_cp("parallel", "parallel"), name="sgu")(p3, p3, g, b, ws, bs)


def _merge_kernel(h_ref, of_ref, ob_ref, r_ref, gn_ref, b_ref, n_ref, d_ref, wg_ref, bg_ref, wb_ref, o_ref,
                  a_ref):
    @pl.when(pl.program_id(1) == 0)
    def _():
        a_ref[...] = _gla_gate(of_ref[...] + ob_ref[...], r_ref[...].astype(F32), gn_ref[...]).astype(BF16)

    h = h_ref[...]
    acc = None
    for i, br in enumerate((a_ref, b_ref, n_ref, d_ref)):
        term = _sigmoid(_dot(h, wg_ref[0, i].astype(BF16)) + bg_ref[i]) * _dot(br[...], wb_ref[i])
        acc = term if acc is None else acc + term
    o_ref[...] = acc.astype(BF16)


def merge(h, o_f, o_b, p, gn, branches, wg, l, bg, wb, tm):
    M, tn = h.shape[0], MERGE_COL_TILE
    tm = min(tm, M)
    br = pl.BlockSpec((tm, 512), lambda i, j: (i, 0))
    return pl.pallas_call(
        _merge_kernel, grid=(M // tm, D // tn),
        in_specs=[pl.BlockSpec((tm, D), lambda i, j: (i, 0)), br, br,
                  pl.BlockSpec((tm, 512), lambda i, j: (i, COL["gla_r"] // 512)),
                  pl.BlockSpec((1, GLA_DV), lambda i, j: (0, 0)), br, br, br,
                  pl.BlockSpec((1, 4, D, tn), lambda i, j: (l, 0, 0, j)),
                  pl.BlockSpec((4, 1, tn), lambda i, j: (0, 0, j)),
                  pl.BlockSpec((4, 512, tn), lambda i, j: (0, 0, j))],
        out_specs=pl.BlockSpec((tm, tn), lambda i, j: (i, j)),
        out_shape=jax.ShapeDtypeStruct((M, D), BF16),
        scratch_shapes=[pltpu.VMEM((tm, 512), BF16)],
        compiler_params=_cp("parallel", "arbitrary"), name="merge")(h, o_f, o_b, p, gn, *branches, wg, bg, wb)


def _outproj_kernel(acc_ref, w_ref, x_ref, gate_ref, g_ref, shift_ref, scale_ref, wr_ref,
                    x1_ref, h2_ref, lg_ref):
    n_sub = 2
    sub = acc_ref.shape[0] // n_sub
    for k in range(n_sub):
        rs = slice(k * sub, (k + 1) * sub)
        x1 = x_ref[rs, :] + gate_ref[0] * _dot(acc_ref[rs, :], w_ref[...])
        x1_ref[rs, :] = x1
        h2 = (_rms(x1, g_ref[...]) * (1.0 + scale_ref[0]) + shift_ref[0]).astype(BF16)
        lg_ref[rs, :] = _dot(h2, wr_ref[...])
        bits = pltpu.bitcast(h2.astype(F32), jnp.int32)
        h2_ref[rs, :] = lax.shift_right_logical(bits[:, :D // 2], 16) | (bits[:, D // 2:] & jnp.int32(-65536))


def outproj(acc, w, x2, mod3, g, wr, tm, mod_row):
    M = acc.shape[0]
    tm = min(tm, M)
    row = lambda k: pl.BlockSpec((1, 1, D), lambda i: (mod_row(i) * 6 + k, 0, 0))
    tile = pl.BlockSpec((tm, D), lambda i: (i, 0))
    return pl.pallas_call(
        _outproj_kernel, grid=(M // tm,),
        in_specs=[tile, pl.BlockSpec((D, D), lambda i: (0, 0)), tile, row(2),
                  pl.BlockSpec((1, D), lambda i: (0, 0)), row(3), row(4),
                  pl.BlockSpec((D, 128), lambda i: (0, 0))],
        out_specs=[tile, pl.BlockSpec((tm, D // 2), lambda i: (i, 0)), pl.BlockSpec((tm, 128), lambda i: (i, 0))],
        out_shape=[jax.ShapeDtypeStruct((M, D), F32), jax.ShapeDtypeStruct((M, D // 2), jnp.int32),
                   jax.ShapeDtypeStruct((M, 128), F32)],
        compiler_params=_cp("parallel"), name="outproj")(acc, w, x2, mod3, g, mod3, mod3, wr)


def _select_kernel(lg_ref, sel_ref, aff_ref, slot_ref, *, cap):
    n = lg_ref.shape[1]
    nt = n // LANES
    lane = lax.broadcasted_iota(jnp.int32, (n, LANES), 1)
    lg = jnp.where(lane < N_EXPERTS, lg_ref[0], NEG)
    e = jnp.exp(lg - lg.max(axis=1, keepdims=True))
    aff = (e / e.sum(axis=1, keepdims=True)).T[:N_EXPERTS]
    aff_ref[0] = aff
    bits = pltpu.bitcast(aff, jnp.int32)

    def count(mask):
        return jnp.sum(jnp.where(mask, 1.0, 0.0), axis=1, keepdims=True)

    def bit_step(i, thr):
        cand = thr | lax.shift_left(jnp.int32(1), 30 - i)
        return jnp.where(count(bits >= cand) >= cap, cand, thr)

    thr = lax.fori_loop(0, 31, bit_step, jnp.zeros((N_EXPERTS, 1), jnp.int32))
    gt, eq = bits > thr, bits == thr

    r = lax.broadcasted_iota(jnp.int32, (LANES, LANES), 0)
    c = lax.broadcasted_iota(jnp.int32, (LANES, LANES), 1)
    upper = (r <= c).astype(BF16)
    r2 = lax.broadcasted_iota(jnp.int32, (nt * N_EXPERTS, nt * N_EXPERTS), 0)
    c2 = lax.broadcasted_iota(jnp.int32, (nt * N_EXPERTS, nt * N_EXPERTS), 1)
    before = ((r2 % N_EXPERTS == c2 % N_EXPERTS) & (c2 // N_EXPERTS < r2 // N_EXPERTS)).astype(BF16)

    def excl_prefix(x):
        xs = jnp.concatenate([x[:, i * LANES:(i + 1) * LANES] for i in range(nt)], axis=0).astype(BF16)
        inc = _dot(xs, upper)
        tot = jnp.broadcast_to(inc[:, LANES - 1:LANES], inc.shape).astype(BF16)
        full = inc + _dot(before, tot)
        return jnp.concatenate([full[i * N_EXPERTS:(i + 1) * N_EXPERTS] for i in range(nt)], axis=1) - x

    eq01 = jnp.where(eq, 1.0, 0.0)
    mask = gt | (eq & (excl_prefix(eq01) < cap - count(gt)))
    slot = jnp.where(mask, excl_prefix(jnp.where(mask, 1.0, 0.0)), -1.0)
    sel_ref[0] = slot.astype(jnp.int32)
    pad = jnp.full((LANES - N_EXPERTS, n), -1.0, F32)
    slot_ref[0] = jnp.concatenate([slot, pad], axis=0).T


def select(lg3, cap):
    B, n = lg3.shape[0], lg3.shape[1]
    tok = pl.BlockSpec((1, n, 128), lambda b: (b, 0, 0))
    row = pl.BlockSpec((1, N_EXPERTS, n), lambda b: (b, 0, 0))
    return pl.pallas_call(
        functools.partial(_select_kernel, cap=cap), grid=(B,),
        in_specs=[tok], out_specs=[row, row, tok],
        out_shape=[jax.ShapeDtypeStruct((B, N_EXPERTS, n), jnp.int32),
                   jax.ShapeDtypeStruct((B, N_EXPERTS, n), F32), jax.ShapeDtypeStruct((B, n, 128), F32)],
        compiler_params=_cp("parallel"), name="select")(lg3)


def _slot_tokens_kernel(sel_ref, aff_ref, idx_ref, g_ref, *, cap):
    n = sel_ref.shape[2]
    row = lax.broadcasted_iota(jnp.int32, (cap, n), 0)
    lane = lax.broadcasted_iota(jnp.int32, (LANES, n), 0)
    tok = lax.broadcasted_iota(jnp.int32, (LANES, n), 1)
    pos = jnp.where(lane == 0, tok // LANES, jnp.where(lane == 1, tok % LANES, 0)).astype(F32)
    for e in range(N_EXPERTS):
        onehot = jnp.where(row == sel_ref[0, e:e + 1, :], 1.0, 0.0).astype(BF16)
        a = aff_ref[0, e:e + 1, :]
        hi = a.astype(BF16)
        r1 = a - hi.astype(F32)
        mid = r1.astype(BF16)
        lo = (r1 - mid.astype(F32)).astype(BF16)
        cols = jnp.where(lane == 2, hi.astype(F32), jnp.where(lane == 3, mid.astype(F32),
                         jnp.where(lane == 4, lo.astype(F32), pos))).astype(BF16)
        r = _dot_nt(onehot, cols)
        idx_ref[e, 0] = (r[:, 0:1] * LANES + r[:, 1:2]).astype(jnp.int32) + pl.program_id(0) * n
        g_ref[e, 0] = r[:, 2:3] + r[:, 3:4] + r[:, 4:5]


def slot_tokens(sel, aff, cap):
    B, n = sel.shape[0], sel.shape[2]
    row = pl.BlockSpec((1, N_EXPERTS, n), lambda b: (b, 0, 0))
    col = pl.BlockSpec((N_EXPERTS, 1, cap, 1), lambda b: (0, b, 0, 0))
    idx, gate = pl.pallas_call(
        functools.partial(_slot_tokens_kernel, cap=cap), grid=(B,),
        in_specs=[row, row], out_specs=[col, col],
        out_shape=[jax.ShapeDtypeStruct((N_EXPERTS, B, cap, 1), jnp.int32),
                   jax.ShapeDtypeStruct((N_EXPERTS, B, cap, 1), F32)],
        compiler_params=_cp("parallel"), name="slot_tokens")(sel, aff)
    return idx.reshape(N_EXPERTS, B * cap, 1), gate.reshape(N_EXPERTS, B * cap, 1)


def _expert_kernel(idx_ref, h_hbm, gate_ref, wg_ref, wu_ref, wd_ref, o_ref, xg_ref, xb_ref, acc_ref, sem,
                   *, tm):
    e, i, f = pl.program_id(0), pl.program_id(1), pl.program_id(2)
    n_i, n_f = pl.num_programs(1), pl.num_programs(2)
    q = e * n_i + i
    n_q = pl.num_programs(0) * n_i
    slot = q % 2
    share = tm // n_f

    def row_copy(tile, r, dst_slot):
        src = h_hbm.at[pl.ds(idx_ref[tile * tm + r], 1), :]
        return pltpu.make_async_copy(src, xg_ref.at[dst_slot, pl.ds(r, 1), :], sem.at[dst_slot])

    def wait_tile(dst_slot):
        pltpu.make_async_copy(h_hbm.at[pl.ds(0, tm), :], xg_ref.at[dst_slot], sem.at[dst_slot]).wait()

    @pl.when((q == 0) & (f == 0))
    def _():
        def first(r, carry):
            row_copy(0, r, 0).start()
            return carry
        lax.fori_loop(0, tm, first, 0)

    @pl.when(f == 0)
    def _():
        wait_tile(slot)
        p = xg_ref[slot]
        xb_ref[:, :D // 2] = pltpu.bitcast(lax.shift_left(p, 16), F32).astype(BF16)
        xb_ref[:, D // 2:] = pltpu.bitcast(p & jnp.int32(-65536), F32).astype(BF16)
        acc_ref[...] = jnp.zeros_like(acc_ref)

    x = xb_ref[...]
    g = _dot(x, wg_ref[0, 0].astype(BF16))
    u = _dot(x, wu_ref[0, 0].astype(BF16))
    hid = ((g * _sigmoid(g)) * u).astype(BF16)
    nxt = jnp.minimum(q + 1, n_q - 1)
    for r in range(share):
        row_copy(nxt, f * share + r, 1 - slot).start()
    acc_ref[...] += _dot(hid, wd_ref[0, 0].astype(BF16))

    @pl.when(f == n_f - 1)
    def _():
        o_ref[0] = (acc_ref[...] * gate_ref[0]).astype(BF16)

    @pl.when((q == n_q - 1) & (f == n_f - 1))
    def _():
        wait_tile(1 - slot)


def experts(idx, h2p, gate, wg, wu, wd, l, tm):
    E, M, tf = gate.shape[0], gate.shape[1], EXPERT_HIDDEN_TILE
    F = wg.shape[3]
    tm = min(tm, M)
    grid_spec = pltpu.PrefetchScalarGridSpec(
        num_scalar_prefetch=1, grid=(E, M // tm, F // tf),
        in_specs=[pl.BlockSpec(memory_space=pl.ANY),
                  pl.BlockSpec((1, tm, 1), lambda e, i, f, idx: (e, i, 0)),
                  pl.BlockSpec((1, 1, D, tf), lambda e, i, f, idx: (l, e, 0, f)),
                  pl.BlockSpec((1, 1, D, tf), lambda e, i, f, idx: (l, e, 0, f)),
                  pl.BlockSpec((1, 1, tf, D), lambda e, i, f, idx: (l, e, f, 0))],
        out_specs=pl.BlockSpec((1, tm, D), lambda e, i, f, idx: (e, i, 0)),
        scratch_shapes=[pltpu.VMEM((2, tm, D // 2), jnp.int32), pltpu.VMEM((tm, D), BF16),
                        pltpu.VMEM((tm, D), F32), pltpu.SemaphoreType.DMA((2,))])
    return pl.pallas_call(
        functools.partial(_expert_kernel, tm=tm), grid_spec=grid_spec,
        out_shape=jax.ShapeDtypeStruct((E, M, D), BF16),
        compiler_params=_cp("arbitrary", "arbitrary", "arbitrary"), name="experts")(idx, h2p, gate, wg, wu, wd)


COMBINE_GROUP = 8


def _combine_kernel(slot_ref, ye_ref, x_ref, gate_ref, gfin_ref, o_ref, acc_ref, *, cap, final):
    g = pl.program_id(2)
    tt = x_ref.shape[1]
    lane = lax.broadcasted_iota(jnp.int32, (tt, LANES), 1)
    j = lax.broadcasted_iota(jnp.int32, (tt, cap), 1).astype(F32)
    slots = slot_ref[0]
    parts = []
    for k in range(COMBINE_GROUP):
        slot = jnp.sum(jnp.where(lane == g * COMBINE_GROUP + k, slots, 0.0), axis=1, keepdims=True)
        parts.append(jnp.where(j == slot, 1.0, 0.0).astype(BF16))
    onehot = jnp.concatenate(parts, axis=1)
    contrib = _dot(onehot, ye_ref[:, 0].reshape(COMBINE_GROUP * cap, D))

    @pl.when(g == 0)
    def _():
        acc_ref[...] = contrib

    @pl.when(g == pl.num_programs(2) - 1)
    def _():
        x2 = x_ref[0] + gate_ref[0] * (acc_ref[...] + contrib)
        if final:
            x2 = _rms(x2, gfin_ref[...])
        o_ref[0] = x2


def combine(slot_tok, ye, x3, mod3, gfin, cap, mod_row, final):
    B, n = x3.shape[0], x3.shape[1]
    tt = min(n, TOKEN_TILE)
    assert N_EXPERTS == 2 * COMBINE_GROUP
    xt = pl.BlockSpec((1, tt, D), lambda b, t, g: (b, t, 0))
    return pl.pallas_call(
        functools.partial(_combine_kernel, cap=cap, final=final), grid=(B, n // tt, N_EXPERTS // COMBINE_GROUP),
        in_specs=[pl.BlockSpec((1, tt, 128), lambda b, t, g: (b, t, 0)),
                  pl.BlockSpec((COMBINE_GROUP, 1, cap, D), lambda b, t, g: (g, b, 0, 0)), xt,
                  pl.BlockSpec((1, 1, D), lambda b, t, g: (mod_row(b) * 6 + 5, 0, 0)),
                  pl.BlockSpec((1, D), lambda b, t, g: (0, 0))],
        out_specs=xt, out_shape=jax.ShapeDtypeStruct((B, n, D), F32),
        scratch_shapes=[pltpu.VMEM((tt, D), F32)],
        compiler_params=_cp("parallel", "parallel", "arbitrary"), name="combine",
    )(slot_tok, ye.reshape(N_EXPERTS, B, cap, D), x3, mod3, gfin)


def _swa_q_perm():
    return np.concatenate([np.arange(HEAD) + (p + 4 * hh) * HEAD for p in range(4) for hh in range(2)])


def _prep_w_in(w):
    seg = lambda name, width: w[:, REF[name]:REF[name] + width]
    swa_q = seg("swa_q", 512)[:, _swa_q_perm()]
    parts = [seg("gla_v", 512), seg("gla_r", 512), swa_q, seg("na_q", 512), seg("na_k", 512), seg("na_v", 512),
             seg("sgu_u", 512), seg("sgu_v", 512), seg("gla_q", 256), seg("gla_k", 256), seg("swa_k", 128),
             seg("swa_v", 128), seg("gla_af", 16), seg("gla_ab", 16)]
    out = jnp.concatenate(parts, axis=1)
    return jnp.pad(out, ((0, 0), (0, NP - out.shape[1]))).astype(BF16)


def _rope_tables():
    quarter = HEAD // 4
    freqs = 10000.0 ** (-jnp.arange(quarter, dtype=F32) / quarter)
    t = jnp.arange(T)
    row = (t // GRID_W).astype(F32)
    col = (t % GRID_W).astype(F32)
    ang = jnp.concatenate([row[:, None] * freqs, col[:, None] * freqs], axis=-1)
    cos, sin = jnp.cos(ang), jnp.sin(ang)
    cos_t = jnp.tile(cos, (1, 4))
    sin_t = jnp.tile(jnp.concatenate([-sin, sin], axis=-1), (1, 2))
    return cos_t, sin_t


def _moe(h2, lg, x3, mod3, gfin, wg, wu, wd, l, cap, tm, mod_row, final):
    B, n = x3.shape[0], x3.shape[1]
    sel, aff, slot_tok = select(lg.reshape(B, n, 128), cap)
    idx, gate = slot_tokens(sel, aff, cap)
    ye = experts(idx.reshape(-1), h2, gate, wg, wu, wd, l, tm)
    return combine(slot_tok, ye, x3, mod3, gfin, cap, mod_row, final)


def kernel(x, c, ctx, c_ctx, w_ada, b_ada, norm1_g, norm2_g, w_in, gla_w_a2, gla_b_a2, gla_norm_g, swa_sink,
           na_rpb, sgu_ln_g, sgu_ln_b, sgu_w_s, sgu_b_s, w_gate, b_gate, w_branch, w_out, w_router,
           w_exp_gate, w_exp_up, w_exp_down, final_norm_g):
    B = x.shape[0]
    ctx_row = B
    n_mod = -(-(B + 1) // SUBLANES) * SUBLANES
    cvec = jnp.zeros((n_mod, D), F32).at[:B].set(c).at[ctx_row].set(c_ctx)
    cos_t, sin_t = _rope_tables()
    perm = _swa_q_perm()
    gfin = final_norm_g.reshape(1, D)
    tm = ROW_TILE
    lat_tile_row = lambda i: i // (T // tm)
    ctx_tile_row = lambda i: ctx_row
    x2 = x.reshape(B * T, D)
    xc2 = ctx.reshape(B * L, D)
    for l in range(DEPTH):
        need_ctx = l < DEPTH - 1
        mod3 = ada(cvec, w_ada, b_ada.reshape(DEPTH, 1, 6 * D), l).reshape(n_mod * 6, 1, D)
        w_in_p = _prep_w_in(w_in[l])
        g1 = norm1_g[l].reshape(1, D)
        h, p = inproj(x2, mod3, g1, w_in_p, tm, lat_tile_row)
        hc, pc = inproj(xc2, mod3, g1, w_in_p, tm, ctx_tile_row)
        p3, pc3 = p.reshape(B, T, NP), pc.reshape(B, L, NP)

        w2 = jnp.zeros((128, 512), F32).at[:GLA_RANK, :256].set(gla_w_a2[l, 0])
        w2 = w2.at[GLA_RANK:2 * GLA_RANK, 256:].set(gla_w_a2[l, 1]).astype(BF16)
        b2 = gla_b_a2[l].reshape(1, 512)
        ocf, ocb, of, ob = gla_scan(p3, pc3, w2, b2)
        gn = gla_norm_g[l].reshape(1, GLA_DV)
        yb = swa(p3, pc3, swa_sink[l], cos_t, sin_t)
        yn = na(p3, pc3, na_bias_table(na_rpb[l]))
        ln_g, ln_b = sgu_ln_g[l].reshape(1, 512), sgu_ln_b[l].reshape(1, 512)
        ws = sgu_w_s[l].astype(BF16)
        bs = jnp.broadcast_to(sgu_b_s[l][:, :, None], (4, SGU_CHUNK, 128))
        yd = sgu(p3, ln_g, ln_b, ws, bs)

        bg = b_gate[l].reshape(4, 1, D)
        wb = w_branch[l].at[1].set(w_branch[l, 1][perm]).astype(BF16)
        wo = w_out[l].astype(BF16)
        wr = jnp.pad(w_router[l], ((0, 0), (0, 128 - N_EXPERTS))).astype(BF16)
        g2 = norm2_g[l].reshape(1, D)
        flat = lambda y: y.reshape(-1, 512)
        acc = merge(h, flat(of), flat(ob), p, gn, [flat(yb), flat(yn), flat(yd)], w_gate, l, bg, wb, tm)
        x1, h2, lg = outproj(acc, wo, x2, mod3, g2, wr, OUTPROJ_ROW_TILE, lambda i: i // (T // OUTPROJ_ROW_TILE))
        x2 = _moe(h2, lg, x1.reshape(B, T, D), mod3, gfin, w_exp_gate, w_exp_up, w_exp_down, l,
                  2 * T // N_EXPERTS, tm, lambda b: b, final=not need_ctx).reshape(B * T, D)
        if need_ctx:
            ybc = ctx_attn(pc3, swa_sink[l], COL["swa_q"], COL["swa_k"], COL["swa_v"], 1, True)
            ync = ctx_attn(pc3, swa_sink[l], COL["na_q"], COL["na_k"], COL["na_v"], 4, False)
            ydc = sgu(pc3, ln_g, ln_b, ws, bs)
            accc = merge(hc, flat(ocf), flat(ocb), pc, gn, [flat(ybc), flat(ync), flat(ydc)], w_gate, l, bg, wb, tm)
            xc1, hc2, lgc = outproj(accc, wo, xc2, mod3, g2, wr, OUTPROJ_ROW_TILE, ctx_tile_row)
            xc2 = _moe(hc2, lgc, xc1.reshape(B, L, D), mod3, gfin, w_exp_gate, w_exp_up, w_exp_down, l,
                       2 * L // N_EXPERTS, tm, lambda b: ctx_row, final=False).reshape(B * L, D)
    return x2.reshape(B, T, D)
```

```python
import functools

import numpy as np
import jax
import jax.numpy as jnp
from jax import lax
from jax.experimental import pallas as pl
from jax.experimental.pallas import tpu as pltpu

F32 = jnp.float32
BF16 = jnp.bfloat16
HIGHEST = lax.Precision.HIGHEST

D = 2048
T = 2048
L = 256
GRID_W = 64
ROWS = T // GRID_W
EPS = 1e-6
DEPTH = 2
NEG = -1e30

GLA_HEADS, GLA_DK, GLA_DV, GLA_RANK, GLA_TAU, GLA_CHUNK = 4, 64, 128, 16, 16.0, 64
GLA_BLOCK = 256
SWA_BLOCK, SWA_WINDOW = 128, 128
NA_KR, NA_KC, NA_HEADS = 8, 16, 8
SGU_CHUNK = 128
N_EXPERTS = 16
LANES = 128
HEAD = 64

NP = 5120
COL = dict(gla_v=0, gla_r=512, swa_q=1024, na_q=1536, na_k=2048, na_v=2560, sgu_u=3072, sgu_v=3584,
           gla_q=4096, gla_k=4352, swa_k=4608, swa_v=4736, gla_a=4864)
REF = dict(gla_q=0, gla_k=256, gla_v=512, gla_r=1024, gla_af=1536, gla_ab=1552, swa_q=1568, swa_k=2080,
           swa_v=2208, na_q=2336, na_k=2848, na_v=3360, sgu_u=3872, sgu_v=4384)

VMEM_LIMIT = 56 * 1024 * 1024
SUBLANES = 8

ROW_TILE = 1024
OUTPROJ_ROW_TILE = 512
ADA_COL_TILE = 1024
INPROJ_ROW_TILE = 512
MERGE_COL_TILE = 256
EXPERT_HIDDEN_TILE = 256
TOKEN_TILE = 512
NA_ROWS_PER_STEP = 4


def _cp(*sem, vmem=VMEM_LIMIT):
    return pltpu.CompilerParams(dimension_semantics=sem, vmem_limit_bytes=vmem)


def _sigmoid(x):
    return 1.0 / (1.0 + jnp.exp(-x))


def _rms(x, g):
    return x * lax.rsqrt(jnp.mean(x * x, axis=-1, keepdims=True) + EPS) * g


def _dot(a, b):
    return jnp.dot(a, b, preferred_element_type=F32)


def _dot_nt(a, b):
    return lax.dot_general(a, b, (((1,), (1,)), ((), ())), preferred_element_type=F32)


def _ada_kernel(c_ref, w_ref, b_ref, o_ref):
    c = c_ref[...]
    act = (c * _sigmoid(c)).astype(BF16)
    o_ref[...] = _dot(act, w_ref[0].astype(BF16)) + b_ref[0]


def ada(cvec, w, b, l):
    rows, tn = cvec.shape[0], ADA_COL_TILE
    return pl.pallas_call(
        _ada_kernel, grid=(6 * D // tn,),
        in_specs=[pl.BlockSpec((rows, D), lambda j: (0, 0)),
                  pl.BlockSpec((1, D, tn), lambda j: (l, 0, j)),
                  pl.BlockSpec((1, 1, tn), lambda j: (l, 0, j))],
        out_specs=pl.BlockSpec((rows, tn), lambda j: (0, j)),
        out_shape=jax.ShapeDtypeStruct((rows, 6 * D), F32),
        compiler_params=_cp("parallel"), name="ada")(cvec, w, b)


def _inproj_kernel(x_ref, shift_ref, scale_ref, g_ref, w_ref, h_ref, p_ref):
    n_sub = 2
    sub = x_ref.shape[0] // n_sub

    def modulate(k):
        rs = slice(k * sub, (k + 1) * sub)
        h = (_rms(x_ref[rs, :], g_ref[...]) * (1.0 + scale_ref[0]) + shift_ref[0]).astype(BF16)
        h_ref[rs, :] = h
        return h

    hs = [modulate(k) for k in range(n_sub)]
    for k in range(n_sub):
        p_ref[k * sub:(k + 1) * sub, :] = _dot(hs[k], w_ref[...]).astype(BF16)


def inproj(x2, mod3, g, w, tm, mod_row):
    M = x2.shape[0]
    tm = min(tm, M)
    return pl.pallas_call(
        _inproj_kernel, grid=(M // tm,),
        in_specs=[pl.BlockSpec((tm, D), lambda i: (i, 0)),
                  pl.BlockSpec((1, 1, D), lambda i: (mod_row(i) * 6 + 0, 0, 0)),
                  pl.BlockSpec((1, 1, D), lambda i: (mod_row(i) * 6 + 1, 0, 0)),
                  pl.BlockSpec((1, D), lambda i: (0, 0)),
                  pl.BlockSpec((D, NP), lambda i: (0, 0), pipeline_mode=pl.Buffered(1))],
        out_specs=[pl.BlockSpec((tm, D), lambda i: (i, 0)),
                   pl.BlockSpec((tm, NP), lambda i: (i, 0))],
        out_shape=[jax.ShapeDtypeStruct((M, D), BF16), jax.ShapeDtypeStruct((M, NP), BF16)],
        compiler_params=_cp("parallel"), name="inproj")(x2, mod3, mod3, g, w)


def _log_sigmoid(x):
    return jnp.minimum(x, 0.0) - jnp.log(1.0 + jnp.exp(-jnp.abs(x)))


def _split3(x):
    hi = x.astype(BF16)
    r1 = x - hi.astype(F32)
    mid = r1.astype(BF16)
    lo = (r1 - mid.astype(F32)).astype(BF16)
    return jnp.concatenate([hi, mid, lo], axis=1)


def _gla_block(fwd, bwd, w2_ref, b2_ref, sf_ref, sb_ref, sums_f, sums_b, cmask, half, row_lo):
    n, ch = GLA_BLOCK, GLA_CHUNK
    n_ch = n // ch
    (qf, kf, vf, af), (qb, kb, vb, ab) = fwd, bwd
    pre = jnp.concatenate([_dot(af.astype(BF16), w2_ref[:, :256]) + b2_ref[:, :256],
                           _dot(ab.astype(BF16), w2_ref[:, 256:]) + b2_ref[:, 256:]], axis=0)
    la = _log_sigmoid(pre) * (1.0 / GLA_TAU)
    la3 = _split3(la)
    cum = jnp.concatenate([_dot(sums_f, la3[:n]), _dot(sums_b, la3[n:])], axis=0)
    bc = cum[:, :256] + cum[:, 256:512] + cum[:, 512:]
    ends = [c * ch + ch - 1 for c in range(n_ch)] + [n + c * ch for c in range(n_ch)]
    bl = jnp.concatenate([jnp.broadcast_to(bc[r:r + 1], (ch, bc.shape[1])) for r in ends], axis=0)
    q = jnp.concatenate([qf, qb], axis=0).astype(F32)
    k = jnp.concatenate([kf, kb], axis=0).astype(F32)
    q_in = (q * GLA_DK ** -0.5) * jnp.exp(bc)
    k_in = (k * jnp.exp(-bc)).astype(BF16)
    kte = k * jnp.exp(bl - bc)
    half2 = jnp.concatenate([half, half], axis=0)
    tiles = range(2)
    sls = [slice(pt * LANES, (pt + 1) * LANES) for pt in tiles]
    qs = [jnp.concatenate([jnp.where(half2 == 0, q_in[:, sl], 0.0), jnp.where(half2 == 1, q_in[:, sl], 0.0)],
                          axis=0).astype(BF16) for sl in sls]
    att = [jnp.where(cmask, jnp.concatenate(
        [_dot_nt(qs[pt][0:n], k_in[:n, sls[pt]]), _dot_nt(qs[pt][n:2 * n], k_in[n:, sls[pt]]),
         _dot_nt(qs[pt][2 * n:3 * n], k_in[:n, sls[pt]]), _dot_nt(qs[pt][3 * n:], k_in[n:, sls[pt]])], axis=0),
        0.0).astype(BF16) for pt in tiles]
    v2 = [[vd[:, 2 * pt * GLA_DV:(2 * pt + 2) * GLA_DV].astype(BF16) for vd in (vf, vb)] for pt in tiles]
    intra = [[_dot(att[pt][(2 * hh + d) * n:(2 * hh + d + 1) * n], v2[pt][d][:, hh * GLA_DV:(hh + 1) * GLA_DV])
              for hh in range(2) for d in range(2)] for pt in tiles]
    kte_t = [kte[:, sl].T.astype(BF16) for sl in sls]
    la_t = [la[:, sl].T for sl in sls]
    st = [[ref[sl, :] for ref in (sf_ref, sb_ref)] for sl in sls]
    inter = [[[None] * n_ch for _ in range(2)] for _ in tiles]
    for step in range(n_ch):
        for pt in tiles:
            for d in range(2):
                c = step if d == 0 else n_ch - 1 - step
                r0, c0 = d * n + c * ch, d * n + c * ch
                qc = jnp.concatenate([qs[pt][r0:r0 + ch], qs[pt][2 * n + r0:2 * n + r0 + ch]], axis=0)
                inter[pt][d][c] = _dot(qc, st[pt][d].astype(BF16))
                upd = _dot(kte_t[pt][:, c0:c0 + ch], v2[pt][d][c * ch:(c + 1) * ch])
                dec = jnp.exp(jnp.sum(la_t[pt][:, c0:c0 + ch], axis=1, keepdims=True))
                st[pt][d] = st[pt][d] * dec + jnp.where(row_lo, upd[:, :GLA_DV], upd[:, GLA_DV:])
    outs = [[None] * GLA_HEADS for _ in range(2)]
    for pt in tiles:
        sf_ref[sls[pt], :] = st[pt][0]
        sb_ref[sls[pt], :] = st[pt][1]
        for hh in range(2):
            rs = slice(hh * HEAD, (hh + 1) * HEAD)
            for d in range(2):
                outs[d][2 * pt + hh] = intra[pt][2 * hh + d] + jnp.concatenate(
                    [inter[pt][d][c][rs] for c in range(n_ch)], axis=0)
    return jnp.concatenate(outs[0], axis=1), jnp.concatenate(outs[1], axis=1)


def _gla_kernel(qc_ref, kc_ref, vc_ref, ac_ref, qf_ref, kf_ref, vf_ref, af_ref,
                qb_ref, kb_ref, vb_ref, ab_ref, w2_ref, b2_ref,
                ocf_ref, ocb_ref, of_ref, ob_ref, sf_ref, sb_ref):
    s = pl.program_id(1)
    n, ch = GLA_BLOCK, GLA_CHUNK
    ri = lax.broadcasted_iota(jnp.int32, (n, n), 0)
    ci = lax.broadcasted_iota(jnp.int32, (n, n), 1)
    same = (ri // ch) == (ci // ch)
    low, up = same & (ri >= ci), same & (ri <= ci)
    sums_f, sums_b = low.astype(BF16), up.astype(BF16)
    cmask = jnp.concatenate([low, up, low, up], axis=0)
    half = lax.broadcasted_iota(jnp.int32, (n, LANES), 1) // HEAD
    row_lo = lax.broadcasted_iota(jnp.int32, (LANES, GLA_DV), 0) < HEAD

    @pl.when(s == 0)
    def _():
        sf_ref[...] = jnp.zeros_like(sf_ref)
        sb_ref[...] = jnp.zeros_like(sb_ref)

    def run(fwd, bwd, o_f, o_b):
        o_f[0], o_b[0] = _gla_block(tuple(r[0] for r in fwd), tuple(r[0] for r in bwd), w2_ref, b2_ref,
                                    sf_ref, sb_ref, sums_f, sums_b, cmask, half, row_lo)

    ctx = (qc_ref, kc_ref, vc_ref, ac_ref)

    @pl.when(s == 0)
    def _():
        run(ctx, ctx, ocf_ref, ocb_ref)

    @pl.when(s > 0)
    def _():
        run((qf_ref, kf_ref, vf_ref, af_ref), (qb_ref, kb_ref, vb_ref, ab_ref), of_ref, ob_ref)


def gla_scan(p3, pc3, w2, b2):
    B = p3.shape[0]
    nb = T // GLA_BLOCK
    cq, ck, cv, ca = COL["gla_q"] // 256, COL["gla_k"] // 256, COL["gla_v"] // 512, COL["gla_a"] // 128
    fwd = lambda s: jnp.maximum(s - 1, 0)
    bwd = lambda s: nb - jnp.maximum(s, 1)

    def spec(width, col, row):
        return pl.BlockSpec((1, GLA_BLOCK, width), lambda b, s: (b, row(s), col))

    zero = lambda s: 0
    in_specs = ([spec(256, cq, zero), spec(256, ck, zero), spec(512, cv, zero), spec(128, ca, zero)]
                + [spec(256, cq, fwd), spec(256, ck, fwd), spec(512, cv, fwd), spec(128, ca, fwd)]
                + [spec(256, cq, bwd), spec(256, ck, bwd), spec(512, cv, bwd), spec(128, ca, bwd)]
                + [pl.BlockSpec((128, 512), lambda b, s: (0, 0)), pl.BlockSpec((1, 512), lambda b, s: (0, 0))])
    out_specs = [spec(512, 0, zero), spec(512, 0, zero), spec(512, 0, fwd), spec(512, 0, bwd)]
    out_shape = [jax.ShapeDtypeStruct((B, L, 512), F32)] * 2 + [jax.ShapeDtypeStruct((B, T, 512), F32)] * 2
    return pl.pallas_call(
        _gla_kernel, grid=(B, nb + 1), in_specs=in_specs, out_specs=out_specs, out_shape=out_shape,
        scratch_shapes=[pltpu.VMEM((256, 128), F32), pltpu.VMEM((256, 128), F32)],
        compiler_params=_cp("parallel", "arbitrary"), name="gla_scan",
    )(pc3, pc3, pc3, pc3, p3, p3, p3, p3, p3, p3, p3, p3, w2, b2)


def _gla_gate(o, r, g):
    outs = []
    for hd in range(GLA_HEADS):
        sl = slice(hd * GLA_DV, (hd + 1) * GLA_DV)
        rh = r[:, sl]
        outs.append(_rms(o[:, sl], g) * (rh * _sigmoid(rh)))
    return jnp.concatenate(outs, axis=1)


def _stack_halves(qt, half):
    zero = jnp.zeros_like(qt)
    return jnp.concatenate([jnp.where(half == 0, qt, zero), jnp.where(half == 1, qt, zero)], axis=0).astype(BF16)


def _merge_halves(o, half):
    n = o.shape[0] // 2
    return jnp.where(half == 0, o[:n], o[n:])


def _softmax_parts(s_list, sink=None):
    m = s_list[0].max(axis=1, keepdims=True)
    for s in s_list[1:]:
        m = jnp.maximum(m, s.max(axis=1, keepdims=True))
    if sink is not None:
        m = jnp.maximum(m, sink)
    es = [jnp.exp(s - m) for s in s_list]
    den = es[0].sum(axis=1, keepdims=True)
    for e in es[1:]:
        den = den + e.sum(axis=1, keepdims=True)
    if sink is not None:
        den = den + jnp.exp(sink - m)
    return [e.astype(BF16) for e in es], 1.0 / den


def _rope(x, cos, sin, lane):
    partner = jnp.where((lane % HEAD) < HEAD // 2, pltpu.roll(x, LANES - HEAD // 2, 1),
                        pltpu.roll(x, HEAD // 2, 1))
    return x * cos + partner * sin


def _swa_kernel(sink_ref, mask_ref, q_ref, kp_ref, kc_ref, kn_ref, vp_ref, vc_ref, vn_ref, kx_ref, vx_ref,
                cos_ref, sin_ref, o_ref):
    n = pl.program_id(1)
    nb = pl.num_programs(1)
    blk = SWA_BLOCK
    lane = lax.broadcasted_iota(jnp.int32, (blk, LANES), 1)
    half = lane // HEAD

    def tab(ref, i):
        return ref[pl.ds(pl.multiple_of(i * blk, blk), blk), :]

    ip, inx = jnp.maximum(n - 1, 0), jnp.minimum(n + 1, nb - 1)
    kb = jnp.concatenate([_rope(kp_ref[0].astype(F32), tab(cos_ref, ip), tab(sin_ref, ip), lane),
                          _rope(kc_ref[0].astype(F32), tab(cos_ref, n), tab(sin_ref, n), lane),
                          _rope(kn_ref[0].astype(F32), tab(cos_ref, inx), tab(sin_ref, inx), lane)],
                         axis=0).astype(BF16)
    vb = jnp.concatenate([vp_ref[0], vc_ref[0], vn_ref[0]], axis=0)
    cos_q, sin_q = tab(cos_ref, n), tab(sin_ref, n)
    qs = jnp.concatenate([_stack_halves(_rope(q_ref[0, :, p * LANES:(p + 1) * LANES].astype(F32), cos_q, sin_q, lane)
                                        * HEAD ** -0.5, half) for p in range(4)], axis=0)
    s_loc = _dot_nt(qs, kb) + mask_ref[...]
    s_loc = jnp.concatenate([s_loc[:, :blk] + jnp.where(n == 0, NEG, 0.0), s_loc[:, blk:2 * blk],
                             s_loc[:, 2 * blk:] + jnp.where(n == nb - 1, NEG, 0.0)], axis=1)
    s_ctx = _dot_nt(qs, kx_ref[0])
    grp = lax.broadcasted_iota(jnp.int32, (8 * blk, 1), 0) // blk
    sink = jnp.zeros((8 * blk, 1), F32)
    for g in range(8):
        sink = jnp.where(grp == g, sink_ref[g // 2 + 4 * (g % 2)], sink)
    (e_loc, e_ctx), inv = _softmax_parts([s_loc, s_ctx], sink=sink)
    o = (_dot(e_loc, vb) + _dot(e_ctx, vx_ref[0])) * inv
    for p in range(4):
        o_ref[0, :, p * LANES:(p + 1) * LANES] = _merge_halves(o[2 * p * blk:2 * (p + 1) * blk], half).astype(BF16)


def swa(p3, pc3, sink, cos_t, sin_t):
    B = p3.shape[0]
    nb = T // SWA_BLOCK
    qi = np.arange(SWA_BLOCK)[:, None]
    kj = np.arange(3 * SWA_BLOCK)[None, :]
    mask = jnp.asarray(np.tile(np.where(np.abs(kj - SWA_BLOCK - qi) <= SWA_WINDOW, 0.0, NEG), (8, 1)).astype(np.float32))
    ck, cv = COL["swa_k"] // 128, COL["swa_v"] // 128
    prev = lambda n: jnp.maximum(n - 1, 0)
    cur = lambda n: n
    nxt = lambda n: jnp.minimum(n + 1, nb - 1)
    kv = lambda col, row: pl.BlockSpec((1, SWA_BLOCK, 128), lambda b, n: (b, row(n), col))
    full = lambda col: pl.BlockSpec((1, L, 128), lambda b, n: (b, 0, col))
    tab = pl.BlockSpec((T, 128), lambda b, n: (0, 0))
    return pl.pallas_call(
        _swa_kernel, grid=(B, nb),
        in_specs=[pl.BlockSpec(memory_space=pltpu.SMEM),
                  pl.BlockSpec((8 * SWA_BLOCK, 3 * SWA_BLOCK), lambda b, n: (0, 0)),
                  pl.BlockSpec((1, SWA_BLOCK, 512), lambda b, n: (b, n, COL["swa_q"] // 512)),
                  kv(ck, prev), kv(ck, cur), kv(ck, nxt), kv(cv, prev), kv(cv, cur), kv(cv, nxt),
                  full(ck), full(cv), tab, tab],
        out_specs=pl.BlockSpec((1, SWA_BLOCK, 512), lambda b, n: (b, n, 0)),
        out_shape=jax.ShapeDtypeStruct((B, T, 512), BF16),
        compiler_params=_cp("parallel", "parallel"), name="swa",
    )(sink, mask, p3, p3, p3, p3, p3, p3, p3, pc3, pc3, cos_t, sin_t)


def _ctx_attn_kernel(sink_ref, q_ref, k_ref, v_ref, o_ref, *, use_sink, kv_tiles):
    n = q_ref.shape[1]
    half = lax.broadcasted_iota(jnp.int32, (n, LANES), 1) // HEAD
    for p in range(4):
        sl = slice(p * LANES, (p + 1) * LANES)
        ks = sl if kv_tiles == 4 else slice(0, LANES)
        qs = _stack_halves(q_ref[0, :, sl] * HEAD ** -0.5, half)
        sink = None
        if use_sink:
            top = lax.broadcasted_iota(jnp.int32, (2 * n, 1), 0) < n
            sink = jnp.where(top, sink_ref[p], sink_ref[p + 4])
        (e,), inv = _softmax_parts([_dot_nt(qs, k_ref[0, :, ks])], sink=sink)
        o = _dot(e, v_ref[0, :, ks]) * inv
        o_ref[0, :, sl] = _merge_halves(o, half).astype(BF16)


def ctx_attn(pc3, sink, qcol, kcol, vcol, kv_tiles, use_sink):
    B, w = pc3.shape[0], kv_tiles * 128
    return pl.pallas_call(
        functools.partial(_ctx_attn_kernel, use_sink=use_sink, kv_tiles=kv_tiles), grid=(B,),
        in_specs=[pl.BlockSpec(memory_space=pltpu.SMEM),
                  pl.BlockSpec((1, L, 512), lambda b: (b, 0, qcol // 512)),
                  pl.BlockSpec((1, L, w), lambda b: (b, 0, kcol // w)),
                  pl.BlockSpec((1, L, w), lambda b: (b, 0, vcol // w))],
        out_specs=pl.BlockSpec((1, L, 512), lambda b: (b, 0, 0)),
        out_shape=jax.ShapeDtypeStruct((B, L, 512), BF16),
        compiler_params=_cp("parallel"), name="ctx_attn")(sink, pc3, pc3, pc3)


def _na_kernel(q_ref, k_ref, v_ref, kx_ref, vx_ref, *rest):
    bias_refs, o_ref = rest[:-1], rest[-1]
    nr = len(bias_refs)
    i = pl.program_id(1)
    band = NA_KR * GRID_W
    half = lax.broadcasted_iota(jnp.int32, (GRID_W, LANES), 1) // HEAD
    starts = [pl.multiple_of(jnp.clip(nr * i + k - NA_KR // 2, 0, ROWS - NA_KR) * GRID_W, GRID_W)
              for k in range(nr)]
    s_loc, s_ctx = [], []
    for p in range(4):
        sl = slice(p * LANES, (p + 1) * LANES)
        qs = [_stack_halves(q_ref[0, k * GRID_W:(k + 1) * GRID_W, sl] * HEAD ** -0.5, half) for k in range(nr)]
        for k in range(nr):
            kb = k_ref[0, pl.ds(starts[k], band), sl]
            s_loc.append(_dot_nt(qs[k], kb) + bias_refs[k][0, 2 * p * GRID_W:(2 * p + 2) * GRID_W, :])
        s_ctx.append(_dot_nt(jnp.concatenate(qs, axis=0), kx_ref[0, :, sl]))
    (e_loc, e_ctx), inv = _softmax_parts([jnp.concatenate(s_loc, axis=0), jnp.concatenate(s_ctx, axis=0)])
    blk = 2 * GRID_W
    for p in range(4):
        sl = slice(p * LANES, (p + 1) * LANES)
        r0 = p * nr * blk
        o_ctx = _dot(e_ctx[r0:r0 + nr * blk], vx_ref[0, :, sl])
        for k in range(nr):
            rs = slice(r0 + k * blk, r0 + (k + 1) * blk)
            vb = v_ref[0, pl.ds(starts[k], band), sl]
            o = (_dot(e_loc[rs], vb) + o_ctx[k * blk:(k + 1) * blk]) * inv[rs]
            o_ref[0, k * GRID_W:(k + 1) * GRID_W, sl] = _merge_halves(o, half).astype(BF16)


def na(p3, pc3, bias, rows_per_step=NA_ROWS_PER_STEP):
    B, nr = p3.shape[0], rows_per_step
    cq, ck, cv = COL["na_q"] // 512, COL["na_k"] // 512, COL["na_v"] // 512
    shift = lambda r: r - jnp.clip(r - NA_KR // 2, 0, ROWS - NA_KR)
    bias_spec = lambda k: pl.BlockSpec((1, NA_HEADS * GRID_W, NA_KR * GRID_W),
                                       lambda b, i: (shift(nr * i + k), 0, 0))
    return pl.pallas_call(
        _na_kernel, grid=(B, ROWS // nr),
        in_specs=[pl.BlockSpec((1, nr * GRID_W, 512), lambda b, i: (b, i, cq)),
                  pl.BlockSpec((1, T, 512), lambda b, i: (b, 0, ck)),
                  pl.BlockSpec((1, T, 512), lambda b, i: (b, 0, cv)),
                  pl.BlockSpec((1, L, 512), lambda b, i: (b, 0, ck)),
                  pl.BlockSpec((1, L, 512), lambda b, i: (b, 0, cv))] + [bias_spec(k) for k in range(nr)],
        out_specs=pl.BlockSpec((1, nr * GRID_W, 512), lambda b, i: (b, i, 0)),
        out_shape=jax.ShapeDtypeStruct((B, T, 512), BF16),
        compiler_params=_cp("parallel", "arbitrary"), name="na")(p3, p3, p3, pc3, pc3, *([bias] * nr))


def na_bias_table(rpb):
    col = np.arange(GRID_W)
    col_start = np.clip(col - NA_KC // 2, 0, GRID_W - NA_KC)
    col_ok = (col[None, :] >= col_start[:, None]) & (col[None, :] < col_start[:, None] + NA_KC)
    col_off = np.clip(col[None, :] - col[:, None], -(NA_KC - 1), NA_KC - 1) + (NA_KC - 1)
    onehot = (col_off[:, :, None] == np.arange(2 * NA_KC - 1)).astype(np.float32)
    toep = jnp.einsum("hro,qko->hrqk", rpb, jnp.asarray(onehot), precision=HIGHEST)
    toep = jnp.where(jnp.asarray(col_ok)[None, None], toep, NEG)
    per_shift = [toep[:, NA_KR - 1 - s:2 * NA_KR - 1 - s] for s in range(NA_KR)]
    tab = jnp.stack(per_shift).transpose(0, 1, 3, 2, 4)
    return tab.reshape(NA_KR, NA_HEADS * GRID_W, NA_KR * GRID_W).astype(F32)


def _gelu(x):
    return 0.5 * x * (1.0 + jnp.tanh(np.sqrt(2.0 / np.pi).astype(np.float32) * (x + 0.044715 * (x * x * x))))


def _sgu_kernel(u_ref, v_ref, g_ref, b_ref, ws_ref, bs_ref, y_ref):
    n_chunk = u_ref.shape[1] // SGU_CHUNK
    v = _gelu(v_ref[0].astype(F32))
    mu = jnp.mean(v, axis=-1, keepdims=True)
    var = jnp.mean(jnp.square(v - mu), axis=-1, keepdims=True)
    vn = ((v - mu) * lax.rsqrt(var + EPS) * g_ref[...] + b_ref[...]).astype(BF16)
    cols = []
    for g in range(4):
        gs = slice(g * 128, (g + 1) * 128)
        rhs = jnp.concatenate([vn[c * SGU_CHUNK:(c + 1) * SGU_CHUNK, gs] for c in range(n_chunk)], axis=1)
        mixed = _dot(ws_ref[g], rhs)
        cols.append(jnp.concatenate([mixed[:, c * 128:(c + 1) * 128] + bs_ref[g] for c in range(n_chunk)], axis=0))
    y_ref[0] = (_gelu(u_ref[0].astype(F32)) * jnp.concatenate(cols, axis=1)).astype(BF16)


def sgu(p3, g, b, ws, bs):
    B, n = p3.shape[0], p3.shape[1]
    tr = min(n, TOKEN_TILE)
    blk = lambda col: pl.BlockSpec((1, tr, 512), lambda bb, i: (bb, i, col))
    const = lambda shape: pl.BlockSpec(shape, lambda bb, i: (0,) * len(shape))
    return pl.pallas_call(
        _sgu_kernel, grid=(B, n // tr),
        in_specs=[blk(COL["sgu_u"] // 512), blk(COL["sgu_v"] // 512), const((1, 512)), const((1, 512)),
                  const((4, 128, 128)), const((4, 128, 128))],
        out_specs=blk(0), out_shape=jax.ShapeDtypeStruct((B, n, 512), BF16),
        compiler_params=_cp("parallel", "parallel"), name="sgu")(p3, p3, g, b, ws, bs)


def _merge_kernel(h_ref, of_ref, ob_ref, r_ref, gn_ref, b_ref, n_ref, d_ref, wg_ref, bg_ref, wb_ref, o_ref,
                  a_ref):
    @pl.when(pl.program_id(1) == 0)
    def _():
        a_ref[...] = _gla_gate(of_ref[...] + ob_ref[...], r_ref[...].astype(F32), gn_ref[...]).astype(BF16)

    h = h_ref[...]
    acc = None
    for i, br in enumerate((a_ref, b_ref, n_ref, d_ref)):
        term = _sigmoid(_dot(h, wg_ref[0, i].astype(BF16)) + bg_ref[i]) * _dot(br[...], wb_ref[i])
        acc = term if acc is None else acc + term
    o_ref[...] = acc.astype(BF16)


def merge(h, o_f, o_b, p, gn, branches, wg, l, bg, wb, tm):
    M, tn = h.shape[0], MERGE_COL_TILE
    tm = min(tm, M)
    br = pl.BlockSpec((tm, 512), lambda i, j: (i, 0))
    return pl.pallas_call(
        _merge_kernel, grid=(M // tm, D // tn),
        in_specs=[pl.BlockSpec((tm, D), lambda i, j: (i, 0)), br, br,
                  pl.BlockSpec((tm, 512), lambda i, j: (i, COL["gla_r"] // 512)),
                  pl.BlockSpec((1, GLA_DV), lambda i, j: (0, 0)), br, br, br,
                  pl.BlockSpec((1, 4, D, tn), lambda i, j: (l, 0, 0, j)),
                  pl.BlockSpec((4, 1, tn), lambda i, j: (0, 0, j)),
                  pl.BlockSpec((4, 512, tn), lambda i, j: (0, 0, j))],
        out_specs=pl.BlockSpec((tm, tn), lambda i, j: (i, j)),
        out_shape=jax.ShapeDtypeStruct((M, D), BF16),
        scratch_shapes=[pltpu.VMEM((tm, 512), BF16)],
        compiler_params=_cp("parallel", "arbitrary"), name="merge")(h, o_f, o_b, p, gn, *branches, wg, bg, wb)


def _outproj_kernel(acc_ref, w_ref, x_ref, gate_ref, g_ref, shift_ref, scale_ref, wr_ref,
                    x1_ref, h2_ref, lg_ref):
    n_sub = 2
    sub = acc_ref.shape[0] // n_sub

    def matmul(k):
        rs = slice(k * sub, (k + 1) * sub)
        return x_ref[rs, :] + gate_ref[0] * _dot(acc_ref[rs, :], w_ref[...])

    def epilogue(k, x1):
        rs = slice(k * sub, (k + 1) * sub)
        x1_ref[rs, :] = x1
        h2 = (_rms(x1, g_ref[...]) * (1.0 + scale_ref[0]) + shift_ref[0]).astype(BF16)
        lg_ref[rs, :] = _dot(h2, wr_ref[...])
        bits = pltpu.bitcast(h2.astype(F32), jnp.int32)
        h2_ref[rs, :] = lax.shift_right_logical(bits[:, :D // 2], 16) | (bits[:, D // 2:] & jnp.int32(-65536))

    pending = matmul(0)
    for k in range(n_sub):
        nxt = matmul(k + 1) if k + 1 < n_sub else None
        epilogue(k, pending)
        pending = nxt


def outproj(acc, w, x2, mod3, g, wr, tm, mod_row):
    M = acc.shape[0]
    tm = min(tm, M)
    row = lambda k: pl.BlockSpec((1, 1, D), lambda i: (mod_row(i) * 6 + k, 0, 0))
    tile = pl.BlockSpec((tm, D), lambda i: (i, 0))
    return pl.pallas_call(
        _outproj_kernel, grid=(M // tm,),
        in_specs=[tile, pl.BlockSpec((D, D), lambda i: (0, 0)), tile, row(2),
                  pl.BlockSpec((1, D), lambda i: (0, 0)), row(3), row(4),
                  pl.BlockSpec((D, 128), lambda i: (0, 0))],
        out_specs=[tile, pl.BlockSpec((tm, D // 2), lambda i: (i, 0)), pl.BlockSpec((tm, 128), lambda i: (i, 0))],
        out_shape=[jax.ShapeDtypeStruct((M, D), F32), jax.ShapeDtypeStruct((M, D // 2), jnp.int32),
                   jax.ShapeDtypeStruct((M, 128), F32)],
        compiler_params=_cp("parallel"), name="outproj")(acc, w, x2, mod3, g, mod3, mod3, wr)


def _select_kernel(lg_ref, sel_ref, aff_ref, slot_ref, *, cap):
    n = lg_ref.shape[1]
    nt = n // LANES
    lane = lax.broadcasted_iota(jnp.int32, (n, LANES), 1)
    lg = jnp.where(lane < N_EXPERTS, lg_ref[0], NEG)
    e = jnp.exp(lg - lg.max(axis=1, keepdims=True))
    aff = (e / e.sum(axis=1, keepdims=True)).T[:N_EXPERTS]
    aff_ref[0] = aff
    bits = pltpu.bitcast(aff, jnp.int32)

    def count(mask):
        return jnp.sum(jnp.where(mask, 1.0, 0.0), axis=1, keepdims=True)

    def bit_step(i, thr):
        cand = thr | lax.shift_left(jnp.int32(1), 30 - i)
        return jnp.where(count(bits >= cand) >= cap, cand, thr)

    thr = lax.fori_loop(0, 31, bit_step, jnp.zeros((N_EXPERTS, 1), jnp.int32))
    gt, eq = bits > thr, bits == thr

    r = lax.broadcasted_iota(jnp.int32, (LANES, LANES), 0)
    c = lax.broadcasted_iota(jnp.int32, (LANES, LANES), 1)
    upper = (r <= c).astype(BF16)
    r2 = lax.broadcasted_iota(jnp.int32, (nt * N_EXPERTS, nt * N_EXPERTS), 0)
    c2 = lax.broadcasted_iota(jnp.int32, (nt * N_EXPERTS, nt * N_EXPERTS), 1)
    before = ((r2 % N_EXPERTS == c2 % N_EXPERTS) & (c2 // N_EXPERTS < r2 // N_EXPERTS)).astype(BF16)

    def excl_prefix(x):
        xs = jnp.concatenate([x[:, i * LANES:(i + 1) * LANES] for i in range(nt)], axis=0).astype(BF16)
        inc = _dot(xs, upper)
        tot = jnp.broadcast_to(inc[:, LANES - 1:LANES], inc.shape).astype(BF16)
        full = inc + _dot(before, tot)
        return jnp.concatenate([full[i * N_EXPERTS:(i + 1) * N_EXPERTS] for i in range(nt)], axis=1) - x

    eq01 = jnp.where(eq, 1.0, 0.0)
    mask = gt | (eq & (excl_prefix(eq01) < cap - count(gt)))
    slot = jnp.where(mask, excl_prefix(jnp.where(mask, 1.0, 0.0)), -1.0)
    sel_ref[0] = slot.astype(jnp.int32)
    pad = jnp.full((LANES - N_EXPERTS, n), -1.0, F32)
    slot_ref[0] = jnp.concatenate([slot, pad], axis=0).T


def select(lg3, cap):
    B, n = lg3.shape[0], lg3.shape[1]
    tok = pl.BlockSpec((1, n, 128), lambda b: (b, 0, 0))
    row = pl.BlockSpec((1, N_EXPERTS, n), lambda b: (b, 0, 0))
    return pl.pallas_call(
        functools.partial(_select_kernel, cap=cap), grid=(B,),
        in_specs=[tok], out_specs=[row, row, tok],
        out_shape=[jax.ShapeDtypeStruct((B, N_EXPERTS, n), jnp.int32),
                   jax.ShapeDtypeStruct((B, N_EXPERTS, n), F32), jax.ShapeDtypeStruct((B, n, 128), F32)],
        compiler_params=_cp("parallel"), name="select")(lg3)


def _slot_tokens_kernel(sel_ref, aff_ref, idx_ref, g_ref, *, cap):
    n = sel_ref.shape[2]
    row = lax.broadcasted_iota(jnp.int32, (cap, n), 0)
    lane = lax.broadcasted_iota(jnp.int32, (LANES, n), 0)
    tok = lax.broadcasted_iota(jnp.int32, (LANES, n), 1)
    pos = jnp.where(lane == 0, tok // LANES, jnp.where(lane == 1, tok % LANES, 0)).astype(F32)
    for e in range(N_EXPERTS):
        onehot = jnp.where(row == sel_ref[0, e:e + 1, :], 1.0, 0.0).astype(BF16)
        a = aff_ref[0, e:e + 1, :]
        hi = a.astype(BF16)
        r1 = a - hi.astype(F32)
        mid = r1.astype(BF16)
        lo = (r1 - mid.astype(F32)).astype(BF16)
        cols = jnp.where(lane == 2, hi.astype(F32), jnp.where(lane == 3, mid.astype(F32),
                         jnp.where(lane == 4, lo.astype(F32), pos))).astype(BF16)
        r = _dot_nt(onehot, cols)
        idx_ref[e, 0] = (r[:, 0:1] * LANES + r[:, 1:2]).astype(jnp.int32) + pl.program_id(0) * n
        g_ref[e, 0] = r[:, 2:3] + r[:, 3:4] + r[:, 4:5]


def slot_tokens(sel, aff, cap):
    B, n = sel.shape[0], sel.shape[2]
    row = pl.BlockSpec((1, N_EXPERTS, n), lambda b: (b, 0, 0))
    col = pl.BlockSpec((N_EXPERTS, 1, cap, 1), lambda b: (0, b, 0, 0))
    idx, gate = pl.pallas_call(
        functools.partial(_slot_tokens_kernel, cap=cap), grid=(B,),
        in_specs=[row, row], out_specs=[col, col],
        out_shape=[jax.ShapeDtypeStruct((N_EXPERTS, B, cap, 1), jnp.int32),
                   jax.ShapeDtypeStruct((N_EXPERTS, B, cap, 1), F32)],
        compiler_params=_cp("parallel"), name="slot_tokens")(sel, aff)
    return idx.reshape(N_EXPERTS, B * cap, 1), gate.reshape(N_EXPERTS, B * cap, 1)


def _expert_kernel(idx_ref, h_hbm, gate_ref, wg_ref, wu_ref, wd_ref, o_ref, xg_ref, xb_ref, acc_ref, sem,
                   *, tm):
    e, i, f = pl.program_id(0), pl.program_id(1), pl.program_id(2)
    n_i, n_f = pl.num_programs(1), pl.num_programs(2)
    q = e * n_i + i
    n_q = pl.num_programs(0) * n_i
    slot = q % 2
    share = tm // n_f

    def row_copy(tile, r, dst_slot):
        src = h_hbm.at[pl.ds(idx_ref[tile * tm + r], 1), :]
        return pltpu.make_async_copy(src, xg_ref.at[dst_slot, pl.ds(r, 1), :], sem.at[dst_slot])

    def wait_tile(dst_slot):
        pltpu.make_async_copy(h_hbm.at[pl.ds(0, tm), :], xg_ref.at[dst_slot], sem.at[dst_slot]).wait()

    @pl.when((q == 0) & (f == 0))
    def _():
        def first(r, carry):
            row_copy(0, r, 0).start()
            return carry
        lax.fori_loop(0, tm, first, 0)

    @pl.when(f == 0)
    def _():
        wait_tile(slot)
        p = xg_ref[slot]
        xb_ref[:, :D // 2] = pltpu.bitcast(lax.shift_left(p, 16), F32).astype(BF16)
        xb_ref[:, D // 2:] = pltpu.bitcast(p & jnp.int32(-65536), F32).astype(BF16)
        acc_ref[...] = jnp.zeros_like(acc_ref)

    x = xb_ref[...]
    g = _dot(x, wg_ref[0, 0].astype(BF16))
    u = _dot(x, wu_ref[0, 0].astype(BF16))
    hid = ((g * _sigmoid(g)) * u).astype(BF16)
    nxt = jnp.minimum(q + 1, n_q - 1)
    for r in range(share):
        row_copy(nxt, f * share + r, 1 - slot).start()
    acc_ref[...] += _dot(hid, wd_ref[0, 0].astype(BF16))

    @pl.when(f == n_f - 1)
    def _():
        o_ref[0] = (acc_ref[...] * gate_ref[0]).astype(BF16)

    @pl.when((q == n_q - 1) & (f == n_f - 1))
    def _():
        wait_tile(1 - slot)


def experts(idx, h2p, gate, wg, wu, wd, l, tm):
    E, M, tf = gate.shape[0], gate.shape[1], EXPERT_HIDDEN_TILE
    F = wg.shape[3]
    tm = min(tm, M)
    grid_spec = pltpu.PrefetchScalarGridSpec(
        num_scalar_prefetch=1, grid=(E, M // tm, F // tf),
        in_specs=[pl.BlockSpec(memory_space=pl.ANY),
                  pl.BlockSpec((1, tm, 1), lambda e, i, f, idx: (e, i, 0)),
                  pl.BlockSpec((1, 1, D, tf), lambda e, i, f, idx: (l, e, 0, f)),
                  pl.BlockSpec((1, 1, D, tf), lambda e, i, f, idx: (l, e, 0, f)),
                  pl.BlockSpec((1, 1, tf, D), lambda e, i, f, idx: (l, e, f, 0))],
        out_specs=pl.BlockSpec((1, tm, D), lambda e, i, f, idx: (e, i, 0)),
        scratch_shapes=[pltpu.VMEM((2, tm, D // 2), jnp.int32), pltpu.VMEM((tm, D), BF16),
                        pltpu.VMEM((tm, D), F32), pltpu.SemaphoreType.DMA((2,))])
    return pl.pallas_call(
        functools.partial(_expert_kernel, tm=tm), grid_spec=grid_spec,
        out_shape=jax.ShapeDtypeStruct((E, M, D), BF16),
        compiler_params=_cp("arbitrary", "arbitrary", "arbitrary"), name="experts")(idx, h2p, gate, wg, wu, wd)


COMBINE_GROUP = 8


def _combine_kernel(slot_ref, ye_ref, x_ref, gate_ref, gfin_ref, o_ref, acc_ref, *, cap, final):
    g = pl.program_id(2)
    tt = x_ref.shape[1]
    lane = lax.broadcasted_iota(jnp.int32, (tt, LANES), 1)
    j = lax.broadcasted_iota(jnp.int32, (tt, cap), 1).astype(F32)
    slots = slot_ref[0]
    parts = []
    for k in range(COMBINE_GROUP):
        slot = jnp.sum(jnp.where(lane == g * COMBINE_GROUP + k, slots, 0.0), axis=1, keepdims=True)
        parts.append(jnp.where(j == slot, 1.0, 0.0).astype(BF16))
    onehot = jnp.concatenate(parts, axis=1)
    contrib = _dot(onehot, ye_ref[:, 0].reshape(COMBINE_GROUP * cap, D))

    @pl.when(g == 0)
    def _():
        acc_ref[...] = contrib

    @pl.when(g == pl.num_programs(2) - 1)
    def _():
        x2 = x_ref[0] + gate_ref[0] * (acc_ref[...] + contrib)
        if final:
            x2 = _rms(x2, gfin_ref[...])
        o_ref[0] = x2


def combine(slot_tok, ye, x3, mod3, gfin, cap, mod_row, final):
    B, n = x3.shape[0], x3.shape[1]
    tt = min(n, TOKEN_TILE)
    assert N_EXPERTS == 2 * COMBINE_GROUP
    xt = pl.BlockSpec((1, tt, D), lambda b, t, g: (b, t, 0))
    return pl.pallas_call(
        functools.partial(_combine_kernel, cap=cap, final=final), grid=(B, n // tt, N_EXPERTS // COMBINE_GROUP),
        in_specs=[pl.BlockSpec((1, tt, 128), lambda b, t, g: (b, t, 0)),
                  pl.BlockSpec((COMBINE_GROUP, 1, cap, D), lambda b, t, g: (g, b, 0, 0)), xt,
                  pl.BlockSpec((1, 1, D), lambda b, t, g: (mod_row(b) * 6 + 5, 0, 0)),
                  pl.BlockSpec((1, D), lambda b, t, g: (0, 0))],
        out_specs=xt, out_shape=jax.ShapeDtypeStruct((B, n, D), F32),
        scratch_shapes=[pltpu.VMEM((tt, D), F32)],
        compiler_params=_cp("parallel", "parallel", "arbitrary"), name="combine",
    )(slot_tok, ye.reshape(N_EXPERTS, B, cap, D), x3, mod3, gfin)


def _swa_q_perm():
    return np.concatenate([np.arange(HEAD) + (p + 4 * hh) * HEAD for p in range(4) for hh in range(2)])


def _prep_w_in(w):
    seg = lambda name, width: w[:, REF[name]:REF[name] + width]
    swa_q = seg("swa_q", 512)[:, _swa_q_perm()]
    parts = [seg("gla_v", 512), seg("gla_r", 512), swa_q, seg("na_q", 512), seg("na_k", 512), seg("na_v", 512),
             seg("sgu_u", 512), seg("sgu_v", 512), seg("gla_q", 256), seg("gla_k", 256), seg("swa_k", 128),
             seg("swa_v", 128), seg("gla_af", 16), seg("gla_ab", 16)]
    out = jnp.concatenate(parts, axis=1)
    return jnp.pad(out, ((0, 0), (0, NP - out.shape[1]))).astype(BF16)


def _rope_tables():
    quarter = HEAD // 4
    freqs = 10000.0 ** (-jnp.arange(quarter, dtype=F32) / quarter)
    t = jnp.arange(T)
    row = (t // GRID_W).astype(F32)
    col = (t % GRID_W).astype(F32)
    ang = jnp.concatenate([row[:, None] * freqs, col[:, None] * freqs], axis=-1)
    cos, sin = jnp.cos(ang), jnp.sin(ang)
    cos_t = jnp.tile(cos, (1, 4))
    sin_t = jnp.tile(jnp.concatenate([-sin, sin], axis=-1), (1, 2))
    return cos_t, sin_t


def _moe(h2, lg, x3, mod3, gfin, wg, wu, wd, l, cap, tm, mod_row, final):
    B, n = x3.shape[0], x3.shape[1]
    sel, aff, slot_tok = select(lg.reshape(B, n, 128), cap)
    idx, gate = slot_tokens(sel, aff, cap)
    ye = experts(idx.reshape(-1), h2, gate, wg, wu, wd, l, tm)
    return combine(slot_tok, ye, x3, mod3, gfin, cap, mod_row, final)


def kernel(x, c, ctx, c_ctx, w_ada, b_ada, norm1_g, norm2_g, w_in, gla_w_a2, gla_b_a2, gla_norm_g, swa_sink,
           na_rpb, sgu_ln_g, sgu_ln_b, sgu_w_s, sgu_b_s, w_gate, b_gate, w_branch, w_out, w_router,
           w_exp_gate, w_exp_up, w_exp_down, final_norm_g):
    B = x.shape[0]
    ctx_row = B
    n_mod = -(-(B + 1) // SUBLANES) * SUBLANES
    cvec = jnp.zeros((n_mod, D), F32).at[:B].set(c).at[ctx_row].set(c_ctx)
    cos_t, sin_t = _rope_tables()
    perm = _swa_q_perm()
    gfin = final_norm_g.reshape(1, D)
    tm = ROW_TILE
    ctx_tile_row = lambda i: ctx_row
    x2 = x.reshape(B * T, D)
    xc2 = ctx.reshape(B * L, D)
    for l in range(DEPTH):
        need_ctx = l < DEPTH - 1
        mod3 = ada(cvec, w_ada, b_ada.reshape(DEPTH, 1, 6 * D), l).reshape(n_mod * 6, 1, D)
        w_in_p = _prep_w_in(w_in[l])
        g1 = norm1_g[l].reshape(1, D)
        h, p = inproj(x2, mod3, g1, w_in_p, INPROJ_ROW_TILE, lambda i: i // (T // INPROJ_ROW_TILE))
        hc, pc = inproj(xc2, mod3, g1, w_in_p, INPROJ_ROW_TILE, ctx_tile_row)
        p3, pc3 = p.reshape(B, T, NP), pc.reshape(B, L, NP)

        w2 = jnp.zeros((128, 512), F32).at[:GLA_RANK, :256].set(gla_w_a2[l, 0])
        w2 = w2.at[GLA_RANK:2 * GLA_RANK, 256:].set(gla_w_a2[l, 1]).astype(BF16)
        b2 = gla_b_a2[l].reshape(1, 512)
        ocf, ocb, of, ob = gla_scan(p3, pc3, w2, b2)
        gn = gla_norm_g[l].reshape(1, GLA_DV)
        yb = swa(p3, pc3, swa_sink[l], cos_t, sin_t)
        yn = na(p3, pc3, na_bias_table(na_rpb[l]))
        ln_g, ln_b = sgu_ln_g[l].reshape(1, 512), sgu_ln_b[l].reshape(1, 512)
        ws = sgu_w_s[l].astype(BF16)
        bs = jnp.broadcast_to(sgu_b_s[l][:, :, None], (4, SGU_CHUNK, 128))
        yd = sgu(p3, ln_g, ln_b, ws, bs)

        bg = b_gate[l].reshape(4, 1, D)
        wb = w_branch[l].at[1].set(w_branch[l, 1][perm]).astype(BF16)
        wo = w_out[l].astype(BF16)
        wr = jnp.pad(w_router[l], ((0, 0), (0, 128 - N_EXPERTS))).astype(BF16)
        g2 = norm2_g[l].reshape(1, D)
        flat = lambda y: y.reshape(-1, 512)
        acc = merge(h, flat(of), flat(ob), p, gn, [flat(yb), flat(yn), flat(yd)], w_gate, l, bg, wb, tm)
        x1, h2, lg = outproj(acc, wo, x2, mod3, g2, wr, OUTPROJ_ROW_TILE, lambda i: i // (T // OUTPROJ_ROW_TILE))
        x2 = _moe(h2, lg, x1.reshape(B, T, D), mod3, gfin, w_exp_gate, w_exp_up, w_exp_down, l,
                  2 * T // N_EXPERTS, tm, lambda b: b, final=not need_ctx).reshape(B * T, D)
        if need_ctx:
            ybc = ctx_attn(pc3, swa_sink[l], COL["swa_q"], COL["swa_k"], COL["swa_v"], 1, True)
            ync = ctx_attn(pc3, swa_sink[l], COL["na_q"], COL["na_k"], COL["na_v"], 4, False)
            ydc = sgu(pc3, ln_g, ln_b, ws, bs)
            accc = merge(hc, flat(ocf), flat(ocb), pc, gn, [flat(ybc), flat(ync), flat(ydc)], w_gate, l, bg, wb, tm)
            xc1, hc2, lgc = outproj(accc, wo, xc2, mod3, g2, wr, OUTPROJ_ROW_TILE, ctx_tile_row)
            xc2 = _moe(hc2, lgc, xc1.reshape(B, L, D), mod3, gfin, w_exp_gate, w_exp_up, w_exp_down, l,
                       2 * L // N_EXPERTS, tm, lambda b: ctx_row, final=False).reshape(B * L, D)
    return x2.reshape(B, T, D)
```

```python
import functools

import numpy as np
import jax
import jax.numpy as jnp
from jax import lax
from jax.experimental import pallas as pl
from jax.experimental.pallas import tpu as pltpu

F32 = jnp.float32
BF16 = jnp.bfloat16
HIGHEST = lax.Precision.HIGHEST

D = 2048
T = 2048
L = 256
GRID_W = 64
ROWS = T // GRID_W
EPS = 1e-6
DEPTH = 2
NEG = -1e30

GLA_HEADS, GLA_DK, GLA_DV, GLA_RANK, GLA_TAU, GLA_CHUNK = 4, 64, 128, 16, 16.0, 64
GLA_BLOCK = 256
SWA_BLOCK, SWA_WINDOW = 128, 128
NA_KR, NA_KC, NA_HEADS = 8, 16, 8
SGU_CHUNK = 128
N_EXPERTS = 16
LANES = 128
HEAD = 64

NP = 5120
COL = dict(gla_v=0, gla_r=512, swa_q=1024, na_q=1536, na_k=2048, na_v=2560, sgu_u=3072, sgu_v=3584,
           gla_q=4096, gla_k=4352, swa_k=4608, swa_v=4736, gla_a=4864)
REF = dict(gla_q=0, gla_k=256, gla_v=512, gla_r=1024, gla_af=1536, gla_ab=1552, swa_q=1568, swa_k=2080,
           swa_v=2208, na_q=2336, na_k=2848, na_v=3360, sgu_u=3872, sgu_v=4384)

VMEM_LIMIT = 56 * 1024 * 1024
SUBLANES = 8

ROW_TILE = 1024
OUTPROJ_ROW_TILE = 512
ADA_COL_TILE = 1024
INPROJ_ROW_TILE = 512
MERGE_COL_TILE = 256
EXPERT_HIDDEN_TILE = 256
TOKEN_TILE = 512
NA_ROWS_PER_STEP = 4


def _cp(*sem, vmem=VMEM_LIMIT):
    return pltpu.CompilerParams(dimension_semantics=sem, vmem_limit_bytes=vmem)


def _sigmoid(x):
    return 1.0 / (1.0 + jnp.exp(-x))


def _rms(x, g):
    return x * lax.rsqrt(jnp.mean(x * x, axis=-1, keepdims=True) + EPS) * g


def _dot(a, b):
    return jnp.dot(a, b, preferred_element_type=F32)


def _dot_nt(a, b):
    return lax.dot_general(a, b, (((1,), (1,)), ((), ())), preferred_element_type=F32)


def _ada_kernel(c_ref, w_ref, b_ref, o_ref):
    c = c_ref[...]
    act = (c * _sigmoid(c)).astype(BF16)
    o_ref[...] = _dot(act, w_ref[0].astype(BF16)) + b_ref[0]


def ada(cvec, w, b, l):
    rows, tn = cvec.shape[0], ADA_COL_TILE
    return pl.pallas_call(
        _ada_kernel, grid=(6 * D // tn,),
        in_specs=[pl.BlockSpec((rows, D), lambda j: (0, 0)),
                  pl.BlockSpec((1, D, tn), lambda j: (l, 0, j)),
                  pl.BlockSpec((1, 1, tn), lambda j: (l, 0, j))],
        out_specs=pl.BlockSpec((rows, tn), lambda j: (0, j)),
        out_shape=jax.ShapeDtypeStruct((rows, 6 * D), F32),
        compiler_params=_cp("parallel"), name="ada")(cvec, w, b)


def _inproj_kernel(x_ref, shift_ref, scale_ref, g_ref, w_ref, h_ref, p_ref):
    n_sub = 2
    sub = x_ref.shape[0] // n_sub

    def modulate(k):
        rs = slice(k * sub, (k + 1) * sub)
        h = (_rms(x_ref[rs, :], g_ref[...]) * (1.0 + scale_ref[0]) + shift_ref[0]).astype(BF16)
        h_ref[rs, :] = h
        return h

    hs = [modulate(k) for k in range(n_sub)]
    for k in range(n_sub):
        p_ref[k * sub:(k + 1) * sub, :] = _dot(hs[k], w_ref[...]).astype(BF16)


def inproj(x2, mod3, g, w, tm, mod_row):
    M = x2.shape[0]
    tm = min(tm, M)
    return pl.pallas_call(
        _inproj_kernel, grid=(M // tm,),
        in_specs=[pl.BlockSpec((tm, D), lambda i: (i, 0)),
                  pl.BlockSpec((1, 1, D), lambda i: (mod_row(i) * 6 + 0, 0, 0)),
                  pl.BlockSpec((1, 1, D), lambda i: (mod_row(i) * 6 + 1, 0, 0)),
                  pl.BlockSpec((1, D), lambda i: (0, 0)),
                  pl.BlockSpec((D, NP), lambda i: (0, 0), pipeline_mode=pl.Buffered(1))],
        out_specs=[pl.BlockSpec((tm, D), lambda i: (i, 0)),
                   pl.BlockSpec((tm, NP), lambda i: (i, 0))],
        out_shape=[jax.ShapeDtypeStruct((M, D), BF16), jax.ShapeDtypeStruct((M, NP), BF16)],
        compiler_params=_cp("parallel"), name="inproj")(x2, mod3, mod3, g, w)


def _log_sigmoid(x):
    return jnp.minimum(x, 0.0) - jnp.log(1.0 + jnp.exp(-jnp.abs(x)))


def _split3(x):
    hi = x.astype(BF16)
    r1 = x - hi.astype(F32)
    mid = r1.astype(BF16)
    lo = (r1 - mid.astype(F32)).astype(BF16)
    return jnp.concatenate([hi, mid, lo], axis=1)


def _gla_block(fwd, bwd, w2_ref, b2_ref, sf_ref, sb_ref, sums_f, sums_b, cmask, half, row_lo):
    n, ch = GLA_BLOCK, GLA_CHUNK
    n_ch = n // ch
    (qf, kf, vf, af), (qb, kb, vb, ab) = fwd, bwd
    pre = jnp.concatenate([_dot(af.astype(BF16), w2_ref[:, :256]) + b2_ref[:, :256],
                           _dot(ab.astype(BF16), w2_ref[:, 256:]) + b2_ref[:, 256:]], axis=0)
    la = _log_sigmoid(pre) * (1.0 / GLA_TAU)
    la3 = _split3(la)
    cum = jnp.concatenate([_dot(sums_f, la3[:n]), _dot(sums_b, la3[n:])], axis=0)
    bc = cum[:, :256] + cum[:, 256:512] + cum[:, 512:]
    ends = [c * ch + ch - 1 for c in range(n_ch)] + [n + c * ch for c in range(n_ch)]
    bl = jnp.concatenate([jnp.broadcast_to(bc[r:r + 1], (ch, bc.shape[1])) for r in ends], axis=0)
    q = jnp.concatenate([qf, qb], axis=0).astype(F32)
    k = jnp.concatenate([kf, kb], axis=0).astype(F32)
    q_in = (q * GLA_DK ** -0.5) * jnp.exp(bc)
    k_in = (k * jnp.exp(-bc)).astype(BF16)
    kte = k * jnp.exp(bl - bc)
    half2 = jnp.concatenate([half, half], axis=0)
    tiles = range(2)
    sls = [slice(pt * LANES, (pt + 1) * LANES) for pt in tiles]
    qs = [jnp.concatenate([jnp.where(half2 == 0, q_in[:, sl], 0.0), jnp.where(half2 == 1, q_in[:, sl], 0.0)],
                          axis=0).astype(BF16) for sl in sls]
    att = [jnp.where(cmask, jnp.concatenate(
        [_dot_nt(qs[pt][0:n], k_in[:n, sls[pt]]), _dot_nt(qs[pt][n:2 * n], k_in[n:, sls[pt]]),
         _dot_nt(qs[pt][2 * n:3 * n], k_in[:n, sls[pt]]), _dot_nt(qs[pt][3 * n:], k_in[n:, sls[pt]])], axis=0),
        0.0).astype(BF16) for pt in tiles]
    v2 = [[vd[:, 2 * pt * GLA_DV:(2 * pt + 2) * GLA_DV].astype(BF16) for vd in (vf, vb)] for pt in tiles]
    intra = [[_dot(att[pt][(2 * hh + d) * n:(2 * hh + d + 1) * n], v2[pt][d][:, hh * GLA_DV:(hh + 1) * GLA_DV])
              for hh in range(2) for d in range(2)] for pt in tiles]
    kte_t = [kte[:, sl].T.astype(BF16) for sl in sls]
    la_t = [la[:, sl].T for sl in sls]
    st = [[ref[sl, :] for ref in (sf_ref, sb_ref)] for sl in sls]
    inter = [[[None] * n_ch for _ in range(2)] for _ in tiles]
    for step in range(n_ch):
        for pt in tiles:
            for d in range(2):
                c = step if d == 0 else n_ch - 1 - step
                r0, c0 = d * n + c * ch, d * n + c * ch
                qc = jnp.concatenate([qs[pt][r0:r0 + ch], qs[pt][2 * n + r0:2 * n + r0 + ch]], axis=0)
                inter[pt][d][c] = _dot(qc, st[pt][d].astype(BF16))
                upd = _dot(kte_t[pt][:, c0:c0 + ch], v2[pt][d][c * ch:(c + 1) * ch])
                dec = jnp.exp(jnp.sum(la_t[pt][:, c0:c0 + ch], axis=1, keepdims=True))
                st[pt][d] = st[pt][d] * dec + jnp.where(row_lo, upd[:, :GLA_DV], upd[:, GLA_DV:])
    outs = [[None] * GLA_HEADS for _ in range(2)]
    for pt in tiles:
        sf_ref[sls[pt], :] = st[pt][0]
        sb_ref[sls[pt], :] = st[pt][1]
        for hh in range(2):
            rs = slice(hh * HEAD, (hh + 1) * HEAD)
            for d in range(2):
                outs[d][2 * pt + hh] = intra[pt][2 * hh + d] + jnp.concatenate(
                    [inter[pt][d][c][rs] for c in range(n_ch)], axis=0)
    return jnp.concatenate(outs[0], axis=1), jnp.concatenate(outs[1], axis=1)


def _gla_kernel(qc_ref, kc_ref, vc_ref, ac_ref, qf_ref, kf_ref, vf_ref, af_ref,
                qb_ref, kb_ref, vb_ref, ab_ref, w2_ref, b2_ref,
                ocf_ref, ocb_ref, of_ref, ob_ref, sf_ref, sb_ref):
    s = pl.program_id(1)
    n, ch = GLA_BLOCK, GLA_CHUNK
    ri = lax.broadcasted_iota(jnp.int32, (n, n), 0)
    ci = lax.broadcasted_iota(jnp.int32, (n, n), 1)
    same = (ri // ch) == (ci // ch)
    low, up = same & (ri >= ci), same & (ri <= ci)
    sums_f, sums_b = low.astype(BF16), up.astype(BF16)
    cmask = jnp.concatenate([low, up, low, up], axis=0)
    half = lax.broadcasted_iota(jnp.int32, (n, LANES), 1) // HEAD
    row_lo = lax.broadcasted_iota(jnp.int32, (LANES, GLA_DV), 0) < HEAD

    @pl.when(s == 0)
    def _():
        sf_ref[...] = jnp.zeros_like(sf_ref)
        sb_ref[...] = jnp.zeros_like(sb_ref)

    def run(fwd, bwd, o_f, o_b):
        o_f[0], o_b[0] = _gla_block(tuple(r[0] for r in fwd), tuple(r[0] for r in bwd), w2_ref, b2_ref,
                                    sf_ref, sb_ref, sums_f, sums_b, cmask, half, row_lo)

    ctx = (qc_ref, kc_ref, vc_ref, ac_ref)

    @pl.when(s == 0)
    def _():
        run(ctx, ctx, ocf_ref, ocb_ref)

    @pl.when(s > 0)
    def _():
        run((qf_ref, kf_ref, vf_ref, af_ref), (qb_ref, kb_ref, vb_ref, ab_ref), of_ref, ob_ref)


def gla_scan(p3, pc3, w2, b2):
    B = p3.shape[0]
    nb = T // GLA_BLOCK
    cq, ck, cv, ca = COL["gla_q"] // 256, COL["gla_k"] // 256, COL["gla_v"] // 512, COL["gla_a"] // 128
    fwd = lambda s: jnp.maximum(s - 1, 0)
    bwd = lambda s: nb - jnp.maximum(s, 1)

    def spec(width, col, row):
        return pl.BlockSpec((1, GLA_BLOCK, width), lambda b, s: (b, row(s), col))

    zero = lambda s: 0
    in_specs = ([spec(256, cq, zero), spec(256, ck, zero), spec(512, cv, zero), spec(128, ca, zero)]
                + [spec(256, cq, fwd), spec(256, ck, fwd), spec(512, cv, fwd), spec(128, ca, fwd)]
                + [spec(256, cq, bwd), spec(256, ck, bwd), spec(512, cv, bwd), spec(128, ca, bwd)]
                + [pl.BlockSpec((128, 512), lambda b, s: (0, 0)), pl.BlockSpec((1, 512), lambda b, s: (0, 0))])
    out_specs = [spec(512, 0, zero), spec(512, 0, zero), spec(512, 0, fwd), spec(512, 0, bwd)]
    out_shape = [jax.ShapeDtypeStruct((B, L, 512), F32)] * 2 + [jax.ShapeDtypeStruct((B, T, 512), F32)] * 2
    return pl.pallas_call(
        _gla_kernel, grid=(B, nb + 1), in_specs=in_specs, out_specs=out_specs, out_shape=out_shape,
        scratch_shapes=[pltpu.VMEM((256, 128), F32), pltpu.VMEM((256, 128), F32)],
        compiler_params=_cp("parallel", "arbitrary"), name="gla_scan",
    )(pc3, pc3, pc3, pc3, p3, p3, p3, p3, p3, p3, p3, p3, w2, b2)


def _gla_gate(o, r, g):
    outs = []
    for hd in range(GLA_HEADS):
        sl = slice(hd * GLA_DV, (hd + 1) * GLA_DV)
        rh = r[:, sl]
        outs.append(_rms(o[:, sl], g) * (rh * _sigmoid(rh)))
    return jnp.concatenate(outs, axis=1)


def _stack_halves(qt, half):
    zero = jnp.zeros_like(qt)
    return jnp.concatenate([jnp.where(half == 0, qt, zero), jnp.where(half == 1, qt, zero)], axis=0).astype(BF16)


def _merge_halves(o, half):
    n = o.shape[0] // 2
    return jnp.where(half == 0, o[:n], o[n:])


def _softmax_parts(s_list, sink=None):
    m = s_list[0].max(axis=1, keepdims=True)
    for s in s_list[1:]:
        m = jnp.maximum(m, s.max(axis=1, keepdims=True))
    if sink is not None:
        m = jnp.maximum(m, sink)
    es = [jnp.exp(s - m) for s in s_list]
    den = es[0].sum(axis=1, keepdims=True)
    for e in es[1:]:
        den = den + e.sum(axis=1, keepdims=True)
    if sink is not None:
        den = den + jnp.exp(sink - m)
    return [e.astype(BF16) for e in es], 1.0 / den


def _rope(x, cos, sin, lane):
    partner = jnp.where((lane % HEAD) < HEAD // 2, pltpu.roll(x, LANES - HEAD // 2, 1),
                        pltpu.roll(x, HEAD // 2, 1))
    return x * cos + partner * sin


def _swa_kernel(sink_ref, mask_ref, q_ref, kp_ref, kc_ref, kn_ref, vp_ref, vc_ref, vn_ref, kx_ref, vx_ref,
                cos_ref, sin_ref, o_ref):
    n = pl.program_id(1)
    nb = pl.num_programs(1)
    blk = SWA_BLOCK
    lane = lax.broadcasted_iota(jnp.int32, (blk, LANES), 1)
    half = lane // HEAD

    def tab(ref, i):
        return ref[pl.ds(pl.multiple_of(i * blk, blk), blk), :]

    ip, inx = jnp.maximum(n - 1, 0), jnp.minimum(n + 1, nb - 1)
    kb = jnp.concatenate([_rope(kp_ref[0].astype(F32), tab(cos_ref, ip), tab(sin_ref, ip), lane),
                          _rope(kc_ref[0].astype(F32), tab(cos_ref, n), tab(sin_ref, n), lane),
                          _rope(kn_ref[0].astype(F32), tab(cos_ref, inx), tab(sin_ref, inx), lane)],
                         axis=0).astype(BF16)
    vb = jnp.concatenate([vp_ref[0], vc_ref[0], vn_ref[0]], axis=0)
    cos_q, sin_q = tab(cos_ref, n), tab(sin_ref, n)
    qs = jnp.concatenate([_stack_halves(_rope(q_ref[0, :, p * LANES:(p + 1) * LANES].astype(F32), cos_q, sin_q, lane)
                                        * HEAD ** -0.5, half) for p in range(4)], axis=0)
    grp = lax.broadcasted_iota(jnp.int32, (8 * blk, 1), 0) // blk
    sink = jnp.zeros((8 * blk, 1), F32)
    for g in range(8):
        sink = jnp.where(grp == g, sink_ref[g // 2 + 4 * (g % 2)], sink)
    n_sub = 2
    sub = 8 * blk // n_sub

    def scores(r):
        rs = slice(r * sub, (r + 1) * sub)
        s_loc = _dot_nt(qs[rs], kb) + mask_ref[rs, :]
        s_loc = jnp.concatenate([s_loc[:, :blk] + jnp.where(n == 0, NEG, 0.0), s_loc[:, blk:2 * blk],
                                 s_loc[:, 2 * blk:] + jnp.where(n == nb - 1, NEG, 0.0)], axis=1)
        return s_loc, _dot_nt(qs[rs], kx_ref[0])

    def finish(r, s_loc, s_ctx):
        (e_loc, e_ctx), inv = _softmax_parts([s_loc, s_ctx], sink=sink[r * sub:(r + 1) * sub])
        o = (_dot(e_loc, vb) + _dot(e_ctx, vx_ref[0])) * inv
        for t in range(sub // (2 * blk)):
            p = r * (sub // (2 * blk)) + t
            o_ref[0, :, p * LANES:(p + 1) * LANES] = _merge_halves(o[2 * t * blk:2 * (t + 1) * blk], half).astype(BF16)

    sc = [scores(r) for r in range(n_sub)]
    for r in range(n_sub):
        finish(r, *sc[r])


def swa(p3, pc3, sink, cos_t, sin_t):
    B = p3.shape[0]
    nb = T // SWA_BLOCK
    qi = np.arange(SWA_BLOCK)[:, None]
    kj = np.arange(3 * SWA_BLOCK)[None, :]
    mask = jnp.asarray(np.tile(np.where(np.abs(kj - SWA_BLOCK - qi) <= SWA_WINDOW, 0.0, NEG), (8, 1)).astype(np.float32))
    ck, cv = COL["swa_k"] // 128, COL["swa_v"] // 128
    prev = lambda n: jnp.maximum(n - 1, 0)
    cur = lambda n: n
    nxt = lambda n: jnp.minimum(n + 1, nb - 1)
    kv = lambda col, row: pl.BlockSpec((1, SWA_BLOCK, 128), lambda b, n: (b, row(n), col))
    full = lambda col: pl.BlockSpec((1, L, 128), lambda b, n: (b, 0, col))
    tab = pl.BlockSpec((T, 128), lambda b, n: (0, 0))
    return pl.pallas_call(
        _swa_kernel, grid=(B, nb),
        in_specs=[pl.BlockSpec(memory_space=pltpu.SMEM),
                  pl.BlockSpec((8 * SWA_BLOCK, 3 * SWA_BLOCK), lambda b, n: (0, 0)),
                  pl.BlockSpec((1, SWA_BLOCK, 512), lambda b, n: (b, n, COL["swa_q"] // 512)),
                  kv(ck, prev), kv(ck, cur), kv(ck, nxt), kv(cv, prev), kv(cv, cur), kv(cv, nxt),
                  full(ck), full(cv), tab, tab],
        out_specs=pl.BlockSpec((1, SWA_BLOCK, 512), lambda b, n: (b, n, 0)),
        out_shape=jax.ShapeDtypeStruct((B, T, 512), BF16),
        compiler_params=_cp("parallel", "parallel"), name="swa",
    )(sink, mask, p3, p3, p3, p3, p3, p3, p3, pc3, pc3, cos_t, sin_t)


def _ctx_attn_kernel(sink_ref, q_ref, k_ref, v_ref, o_ref, *, use_sink, kv_tiles):
    n = q_ref.shape[1]
    half = lax.broadcasted_iota(jnp.int32, (n, LANES), 1) // HEAD
    for p in range(4):
        sl = slice(p * LANES, (p + 1) * LANES)
        ks = sl if kv_tiles == 4 else slice(0, LANES)
        qs = _stack_halves(q_ref[0, :, sl] * HEAD ** -0.5, half)
        sink = None
        if use_sink:
            top = lax.broadcasted_iota(jnp.int32, (2 * n, 1), 0) < n
            sink = jnp.where(top, sink_ref[p], sink_ref[p + 4])
        (e,), inv = _softmax_parts([_dot_nt(qs, k_ref[0, :, ks])], sink=sink)
        o = _dot(e, v_ref[0, :, ks]) * inv
        o_ref[0, :, sl] = _merge_halves(o, half).astype(BF16)


def ctx_attn(pc3, sink, qcol, kcol, vcol, kv_tiles, use_sink):
    B, w = pc3.shape[0], kv_tiles * 128
    return pl.pallas_call(
        functools.partial(_ctx_attn_kernel, use_sink=use_sink, kv_tiles=kv_tiles), grid=(B,),
        in_specs=[pl.BlockSpec(memory_space=pltpu.SMEM),
                  pl.BlockSpec((1, L, 512), lambda b: (b, 0, qcol // 512)),
                  pl.BlockSpec((1, L, w), lambda b: (b, 0, kcol // w)),
                  pl.BlockSpec((1, L, w), lambda b: (b, 0, vcol // w))],
        out_specs=pl.BlockSpec((1, L, 512), lambda b: (b, 0, 0)),
        out_shape=jax.ShapeDtypeStruct((B, L, 512), BF16),
        compiler_params=_cp("parallel"), name="ctx_attn")(sink, pc3, pc3, pc3)


def _na_kernel(q_ref, k_ref, v_ref, kx_ref, vx_ref, *rest):
    bias_refs, o_ref = rest[:-1], rest[-1]
    nr = len(bias_refs)
    i = pl.program_id(1)
    band = NA_KR * GRID_W
    half = lax.broadcasted_iota(jnp.int32, (GRID_W, LANES), 1) // HEAD
    starts = [pl.multiple_of(jnp.clip(nr * i + k - NA_KR // 2, 0, ROWS - NA_KR) * GRID_W, GRID_W)
              for k in range(nr)]
    s_loc, s_ctx = [], []
    for p in range(4):
        sl = slice(p * LANES, (p + 1) * LANES)
        qs = [_stack_halves(q_ref[0, k * GRID_W:(k + 1) * GRID_W, sl] * HEAD ** -0.5, half) for k in range(nr)]
        for k in range(nr):
            kb = k_ref[0, pl.ds(starts[k], band), sl]
            s_loc.append(_dot_nt(qs[k], kb) + bias_refs[k][0, 2 * p * GRID_W:(2 * p + 2) * GRID_W, :])
        s_ctx.append(_dot_nt(jnp.concatenate(qs, axis=0), kx_ref[0, :, sl]))
    (e_loc, e_ctx), inv = _softmax_parts([jnp.concatenate(s_loc, axis=0), jnp.concatenate(s_ctx, axis=0)])
    blk = 2 * GRID_W
    for p in range(4):
        sl = slice(p * LANES, (p + 1) * LANES)
        r0 = p * nr * blk
        o_ctx = _dot(e_ctx[r0:r0 + nr * blk], vx_ref[0, :, sl])
        for k in range(nr):
            rs = slice(r0 + k * blk, r0 + (k + 1) * blk)
            vb = v_ref[0, pl.ds(starts[k], band), sl]
            o = (_dot(e_loc[rs], vb) + o_ctx[k * blk:(k + 1) * blk]) * inv[rs]
            o_ref[0, k * GRID_W:(k + 1) * GRID_W, sl] = _merge_halves(o, half).astype(BF16)


def na(p3, pc3, bias, rows_per_step=NA_ROWS_PER_STEP):
    B, nr = p3.shape[0], rows_per_step
    cq, ck, cv = COL["na_q"] // 512, COL["na_k"] // 512, COL["na_v"] // 512
    shift = lambda r: r - jnp.clip(r - NA_KR // 2, 0, ROWS - NA_KR)
    bias_spec = lambda k: pl.BlockSpec((1, NA_HEADS * GRID_W, NA_KR * GRID_W),
                                       lambda b, i: (shift(nr * i + k), 0, 0))
    return pl.pallas_call(
        _na_kernel, grid=(B, ROWS // nr),
        in_specs=[pl.BlockSpec((1, nr * GRID_W, 512), lambda b, i: (b, i, cq)),
                  pl.BlockSpec((1, T, 512), lambda b, i: (b, 0, ck)),
                  pl.BlockSpec((1, T, 512), lambda b, i: (b, 0, cv)),
                  pl.BlockSpec((1, L, 512), lambda b, i: (b, 0, ck)),
                  pl.BlockSpec((1, L, 512), lambda b, i: (b, 0, cv))] + [bias_spec(k) for k in range(nr)],
        out_specs=pl.BlockSpec((1, nr * GRID_W, 512), lambda b, i: (b, i, 0)),
        out_shape=jax.ShapeDtypeStruct((B, T, 512), BF16),
        compiler_params=_cp("parallel", "arbitrary"), name="na")(p3, p3, p3, pc3, pc3, *([bias] * nr))


def na_bias_table(rpb):
    col = np.arange(GRID_W)
    col_start = np.clip(col - NA_KC // 2, 0, GRID_W - NA_KC)
    col_ok = (col[None, :] >= col_start[:, None]) & (col[None, :] < col_start[:, None] + NA_KC)
    col_off = np.clip(col[None, :] - col[:, None], -(NA_KC - 1), NA_KC - 1) + (NA_KC - 1)
    onehot = (col_off[:, :, None] == np.arange(2 * NA_KC - 1)).astype(np.float32)
    toep = jnp.einsum("hro,qko->hrqk", rpb, jnp.asarray(onehot), precision=HIGHEST)
    toep = jnp.where(jnp.asarray(col_ok)[None, None], toep, NEG)
    per_shift = [toep[:, NA_KR - 1 - s:2 * NA_KR - 1 - s] for s in range(NA_KR)]
    tab = jnp.stack(per_shift).transpose(0, 1, 3, 2, 4)
    return tab.reshape(NA_KR, NA_HEADS * GRID_W, NA_KR * GRID_W).astype(F32)


def _gelu(x):
    return 0.5 * x * (1.0 + jnp.tanh(np.sqrt(2.0 / np.pi).astype(np.float32) * (x + 0.044715 * (x * x * x))))


def _sgu_kernel(u_ref, v_ref, g_ref, b_ref, ws_ref, bs_ref, y_ref):
    n_chunk = u_ref.shape[1] // SGU_CHUNK
    v = _gelu(v_ref[0].astype(F32))
    mu = jnp.mean(v, axis=-1, keepdims=True)
    var = jnp.mean(jnp.square(v - mu), axis=-1, keepdims=True)
    vn = ((v - mu) * lax.rsqrt(var + EPS) * g_ref[...] + b_ref[...]).astype(BF16)
    cols = []
    for g in range(4):
        gs = slice(g * 128, (g + 1) * 128)
        rhs = jnp.concatenate([vn[c * SGU_CHUNK:(c + 1) * SGU_CHUNK, gs] for c in range(n_chunk)], axis=1)
        mixed = _dot(ws_ref[g], rhs)
        cols.append(jnp.concatenate([mixed[:, c * 128:(c + 1) * 128] + bs_ref[g] for c in range(n_chunk)], axis=0))
    y_ref[0] = (_gelu(u_ref[0].astype(F32)) * jnp.concatenate(cols, axis=1)).astype(BF16)


def sgu(p3, g, b, ws, bs):
    B, n = p3.shape[0], p3.shape[1]
    tr = min(n, TOKEN_TILE)
    blk = lambda col: pl.BlockSpec((1, tr, 512), lambda bb, i: (bb, i, col))
    const = lambda shape: pl.BlockSpec(shape, lambda bb, i: (0,) * len(shape))
    return pl.pallas_call(
        _sgu_kernel, grid=(B, n // tr),
        in_specs=[blk(COL["sgu_u"] // 512), blk(COL["sgu_v"] // 512), const((1, 512)), const((1, 512)),
                  const((4, 128, 128)), const((4, 128, 128))],
        out_specs=blk(0), out_shape=jax.ShapeDtypeStruct((B, n, 512), BF16),
        compiler_params=_cp("parallel", "parallel"), name="sgu")(p3, p3, g, b, ws, bs)


def _merge_kernel(h_ref, of_ref, ob_ref, r_ref, gn_ref, b_ref, n_ref, d_ref, wg_ref, bg_ref, wb_ref, o_ref,
                  a_ref):
    @pl.when(pl.program_id(1) == 0)
    def _():
        a_ref[...] = _gla_gate(of_ref[...] + ob_ref[...], r_ref[...].astype(F32), gn_ref[...]).astype(BF16)

    h = h_ref[...]
    acc = None
    for i, br in enumerate((a_ref, b_ref, n_ref, d_ref)):
        term = _sigmoid(_dot(h, wg_ref[0, i].astype(BF16)) + bg_ref[i]) * _dot(br[...], wb_ref[i])
        acc = term if acc is None else acc + term
    o_ref[...] = acc.astype(BF16)


def merge(h, o_f, o_b, p, gn, branches, wg, l, bg, wb, tm):
    M, tn = h.shape[0], MERGE_COL_TILE
    tm = min(tm, M)
    br = pl.BlockSpec((tm, 512), lambda i, j: (i, 0))
    return pl.pallas_call(
        _merge_kernel, grid=(M // tm, D // tn),
        in_specs=[pl.BlockSpec((tm, D), lambda i, j: (i, 0)), br, br,
                  pl.BlockSpec((tm, 512), lambda i, j: (i, COL["gla_r"] // 512)),
                  pl.BlockSpec((1, GLA_DV), lambda i, j: (0, 0)), br, br, br,
                  pl.BlockSpec((1, 4, D, tn), lambda i, j: (l, 0, 0, j)),
                  pl.BlockSpec((4, 1, tn), lambda i, j: (0, 0, j)),
                  pl.BlockSpec((4, 512, tn), lambda i, j: (0, 0, j))],
        out_specs=pl.BlockSpec((tm, tn), lambda i, j: (i, j)),
        out_shape=jax.ShapeDtypeStruct((M, D), BF16),
        scratch_shapes=[pltpu.VMEM((tm, 512), BF16)],
        compiler_params=_cp("parallel", "arbitrary"), name="merge")(h, o_f, o_b, p, gn, *branches, wg, bg, wb)


def _outproj_kernel(acc_ref, w_ref, x_ref, gate_ref, g_ref, shift_ref, scale_ref, wr_ref,
                    x1_ref, h2_ref, lg_ref):
    n_sub = 2
    sub = acc_ref.shape[0] // n_sub

    def matmul(k):
        rs = slice(k * sub, (k + 1) * sub)
        return x_ref[rs, :] + gate_ref[0] * _dot(acc_ref[rs, :], w_ref[...])

    def epilogue(k, x1):
        rs = slice(k * sub, (k + 1) * sub)
        x1_ref[rs, :] = x1
        h2 = (_rms(x1, g_ref[...]) * (1.0 + scale_ref[0]) + shift_ref[0]).astype(BF16)
        lg_ref[rs, :] = _dot(h2, wr_ref[...])
        bits = pltpu.bitcast(h2.astype(F32), jnp.int32)
        h2_ref[rs, :] = lax.shift_right_logical(bits[:, :D // 2], 16) | (bits[:, D // 2:] & jnp.int32(-65536))

    pending = matmul(0)
    for k in range(n_sub):
        nxt = matmul(k + 1) if k + 1 < n_sub else None
        epilogue(k, pending)
        pending = nxt


def outproj(acc, w, x2, mod3, g, wr, tm, mod_row):
    M = acc.shape[0]
    tm = min(tm, M)
    row = lambda k: pl.BlockSpec((1, 1, D), lambda i: (mod_row(i) * 6 + k, 0, 0))
    tile = pl.BlockSpec((tm, D), lambda i: (i, 0))
    return pl.pallas_call(
        _outproj_kernel, grid=(M // tm,),
        in_specs=[tile, pl.BlockSpec((D, D), lambda i: (0, 0)), tile, row(2),
                  pl.BlockSpec((1, D), lambda i: (0, 0)), row(3), row(4),
                  pl.BlockSpec((D, 128), lambda i: (0, 0))],
        out_specs=[tile, pl.BlockSpec((tm, D // 2), lambda i: (i, 0)), pl.BlockSpec((tm, 128), lambda i: (i, 0))],
        out_shape=[jax.ShapeDtypeStruct((M, D), F32), jax.ShapeDtypeStruct((M, D // 2), jnp.int32),
                   jax.ShapeDtypeStruct((M, 128), F32)],
        compiler_params=_cp("parallel"), name="outproj")(acc, w, x2, mod3, g, mod3, mod3, wr)


def _select_kernel(lg_ref, sel_ref, aff_ref, slot_ref, *, cap):
    n = lg_ref.shape[1]
    nt = n // LANES
    lane = lax.broadcasted_iota(jnp.int32, (n, LANES), 1)
    lg = jnp.where(lane < N_EXPERTS, lg_ref[0], NEG)
    e = jnp.exp(lg - lg.max(axis=1, keepdims=True))
    aff = (e / e.sum(axis=1, keepdims=True)).T[:N_EXPERTS]
    aff_ref[0] = aff
    bits = pltpu.bitcast(aff, jnp.int32)

    def count(mask):
        return jnp.sum(jnp.where(mask, 1.0, 0.0), axis=1, keepdims=True)

    def bit_step(i, thr):
        cand = thr | lax.shift_left(jnp.int32(1), 30 - i)
        return jnp.where(count(bits >= cand) >= cap, cand, thr)

    thr = lax.fori_loop(0, 31, bit_step, jnp.zeros((N_EXPERTS, 1), jnp.int32))
    gt, eq = bits > thr, bits == thr

    r = lax.broadcasted_iota(jnp.int32, (LANES, LANES), 0)
    c = lax.broadcasted_iota(jnp.int32, (LANES, LANES), 1)
    upper = (r <= c).astype(BF16)
    r2 = lax.broadcasted_iota(jnp.int32, (nt * N_EXPERTS, nt * N_EXPERTS), 0)
    c2 = lax.broadcasted_iota(jnp.int32, (nt * N_EXPERTS, nt * N_EXPERTS), 1)
    before = ((r2 % N_EXPERTS == c2 % N_EXPERTS) & (c2 // N_EXPERTS < r2 // N_EXPERTS)).astype(BF16)

    def excl_prefix(x):
        xs = jnp.concatenate([x[:, i * LANES:(i + 1) * LANES] for i in range(nt)], axis=0).astype(BF16)
        inc = _dot(xs, upper)
        tot = jnp.broadcast_to(inc[:, LANES - 1:LANES], inc.shape).astype(BF16)
        full = inc + _dot(before, tot)
        return jnp.concatenate([full[i * N_EXPERTS:(i + 1) * N_EXPERTS] for i in range(nt)], axis=1) - x

    eq01 = jnp.where(eq, 1.0, 0.0)
    mask = gt | (eq & (excl_prefix(eq01) < cap - count(gt)))
    slot = jnp.where(mask, excl_prefix(jnp.where(mask, 1.0, 0.0)), -1.0)
    sel_ref[0] = slot.astype(jnp.int32)
    pad = jnp.full((LANES - N_EXPERTS, n), -1.0, F32)
    slot_ref[0] = jnp.concatenate([slot, pad], axis=0).T


def select(lg3, cap):
    B, n = lg3.shape[0], lg3.shape[1]
    tok = pl.BlockSpec((1, n, 128), lambda b: (b, 0, 0))
    row = pl.BlockSpec((1, N_EXPERTS, n), lambda b: (b, 0, 0))
    return pl.pallas_call(
        functools.partial(_select_kernel, cap=cap), grid=(B,),
        in_specs=[tok], out_specs=[row, row, tok],
        out_shape=[jax.ShapeDtypeStruct((B, N_EXPERTS, n), jnp.int32),
                   jax.ShapeDtypeStruct((B, N_EXPERTS, n), F32), jax.ShapeDtypeStruct((B, n, 128), F32)],
        compiler_params=_cp("parallel"), name="select")(lg3)


def _slot_tokens_kernel(sel_ref, aff_ref, idx_ref, g_ref, *, cap):
    n = sel_ref.shape[2]
    row = lax.broadcasted_iota(jnp.int32, (cap, n), 0)
    lane = lax.broadcasted_iota(jnp.int32, (LANES, n), 0)
    tok = lax.broadcasted_iota(jnp.int32, (LANES, n), 1)
    pos = jnp.where(lane == 0, tok // LANES, jnp.where(lane == 1, tok % LANES, 0)).astype(F32)
    for e in range(N_EXPERTS):
        onehot = jnp.where(row == sel_ref[0, e:e + 1, :], 1.0, 0.0).astype(BF16)
        a = aff_ref[0, e:e + 1, :]
        hi = a.astype(BF16)
        r1 = a - hi.astype(F32)
        mid = r1.astype(BF16)
        lo = (r1 - mid.astype(F32)).astype(BF16)
        cols = jnp.where(lane == 2, hi.astype(F32), jnp.where(lane == 3, mid.astype(F32),
                         jnp.where(lane == 4, lo.astype(F32), pos))).astype(BF16)
        r = _dot_nt(onehot, cols)
        idx_ref[e, 0] = (r[:, 0:1] * LANES + r[:, 1:2]).astype(jnp.int32) + pl.program_id(0) * n
        g_ref[e, 0] = r[:, 2:3] + r[:, 3:4] + r[:, 4:5]


def slot_tokens(sel, aff, cap):
    B, n = sel.shape[0], sel.shape[2]
    row = pl.BlockSpec((1, N_EXPERTS, n), lambda b: (b, 0, 0))
    col = pl.BlockSpec((N_EXPERTS, 1, cap, 1), lambda b: (0, b, 0, 0))
    idx, gate = pl.pallas_call(
        functools.partial(_slot_tokens_kernel, cap=cap), grid=(B,),
        in_specs=[row, row], out_specs=[col, col],
        out_shape=[jax.ShapeDtypeStruct((N_EXPERTS, B, cap, 1), jnp.int32),
                   jax.ShapeDtypeStruct((N_EXPERTS, B, cap, 1), F32)],
        compiler_params=_cp("parallel"), name="slot_tokens")(sel, aff)
    return idx.reshape(N_EXPERTS, B * cap, 1), gate.reshape(N_EXPERTS, B * cap, 1)


def _expert_kernel(idx_ref, h_hbm, gate_ref, wg_ref, wu_ref, wd_ref, o_ref, xg_ref, xb_ref, acc_ref, sem,
                   *, tm):
    e, i, f = pl.program_id(0), pl.program_id(1), pl.program_id(2)
    n_i, n_f = pl.num_programs(1), pl.num_programs(2)
    q = e * n_i + i
    n_q = pl.num_programs(0) * n_i
    slot = q % 2
    share = tm // n_f

    def row_copy(tile, r, dst_slot):
        src = h_hbm.at[pl.ds(idx_ref[tile * tm + r], 1), :]
        return pltpu.make_async_copy(src, xg_ref.at[dst_slot, pl.ds(r, 1), :], sem.at[dst_slot])

    def wait_tile(dst_slot):
        pltpu.make_async_copy(h_hbm.at[pl.ds(0, tm), :], xg_ref.at[dst_slot], sem.at[dst_slot]).wait()

    @pl.when((q == 0) & (f == 0))
    def _():
        def first(r, carry):
            row_copy(0, r, 0).start()
            return carry
        lax.fori_loop(0, tm, first, 0)

    @pl.when(f == 0)
    def _():
        wait_tile(slot)
        p = xg_ref[slot]
        xb_ref[:, :D // 2] = pltpu.bitcast(lax.shift_left(p, 16), F32).astype(BF16)
        xb_ref[:, D // 2:] = pltpu.bitcast(p & jnp.int32(-65536), F32).astype(BF16)
        acc_ref[...] = jnp.zeros_like(acc_ref)

    x = xb_ref[...]
    g = _dot(x, wg_ref[0, 0].astype(BF16))
    u = _dot(x, wu_ref[0, 0].astype(BF16))
    hid = ((g * _sigmoid(g)) * u).astype(BF16)
    nxt = jnp.minimum(q + 1, n_q - 1)
    for r in range(share):
        row_copy(nxt, f * share + r, 1 - slot).start()
    acc_ref[...] += _dot(hid, wd_ref[0, 0].astype(BF16))

    @pl.when(f == n_f - 1)
    def _():
        o_ref[0] = (acc_ref[...] * gate_ref[0]).astype(BF16)

    @pl.when((q == n_q - 1) & (f == n_f - 1))
    def _():
        wait_tile(1 - slot)


def experts(idx, h2p, gate, wg, wu, wd, l, tm):
    E, M, tf = gate.shape[0], gate.shape[1], EXPERT_HIDDEN_TILE
    F = wg.shape[3]
    tm = min(tm, M)
    grid_spec = pltpu.PrefetchScalarGridSpec(
        num_scalar_prefetch=1, grid=(E, M // tm, F // tf),
        in_specs=[pl.BlockSpec(memory_space=pl.ANY),
                  pl.BlockSpec((1, tm, 1), lambda e, i, f, idx: (e, i, 0)),
                  pl.BlockSpec((1, 1, D, tf), lambda e, i, f, idx: (l, e, 0, f)),
                  pl.BlockSpec((1, 1, D, tf), lambda e, i, f, idx: (l, e, 0, f)),
                  pl.BlockSpec((1, 1, tf, D), lambda e, i, f, idx: (l, e, f, 0))],
        out_specs=pl.BlockSpec((1, tm, D), lambda e, i, f, idx: (e, i, 0)),
        scratch_shapes=[pltpu.VMEM((2, tm, D // 2), jnp.int32), pltpu.VMEM((tm, D), BF16),
                        pltpu.VMEM((tm, D), F32), pltpu.SemaphoreType.DMA((2,))])
    return pl.pallas_call(
        functools.partial(_expert_kernel, tm=tm), grid_spec=grid_spec,
        out_shape=jax.ShapeDtypeStruct((E, M, D), BF16),
        compiler_params=_cp("arbitrary", "arbitrary", "arbitrary"), name="experts")(idx, h2p, gate, wg, wu, wd)


COMBINE_GROUP = 8


def _combine_kernel(slot_ref, ye_ref, x_ref, gate_ref, gfin_ref, o_ref, acc_ref, *, cap, final):
    g = pl.program_id(2)
    tt = x_ref.shape[1]
    lane = lax.broadcasted_iota(jnp.int32, (tt, LANES), 1)
    j = lax.broadcasted_iota(jnp.int32, (tt, cap), 1).astype(F32)
    slots = slot_ref[0]
    parts = []
    for k in range(COMBINE_GROUP):
        slot = jnp.sum(jnp.where(lane == g * COMBINE_GROUP + k, slots, 0.0), axis=1, keepdims=True)
        parts.append(jnp.where(j == slot, 1.0, 0.0).astype(BF16))
    onehot = jnp.concatenate(parts, axis=1)
    contrib = _dot(onehot, ye_ref[:, 0].reshape(COMBINE_GROUP * cap, D))

    @pl.when(g == 0)
    def _():
        acc_ref[...] = contrib

    @pl.when(g == pl.num_programs(2) - 1)
    def _():
        x2 = x_ref[0] + gate_ref[0] * (acc_ref[...] + contrib)
        if final:
            x2 = _rms(x2, gfin_ref[...])
        o_ref[0] = x2


def combine(slot_tok, ye, x3, mod3, gfin, cap, mod_row, final):
    B, n = x3.shape[0], x3.shape[1]
    tt = min(n, TOKEN_TILE)
    assert N_EXPERTS == 2 * COMBINE_GROUP
    xt = pl.BlockSpec((1, tt, D), lambda b, t, g: (b, t, 0))
    return pl.pallas_call(
        functools.partial(_combine_kernel, cap=cap, final=final), grid=(B, n // tt, N_EXPERTS // COMBINE_GROUP),
        in_specs=[pl.BlockSpec((1, tt, 128), lambda b, t, g: (b, t, 0)),
                  pl.BlockSpec((COMBINE_GROUP, 1, cap, D), lambda b, t, g: (g, b, 0, 0)), xt,
                  pl.BlockSpec((1, 1, D), lambda b, t, g: (mod_row(b) * 6 + 5, 0, 0)),
                  pl.BlockSpec((1, D), lambda b, t, g: (0, 0))],
        out_specs=xt, out_shape=jax.ShapeDtypeStruct((B, n, D), F32),
        scratch_shapes=[pltpu.VMEM((tt, D), F32)],
        compiler_params=_cp("parallel", "parallel", "arbitrary"), name="combine",
    )(slot_tok, ye.reshape(N_EXPERTS, B, cap, D), x3, mod3, gfin)


def _swa_q_perm():
    return np.concatenate([np.arange(HEAD) + (p + 4 * hh) * HEAD for p in range(4) for hh in range(2)])


def _prep_w_in(w):
    seg = lambda name, width: w[:, REF[name]:REF[name] + width]
    swa_q = seg("swa_q", 512)[:, _swa_q_perm()]
    parts = [seg("gla_v", 512), seg("gla_r", 512), swa_q, seg("na_q", 512), seg("na_k", 512), seg("na_v", 512),
             seg("sgu_u", 512), seg("sgu_v", 512), seg("gla_q", 256), seg("gla_k", 256), seg("swa_k", 128),
             seg("swa_v", 128), seg("gla_af", 16), seg("gla_ab", 16)]
    out = jnp.concatenate(parts, axis=1)
    return jnp.pad(out, ((0, 0), (0, NP - out.shape[1]))).astype(BF16)


def _rope_tables():
    quarter = HEAD // 4
    freqs = 10000.0 ** (-jnp.arange(quarter, dtype=F32) / quarter)
    t = jnp.arange(T)
    row = (t // GRID_W).astype(F32)
    col = (t % GRID_W).astype(F32)
    ang = jnp.concatenate([row[:, None] * freqs, col[:, None] * freqs], axis=-1)
    cos, sin = jnp.cos(ang), jnp.sin(ang)
    cos_t = jnp.tile(cos, (1, 4))
    sin_t = jnp.tile(jnp.concatenate([-sin, sin], axis=-1), (1, 2))
    return cos_t, sin_t


def _moe(h2, lg, x3, mod3, gfin, wg, wu, wd, l, cap, tm, mod_row, final):
    B, n = x3.shape[0], x3.shape[1]
    sel, aff, slot_tok = select(lg.reshape(B, n, 128), cap)
    idx, gate = slot_tokens(sel, aff, cap)
    ye = experts(idx.reshape(-1), h2, gate, wg, wu, wd, l, tm)
    return combine(slot_tok, ye, x3, mod3, gfin, cap, mod_row, final)


def kernel(x, c, ctx, c_ctx, w_ada, b_ada, norm1_g, norm2_g, w_in, gla_w_a2, gla_b_a2, gla_norm_g, swa_sink,
           na_rpb, sgu_ln_g, sgu_ln_b, sgu_w_s, sgu_b_s, w_gate, b_gate, w_branch, w_out, w_router,
           w_exp_gate, w_exp_up, w_exp_down, final_norm_g):
    B = x.shape[0]
    ctx_row = B
    n_mod = -(-(B + 1) // SUBLANES) * SUBLANES
    cvec = jnp.zeros((n_mod, D), F32).at[:B].set(c).at[ctx_row].set(c_ctx)
    cos_t, sin_t = _rope_tables()
    perm = _swa_q_perm()
    gfin = final_norm_g.reshape(1, D)
    tm = ROW_TILE
    ctx_tile_row = lambda i: ctx_row
    x2 = x.reshape(B * T, D)
    xc2 = ctx.reshape(B * L, D)
    for l in range(DEPTH):
        need_ctx = l < DEPTH - 1
        mod3 = ada(cvec, w_ada, b_ada.reshape(DEPTH, 1, 6 * D), l).reshape(n_mod * 6, 1, D)
        w_in_p = _prep_w_in(w_in[l])
        g1 = norm1_g[l].reshape(1, D)
        h, p = inproj(x2, mod3, g1, w_in_p, INPROJ_ROW_TILE, lambda i: i // (T // INPROJ_ROW_TILE))
        hc, pc = inproj(xc2, mod3, g1, w_in_p, INPROJ_ROW_TILE, ctx_tile_row)
        p3, pc3 = p.reshape(B, T, NP), pc.reshape(B, L, NP)

        w2 = jnp.zeros((128, 512), F32).at[:GLA_RANK, :256].set(gla_w_a2[l, 0])
        w2 = w2.at[GLA_RANK:2 * GLA_RANK, 256:].set(gla_w_a2[l, 1]).astype(BF16)
        b2 = gla_b_a2[l].reshape(1, 512)
        ocf, ocb, of, ob = gla_scan(p3, pc3, w2, b2)
        gn = gla_norm_g[l].reshape(1, GLA_DV)
        yb = swa(p3, pc3, swa_sink[l], cos_t, sin_t)
        yn = na(p3, pc3, na_bias_table(na_rpb[l]))
        ln_g, ln_b = sgu_ln_g[l].reshape(1, 512), sgu_ln_b[l].reshape(1, 512)
        ws = sgu_w_s[l].astype(BF16)
        bs = jnp.broadcast_to(sgu_b_s[l][:, :, None], (4, SGU_CHUNK, 128))
        yd = sgu(p3, ln_g, ln_b, ws, bs)

        bg = b_gate[l].reshape(4, 1, D)
        wb = w_branch[l].at[1].set(w_branch[l, 1][perm]).astype(BF16)
        wo = w_out[l].astype(BF16)
        wr = jnp.pad(w_router[l], ((0, 0), (0, 128 - N_EXPERTS))).astype(BF16)
        g2 = norm2_g[l].reshape(1, D)
        flat = lambda y: y.reshape(-1, 512)
        acc = merge(h, flat(of), flat(ob), p, gn, [flat(yb), flat(yn), flat(yd)], w_gate, l, bg, wb, tm)
        x1, h2, lg = outproj(acc, wo, x2, mod3, g2, wr, OUTPROJ_ROW_TILE, lambda i: i // (T // OUTPROJ_ROW_TILE))
        x2 = _moe(h2, lg, x1.reshape(B, T, D), mod3, gfin, w_exp_gate, w_exp_up, w_exp_down, l,
                  2 * T // N_EXPERTS, tm, lambda b: b, final=not need_ctx).reshape(B * T, D)
        if need_ctx:
            ybc = ctx_attn(pc3, swa_sink[l], COL["swa_q"], COL["swa_k"], COL["swa_v"], 1, True)
            ync = ctx_attn(pc3, swa_sink[l], COL["na_q"], COL["na_k"], COL["na_v"], 4, False)
            ydc = sgu(pc3, ln_g, ln_b, ws, bs)
            accc = merge(hc, flat(ocf), flat(ocb), pc, gn, [flat(ybc), flat(ync), flat(ydc)], w_gate, l, bg, wb, tm)
            xc1, hc2, lgc = outproj(accc, wo, xc2, mod3, g2, wr, OUTPROJ_ROW_TILE, ctx_tile_row)
            xc2 = _moe(hc2, lgc, xc1.reshape(B, L, D), mod3, gfin, w_exp_gate, w_exp_up, w_exp_down, l,
                       2 * L // N_EXPERTS, tm, lambda b: ctx_row, final=False).reshape(B * L, D)
    return x2.reshape(B, T, D)
```

```python
import functools

import numpy as np
import jax
import jax.numpy as jnp
from jax import lax
from jax.experimental import pallas as pl
from jax.experimental.pallas import tpu as pltpu

F32 = jnp.float32
BF16 = jnp.bfloat16
HIGHEST = lax.Precision.HIGHEST

D = 2048
T = 2048
L = 256
GRID_W = 64
ROWS = T // GRID_W
EPS = 1e-6
DEPTH = 2
NEG = -1e30

GLA_HEADS, GLA_DK, GLA_DV, GLA_RANK, GLA_TAU, GLA_CHUNK = 4, 64, 128, 16, 16.0, 64
GLA_BLOCK = 256
SWA_BLOCK, SWA_WINDOW = 128, 128
NA_KR, NA_KC, NA_HEADS = 8, 16, 8
SGU_CHUNK = 128
N_EXPERTS = 16
LANES = 128
HEAD = 64

NP = 5120
COL = dict(gla_v=0, gla_r=512, swa_q=1024, na_q=1536, na_k=2048, na_v=2560, sgu_u=3072, sgu_v=3584,
           gla_q=4096, gla_k=4352, swa_k=4608, swa_v=4736, gla_a=4864)
REF = dict(gla_q=0, gla_k=256, gla_v=512, gla_r=1024, gla_af=1536, gla_ab=1552, swa_q=1568, swa_k=2080,
           swa_v=2208, na_q=2336, na_k=2848, na_v=3360, sgu_u=3872, sgu_v=4384)

VMEM_LIMIT = 56 * 1024 * 1024
SUBLANES = 8

ROW_TILE = 1024
OUTPROJ_ROW_TILE = 512
ADA_COL_TILE = 1024
INPROJ_ROW_TILE = 512
MERGE_COL_TILE = 256
EXPERT_HIDDEN_TILE = 256
TOKEN_TILE = 512
NA_ROWS_PER_STEP = 4


def _cp(*sem, vmem=VMEM_LIMIT):
    return pltpu.CompilerParams(dimension_semantics=sem, vmem_limit_bytes=vmem)


def _sigmoid(x):
    return 1.0 / (1.0 + jnp.exp(-x))


def _rms(x, g):
    return x * lax.rsqrt(jnp.mean(x * x, axis=-1, keepdims=True) + EPS) * g


def _dot(a, b):
    return jnp.dot(a, b, preferred_element_type=F32)


def _dot_nt(a, b):
    return lax.dot_general(a, b, (((1,), (1,)), ((), ())), preferred_element_type=F32)


def _ada_kernel(c_ref, w_ref, b_ref, o_ref):
    c = c_ref[...]
    act = (c * _sigmoid(c)).astype(BF16)
    o_ref[...] = _dot(act, w_ref[0].astype(BF16)) + b_ref[0]


def ada(cvec, w, b, l):
    rows, tn = cvec.shape[0], ADA_COL_TILE
    return pl.pallas_call(
        _ada_kernel, grid=(6 * D // tn,),
        in_specs=[pl.BlockSpec((rows, D), lambda j: (0, 0)),
                  pl.BlockSpec((1, D, tn), lambda j: (l, 0, j)),
                  pl.BlockSpec((1, 1, tn), lambda j: (l, 0, j))],
        out_specs=pl.BlockSpec((rows, tn), lambda j: (0, j)),
        out_shape=jax.ShapeDtypeStruct((rows, 6 * D), F32),
        compiler_params=_cp("parallel"), name="ada")(cvec, w, b)


def _inproj_kernel(x_ref, shift_ref, scale_ref, g_ref, w_ref, h_ref, p_ref):
    n_sub = 2
    sub = x_ref.shape[0] // n_sub

    def modulate(k):
        rs = slice(k * sub, (k + 1) * sub)
        h = (_rms(x_ref[rs, :], g_ref[...]) * (1.0 + scale_ref[0]) + shift_ref[0]).astype(BF16)
        h_ref[rs, :] = h
        return h

    hs = [modulate(k) for k in range(n_sub)]
    for k in range(n_sub):
        p_ref[k * sub:(k + 1) * sub, :] = _dot(hs[k], w_ref[...]).astype(BF16)


def inproj(x2, mod3, g, w, tm, mod_row):
    M = x2.shape[0]
    tm = min(tm, M)
    return pl.pallas_call(
        _inproj_kernel, grid=(M // tm,),
        in_specs=[pl.BlockSpec((tm, D), lambda i: (i, 0)),
                  pl.BlockSpec((1, 1, D), lambda i: (mod_row(i) * 6 + 0, 0, 0)),
                  pl.BlockSpec((1, 1, D), lambda i: (mod_row(i) * 6 + 1, 0, 0)),
                  pl.BlockSpec((1, D), lambda i: (0, 0)),
                  pl.BlockSpec((D, NP), lambda i: (0, 0), pipeline_mode=pl.Buffered(1))],
        out_specs=[pl.BlockSpec((tm, D), lambda i: (i, 0)),
                   pl.BlockSpec((tm, NP), lambda i: (i, 0))],
        out_shape=[jax.ShapeDtypeStruct((M, D), BF16), jax.ShapeDtypeStruct((M, NP), BF16)],
        compiler_params=_cp("parallel"), name="inproj")(x2, mod3, mod3, g, w)


def _log_sigmoid(x):
    return jnp.minimum(x, 0.0) - jnp.log(1.0 + jnp.exp(-jnp.abs(x)))


def _split3(x):
    hi = x.astype(BF16)
    r1 = x - hi.astype(F32)
    mid = r1.astype(BF16)
    lo = (r1 - mid.astype(F32)).astype(BF16)
    return jnp.concatenate([hi, mid, lo], axis=1)


def _gla_block(fwd, bwd, w2_ref, b2_ref, sf_ref, sb_ref, sums_f, sums_b, cmask, half, row_lo):
    n, ch = GLA_BLOCK, GLA_CHUNK
    n_ch = n // ch
    (qf, kf, vf, af), (qb, kb, vb, ab) = fwd, bwd
    pre = jnp.concatenate([_dot(af.astype(BF16), w2_ref[:, :256]) + b2_ref[:, :256],
                           _dot(ab.astype(BF16), w2_ref[:, 256:]) + b2_ref[:, 256:]], axis=0)
    la = _log_sigmoid(pre) * (1.0 / GLA_TAU)
    la3 = _split3(la)
    cum = jnp.concatenate([_dot(sums_f, la3[:n]), _dot(sums_b, la3[n:])], axis=0)
    bc = cum[:, :256] + cum[:, 256:512] + cum[:, 512:]
    ends = [c * ch + ch - 1 for c in range(n_ch)] + [n + c * ch for c in range(n_ch)]
    bl = jnp.concatenate([jnp.broadcast_to(bc[r:r + 1], (ch, bc.shape[1])) for r in ends], axis=0)
    q = jnp.concatenate([qf, qb], axis=0).astype(F32)
    k = jnp.concatenate([kf, kb], axis=0).astype(F32)
    q_in = (q * GLA_DK ** -0.5) * jnp.exp(bc)
    k_in = (k * jnp.exp(-bc)).astype(BF16)
    kte = k * jnp.exp(bl - bc)
    half2 = jnp.concatenate([half, half], axis=0)
    tiles = range(2)
    sls = [slice(pt * LANES, (pt + 1) * LANES) for pt in tiles]
    qs = [jnp.concatenate([jnp.where(half2 == 0, q_in[:, sl], 0.0), jnp.where(half2 == 1, q_in[:, sl], 0.0)],
                          axis=0).astype(BF16) for sl in sls]
    att = [jnp.where(cmask, jnp.concatenate(
        [_dot_nt(qs[pt][0:n], k_in[:n, sls[pt]]), _dot_nt(qs[pt][n:2 * n], k_in[n:, sls[pt]]),
         _dot_nt(qs[pt][2 * n:3 * n], k_in[:n, sls[pt]]), _dot_nt(qs[pt][3 * n:], k_in[n:, sls[pt]])], axis=0),
        0.0).astype(BF16) for pt in tiles]
    v2 = [[vd[:, 2 * pt * GLA_DV:(2 * pt + 2) * GLA_DV].astype(BF16) for vd in (vf, vb)] for pt in tiles]
    intra = [[_dot(att[pt][(2 * hh + d) * n:(2 * hh + d + 1) * n], v2[pt][d][:, hh * GLA_DV:(hh + 1) * GLA_DV])
              for hh in range(2) for d in range(2)] for pt in tiles]
    kte_t = [kte[:, sl].T.astype(BF16) for sl in sls]
    la_t = [la[:, sl].T for sl in sls]
    st = [[ref[sl, :] for ref in (sf_ref, sb_ref)] for sl in sls]
    inter = [[[None] * n_ch for _ in range(2)] for _ in tiles]
    for step in range(n_ch):
        for pt in tiles:
            for d in range(2):
                c = step if d == 0 else n_ch - 1 - step
                r0, c0 = d * n + c * ch, d * n + c * ch
                qc = jnp.concatenate([qs[pt][r0:r0 + ch], qs[pt][2 * n + r0:2 * n + r0 + ch]], axis=0)
                inter[pt][d][c] = _dot(qc, st[pt][d].astype(BF16))
                upd = _dot(kte_t[pt][:, c0:c0 + ch], v2[pt][d][c * ch:(c + 1) * ch])
                dec = jnp.exp(jnp.sum(la_t[pt][:, c0:c0 + ch], axis=1, keepdims=True))
                st[pt][d] = st[pt][d] * dec + jnp.where(row_lo, upd[:, :GLA_DV], upd[:, GLA_DV:])
    outs = [[None] * GLA_HEADS for _ in range(2)]
    for pt in tiles:
        sf_ref[sls[pt], :] = st[pt][0]
        sb_ref[sls[pt], :] = st[pt][1]
        for hh in range(2):
            rs = slice(hh * HEAD, (hh + 1) * HEAD)
            for d in range(2):
                outs[d][2 * pt + hh] = intra[pt][2 * hh + d] + jnp.concatenate(
                    [inter[pt][d][c][rs] for c in range(n_ch)], axis=0)
    return jnp.concatenate(outs[0], axis=1), jnp.concatenate(outs[1], axis=1)


def _gla_kernel(qc_ref, kc_ref, vc_ref, ac_ref, qf_ref, kf_ref, vf_ref, af_ref,
                qb_ref, kb_ref, vb_ref, ab_ref, w2_ref, b2_ref,
                ocf_ref, ocb_ref, of_ref, ob_ref, sf_ref, sb_ref):
    s = pl.program_id(1)
    n, ch = GLA_BLOCK, GLA_CHUNK
    ri = lax.broadcasted_iota(jnp.int32, (n, n), 0)
    ci = lax.broadcasted_iota(jnp.int32, (n, n), 1)
    same = (ri // ch) == (ci // ch)
    low, up = same & (ri >= ci), same & (ri <= ci)
    sums_f, sums_b = low.astype(BF16), up.astype(BF16)
    cmask = jnp.concatenate([low, up, low, up], axis=0)
    half = lax.broadcasted_iota(jnp.int32, (n, LANES), 1) // HEAD
    row_lo = lax.broadcasted_iota(jnp.int32, (LANES, GLA_DV), 0) < HEAD

    @pl.when(s == 0)
    def _():
        sf_ref[...] = jnp.zeros_like(sf_ref)
        sb_ref[...] = jnp.zeros_like(sb_ref)

    def run(fwd, bwd, o_f, o_b):
        o_f[0], o_b[0] = _gla_block(tuple(r[0] for r in fwd), tuple(r[0] for r in bwd), w2_ref, b2_ref,
                                    sf_ref, sb_ref, sums_f, sums_b, cmask, half, row_lo)

    ctx = (qc_ref, kc_ref, vc_ref, ac_ref)

    @pl.when(s == 0)
    def _():
        run(ctx, ctx, ocf_ref, ocb_ref)

    @pl.when(s > 0)
    def _():
        run((qf_ref, kf_ref, vf_ref, af_ref), (qb_ref, kb_ref, vb_ref, ab_ref), of_ref, ob_ref)


def gla_scan(p3, pc3, w2, b2):
    B = p3.shape[0]
    nb = T // GLA_BLOCK
    cq, ck, cv, ca = COL["gla_q"] // 256, COL["gla_k"] // 256, COL["gla_v"] // 512, COL["gla_a"] // 128
    fwd = lambda s: jnp.maximum(s - 1, 0)
    bwd = lambda s: nb - jnp.maximum(s, 1)

    def spec(width, col, row):
        return pl.BlockSpec((1, GLA_BLOCK, width), lambda b, s: (b, row(s), col))

    zero = lambda s: 0
    in_specs = ([spec(256, cq, zero), spec(256, ck, zero), spec(512, cv, zero), spec(128, ca, zero)]
                + [spec(256, cq, fwd), spec(256, ck, fwd), spec(512, cv, fwd), spec(128, ca, fwd)]
                + [spec(256, cq, bwd), spec(256, ck, bwd), spec(512, cv, bwd), spec(128, ca, bwd)]
                + [pl.BlockSpec((128, 512), lambda b, s: (0, 0)), pl.BlockSpec((1, 512), lambda b, s: (0, 0))])
    out_specs = [spec(512, 0, zero), spec(512, 0, zero), spec(512, 0, fwd), spec(512, 0, bwd)]
    out_shape = [jax.ShapeDtypeStruct((B, L, 512), F32)] * 2 + [jax.ShapeDtypeStruct((B, T, 512), F32)] * 2
    return pl.pallas_call(
        _gla_kernel, grid=(B, nb + 1), in_specs=in_specs, out_specs=out_specs, out_shape=out_shape,
        scratch_shapes=[pltpu.VMEM((256, 128), F32), pltpu.VMEM((256, 128), F32)],
        compiler_params=_cp("parallel", "arbitrary"), name="gla_scan",
    )(pc3, pc3, pc3, pc3, p3, p3, p3, p3, p3, p3, p3, p3, w2, b2)


def _gla_gate(o, r, g):
    outs = []
    for hd in range(GLA_HEADS):
        sl = slice(hd * GLA_DV, (hd + 1) * GLA_DV)
        rh = r[:, sl]
        outs.append(_rms(o[:, sl], g) * (rh * _sigmoid(rh)))
    return jnp.concatenate(outs, axis=1)


def _stack_halves(qt, half):
    zero = jnp.zeros_like(qt)
    return jnp.concatenate([jnp.where(half == 0, qt, zero), jnp.where(half == 1, qt, zero)], axis=0).astype(BF16)


def _merge_halves(o, half):
    n = o.shape[0] // 2
    return jnp.where(half == 0, o[:n], o[n:])


def _softmax_parts(s_list, sink=None):
    m = s_list[0].max(axis=1, keepdims=True)
    for s in s_list[1:]:
        m = jnp.maximum(m, s.max(axis=1, keepdims=True))
    if sink is not None:
        m = jnp.maximum(m, sink)
    es = [jnp.exp(s - m) for s in s_list]
    den = es[0].sum(axis=1, keepdims=True)
    for e in es[1:]:
        den = den + e.sum(axis=1, keepdims=True)
    if sink is not None:
        den = den + jnp.exp(sink - m)
    return [e.astype(BF16) for e in es], 1.0 / den


def _rope(x, cos, sin, lane):
    partner = jnp.where((lane % HEAD) < HEAD // 2, pltpu.roll(x, LANES - HEAD // 2, 1),
                        pltpu.roll(x, HEAD // 2, 1))
    return x * cos + partner * sin


def _swa_kernel(sink_ref, mask_ref, q_ref, kp_ref, kc_ref, kn_ref, vp_ref, vc_ref, vn_ref, kx_ref, vx_ref,
                cos_ref, sin_ref, o_ref):
    n = pl.program_id(1)
    nb = pl.num_programs(1)
    blk = SWA_BLOCK
    lane = lax.broadcasted_iota(jnp.int32, (blk, LANES), 1)
    half = lane // HEAD

    def tab(ref, i):
        return ref[pl.ds(pl.multiple_of(i * blk, blk), blk), :]

    ip, inx = jnp.maximum(n - 1, 0), jnp.minimum(n + 1, nb - 1)
    kb = jnp.concatenate([_rope(kp_ref[0].astype(F32), tab(cos_ref, ip), tab(sin_ref, ip), lane),
                          _rope(kc_ref[0].astype(F32), tab(cos_ref, n), tab(sin_ref, n), lane),
                          _rope(kn_ref[0].astype(F32), tab(cos_ref, inx), tab(sin_ref, inx), lane)],
                         axis=0).astype(BF16)
    vb = jnp.concatenate([vp_ref[0], vc_ref[0], vn_ref[0]], axis=0)
    cos_q, sin_q = tab(cos_ref, n), tab(sin_ref, n)
    qs = jnp.concatenate([_stack_halves(_rope(q_ref[0, :, p * LANES:(p + 1) * LANES].astype(F32), cos_q, sin_q, lane)
                                        * HEAD ** -0.5, half) for p in range(4)], axis=0)
    grp = lax.broadcasted_iota(jnp.int32, (8 * blk, 1), 0) // blk
    sink = jnp.zeros((8 * blk, 1), F32)
    for g in range(8):
        sink = jnp.where(grp == g, sink_ref[g // 2 + 4 * (g % 2)], sink)
    n_sub = 2
    sub = 8 * blk // n_sub

    def scores(r):
        rs = slice(r * sub, (r + 1) * sub)
        s_loc = _dot_nt(qs[rs], kb) + mask_ref[rs, :]
        s_loc = jnp.concatenate([s_loc[:, :blk] + jnp.where(n == 0, NEG, 0.0), s_loc[:, blk:2 * blk],
                                 s_loc[:, 2 * blk:] + jnp.where(n == nb - 1, NEG, 0.0)], axis=1)
        return s_loc, _dot_nt(qs[rs], kx_ref[0])

    def finish(r, s_loc, s_ctx):
        (e_loc, e_ctx), inv = _softmax_parts([s_loc, s_ctx], sink=sink[r * sub:(r + 1) * sub])
        o = (_dot(e_loc, vb) + _dot(e_ctx, vx_ref[0])) * inv
        for t in range(sub // (2 * blk)):
            p = r * (sub // (2 * blk)) + t
            o_ref[0, :, p * LANES:(p + 1) * LANES] = _merge_halves(o[2 * t * blk:2 * (t + 1) * blk], half).astype(BF16)

    sc = [scores(r) for r in range(n_sub)]
    for r in range(n_sub):
        finish(r, *sc[r])


def swa(p3, pc3, sink, cos_t, sin_t):
    B = p3.shape[0]
    nb = T // SWA_BLOCK
    qi = np.arange(SWA_BLOCK)[:, None]
    kj = np.arange(3 * SWA_BLOCK)[None, :]
    mask = jnp.asarray(np.tile(np.where(np.abs(kj - SWA_BLOCK - qi) <= SWA_WINDOW, 0.0, NEG), (8, 1)).astype(np.float32))
    ck, cv = COL["swa_k"] // 128, COL["swa_v"] // 128
    prev = lambda n: jnp.maximum(n - 1, 0)
    cur = lambda n: n
    nxt = lambda n: jnp.minimum(n + 1, nb - 1)
    kv = lambda col, row: pl.BlockSpec((1, SWA_BLOCK, 128), lambda b, n: (b, row(n), col))
    full = lambda col: pl.BlockSpec((1, L, 128), lambda b, n: (b, 0, col))
    tab = pl.BlockSpec((T, 128), lambda b, n: (0, 0))
    return pl.pallas_call(
        _swa_kernel, grid=(B, nb),
        in_specs=[pl.BlockSpec(memory_space=pltpu.SMEM),
                  pl.BlockSpec((8 * SWA_BLOCK, 3 * SWA_BLOCK), lambda b, n: (0, 0)),
                  pl.BlockSpec((1, SWA_BLOCK, 512), lambda b, n: (b, n, COL["swa_q"] // 512)),
                  kv(ck, prev), kv(ck, cur), kv(ck, nxt), kv(cv, prev), kv(cv, cur), kv(cv, nxt),
                  full(ck), full(cv), tab, tab],
        out_specs=pl.BlockSpec((1, SWA_BLOCK, 512), lambda b, n: (b, n, 0)),
        out_shape=jax.ShapeDtypeStruct((B, T, 512), BF16),
        compiler_params=_cp("parallel", "parallel"), name="swa",
    )(sink, mask, p3, p3, p3, p3, p3, p3, p3, pc3, pc3, cos_t, sin_t)


def _ctx_attn_kernel(sink_ref, q_ref, k_ref, v_ref, o_ref, *, use_sink, kv_tiles):
    n = q_ref.shape[1]
    half = lax.broadcasted_iota(jnp.int32, (n, LANES), 1) // HEAD
    for p in range(4):
        sl = slice(p * LANES, (p + 1) * LANES)
        ks = sl if kv_tiles == 4 else slice(0, LANES)
        qs = _stack_halves(q_ref[0, :, sl] * HEAD ** -0.5, half)
        sink = None
        if use_sink:
            top = lax.broadcasted_iota(jnp.int32, (2 * n, 1), 0) < n
            sink = jnp.where(top, sink_ref[p], sink_ref[p + 4])
        (e,), inv = _softmax_parts([_dot_nt(qs, k_ref[0, :, ks])], sink=sink)
        o = _dot(e, v_ref[0, :, ks]) * inv
        o_ref[0, :, sl] = _merge_halves(o, half).astype(BF16)


def ctx_attn(pc3, sink, qcol, kcol, vcol, kv_tiles, use_sink):
    B, w = pc3.shape[0], kv_tiles * 128
    return pl.pallas_call(
        functools.partial(_ctx_attn_kernel, use_sink=use_sink, kv_tiles=kv_tiles), grid=(B,),
        in_specs=[pl.BlockSpec(memory_space=pltpu.SMEM),
                  pl.BlockSpec((1, L, 512), lambda b: (b, 0, qcol // 512)),
                  pl.BlockSpec((1, L, w), lambda b: (b, 0, kcol // w)),
                  pl.BlockSpec((1, L, w), lambda b: (b, 0, vcol // w))],
        out_specs=pl.BlockSpec((1, L, 512), lambda b: (b, 0, 0)),
        out_shape=jax.ShapeDtypeStruct((B, L, 512), BF16),
        compiler_params=_cp("parallel"), name="ctx_attn")(sink, pc3, pc3, pc3)


def _na_kernel(q_ref, k_ref, v_ref, kx_ref, vx_ref, *rest):
    bias_refs, o_ref = rest[:-1], rest[-1]
    nr = len(bias_refs)
    i = pl.program_id(1)
    band = NA_KR * GRID_W
    half = lax.broadcasted_iota(jnp.int32, (GRID_W, LANES), 1) // HEAD
    starts = [pl.multiple_of(jnp.clip(nr * i + k - NA_KR // 2, 0, ROWS - NA_KR) * GRID_W, GRID_W)
              for k in range(nr)]
    s_loc, s_ctx = [], []
    for p in range(4):
        sl = slice(p * LANES, (p + 1) * LANES)
        qs = [_stack_halves(q_ref[0, k * GRID_W:(k + 1) * GRID_W, sl] * HEAD ** -0.5, half) for k in range(nr)]
        for k in range(nr):
            kb = k_ref[0, pl.ds(starts[k], band), sl]
            s_loc.append(_dot_nt(qs[k], kb) + bias_refs[k][0, 2 * p * GRID_W:(2 * p + 2) * GRID_W, :])
        s_ctx.append(_dot_nt(jnp.concatenate(qs, axis=0), kx_ref[0, :, sl]))
    (e_loc, e_ctx), inv = _softmax_parts([jnp.concatenate(s_loc, axis=0), jnp.concatenate(s_ctx, axis=0)])
    blk = 2 * GRID_W
    for p in range(4):
        sl = slice(p * LANES, (p + 1) * LANES)
        r0 = p * nr * blk
        o_ctx = _dot(e_ctx[r0:r0 + nr * blk], vx_ref[0, :, sl])
        for k in range(nr):
            rs = slice(r0 + k * blk, r0 + (k + 1) * blk)
            vb = v_ref[0, pl.ds(starts[k], band), sl]
            o = (_dot(e_loc[rs], vb) + o_ctx[k * blk:(k + 1) * blk]) * inv[rs]
            o_ref[0, k * GRID_W:(k + 1) * GRID_W, sl] = _merge_halves(o, half).astype(BF16)


def na(p3, pc3, bias, rows_per_step=NA_ROWS_PER_STEP):
    B, nr = p3.shape[0], rows_per_step
    cq, ck, cv = COL["na_q"] // 512, COL["na_k"] // 512, COL["na_v"] // 512
    shift = lambda r: r - jnp.clip(r - NA_KR // 2, 0, ROWS - NA_KR)
    bias_spec = lambda k: pl.BlockSpec((1, NA_HEADS * GRID_W, NA_KR * GRID_W),
                                       lambda b, i: (shift(nr * i + k), 0, 0))
    return pl.pallas_call(
        _na_kernel, grid=(B, ROWS // nr),
        in_specs=[pl.BlockSpec((1, nr * GRID_W, 512), lambda b, i: (b, i, cq)),
                  pl.BlockSpec((1, T, 512), lambda b, i: (b, 0, ck)),
                  pl.BlockSpec((1, T, 512), lambda b, i: (b, 0, cv)),
                  pl.BlockSpec((1, L, 512), lambda b, i: (b, 0, ck)),
                  pl.BlockSpec((1, L, 512), lambda b, i: (b, 0, cv))] + [bias_spec(k) for k in range(nr)],
        out_specs=pl.BlockSpec((1, nr * GRID_W, 512), lambda b, i: (b, i, 0)),
        out_shape=jax.ShapeDtypeStruct((B, T, 512), BF16),
        compiler_params=_cp("parallel", "arbitrary"), name="na")(p3, p3, p3, pc3, pc3, *([bias] * nr))


def na_bias_table(rpb):
    col = np.arange(GRID_W)
    col_start = np.clip(col - NA_KC // 2, 0, GRID_W - NA_KC)
    col_ok = (col[None, :] >= col_start[:, None]) & (col[None, :] < col_start[:, None] + NA_KC)
    col_off = np.clip(col[None, :] - col[:, None], -(NA_KC - 1), NA_KC - 1) + (NA_KC - 1)
    onehot = (col_off[:, :, None] == np.arange(2 * NA_KC - 1)).astype(np.float32)
    toep = jnp.einsum("hro,qko->hrqk", rpb, jnp.asarray(onehot), precision=HIGHEST)
    toep = jnp.where(jnp.asarray(col_ok)[None, None], toep, NEG)
    per_shift = [toep[:, NA_KR - 1 - s:2 * NA_KR - 1 - s] for s in range(NA_KR)]
    tab = jnp.stack(per_shift).transpose(0, 1, 3, 2, 4)
    return tab.reshape(NA_KR, NA_HEADS * GRID_W, NA_KR * GRID_W).astype(F32)


def _gelu(x):
    return 0.5 * x * (1.0 + jnp.tanh(np.sqrt(2.0 / np.pi).astype(np.float32) * (x + 0.044715 * (x * x * x))))


def _sgu_kernel(u_ref, v_ref, g_ref, b_ref, ws_ref, bs_ref, y_ref):
    n_chunk = u_ref.shape[1] // SGU_CHUNK
    v = _gelu(v_ref[0].astype(F32))
    mu = jnp.mean(v, axis=-1, keepdims=True)
    var = jnp.mean(jnp.square(v - mu), axis=-1, keepdims=True)
    vn = ((v - mu) * lax.rsqrt(var + EPS) * g_ref[...] + b_ref[...]).astype(BF16)
    cols = []
    for g in range(4):
        gs = slice(g * 128, (g + 1) * 128)
        rhs = jnp.concatenate([vn[c * SGU_CHUNK:(c + 1) * SGU_CHUNK, gs] for c in range(n_chunk)], axis=1)
        mixed = _dot(ws_ref[g], rhs)
        cols.append(jnp.concatenate([mixed[:, c * 128:(c + 1) * 128] + bs_ref[g] for c in range(n_chunk)], axis=0))
    y_ref[0] = (_gelu(u_ref[0].astype(F32)) * jnp.concatenate(cols, axis=1)).astype(BF16)


def sgu(p3, g, b, ws, bs):
    B, n = p3.shape[0], p3.shape[1]
    tr = min(n, TOKEN_TILE)
    blk = lambda col: pl.BlockSpec((1, tr, 512), lambda bb, i: (bb, i, col))
    const = lambda shape: pl.BlockSpec(shape, lambda bb, i: (0,) * len(shape))
    return pl.pallas_call(
        _sgu_kernel, grid=(B, n // tr),
        in_specs=[blk(COL["sgu_u"] // 512), blk(COL["sgu_v"] // 512), const((1, 512)), const((1, 512)),
                  const((4, 128, 128)), const((4, 128, 128))],
        out_specs=blk(0), out_shape=jax.ShapeDtypeStruct((B, n, 512), BF16),
        compiler_params=_cp("parallel", "parallel"), name="sgu")(p3, p3, g, b, ws, bs)


def _merge_kernel(h_ref, of_ref, ob_ref, r_ref, gn_ref, b_ref, n_ref, d_ref, wg_ref, bg_ref, wb_ref, o_ref,
                  a_ref):
    @pl.when(pl.program_id(1) == 0)
    def _():
        a_ref[...] = _gla_gate(of_ref[...] + ob_ref[...], r_ref[...].astype(F32), gn_ref[...]).astype(BF16)

    h = h_ref[...]
    acc = None
    for i, br in enumerate((a_ref, b_ref, n_ref, d_ref)):
        term = _sigmoid(_dot(h, wg_ref[0, i].astype(BF16)) + bg_ref[i]) * _dot(br[...], wb_ref[i])
        acc = term if acc is None else acc + term
    o_ref[...] = acc.astype(BF16)


def merge(h, o_f, o_b, p, gn, branches, wg, l, bg, wb, tm):
    M, tn = h.shape[0], MERGE_COL_TILE
    tm = min(tm, M)
    br = pl.BlockSpec((tm, 512), lambda i, j: (i, 0))
    return pl.pallas_call(
        _merge_kernel, grid=(M // tm, D // tn),
        in_specs=[pl.BlockSpec((tm, D), lambda i, j: (i, 0)), br, br,
                  pl.BlockSpec((tm, 512), lambda i, j: (i, COL["gla_r"] // 512)),
                  pl.BlockSpec((1, GLA_DV), lambda i, j: (0, 0)), br, br, br,
                  pl.BlockSpec((1, 4, D, tn), lambda i, j: (l, 0, 0, j)),
                  pl.BlockSpec((4, 1, tn), lambda i, j: (0, 0, j)),
                  pl.BlockSpec((4, 512, tn), lambda i, j: (0, 0, j))],
        out_specs=pl.BlockSpec((tm, tn), lambda i, j: (i, j)),
        out_shape=jax.ShapeDtypeStruct((M, D), BF16),
        scratch_shapes=[pltpu.VMEM((tm, 512), BF16)],
        compiler_params=_cp("parallel", "arbitrary"), name="merge")(h, o_f, o_b, p, gn, *branches, wg, bg, wb)


def _outproj_kernel(acc_ref, w_ref, x_ref, gate_ref, g_ref, shift_ref, scale_ref, wr_ref,
                    x1_ref, h2_ref, lg_ref):
    n_sub = 2
    sub = acc_ref.shape[0] // n_sub

    def matmul(k):
        rs = slice(k * sub, (k + 1) * sub)
        return x_ref[rs, :] + gate_ref[0] * _dot(acc_ref[rs, :], w_ref[...])

    def epilogue(k, x1):
        rs = slice(k * sub, (k + 1) * sub)
        x1_ref[rs, :] = x1
        h2 = (_rms(x1, g_ref[...]) * (1.0 + scale_ref[0]) + shift_ref[0]).astype(BF16)
        lg_ref[rs, :] = _dot(h2, wr_ref[...])
        bits = pltpu.bitcast(h2.astype(F32), jnp.int32)
        h2_ref[rs, :] = lax.shift_right_logical(bits[:, :D // 2], 16) | (bits[:, D // 2:] & jnp.int32(-65536))

    pending = matmul(0)
    for k in range(n_sub):
        nxt = matmul(k + 1) if k + 1 < n_sub else None
        epilogue(k, pending)
        pending = nxt


def outproj(acc, w, x2, mod3, g, wr, tm, mod_row):
    M = acc.shape[0]
    tm = min(tm, M)
    row = lambda k: pl.BlockSpec((1, 1, D), lambda i: (mod_row(i) * 6 + k, 0, 0))
    tile = pl.BlockSpec((tm, D), lambda i: (i, 0))
    return pl.pallas_call(
        _outproj_kernel, grid=(M // tm,),
        in_specs=[tile, pl.BlockSpec((D, D), lambda i: (0, 0)), tile, row(2),
                  pl.BlockSpec((1, D), lambda i: (0, 0)), row(3), row(4),
                  pl.BlockSpec((D, 128), lambda i: (0, 0))],
        out_specs=[tile, pl.BlockSpec((tm, D // 2), lambda i: (i, 0)), pl.BlockSpec((tm, 128), lambda i: (i, 0))],
        out_shape=[jax.ShapeDtypeStruct((M, D), F32), jax.ShapeDtypeStruct((M, D // 2), jnp.int32),
                   jax.ShapeDtypeStruct((M, 128), F32)],
        compiler_params=_cp("parallel"), name="outproj")(acc, w, x2, mod3, g, mod3, mod3, wr)


def _select_kernel(lg_ref, sel_ref, aff_ref, slot_ref, *, cap):
    n = lg_ref.shape[1]
    nt = n // LANES
    lane = lax.broadcasted_iota(jnp.int32, (n, LANES), 1)
    lg = jnp.where(lane < N_EXPERTS, lg_ref[0], NEG)
    e = jnp.exp(lg - lg.max(axis=1, keepdims=True))
    aff = (e / e.sum(axis=1, keepdims=True)).T[:N_EXPERTS]
    aff_ref[0] = aff
    bits = pltpu.bitcast(aff, jnp.int32)

    def count(mask):
        return jnp.sum(jnp.where(mask, 1.0, 0.0), axis=1, keepdims=True)

    def bit_step(i, thr):
        cand = thr | lax.shift_left(jnp.int32(1), 30 - i)
        return jnp.where(count(bits >= cand) >= cap, cand, thr)

    thr = lax.fori_loop(0, 31, bit_step, jnp.zeros((N_EXPERTS, 1), jnp.int32))
    gt, eq = bits > thr, bits == thr

    r = lax.broadcasted_iota(jnp.int32, (LANES, LANES), 0)
    c = lax.broadcasted_iota(jnp.int32, (LANES, LANES), 1)
    upper = (r <= c).astype(BF16)
    r2 = lax.broadcasted_iota(jnp.int32, (nt * N_EXPERTS, nt * N_EXPERTS), 0)
    c2 = lax.broadcasted_iota(jnp.int32, (nt * N_EXPERTS, nt * N_EXPERTS), 1)
    before = ((r2 % N_EXPERTS == c2 % N_EXPERTS) & (c2 // N_EXPERTS < r2 // N_EXPERTS)).astype(BF16)

    def excl_prefix(x):
        xs = jnp.concatenate([x[:, i * LANES:(i + 1) * LANES] for i in range(nt)], axis=0).astype(BF16)
        inc = _dot(xs, upper)
        tot = jnp.broadcast_to(inc[:, LANES - 1:LANES], inc.shape).astype(BF16)
        full = inc + _dot(before, tot)
        return jnp.concatenate([full[i * N_EXPERTS:(i + 1) * N_EXPERTS] for i in range(nt)], axis=1) - x

    eq01 = jnp.where(eq, 1.0, 0.0)
    mask = gt | (eq & (excl_prefix(eq01) < cap - count(gt)))
    slot = jnp.where(mask, excl_prefix(jnp.where(mask, 1.0, 0.0)), -1.0)
    sel_ref[0] = slot.astype(jnp.int32)
    pad = jnp.full((LANES - N_EXPERTS, n), -1.0, F32)
    slot_ref[0] = jnp.concatenate([slot, pad], axis=0).T


def select(lg3, cap):
    B, n = lg3.shape[0], lg3.shape[1]
    tok = pl.BlockSpec((1, n, 128), lambda b: (b, 0, 0))
    row = pl.BlockSpec((1, N_EXPERTS, n), lambda b: (b, 0, 0))
    return pl.pallas_call(
        functools.partial(_select_kernel, cap=cap), grid=(B,),
        in_specs=[tok], out_specs=[row, row, tok],
        out_shape=[jax.ShapeDtypeStruct((B, N_EXPERTS, n), jnp.int32),
                   jax.ShapeDtypeStruct((B, N_EXPERTS, n), F32), jax.ShapeDtypeStruct((B, n, 128), F32)],
        compiler_params=_cp("parallel"), name="select")(lg3)


def _slot_tokens_kernel(sel_ref, aff_ref, idx_ref, g_ref, *, cap):
    n = sel_ref.shape[2]
    row = lax.broadcasted_iota(jnp.int32, (cap, n), 0)
    lane = lax.broadcasted_iota(jnp.int32, (LANES, n), 0)
    tok = lax.broadcasted_iota(jnp.int32, (LANES, n), 1)
    pos = jnp.where(lane == 0, tok // LANES, jnp.where(lane == 1, tok % LANES, 0)).astype(F32)
    for e in range(N_EXPERTS):
        onehot = jnp.where(row == sel_ref[0, e:e + 1, :], 1.0, 0.0).astype(BF16)
        a = aff_ref[0, e:e + 1, :]
        hi = a.astype(BF16)
        r1 = a - hi.astype(F32)
        mid = r1.astype(BF16)
        lo = (r1 - mid.astype(F32)).astype(BF16)
        cols = jnp.where(lane == 2, hi.astype(F32), jnp.where(lane == 3, mid.astype(F32),
                         jnp.where(lane == 4, lo.astype(F32), pos))).astype(BF16)
        r = _dot_nt(onehot, cols)
        idx_ref[e, 0] = (r[:, 0:1] * LANES + r[:, 1:2]).astype(jnp.int32) + pl.program_id(0) * n
        g_ref[e, 0] = r[:, 2:3] + r[:, 3:4] + r[:, 4:5]


def slot_tokens(sel, aff, cap):
    B, n = sel.shape[0], sel.shape[2]
    row = pl.BlockSpec((1, N_EXPERTS, n), lambda b: (b, 0, 0))
    col = pl.BlockSpec((N_EXPERTS, 1, cap, 1), lambda b: (0, b, 0, 0))
    idx, gate = pl.pallas_call(
        functools.partial(_slot_tokens_kernel, cap=cap), grid=(B,),
        in_specs=[row, row], out_specs=[col, col],
        out_shape=[jax.ShapeDtypeStruct((N_EXPERTS, B, cap, 1), jnp.int32),
                   jax.ShapeDtypeStruct((N_EXPERTS, B, cap, 1), F32)],
        compiler_params=_cp("parallel"), name="slot_tokens")(sel, aff)
    return idx.reshape(N_EXPERTS, B * cap, 1), gate.reshape(N_EXPERTS, B * cap, 1)


def _expert_kernel(idx_ref, h_hbm, gate_ref, wg_ref, wu_ref, wd_ref, o_ref, xg_ref, xb_ref, acc_ref, sem,
                   *, tm):
    e, i, f = pl.program_id(0), pl.program_id(1), pl.program_id(2)
    n_i, n_f = pl.num_programs(1), pl.num_programs(2)
    q = e * n_i + i
    n_q = pl.num_programs(0) * n_i
    slot = q % 2
    share = tm // n_f

    def row_copy(tile, r, dst_slot):
        src = h_hbm.at[pl.ds(idx_ref[tile * tm + r], 1), :]
        return pltpu.make_async_copy(src, xg_ref.at[dst_slot, pl.ds(r, 1), :], sem.at[dst_slot])

    def wait_tile(dst_slot):
        pltpu.make_async_copy(h_hbm.at[pl.ds(0, tm), :], xg_ref.at[dst_slot], sem.at[dst_slot]).wait()

    @pl.when((q == 0) & (f == 0))
    def _():
        def first(r, carry):
            row_copy(0, r, 0).start()
            return carry
        lax.fori_loop(0, tm, first, 0)

    @pl.when(f == 0)
    def _():
        wait_tile(slot)
        p = xg_ref[slot]
        xb_ref[:, :D // 2] = pltpu.bitcast(lax.shift_left(p, 16), F32).astype(BF16)
        xb_ref[:, D // 2:] = pltpu.bitcast(p & jnp.int32(-65536), F32).astype(BF16)
        acc_ref[...] = jnp.zeros_like(acc_ref)

    wg, wu, wd = wg_ref[0, 0].astype(BF16), wu_ref[0, 0].astype(BF16), wd_ref[0, 0].astype(BF16)
    n_sub = 2
    sub = tm // n_sub
    gu = [(_dot(xb_ref[r * sub:(r + 1) * sub, :], wg), _dot(xb_ref[r * sub:(r + 1) * sub, :], wu)) for r in range(n_sub)]
    nxt = jnp.minimum(q + 1, n_q - 1)
    for r in range(share):
        row_copy(nxt, f * share + r, 1 - slot).start()
    for r, (g, u) in enumerate(gu):
        hid = ((g * _sigmoid(g)) * u).astype(BF16)
        acc_ref[r * sub:(r + 1) * sub, :] += _dot(hid, wd)

    @pl.when(f == n_f - 1)
    def _():
        o_ref[0] = (acc_ref[...] * gate_ref[0]).astype(BF16)

    @pl.when((q == n_q - 1) & (f == n_f - 1))
    def _():
        wait_tile(1 - slot)


def experts(idx, h2p, gate, wg, wu, wd, l, tm):
    E, M, tf = gate.shape[0], gate.shape[1], EXPERT_HIDDEN_TILE
    F = wg.shape[3]
    tm = min(tm, M)
    grid_spec = pltpu.PrefetchScalarGridSpec(
        num_scalar_prefetch=1, grid=(E, M // tm, F // tf),
        in_specs=[pl.BlockSpec(memory_space=pl.ANY),
                  pl.BlockSpec((1, tm, 1), lambda e, i, f, idx: (e, i, 0)),
                  pl.BlockSpec((1, 1, D, tf), lambda e, i, f, idx: (l, e, 0, f)),
                  pl.BlockSpec((1, 1, D, tf), lambda e, i, f, idx: (l, e, 0, f)),
                  pl.BlockSpec((1, 1, tf, D), lambda e, i, f, idx: (l, e, f, 0))],
        out_specs=pl.BlockSpec((1, tm, D), lambda e, i, f, idx: (e, i, 0)),
        scratch_shapes=[pltpu.VMEM((2, tm, D // 2), jnp.int32), pltpu.VMEM((tm, D), BF16),
                        pltpu.VMEM((tm, D), F32), pltpu.SemaphoreType.DMA((2,))])
    return pl.pallas_call(
        functools.partial(_expert_kernel, tm=tm), grid_spec=grid_spec,
        out_shape=jax.ShapeDtypeStruct((E, M, D), BF16),
        compiler_params=_cp("arbitrary", "arbitrary", "arbitrary"), name="experts")(idx, h2p, gate, wg, wu, wd)


COMBINE_GROUP = 8


def _combine_kernel(slot_ref, ye_ref, x_ref, gate_ref, gfin_ref, o_ref, acc_ref, *, cap, final):
    g = pl.program_id(2)
    tt = x_ref.shape[1]
    lane = lax.broadcasted_iota(jnp.int32, (tt, LANES), 1)
    j = lax.broadcasted_iota(jnp.int32, (tt, cap), 1).astype(F32)
    slots = slot_ref[0]
    parts = []
    for k in range(COMBINE_GROUP):
        slot = jnp.sum(jnp.where(lane == g * COMBINE_GROUP + k, slots, 0.0), axis=1, keepdims=True)
        parts.append(jnp.where(j == slot, 1.0, 0.0).astype(BF16))
    onehot = jnp.concatenate(parts, axis=1)
    contrib = _dot(onehot, ye_ref[:, 0].reshape(COMBINE_GROUP * cap, D))

    @pl.when(g == 0)
    def _():
        acc_ref[...] = contrib

    @pl.when(g == pl.num_programs(2) - 1)
    def _():
        x2 = x_ref[0] + gate_ref[0] * (acc_ref[...] + contrib)
        if final:
            x2 = _rms(x2, gfin_ref[...])
        o_ref[0] = x2


def combine(slot_tok, ye, x3, mod3, gfin, cap, mod_row, final):
    B, n = x3.shape[0], x3.shape[1]
    tt = min(n, TOKEN_TILE)
    assert N_EXPERTS == 2 * COMBINE_GROUP
    xt = pl.BlockSpec((1, tt, D), lambda b, t, g: (b, t, 0))
    return pl.pallas_call(
        functools.partial(_combine_kernel, cap=cap, final=final), grid=(B, n // tt, N_EXPERTS // COMBINE_GROUP),
        in_specs=[pl.BlockSpec((1, tt, 128), lambda b, t, g: (b, t, 0)),
                  pl.BlockSpec((COMBINE_GROUP, 1, cap, D), lambda b, t, g: (g, b, 0, 0)), xt,
                  pl.BlockSpec((1, 1, D), lambda b, t, g: (mod_row(b) * 6 + 5, 0, 0)),
                  pl.BlockSpec((1, D), lambda b, t, g: (0, 0))],
        out_specs=xt, out_shape=jax.ShapeDtypeStruct((B, n, D), F32),
        scratch_shapes=[pltpu.VMEM((tt, D), F32)],
        compiler_params=_cp("parallel", "parallel", "arbitrary"), name="combine",
    )(slot_tok, ye.reshape(N_EXPERTS, B, cap, D), x3, mod3, gfin)


def _swa_q_perm():
    return np.concatenate([np.arange(HEAD) + (p + 4 * hh) * HEAD for p in range(4) for hh in range(2)])


def _prep_w_in(w):
    seg = lambda name, width: w[:, REF[name]:REF[name] + width]
    swa_q = seg("swa_q", 512)[:, _swa_q_perm()]
    parts = [seg("gla_v", 512), seg("gla_r", 512), swa_q, seg("na_q", 512), seg("na_k", 512), seg("na_v", 512),
             seg("sgu_u", 512), seg("sgu_v", 512), seg("gla_q", 256), seg("gla_k", 256), seg("swa_k", 128),
             seg("swa_v", 128), seg("gla_af", 16), seg("gla_ab", 16)]
    out = jnp.concatenate(parts, axis=1)
    return jnp.pad(out, ((0, 0), (0, NP - out.shape[1]))).astype(BF16)


def _rope_tables():
    quarter = HEAD // 4
    freqs = 10000.0 ** (-jnp.arange(quarter, dtype=F32) / quarter)
    t = jnp.arange(T)
    row = (t // GRID_W).astype(F32)
    col = (t % GRID_W).astype(F32)
    ang = jnp.concatenate([row[:, None] * freqs, col[:, None] * freqs], axis=-1)
    cos, sin = jnp.cos(ang), jnp.sin(ang)
    cos_t = jnp.tile(cos, (1, 4))
    sin_t = jnp.tile(jnp.concatenate([-sin, sin], axis=-1), (1, 2))
    return cos_t, sin_t


def _moe(h2, lg, x3, mod3, gfin, wg, wu, wd, l, cap, tm, mod_row, final):
    B, n = x3.shape[0], x3.shape[1]
    sel, aff, slot_tok = select(lg.reshape(B, n, 128), cap)
    idx, gate = slot_tokens(sel, aff, cap)
    ye = experts(idx.reshape(-1), h2, gate, wg, wu, wd, l, tm)
    return combine(slot_tok, ye, x3, mod3, gfin, cap, mod_row, final)


def kernel(x, c, ctx, c_ctx, w_ada, b_ada, norm1_g, norm2_g, w_in, gla_w_a2, gla_b_a2, gla_norm_g, swa_sink,
           na_rpb, sgu_ln_g, sgu_ln_b, sgu_w_s, sgu_b_s, w_gate, b_gate, w_branch, w_out, w_router,
           w_exp_gate, w_exp_up, w_exp_down, final_norm_g):
    B = x.shape[0]
    ctx_row = B
    n_mod = -(-(B + 1) // SUBLANES) * SUBLANES
    cvec = jnp.zeros((n_mod, D), F32).at[:B].set(c).at[ctx_row].set(c_ctx)
    cos_t, sin_t = _rope_tables()
    perm = _swa_q_perm()
    gfin = final_norm_g.reshape(1, D)
    tm = ROW_TILE
    ctx_tile_row = lambda i: ctx_row
    x2 = x.reshape(B * T, D)
    xc2 = ctx.reshape(B * L, D)
    for l in range(DEPTH):
        need_ctx = l < DEPTH - 1
        mod3 = ada(cvec, w_ada, b_ada.reshape(DEPTH, 1, 6 * D), l).reshape(n_mod * 6, 1, D)
        w_in_p = _prep_w_in(w_in[l])
        g1 = norm1_g[l].reshape(1, D)
        h, p = inproj(x2, mod3, g1, w_in_p, INPROJ_ROW_TILE, lambda i: i // (T // INPROJ_ROW_TILE))
        hc, pc = inproj(xc2, mod3, g1, w_in_p, INPROJ_ROW_TILE, ctx_tile_row)
        p3, pc3 = p.reshape(B, T, NP), pc.reshape(B, L, NP)

        w2 = jnp.zeros((128, 512), F32).at[:GLA_RANK, :256].set(gla_w_a2[l, 0])
        w2 = w2.at[GLA_RANK:2 * GLA_RANK, 256:].set(gla_w_a2[l, 1]).astype(BF16)
        b2 = gla_b_a2[l].reshape(1, 512)
        ocf, ocb, of, ob = gla_scan(p3, pc3, w2, b2)
        gn = gla_norm_g[l].reshape(1, GLA_DV)
        yb = swa(p3, pc3, swa_sink[l], cos_t, sin_t)
        yn = na(p3, pc3, na_bias_table(na_rpb[l]))
        ln_g, ln_b = sgu_ln_g[l].reshape(1, 512), sgu_ln_b[l].reshape(1, 512)
        ws = sgu_w_s[l].astype(BF16)
        bs = jnp.broadcast_to(sgu_b_s[l][:, :, None], (4, SGU_CHUNK, 128))
        yd = sgu(p3, ln_g, ln_b, ws, bs)

        bg = b_gate[l].reshape(4, 1, D)
        wb = w_branch[l].at[1].set(w_branch[l, 1][perm]).astype(BF16)
        wo = w_out[l].astype(BF16)
        wr = jnp.pad(w_router[l], ((0, 0), (0, 128 - N_EXPERTS))).astype(BF16)
        g2 = norm2_g[l].reshape(1, D)
        flat = lambda y: y.reshape(-1, 512)
        acc = merge(h, flat(of), flat(ob), p, gn, [flat(yb), flat(yn), flat(yd)], w_gate, l, bg, wb, tm)
        x1, h2, lg = outproj(acc, wo, x2, mod3, g2, wr, OUTPROJ_ROW_TILE, lambda i: i // (T // OUTPROJ_ROW_TILE))
        x2 = _moe(h2, lg, x1.reshape(B, T, D), mod3, gfin, w_exp_gate, w_exp_up, w_exp_down, l,
                  2 * T // N_EXPERTS, tm, lambda b: b, final=not need_ctx).reshape(B * T, D)
        if need_ctx:
            ybc = ctx_attn(pc3, swa_sink[l], COL["swa_q"], COL["swa_k"], COL["swa_v"], 1, True)
            ync = ctx_attn(pc3, swa_sink[l], COL["na_q"], COL["na_k"], COL["na_v"], 4, False)
            ydc = sgu(pc3, ln_g, ln_b, ws, bs)
            accc = merge(hc, flat(ocf), flat(ocb), pc, gn, [flat(ybc), flat(ync), flat(ydc)], w_gate, l, bg, wb, tm)
            xc1, hc2, lgc = outproj(accc, wo, xc2, mod3, g2, wr, OUTPROJ_ROW_TILE, ctx_tile_row)
            xc2 = _moe(hc2, lgc, xc1.reshape(B, L, D), mod3, gfin, w_exp_gate, w_exp_up, w_exp_down, l,
                       2 * L // N_EXPERTS, tm, lambda b: ctx_row, final=False).reshape(B * L, D)
    return x2.reshape(B, T, D)
```

```python
import functools

import numpy as np
import jax
import jax.numpy as jnp
from jax import lax
from jax.experimental import pallas as pl
from jax.experimental.pallas import tpu as pltpu

F32 = jnp.float32
BF16 = jnp.bfloat16
HIGHEST = lax.Precision.HIGHEST

D = 2048
T = 2048
L = 256
GRID_W = 64
ROWS = T // GRID_W
EPS = 1e-6
DEPTH = 2
NEG = -1e30

GLA_HEADS, GLA_DK, GLA_DV, GLA_RANK, GLA_TAU, GLA_CHUNK = 4, 64, 128, 16, 16.0, 64
GLA_BLOCK = 256
SWA_BLOCK, SWA_WINDOW = 128, 128
NA_KR, NA_KC, NA_HEADS = 8, 16, 8
SGU_CHUNK = 128
N_EXPERTS = 16
LANES = 128
HEAD = 64

NP = 5120
COL = dict(gla_v=0, gla_r=512, swa_q=1024, na_q=1536, na_k=2048, na_v=2560, sgu_u=3072, sgu_v=3584,
           gla_q=4096, gla_k=4352, swa_k=4608, swa_v=4736, gla_a=4864)
REF = dict(gla_q=0, gla_k=256, gla_v=512, gla_r=1024, gla_af=1536, gla_ab=1552, swa_q=1568, swa_k=2080,
           swa_v=2208, na_q=2336, na_k=2848, na_v=3360, sgu_u=3872, sgu_v=4384)

VMEM_LIMIT = 56 * 1024 * 1024
SUBLANES = 8

ROW_TILE = 1024
OUTPROJ_ROW_TILE = 512
ADA_COL_TILE = 1024
INPROJ_ROW_TILE = 512
MERGE_COL_TILE = 256
EXPERT_HIDDEN_TILE = 256
TOKEN_TILE = 512
NA_ROWS_PER_STEP = 8


def _cp(*sem, vmem=VMEM_LIMIT):
    return pltpu.CompilerParams(dimension_semantics=sem, vmem_limit_bytes=vmem)


def _sigmoid(x):
    return 1.0 / (1.0 + jnp.exp(-x))


def _rms(x, g):
    return x * lax.rsqrt(jnp.mean(x * x, axis=-1, keepdims=True) + EPS) * g


def _dot(a, b):
    return jnp.dot(a, b, preferred_element_type=F32)


def _dot_nt(a, b):
    return lax.dot_general(a, b, (((1,), (1,)), ((), ())), preferred_element_type=F32)


def _ada_kernel(c_ref, w_ref, b_ref, o_ref):
    c = c_ref[...]
    act = (c * _sigmoid(c)).astype(BF16)
    o_ref[...] = _dot(act, w_ref[0].astype(BF16)) + b_ref[0]


def ada(cvec, w, b, l):
    rows, tn = cvec.shape[0], ADA_COL_TILE
    return pl.pallas_call(
        _ada_kernel, grid=(6 * D // tn,),
        in_specs=[pl.BlockSpec((rows, D), lambda j: (0, 0)),
                  pl.BlockSpec((1, D, tn), lambda j: (l, 0, j)),
                  pl.BlockSpec((1, 1, tn), lambda j: (l, 0, j))],
        out_specs=pl.BlockSpec((rows, tn), lambda j: (0, j)),
        out_shape=jax.ShapeDtypeStruct((rows, 6 * D), F32),
        compiler_params=_cp("parallel"), name="ada")(cvec, w, b)


def _inproj_kernel(x_ref, shift_ref, scale_ref, g_ref, w_ref, h_ref, p_ref):
    n_sub = 2
    sub = x_ref.shape[0] // n_sub

    def modulate(k):
        rs = slice(k * sub, (k + 1) * sub)
        h = (_rms(x_ref[rs, :], g_ref[...]) * (1.0 + scale_ref[0]) + shift_ref[0]).astype(BF16)
        h_ref[rs, :] = h
        return h

    hs = [modulate(k) for k in range(n_sub)]
    for k in range(n_sub):
        p_ref[k * sub:(k + 1) * sub, :] = _dot(hs[k], w_ref[...]).astype(BF16)


def inproj(x2, mod3, g, w, tm, mod_row):
    M = x2.shape[0]
    tm = min(tm, M)
    return pl.pallas_call(
        _inproj_kernel, grid=(M // tm,),
        in_specs=[pl.BlockSpec((tm, D), lambda i: (i, 0)),
                  pl.BlockSpec((1, 1, D), lambda i: (mod_row(i) * 6 + 0, 0, 0)),
                  pl.BlockSpec((1, 1, D), lambda i: (mod_row(i) * 6 + 1, 0, 0)),
                  pl.BlockSpec((1, D), lambda i: (0, 0)),
                  pl.BlockSpec((D, NP), lambda i: (0, 0), pipeline_mode=pl.Buffered(1))],
        out_specs=[pl.BlockSpec((tm, D), lambda i: (i, 0)),
                   pl.BlockSpec((tm, NP), lambda i: (i, 0))],
        out_shape=[jax.ShapeDtypeStruct((M, D), BF16), jax.ShapeDtypeStruct((M, NP), BF16)],
        compiler_params=_cp("parallel"), name="inproj")(x2, mod3, mod3, g, w)


def _log_sigmoid(x):
    return jnp.minimum(x, 0.0) - jnp.log(1.0 + jnp.exp(-jnp.abs(x)))


def _split3(x):
    hi = x.astype(BF16)
    r1 = x - hi.astype(F32)
    mid = r1.astype(BF16)
    lo = (r1 - mid.astype(F32)).astype(BF16)
    return jnp.concatenate([hi, mid, lo], axis=1)


def _gla_block(fwd, bwd, w2_ref, b2_ref, sf_ref, sb_ref, sums_f, sums_b, cmask, half, row_lo):
    n, ch = GLA_BLOCK, GLA_CHUNK
    n_ch = n // ch
    (qf, kf, vf, af), (qb, kb, vb, ab) = fwd, bwd
    pre = jnp.concatenate([_dot(af.astype(BF16), w2_ref[:, :256]) + b2_ref[:, :256],
                           _dot(ab.astype(BF16), w2_ref[:, 256:]) + b2_ref[:, 256:]], axis=0)
    la = _log_sigmoid(pre) * (1.0 / GLA_TAU)
    la3 = _split3(la)
    cum = jnp.concatenate([_dot(sums_f, la3[:n]), _dot(sums_b, la3[n:])], axis=0)
    bc = cum[:, :256] + cum[:, 256:512] + cum[:, 512:]
    ends = [c * ch + ch - 1 for c in range(n_ch)] + [n + c * ch for c in range(n_ch)]
    bl = jnp.concatenate([jnp.broadcast_to(bc[r:r + 1], (ch, bc.shape[1])) for r in ends], axis=0)
    q = jnp.concatenate([qf, qb], axis=0).astype(F32)
    k = jnp.concatenate([kf, kb], axis=0).astype(F32)
    q_in = (q * GLA_DK ** -0.5) * jnp.exp(bc)
    k_in = (k * jnp.exp(-bc)).astype(BF16)
    kte = k * jnp.exp(bl - bc)
    half2 = jnp.concatenate([half, half], axis=0)
    tiles = range(2)
    sls = [slice(pt * LANES, (pt + 1) * LANES) for pt in tiles]
    qs = [jnp.concatenate([jnp.where(half2 == 0, q_in[:, sl], 0.0), jnp.where(half2 == 1, q_in[:, sl], 0.0)],
                          axis=0).astype(BF16) for sl in sls]
    att = [jnp.where(cmask, jnp.concatenate(
        [_dot_nt(qs[pt][0:n], k_in[:n, sls[pt]]), _dot_nt(qs[pt][n:2 * n], k_in[n:, sls[pt]]),
         _dot_nt(qs[pt][2 * n:3 * n], k_in[:n, sls[pt]]), _dot_nt(qs[pt][3 * n:], k_in[n:, sls[pt]])], axis=0),
        0.0).astype(BF16) for pt in tiles]
    v2 = [[vd[:, 2 * pt * GLA_DV:(2 * pt + 2) * GLA_DV].astype(BF16) for vd in (vf, vb)] for pt in tiles]
    intra = [[_dot(att[pt][(2 * hh + d) * n:(2 * hh + d + 1) * n], v2[pt][d][:, hh * GLA_DV:(hh + 1) * GLA_DV])
              for hh in range(2) for d in range(2)] for pt in tiles]
    kte_t = [kte[:, sl].T.astype(BF16) for sl in sls]
    la_t = [la[:, sl].T for sl in sls]
    st = [[ref[sl, :] for ref in (sf_ref, sb_ref)] for sl in sls]
    inter = [[[None] * n_ch for _ in range(2)] for _ in tiles]
    for step in range(n_ch):
        for pt in tiles:
            for d in range(2):
                c = step if d == 0 else n_ch - 1 - step
                r0, c0 = d * n + c * ch, d * n + c * ch
                qc = jnp.concatenate([qs[pt][r0:r0 + ch], qs[pt][2 * n + r0:2 * n + r0 + ch]], axis=0)
                inter[pt][d][c] = _dot(qc, st[pt][d].astype(BF16))
                upd = _dot(kte_t[pt][:, c0:c0 + ch], v2[pt][d][c * ch:(c + 1) * ch])
                dec = jnp.exp(jnp.sum(la_t[pt][:, c0:c0 + ch], axis=1, keepdims=True))
                st[pt][d] = st[pt][d] * dec + jnp.where(row_lo, upd[:, :GLA_DV], upd[:, GLA_DV:])
    outs = [[None] * GLA_HEADS for _ in range(2)]
    for pt in tiles:
        sf_ref[sls[pt], :] = st[pt][0]
        sb_ref[sls[pt], :] = st[pt][1]
        for hh in range(2):
            rs = slice(hh * HEAD, (hh + 1) * HEAD)
            for d in range(2):
                outs[d][2 * pt + hh] = intra[pt][2 * hh + d] + jnp.concatenate(
                    [inter[pt][d][c][rs] for c in range(n_ch)], axis=0)
    return jnp.concatenate(outs[0], axis=1), jnp.concatenate(outs[1], axis=1)


def _gla_kernel(qc_ref, kc_ref, vc_ref, ac_ref, qf_ref, kf_ref, vf_ref, af_ref,
                qb_ref, kb_ref, vb_ref, ab_ref, w2_ref, b2_ref,
                ocf_ref, ocb_ref, of_ref, ob_ref, sf_ref, sb_ref):
    s = pl.program_id(1)
    n, ch = GLA_BLOCK, GLA_CHUNK
    ri = lax.broadcasted_iota(jnp.int32, (n, n), 0)
    ci = lax.broadcasted_iota(jnp.int32, (n, n), 1)
    same = (ri // ch) == (ci // ch)
    low, up = same & (ri >= ci), same & (ri <= ci)
    sums_f, sums_b = low.astype(BF16), up.astype(BF16)
    cmask = jnp.concatenate([low, up, low, up], axis=0)
    half = lax.broadcasted_iota(jnp.int32, (n, LANES), 1) // HEAD
    row_lo = lax.broadcasted_iota(jnp.int32, (LANES, GLA_DV), 0) < HEAD

    @pl.when(s == 0)
    def _():
        sf_ref[...] = jnp.zeros_like(sf_ref)
        sb_ref[...] = jnp.zeros_like(sb_ref)

    def run(fwd, bwd, o_f, o_b):
        o_f[0], o_b[0] = _gla_block(tuple(r[0] for r in fwd), tuple(r[0] for r in bwd), w2_ref, b2_ref,
                                    sf_ref, sb_ref, sums_f, sums_b, cmask, half, row_lo)

    ctx = (qc_ref, kc_ref, vc_ref, ac_ref)

    @pl.when(s == 0)
    def _():
        run(ctx, ctx, ocf_ref, ocb_ref)

    @pl.when(s > 0)
    def _():
        run((qf_ref, kf_ref, vf_ref, af_ref), (qb_ref, kb_ref, vb_ref, ab_ref), of_ref, ob_ref)


def gla_scan(p3, pc3, w2, b2):
    B = p3.shape[0]
    nb = T // GLA_BLOCK
    cq, ck, cv, ca = COL["gla_q"] // 256, COL["gla_k"] // 256, COL["gla_v"] // 512, COL["gla_a"] // 128
    fwd = lambda s: jnp.maximum(s - 1, 0)
    bwd = lambda s: nb - jnp.maximum(s, 1)

    def spec(width, col, row):
        return pl.BlockSpec((1, GLA_BLOCK, width), lambda b, s: (b, row(s), col))

    zero = lambda s: 0
    in_specs = ([spec(256, cq, zero), spec(256, ck, zero), spec(512, cv, zero), spec(128, ca, zero)]
                + [spec(256, cq, fwd), spec(256, ck, fwd), spec(512, cv, fwd), spec(128, ca, fwd)]
                + [spec(256, cq, bwd), spec(256, ck, bwd), spec(512, cv, bwd), spec(128, ca, bwd)]
                + [pl.BlockSpec((128, 512), lambda b, s: (0, 0)), pl.BlockSpec((1, 512), lambda b, s: (0, 0))])
    out_specs = [spec(512, 0, zero), spec(512, 0, zero), spec(512, 0, fwd), spec(512, 0, bwd)]
    out_shape = [jax.ShapeDtypeStruct((B, L, 512), F32)] * 2 + [jax.ShapeDtypeStruct((B, T, 512), F32)] * 2
    return pl.pallas_call(
        _gla_kernel, grid=(B, nb + 1), in_specs=in_specs, out_specs=out_specs, out_shape=out_shape,
        scratch_shapes=[pltpu.VMEM((256, 128), F32), pltpu.VMEM((256, 128), F32)],
        compiler_params=_cp("parallel", "arbitrary"), name="gla_scan",
    )(pc3, pc3, pc3, pc3, p3, p3, p3, p3, p3, p3, p3, p3, w2, b2)


def _gla_gate(o, r, g):
    outs = []
    for hd in range(GLA_HEADS):
        sl = slice(hd * GLA_DV, (hd + 1) * GLA_DV)
        rh = r[:, sl]
        outs.append(_rms(o[:, sl], g) * (rh * _sigmoid(rh)))
    return jnp.concatenate(outs, axis=1)


def _stack_halves(qt, half):
    zero = jnp.zeros_like(qt)
    return jnp.concatenate([jnp.where(half == 0, qt, zero), jnp.where(half == 1, qt, zero)], axis=0).astype(BF16)


def _merge_halves(o, half):
    n = o.shape[0] // 2
    return jnp.where(half == 0, o[:n], o[n:])


def _softmax_parts(s_list, sink=None):
    m = s_list[0].max(axis=1, keepdims=True)
    for s in s_list[1:]:
        m = jnp.maximum(m, s.max(axis=1, keepdims=True))
    if sink is not None:
        m = jnp.maximum(m, sink)
    es = [jnp.exp(s - m) for s in s_list]
    den = es[0].sum(axis=1, keepdims=True)
    for e in es[1:]:
        den = den + e.sum(axis=1, keepdims=True)
    if sink is not None:
        den = den + jnp.exp(sink - m)
    return [e.astype(BF16) for e in es], 1.0 / den


def _rope(x, cos, sin, lane):
    partner = jnp.where((lane % HEAD) < HEAD // 2, pltpu.roll(x, LANES - HEAD // 2, 1),
                        pltpu.roll(x, HEAD // 2, 1))
    return x * cos + partner * sin


def _swa_kernel(sink_ref, mask_ref, q_ref, kp_ref, kc_ref, kn_ref, vp_ref, vc_ref, vn_ref, kx_ref, vx_ref,
                cos_ref, sin_ref, o_ref):
    n = pl.program_id(1)
    nb = pl.num_programs(1)
    blk = SWA_BLOCK
    lane = lax.broadcasted_iota(jnp.int32, (blk, LANES), 1)
    half = lane // HEAD

    def tab(ref, i):
        return ref[pl.ds(pl.multiple_of(i * blk, blk), blk), :]

    ip, inx = jnp.maximum(n - 1, 0), jnp.minimum(n + 1, nb - 1)
    kb = jnp.concatenate([_rope(kp_ref[0].astype(F32), tab(cos_ref, ip), tab(sin_ref, ip), lane),
                          _rope(kc_ref[0].astype(F32), tab(cos_ref, n), tab(sin_ref, n), lane),
                          _rope(kn_ref[0].astype(F32), tab(cos_ref, inx), tab(sin_ref, inx), lane)],
                         axis=0).astype(BF16)
    vb = jnp.concatenate([vp_ref[0], vc_ref[0], vn_ref[0]], axis=0)
    cos_q, sin_q = tab(cos_ref, n), tab(sin_ref, n)
    qs = jnp.concatenate([_stack_halves(_rope(q_ref[0, :, p * LANES:(p + 1) * LANES].astype(F32), cos_q, sin_q, lane)
                                        * HEAD ** -0.5, half) for p in range(4)], axis=0)
    grp = lax.broadcasted_iota(jnp.int32, (8 * blk, 1), 0) // blk
    sink = jnp.zeros((8 * blk, 1), F32)
    for g in range(8):
        sink = jnp.where(grp == g, sink_ref[g // 2 + 4 * (g % 2)], sink)
    n_sub = 2
    sub = 8 * blk // n_sub

    def scores(r):
        rs = slice(r * sub, (r + 1) * sub)
        s_loc = _dot_nt(qs[rs], kb) + mask_ref[rs, :]
        s_loc = jnp.concatenate([s_loc[:, :blk] + jnp.where(n == 0, NEG, 0.0), s_loc[:, blk:2 * blk],
                                 s_loc[:, 2 * blk:] + jnp.where(n == nb - 1, NEG, 0.0)], axis=1)
        return s_loc, _dot_nt(qs[rs], kx_ref[0])

    def finish(r, s_loc, s_ctx):
        (e_loc, e_ctx), inv = _softmax_parts([s_loc, s_ctx], sink=sink[r * sub:(r + 1) * sub])
        o = (_dot(e_loc, vb) + _dot(e_ctx, vx_ref[0])) * inv
        for t in range(sub // (2 * blk)):
            p = r * (sub // (2 * blk)) + t
            o_ref[0, :, p * LANES:(p + 1) * LANES] = _merge_halves(o[2 * t * blk:2 * (t + 1) * blk], half).astype(BF16)

    sc = [scores(r) for r in range(n_sub)]
    for r in range(n_sub):
        finish(r, *sc[r])


def swa(p3, pc3, sink, cos_t, sin_t):
    B = p3.shape[0]
    nb = T // SWA_BLOCK
    qi = np.arange(SWA_BLOCK)[:, None]
    kj = np.arange(3 * SWA_BLOCK)[None, :]
    mask = jnp.asarray(np.tile(np.where(np.abs(kj - SWA_BLOCK - qi) <= SWA_WINDOW, 0.0, NEG), (8, 1)).astype(np.float32))
    ck, cv = COL["swa_k"] // 128, COL["swa_v"] // 128
    prev = lambda n: jnp.maximum(n - 1, 0)
    cur = lambda n: n
    nxt = lambda n: jnp.minimum(n + 1, nb - 1)
    kv = lambda col, row: pl.BlockSpec((1, SWA_BLOCK, 128), lambda b, n: (b, row(n), col))
    full = lambda col: pl.BlockSpec((1, L, 128), lambda b, n: (b, 0, col))
    tab = pl.BlockSpec((T, 128), lambda b, n: (0, 0))
    return pl.pallas_call(
        _swa_kernel, grid=(B, nb),
        in_specs=[pl.BlockSpec(memory_space=pltpu.SMEM),
                  pl.BlockSpec((8 * SWA_BLOCK, 3 * SWA_BLOCK), lambda b, n: (0, 0)),
                  pl.BlockSpec((1, SWA_BLOCK, 512), lambda b, n: (b, n, COL["swa_q"] // 512)),
                  kv(ck, prev), kv(ck, cur), kv(ck, nxt), kv(cv, prev), kv(cv, cur), kv(cv, nxt),
                  full(ck), full(cv), tab, tab],
        out_specs=pl.BlockSpec((1, SWA_BLOCK, 512), lambda b, n: (b, n, 0)),
        out_shape=jax.ShapeDtypeStruct((B, T, 512), BF16),
        compiler_params=_cp("parallel", "parallel"), name="swa",
    )(sink, mask, p3, p3, p3, p3, p3, p3, p3, pc3, pc3, cos_t, sin_t)


def _ctx_attn_kernel(sink_ref, q_ref, k_ref, v_ref, o_ref, *, use_sink, kv_tiles):
    n = q_ref.shape[1]
    half = lax.broadcasted_iota(jnp.int32, (n, LANES), 1) // HEAD
    for p in range(4):
        sl = slice(p * LANES, (p + 1) * LANES)
        ks = sl if kv_tiles == 4 else slice(0, LANES)
        qs = _stack_halves(q_ref[0, :, sl] * HEAD ** -0.5, half)
        sink = None
        if use_sink:
            top = lax.broadcasted_iota(jnp.int32, (2 * n, 1), 0) < n
            sink = jnp.where(top, sink_ref[p], sink_ref[p + 4])
        (e,), inv = _softmax_parts([_dot_nt(qs, k_ref[0, :, ks])], sink=sink)
        o = _dot(e, v_ref[0, :, ks]) * inv
        o_ref[0, :, sl] = _merge_halves(o, half).astype(BF16)


def ctx_attn(pc3, sink, qcol, kcol, vcol, kv_tiles, use_sink):
    B, w = pc3.shape[0], kv_tiles * 128
    return pl.pallas_call(
        functools.partial(_ctx_attn_kernel, use_sink=use_sink, kv_tiles=kv_tiles), grid=(B,),
        in_specs=[pl.BlockSpec(memory_space=pltpu.SMEM),
                  pl.BlockSpec((1, L, 512), lambda b: (b, 0, qcol // 512)),
                  pl.BlockSpec((1, L, w), lambda b: (b, 0, kcol // w)),
                  pl.BlockSpec((1, L, w), lambda b: (b, 0, vcol // w))],
        out_specs=pl.BlockSpec((1, L, 512), lambda b: (b, 0, 0)),
        out_shape=jax.ShapeDtypeStruct((B, L, 512), BF16),
        compiler_params=_cp("parallel"), name="ctx_attn")(sink, pc3, pc3, pc3)


def _na_kernel(q_ref, k_ref, v_ref, kx_ref, vx_ref, *rest):
    bias_refs, o_ref = rest[:-1], rest[-1]
    nr = len(bias_refs)
    i = pl.program_id(1)
    band = NA_KR * GRID_W
    half = lax.broadcasted_iota(jnp.int32, (GRID_W, LANES), 1) // HEAD
    starts = [pl.multiple_of(jnp.clip(nr * i + k - NA_KR // 2, 0, ROWS - NA_KR) * GRID_W, GRID_W)
              for k in range(nr)]
    s_loc, s_ctx = [], []
    for p in range(4):
        sl = slice(p * LANES, (p + 1) * LANES)
        qs = [_stack_halves(q_ref[0, k * GRID_W:(k + 1) * GRID_W, sl] * HEAD ** -0.5, half) for k in range(nr)]
        for k in range(nr):
            kb = k_ref[0, pl.ds(starts[k], band), sl]
            s_loc.append(_dot_nt(qs[k], kb) + bias_refs[k][0, 2 * p * GRID_W:(2 * p + 2) * GRID_W, :])
        s_ctx.append(_dot_nt(jnp.concatenate(qs, axis=0), kx_ref[0, :, sl]))
    (e_loc, e_ctx), inv = _softmax_parts([jnp.concatenate(s_loc, axis=0), jnp.concatenate(s_ctx, axis=0)])
    blk = 2 * GRID_W
    for p in range(4):
        sl = slice(p * LANES, (p + 1) * LANES)
        r0 = p * nr * blk
        o_ctx = _dot(e_ctx[r0:r0 + nr * blk], vx_ref[0, :, sl])
        for k in range(nr):
            rs = slice(r0 + k * blk, r0 + (k + 1) * blk)
            vb = v_ref[0, pl.ds(starts[k], band), sl]
            o = (_dot(e_loc[rs], vb) + o_ctx[k * blk:(k + 1) * blk]) * inv[rs]
            o_ref[0, k * GRID_W:(k + 1) * GRID_W, sl] = _merge_halves(o, half).astype(BF16)


def na(p3, pc3, bias, rows_per_step=NA_ROWS_PER_STEP):
    B, nr = p3.shape[0], rows_per_step
    cq, ck, cv = COL["na_q"] // 512, COL["na_k"] // 512, COL["na_v"] // 512
    shift = lambda r: r - jnp.clip(r - NA_KR // 2, 0, ROWS - NA_KR)
    bias_spec = lambda k: pl.BlockSpec((1, NA_HEADS * GRID_W, NA_KR * GRID_W),
                                       lambda b, i: (shift(nr * i + k), 0, 0))
    return pl.pallas_call(
        _na_kernel, grid=(B, ROWS // nr),
        in_specs=[pl.BlockSpec((1, nr * GRID_W, 512), lambda b, i: (b, i, cq)),
                  pl.BlockSpec((1, T, 512), lambda b, i: (b, 0, ck)),
                  pl.BlockSpec((1, T, 512), lambda b, i: (b, 0, cv)),
                  pl.BlockSpec((1, L, 512), lambda b, i: (b, 0, ck)),
                  pl.BlockSpec((1, L, 512), lambda b, i: (b, 0, cv))] + [bias_spec(k) for k in range(nr)],
        out_specs=pl.BlockSpec((1, nr * GRID_W, 512), lambda b, i: (b, i, 0)),
        out_shape=jax.ShapeDtypeStruct((B, T, 512), BF16),
        compiler_params=_cp("parallel", "arbitrary"), name="na")(p3, p3, p3, pc3, pc3, *([bias] * nr))


def na_bias_table(rpb):
    col = np.arange(GRID_W)
    col_start = np.clip(col - NA_KC // 2, 0, GRID_W - NA_KC)
    col_ok = (col[None, :] >= col_start[:, None]) & (col[None, :] < col_start[:, None] + NA_KC)
    col_off = np.clip(col[None, :] - col[:, None], -(NA_KC - 1), NA_KC - 1) + (NA_KC - 1)
    onehot = (col_off[:, :, None] == np.arange(2 * NA_KC - 1)).astype(np.float32)
    toep = jnp.einsum("hro,qko->hrqk", rpb, jnp.asarray(onehot), precision=HIGHEST)
    toep = jnp.where(jnp.asarray(col_ok)[None, None], toep, NEG)
    per_shift = [toep[:, NA_KR - 1 - s:2 * NA_KR - 1 - s] for s in range(NA_KR)]
    tab = jnp.stack(per_shift).transpose(0, 1, 3, 2, 4)
    return tab.reshape(NA_KR, NA_HEADS * GRID_W, NA_KR * GRID_W).astype(F32)


def _gelu(x):
    return 0.5 * x * (1.0 + jnp.tanh(np.sqrt(2.0 / np.pi).astype(np.float32) * (x + 0.044715 * (x * x * x))))


def _sgu_kernel(u_ref, v_ref, g_ref, b_ref, ws_ref, bs_ref, y_ref):
    n_chunk = u_ref.shape[1] // SGU_CHUNK
    v = _gelu(v_ref[0].astype(F32))
    mu = jnp.mean(v, axis=-1, keepdims=True)
    var = jnp.mean(jnp.square(v - mu), axis=-1, keepdims=True)
    vn = ((v - mu) * lax.rsqrt(var + EPS) * g_ref[...] + b_ref[...]).astype(BF16)
    cols = []
    for g in range(4):
        gs = slice(g * 128, (g + 1) * 128)
        rhs = jnp.concatenate([vn[c * SGU_CHUNK:(c + 1) * SGU_CHUNK, gs] for c in range(n_chunk)], axis=1)
        mixed = _dot(ws_ref[g], rhs)
        cols.append(jnp.concatenate([mixed[:, c * 128:(c + 1) * 128] + bs_ref[g] for c in range(n_chunk)], axis=0))
    y_ref[0] = (_gelu(u_ref[0].astype(F32)) * jnp.concatenate(cols, axis=1)).astype(BF16)


def sgu(p3, g, b, ws, bs):
    B, n = p3.shape[0], p3.shape[1]
    tr = min(n, TOKEN_TILE)
    blk = lambda col: pl.BlockSpec((1, tr, 512), lambda bb, i: (bb, i, col))
    const = lambda shape: pl.BlockSpec(shape, lambda bb, i: (0,) * len(shape))
    return pl.pallas_call(
        _sgu_kernel, grid=(B, n // tr),
        in_specs=[blk(COL["sgu_u"] // 512), blk(COL["sgu_v"] // 512), const((1, 512)), const((1, 512)),
                  const((4, 128, 128)), const((4, 128, 128))],
        out_specs=blk(0), out_shape=jax.ShapeDtypeStruct((B, n, 512), BF16),
        compiler_params=_cp("parallel", "parallel"), name="sgu")(p3, p3, g, b, ws, bs)


def _merge_kernel(h_ref, of_ref, ob_ref, r_ref, gn_ref, b_ref, n_ref, d_ref, wg_ref, bg_ref, wb_ref, o_ref,
                  a_ref):
    @pl.when(pl.program_id(1) == 0)
    def _():
        a_ref[...] = _gla_gate(of_ref[...] + ob_ref[...], r_ref[...].astype(F32), gn_ref[...]).astype(BF16)

    h = h_ref[...]
    acc = None
    for i, br in enumerate((a_ref, b_ref, n_ref, d_ref)):
        term = _sigmoid(_dot(h, wg_ref[0, i].astype(BF16)) + bg_ref[i]) * _dot(br[...], wb_ref[i])
        acc = term if acc is None else acc + term
    o_ref[...] = acc.astype(BF16)


def merge(h, o_f, o_b, p, gn, branches, wg, l, bg, wb, tm):
    M, tn = h.shape[0], MERGE_COL_TILE
    tm = min(tm, M)
    br = pl.BlockSpec((tm, 512), lambda i, j: (i, 0))
    return pl.pallas_call(
        _merge_kernel, grid=(M // tm, D // tn),
        in_specs=[pl.BlockSpec((tm, D), lambda i, j: (i, 0)), br, br,
                  pl.BlockSpec((tm, 512), lambda i, j: (i, COL["gla_r"] // 512)),
                  pl.BlockSpec((1, GLA_DV), lambda i, j: (0, 0)), br, br, br,
                  pl.BlockSpec((1, 4, D, tn), lambda i, j: (l, 0, 0, j)),
                  pl.BlockSpec((4, 1, tn), lambda i, j: (0, 0, j)),
                  pl.BlockSpec((4, 512, tn), lambda i, j: (0, 0, j))],
        out_specs=pl.BlockSpec((tm, tn), lambda i, j: (i, j)),
        out_shape=jax.ShapeDtypeStruct((M, D), BF16),
        scratch_shapes=[pltpu.VMEM((tm, 512), BF16)],
        compiler_params=_cp("parallel", "arbitrary"), name="merge")(h, o_f, o_b, p, gn, *branches, wg, bg, wb)


def _outproj_kernel(acc_ref, w_ref, x_ref, gate_ref, g_ref, shift_ref, scale_ref, wr_ref,
                    x1_ref, h2_ref, lg_ref):
    n_sub = 2
    sub = acc_ref.shape[0] // n_sub

    def matmul(k):
        rs = slice(k * sub, (k + 1) * sub)
        return x_ref[rs, :] + gate_ref[0] * _dot(acc_ref[rs, :], w_ref[...])

    def epilogue(k, x1):
        rs = slice(k * sub, (k + 1) * sub)
        x1_ref[rs, :] = x1
        h2 = (_rms(x1, g_ref[...]) * (1.0 + scale_ref[0]) + shift_ref[0]).astype(BF16)
        lg_ref[rs, :] = _dot(h2, wr_ref[...])
        bits = pltpu.bitcast(h2.astype(F32), jnp.int32)
        h2_ref[rs, :] = lax.shift_right_logical(bits[:, :D // 2], 16) | (bits[:, D // 2:] & jnp.int32(-65536))

    pending = matmul(0)
    for k in range(n_sub):
        nxt = matmul(k + 1) if k + 1 < n_sub else None
        epilogue(k, pending)
        pending = nxt


def outproj(acc, w, x2, mod3, g, wr, tm, mod_row):
    M = acc.shape[0]
    tm = min(tm, M)
    row = lambda k: pl.BlockSpec((1, 1, D), lambda i: (mod_row(i) * 6 + k, 0, 0))
    tile = pl.BlockSpec((tm, D), lambda i: (i, 0))
    return pl.pallas_call(
        _outproj_kernel, grid=(M // tm,),
        in_specs=[tile, pl.BlockSpec((D, D), lambda i: (0, 0)), tile, row(2),
                  pl.BlockSpec((1, D), lambda i: (0, 0)), row(3), row(4),
                  pl.BlockSpec((D, 128), lambda i: (0, 0))],
        out_specs=[tile, pl.BlockSpec((tm, D // 2), lambda i: (i, 0)), pl.BlockSpec((tm, 128), lambda i: (i, 0))],
        out_shape=[jax.ShapeDtypeStruct((M, D), F32), jax.ShapeDtypeStruct((M, D // 2), jnp.int32),
                   jax.ShapeDtypeStruct((M, 128), F32)],
        compiler_params=_cp("parallel"), name="outproj")(acc, w, x2, mod3, g, mod3, mod3, wr)


def _select_kernel(lg_ref, sel_ref, aff_ref, slot_ref, *, cap):
    n = lg_ref.shape[1]
    nt = n // LANES
    lane = lax.broadcasted_iota(jnp.int32, (n, LANES), 1)
    lg = jnp.where(lane < N_EXPERTS, lg_ref[0], NEG)
    e = jnp.exp(lg - lg.max(axis=1, keepdims=True))
    aff = (e / e.sum(axis=1, keepdims=True)).T[:N_EXPERTS]
    aff_ref[0] = aff
    bits = pltpu.bitcast(aff, jnp.int32)

    def count(mask):
        return jnp.sum(jnp.where(mask, 1.0, 0.0), axis=1, keepdims=True)

    def bit_step(i, thr):
        cand = thr | lax.shift_left(jnp.int32(1), 30 - i)
        return jnp.where(count(bits >= cand) >= cap, cand, thr)

    thr = lax.fori_loop(0, 31, bit_step, jnp.zeros((N_EXPERTS, 1), jnp.int32))
    gt, eq = bits > thr, bits == thr

    r = lax.broadcasted_iota(jnp.int32, (LANES, LANES), 0)
    c = lax.broadcasted_iota(jnp.int32, (LANES, LANES), 1)
    upper = (r <= c).astype(BF16)
    r2 = lax.broadcasted_iota(jnp.int32, (nt * N_EXPERTS, nt * N_EXPERTS), 0)
    c2 = lax.broadcasted_iota(jnp.int32, (nt * N_EXPERTS, nt * N_EXPERTS), 1)
    before = ((r2 % N_EXPERTS == c2 % N_EXPERTS) & (c2 // N_EXPERTS < r2 // N_EXPERTS)).astype(BF16)

    def excl_prefix(x):
        xs = jnp.concatenate([x[:, i * LANES:(i + 1) * LANES] for i in range(nt)], axis=0).astype(BF16)
        inc = _dot(xs, upper)
        tot = jnp.broadcast_to(inc[:, LANES - 1:LANES], inc.shape).astype(BF16)
        full = inc + _dot(before, tot)
        return jnp.concatenate([full[i * N_EXPERTS:(i + 1) * N_EXPERTS] for i in range(nt)], axis=1) - x

    eq01 = jnp.where(eq, 1.0, 0.0)
    mask = gt | (eq & (excl_prefix(eq01) < cap - count(gt)))
    slot = jnp.where(mask, excl_prefix(jnp.where(mask, 1.0, 0.0)), -1.0)
    sel_ref[0] = slot.astype(jnp.int32)
    pad = jnp.full((LANES - N_EXPERTS, n), -1.0, F32)
    slot_ref[0] = jnp.concatenate([slot, pad], axis=0).T


def select(lg3, cap):
    B, n = lg3.shape[0], lg3.shape[1]
    tok = pl.BlockSpec((1, n, 128), lambda b: (b, 0, 0))
    row = pl.BlockSpec((1, N_EXPERTS, n), lambda b: (b, 0, 0))
    return pl.pallas_call(
        functools.partial(_select_kernel, cap=cap), grid=(B,),
        in_specs=[tok], out_specs=[row, row, tok],
        out_shape=[jax.ShapeDtypeStruct((B, N_EXPERTS, n), jnp.int32),
                   jax.ShapeDtypeStruct((B, N_EXPERTS, n), F32), jax.ShapeDtypeStruct((B, n, 128), F32)],
        compiler_params=_cp("parallel"), name="select")(lg3)


def _slot_tokens_kernel(sel_ref, aff_ref, idx_ref, g_ref, *, cap):
    n = sel_ref.shape[2]
    row = lax.broadcasted_iota(jnp.int32, (cap, n), 0)
    lane = lax.broadcasted_iota(jnp.int32, (LANES, n), 0)
    tok = lax.broadcasted_iota(jnp.int32, (LANES, n), 1)
    pos = jnp.where(lane == 0, tok // LANES, jnp.where(lane == 1, tok % LANES, 0)).astype(F32)
    for e in range(N_EXPERTS):
        onehot = jnp.where(row == sel_ref[0, e:e + 1, :], 1.0, 0.0).astype(BF16)
        a = aff_ref[0, e:e + 1, :]
        hi = a.astype(BF16)
        r1 = a - hi.astype(F32)
        mid = r1.astype(BF16)
        lo = (r1 - mid.astype(F32)).astype(BF16)
        cols = jnp.where(lane == 2, hi.astype(F32), jnp.where(lane == 3, mid.astype(F32),
                         jnp.where(lane == 4, lo.astype(F32), pos))).astype(BF16)
        r = _dot_nt(onehot, cols)
        idx_ref[e, 0] = (r[:, 0:1] * LANES + r[:, 1:2]).astype(jnp.int32) + pl.program_id(0) * n
        g_ref[e, 0] = r[:, 2:3] + r[:, 3:4] + r[:, 4:5]


def slot_tokens(sel, aff, cap):
    B, n = sel.shape[0], sel.shape[2]
    row = pl.BlockSpec((1, N_EXPERTS, n), lambda b: (b, 0, 0))
    col = pl.BlockSpec((N_EXPERTS, 1, cap, 1), lambda b: (0, b, 0, 0))
    idx, gate = pl.pallas_call(
        functools.partial(_slot_tokens_kernel, cap=cap), grid=(B,),
        in_specs=[row, row], out_specs=[col, col],
        out_shape=[jax.ShapeDtypeStruct((N_EXPERTS, B, cap, 1), jnp.int32),
                   jax.ShapeDtypeStruct((N_EXPERTS, B, cap, 1), F32)],
        compiler_params=_cp("parallel"), name="slot_tokens")(sel, aff)
    return idx.reshape(N_EXPERTS, B * cap, 1), gate.reshape(N_EXPERTS, B * cap, 1)


def _expert_kernel(idx_ref, h_hbm, gate_ref, wg_ref, wu_ref, wd_ref, o_ref, xg_ref, xb_ref, acc_ref, sem,
                   *, tm):
    e, i, f = pl.program_id(0), pl.program_id(1), pl.program_id(2)
    n_i, n_f = pl.num_programs(1), pl.num_programs(2)
    q = e * n_i + i
    n_q = pl.num_programs(0) * n_i
    slot = q % 2
    share = tm // n_f

    def row_copy(tile, r, dst_slot):
        src = h_hbm.at[pl.ds(idx_ref[tile * tm + r], 1), :]
        return pltpu.make_async_copy(src, xg_ref.at[dst_slot, pl.ds(r, 1), :], sem.at[dst_slot])

    def wait_tile(dst_slot):
        pltpu.make_async_copy(h_hbm.at[pl.ds(0, tm), :], xg_ref.at[dst_slot], sem.at[dst_slot]).wait()

    @pl.when((q == 0) & (f == 0))
    def _():
        def first(r, carry):
            row_copy(0, r, 0).start()
            return carry
        lax.fori_loop(0, tm, first, 0)

    @pl.when(f == 0)
    def _():
        wait_tile(slot)
        p = xg_ref[slot]
        xb_ref[:, :D // 2] = pltpu.bitcast(lax.shift_left(p, 16), F32).astype(BF16)
        xb_ref[:, D // 2:] = pltpu.bitcast(p & jnp.int32(-65536), F32).astype(BF16)
        acc_ref[...] = jnp.zeros_like(acc_ref)

    wg, wu, wd = wg_ref[0, 0].astype(BF16), wu_ref[0, 0].astype(BF16), wd_ref[0, 0].astype(BF16)
    n_sub = 2
    sub = tm // n_sub
    gu = [(_dot(xb_ref[r * sub:(r + 1) * sub, :], wg), _dot(xb_ref[r * sub:(r + 1) * sub, :], wu)) for r in range(n_sub)]
    nxt = jnp.minimum(q + 1, n_q - 1)
    for r in range(share):
        row_copy(nxt, f * share + r, 1 - slot).start()
    for r, (g, u) in enumerate(gu):
        hid = ((g * _sigmoid(g)) * u).astype(BF16)
        acc_ref[r * sub:(r + 1) * sub, :] += _dot(hid, wd)

    @pl.when(f == n_f - 1)
    def _():
        o_ref[0] = (acc_ref[...] * gate_ref[0]).astype(BF16)

    @pl.when((q == n_q - 1) & (f == n_f - 1))
    def _():
        wait_tile(1 - slot)


def experts(idx, h2p, gate, wg, wu, wd, l, tm):
    E, M, tf = gate.shape[0], gate.shape[1], EXPERT_HIDDEN_TILE
    F = wg.shape[3]
    tm = min(tm, M)
    grid_spec = pltpu.PrefetchScalarGridSpec(
        num_scalar_prefetch=1, grid=(E, M // tm, F // tf),
        in_specs=[pl.BlockSpec(memory_space=pl.ANY),
                  pl.BlockSpec((1, tm, 1), lambda e, i, f, idx: (e, i, 0)),
                  pl.BlockSpec((1, 1, D, tf), lambda e, i, f, idx: (l, e, 0, f)),
                  pl.BlockSpec((1, 1, D, tf), lambda e, i, f, idx: (l, e, 0, f)),
                  pl.BlockSpec((1, 1, tf, D), lambda e, i, f, idx: (l, e, f, 0))],
        out_specs=pl.BlockSpec((1, tm, D), lambda e, i, f, idx: (e, i, 0)),
        scratch_shapes=[pltpu.VMEM((2, tm, D // 2), jnp.int32), pltpu.VMEM((tm, D), BF16),
                        pltpu.VMEM((tm, D), F32), pltpu.SemaphoreType.DMA((2,))])
    return pl.pallas_call(
        functools.partial(_expert_kernel, tm=tm), grid_spec=grid_spec,
        out_shape=jax.ShapeDtypeStruct((E, M, D), BF16),
        compiler_params=_cp("arbitrary", "arbitrary", "arbitrary"), name="experts")(idx, h2p, gate, wg, wu, wd)


COMBINE_GROUP = 8


def _combine_kernel(slot_ref, ye_ref, x_ref, gate_ref, gfin_ref, o_ref, acc_ref, *, cap, final):
    g = pl.program_id(2)
    tt = x_ref.shape[1]
    lane = lax.broadcasted_iota(jnp.int32, (tt, LANES), 1)
    j = lax.broadcasted_iota(jnp.int32, (tt, cap), 1).astype(F32)
    slots = slot_ref[0]
    parts = []
    for k in range(COMBINE_GROUP):
        slot = jnp.sum(jnp.where(lane == g * COMBINE_GROUP + k, slots, 0.0), axis=1, keepdims=True)
        parts.append(jnp.where(j == slot, 1.0, 0.0).astype(BF16))
    onehot = jnp.concatenate(parts, axis=1)
    contrib = _dot(onehot, ye_ref[:, 0].reshape(COMBINE_GROUP * cap, D))

    @pl.when(g == 0)
    def _():
        acc_ref[...] = contrib

    @pl.when(g == pl.num_programs(2) - 1)
    def _():
        x2 = x_ref[0] + gate_ref[0] * (acc_ref[...] + contrib)
        if final:
            x2 = _rms(x2, gfin_ref[...])
        o_ref[0] = x2


def combine(slot_tok, ye, x3, mod3, gfin, cap, mod_row, final):
    B, n = x3.shape[0], x3.shape[1]
    tt = min(n, TOKEN_TILE)
    assert N_EXPERTS == 2 * COMBINE_GROUP
    xt = pl.BlockSpec((1, tt, D), lambda b, t, g: (b, t, 0))
    return pl.pallas_call(
        functools.partial(_combine_kernel, cap=cap, final=final), grid=(B, n // tt, N_EXPERTS // COMBINE_GROUP),
        in_specs=[pl.BlockSpec((1, tt, 128), lambda b, t, g: (b, t, 0)),
                  pl.BlockSpec((COMBINE_GROUP, 1, cap, D), lambda b, t, g: (g, b, 0, 0)), xt,
                  pl.BlockSpec((1, 1, D), lambda b, t, g: (mod_row(b) * 6 + 5, 0, 0)),
                  pl.BlockSpec((1, D), lambda b, t, g: (0, 0))],
        out_specs=xt, out_shape=jax.ShapeDtypeStruct((B, n, D), F32),
        scratch_shapes=[pltpu.VMEM((tt, D), F32)],
        compiler_params=_cp("parallel", "parallel", "arbitrary"), name="combine",
    )(slot_tok, ye.reshape(N_EXPERTS, B, cap, D), x3, mod3, gfin)


def _swa_q_perm():
    return np.concatenate([np.arange(HEAD) + (p + 4 * hh) * HEAD for p in range(4) for hh in range(2)])


def _prep_w_in(w):
    seg = lambda name, width: w[:, REF[name]:REF[name] + width]
    swa_q = seg("swa_q", 512)[:, _swa_q_perm()]
    parts = [seg("gla_v", 512), seg("gla_r", 512), swa_q, seg("na_q", 512), seg("na_k", 512), seg("na_v", 512),
             seg("sgu_u", 512), seg("sgu_v", 512), seg("gla_q", 256), seg("gla_k", 256), seg("swa_k", 128),
             seg("swa_v", 128), seg("gla_af", 16), seg("gla_ab", 16)]
    out = jnp.concatenate(parts, axis=1)
    return jnp.pad(out, ((0, 0), (0, NP - out.shape[1]))).astype(BF16)


def _rope_tables():
    quarter = HEAD // 4
    freqs = 10000.0 ** (-jnp.arange(quarter, dtype=F32) / quarter)
    t = jnp.arange(T)
    row = (t // GRID_W).astype(F32)
    col = (t % GRID_W).astype(F32)
    ang = jnp.concatenate([row[:, None] * freqs, col[:, None] * freqs], axis=-1)
    cos, sin = jnp.cos(ang), jnp.sin(ang)
    cos_t = jnp.tile(cos, (1, 4))
    sin_t = jnp.tile(jnp.concatenate([-sin, sin], axis=-1), (1, 2))
    return cos_t, sin_t


def _moe(h2, lg, x3, mod3, gfin, wg, wu, wd, l, cap, tm, mod_row, final):
    B, n = x3.shape[0], x3.shape[1]
    sel, aff, slot_tok = select(lg.reshape(B, n, 128), cap)
    idx, gate = slot_tokens(sel, aff, cap)
    ye = experts(idx.reshape(-1), h2, gate, wg, wu, wd, l, tm)
    return combine(slot_tok, ye, x3, mod3, gfin, cap, mod_row, final)


def kernel(x, c, ctx, c_ctx, w_ada, b_ada, norm1_g, norm2_g, w_in, gla_w_a2, gla_b_a2, gla_norm_g, swa_sink,
           na_rpb, sgu_ln_g, sgu_ln_b, sgu_w_s, sgu_b_s, w_gate, b_gate, w_branch, w_out, w_router,
           w_exp_gate, w_exp_up, w_exp_down, final_norm_g):
    B = x.shape[0]
    ctx_row = B
    n_mod = -(-(B + 1) // SUBLANES) * SUBLANES
    cvec = jnp.zeros((n_mod, D), F32).at[:B].set(c).at[ctx_row].set(c_ctx)
    cos_t, sin_t = _rope_tables()
    perm = _swa_q_perm()
    gfin = final_norm_g.reshape(1, D)
    tm = ROW_TILE
    ctx_tile_row = lambda i: ctx_row
    x2 = x.reshape(B * T, D)
    xc2 = ctx.reshape(B * L, D)
    for l in range(DEPTH):
        need_ctx = l < DEPTH - 1
        mod3 = ada(cvec, w_ada, b_ada.reshape(DEPTH, 1, 6 * D), l).reshape(n_mod * 6, 1, D)
        w_in_p = _prep_w_in(w_in[l])
        g1 = norm1_g[l].reshape(1, D)
        h, p = inproj(x2, mod3, g1, w_in_p, INPROJ_ROW_TILE, lambda i: i // (T // INPROJ_ROW_TILE))
        hc, pc = inproj(xc2, mod3, g1, w_in_p, INPROJ_ROW_TILE, ctx_tile_row)
        p3, pc3 = p.reshape(B, T, NP), pc.reshape(B, L, NP)

        w2 = jnp.zeros((128, 512), F32).at[:GLA_RANK, :256].set(gla_w_a2[l, 0])
        w2 = w2.at[GLA_RANK:2 * GLA_RANK, 256:].set(gla_w_a2[l, 1]).astype(BF16)
        b2 = gla_b_a2[l].reshape(1, 512)
        ocf, ocb, of, ob = gla_scan(p3, pc3, w2, b2)
        gn = gla_norm_g[l].reshape(1, GLA_DV)
        yb = swa(p3, pc3, swa_sink[l], cos_t, sin_t)
        yn = na(p3, pc3, na_bias_table(na_rpb[l]))
        ln_g, ln_b = sgu_ln_g[l].reshape(1, 512), sgu_ln_b[l].reshape(1, 512)
        ws = sgu_w_s[l].astype(BF16)
        bs = jnp.broadcast_to(sgu_b_s[l][:, :, None], (4, SGU_CHUNK, 128))
        yd = sgu(p3, ln_g, ln_b, ws, bs)

        bg = b_gate[l].reshape(4, 1, D)
        wb = w_branch[l].at[1].set(w_branch[l, 1][perm]).astype(BF16)
        wo = w_out[l].astype(BF16)
        wr = jnp.pad(w_router[l], ((0, 0), (0, 128 - N_EXPERTS))).astype(BF16)
        g2 = norm2_g[l].reshape(1, D)
        flat = lambda y: y.reshape(-1, 512)
        acc = merge(h, flat(of), flat(ob), p, gn, [flat(yb), flat(yn), flat(yd)], w_gate, l, bg, wb, tm)
        x1, h2, lg = outproj(acc, wo, x2, mod3, g2, wr, OUTPROJ_ROW_TILE, lambda i: i // (T // OUTPROJ_ROW_TILE))
        x2 = _moe(h2, lg, x1.reshape(B, T, D), mod3, gfin, w_exp_gate, w_exp_up, w_exp_down, l,
                  2 * T // N_EXPERTS, tm, lambda b: b, final=not need_ctx).reshape(B * T, D)
        if need_ctx:
            ybc = ctx_attn(pc3, swa_sink[l], COL["swa_q"], COL["swa_k"], COL["swa_v"], 1, True)
            ync = ctx_attn(pc3, swa_sink[l], COL["na_q"], COL["na_k"], COL["na_v"], 4, False)
            ydc = sgu(pc3, ln_g, ln_b, ws, bs)
            accc = merge(hc, flat(ocf), flat(ocb), pc, gn, [flat(ybc), flat(ync), flat(ydc)], w_gate, l, bg, wb, tm)
            xc1, hc2, lgc = outproj(accc, wo, xc2, mod3, g2, wr, OUTPROJ_ROW_TILE, ctx_tile_row)
            xc2 = _moe(hc2, lgc, xc1.reshape(B, L, D), mod3, gfin, w_exp_gate, w_exp_up, w_exp_down, l,
                       2 * L // N_EXPERTS, tm, lambda b: ctx_row, final=False).reshape(B * L, D)
    return x2.reshape(B, T, D)
```
